```python
import math, functools
import jax, jax.numpy as jnp
from jax import lax
import numpy as np

D_MODEL = 1024
BATCH = 32
SEQ = 256
DEPTH = 4
DEC_BATCH = 8
DEC_SEQ = 2048
PAST_LEN = 256

GRID_W = 64
N_EVEN = (DEPTH + 1) // 2
N_ODD = DEPTH // 2
D_CONV = D_MODEL // 2
CONV_W = 3
MLA_HEADS = 8
NOPE_DIM = 64
ROPE_DIM = 32
V_DIM = (D_MODEL // 2) // MLA_HEADS
QK_DIM = NOPE_DIM + ROPE_DIM
Q_RANK = 3 * D_MODEL // 8
KV_RANK = D_MODEL // 4
AX_FREQS = ROPE_DIM // 4
ROPE_BASE = 10000.0
Q_BLOCK = 128
EV_SPLITS = [D_CONV, 2 * D_CONV, 3 * D_CONV, 3 * D_CONV + Q_RANK, 3 * D_CONV + Q_RANK + KV_RANK]
EV_IN = 3 * D_CONV + Q_RANK + KV_RANK + ROPE_DIM
EV_MIX = D_CONV + MLA_HEADS * V_DIM
CHUNK = 128
D_GM = D_MODEL
GM_GROUPS = 8
GM_CH = D_GM // GM_GROUPS
N_EXPERTS = 16
N_GROUPS = 4
EXP_PER_GROUP = N_EXPERTS // N_GROUPS
TOP_K = 2
D_EXPERT = D_MODEL // 4
EPS = 1e-6

kernel_name = "hybrid_flow_prefix_conv_mla_gmlp_moe"


def rmsnorm(x, g):
    xf = x.astype(jnp.float32)
    y = xf * lax.rsqrt(jnp.mean(xf * xf, axis=-1, keepdims=True) + EPS)
    return y.astype(x.dtype) * g


def modulate(h, shift, scale):
    return h * (1 + scale) + shift


def short_conv(v, w):
    vp = jnp.pad(v, ((0, 0), (1, 1), (0, 0)))
    return vp[:, :-2] * w[0] + vp[:, 1:-1] * w[1] + vp[:, 2:] * w[2]


def axial_rope_tables(n_tokens, dtype):
    rows = n_tokens // GRID_W
    r = jnp.repeat(jnp.arange(rows, dtype=jnp.float32), GRID_W)
    col = jnp.tile(jnp.arange(GRID_W, dtype=jnp.float32), rows)
    inv = ROPE_BASE ** (-jnp.arange(AX_FREQS, dtype=jnp.float32) / AX_FREQS)
    ang = jnp.stack([r[:, None] * inv, col[:, None] * inv], axis=1)
    return jnp.cos(ang)[:, :, None, :].astype(dtype), jnp.sin(ang)[:, :, None, :].astype(dtype)


def axial_rope(x, cos, sin):
    xs = x.reshape(x.shape[:-1] + (2, 2, AX_FREQS))
    rot = jnp.stack([-xs[..., 1, :], xs[..., 0, :]], axis=-2)
    return (xs * cos + rot * sin).reshape(x.shape)


def block_attention(q, k, v):
    b, sq, h, dq = q.shape
    nb = sq // Q_BLOCK
    scale = QK_DIM ** -0.5
    qb = q.reshape(b, nb, Q_BLOCK, h, dq).transpose(1, 0, 2, 3, 4)

    def one(qi):
        s = jnp.einsum('bqhd,bkhd->bhqk', qi, k).astype(jnp.float32) * scale
        p = jax.nn.softmax(s, axis=-1).astype(v.dtype)
        return jnp.einsum('bhqk,bkhd->bqhd', p, v)

    o = lax.map(one, qb)
    return o.transpose(1, 0, 2, 3, 4).reshape(b, sq, h * v.shape[-1])


def even_mixer(h, w_in, conv_w, q_norm_g, w_q_up, kv_norm_g, w_kv_up, w_out, rope=None, ctx=None):
    bsz, s, _ = h.shape
    proj = h @ w_in
    b_g, c_g, v_in, q_a, kv_a, k_rope = jnp.split(proj, EV_SPLITS, axis=-1)
    y_conv = b_g * short_conv(c_g * v_in, conv_w)
    q = (rmsnorm(q_a, q_norm_g) @ w_q_up).reshape(bsz, s, MLA_HEADS, QK_DIM)
    q_nope, q_rope = q[..., :NOPE_DIM], q[..., NOPE_DIM:]
    ckv = rmsnorm(kv_a, kv_norm_g)
    if rope is not None:
        cos, sin = rope
        q_rope = axial_rope(q_rope, cos[:, None], sin[:, None])
        k_rope_used = axial_rope(k_rope, cos, sin)
    else:
        k_rope_used = k_rope
    keys_ckv, keys_kr = ckv, k_rope_used
    if ctx is not None:
        keys_ckv = jnp.concatenate([ckv, ctx[0]], axis=1)
        keys_kr = jnp.concatenate([k_rope_used, ctx[1]], axis=1)
    sk = keys_ckv.shape[1]
    kv = (keys_ckv @ w_kv_up).reshape(bsz, sk, MLA_HEADS, NOPE_DIM + V_DIM)
    k = jnp.concatenate([kv[..., :NOPE_DIM],
                         jnp.broadcast_to(keys_kr[:, :, None, :], (bsz, sk, MLA_HEADS, ROPE_DIM))], axis=-1)
    v = kv[..., NOPE_DIM:]
    attn = block_attention(jnp.concatenate([q_nope, q_rope], axis=-1), k, v)
    out = jnp.concatenate([y_conv, attn], axis=-1) @ w_out
    return out, ckv, k_rope


def odd_mixer(h, w_in, v_g, w_s, b_s, w_out):
    bsz, s, _ = h.shape
    z = jax.nn.gelu(h @ w_in, approximate=False)
    u, v = jnp.split(z, 2, axis=-1)
    v = rmsnorm(v, v_g)
    vc = v.reshape(bsz, s // CHUNK, CHUNK, GM_GROUPS, GM_CH)
    sg = jnp.einsum('gpq,bnqgc->bnpgc', w_s, vc) + b_s.T[None, None, :, :, None]
    return (u * sg.reshape(bsz, s, D_GM)) @ w_out


def moe(h, router_w, router_b, w1, w3, w2):
    bsz, s, d = h.shape
    t = h.reshape(-1, d)
    n = t.shape[0]
    scores = jax.nn.sigmoid((t @ router_w).astype(jnp.float32))
    sel = scores + router_b.astype(jnp.float32)
    grp = sel.reshape(n, N_GROUPS, EXP_PER_GROUP)
    gscore = lax.top_k(grp, TOP_K)[0].sum(-1)
    g_idx = jnp.argmax(gscore, axis=-1)
    sel_g = jnp.take_along_axis(grp, g_idx[:, None, None], axis=1)[:, 0]
    _, loc = lax.top_k(sel_g, TOP_K)
    e_idx = g_idx[:, None] * EXP_PER_GROUP + loc
    wts = jnp.take_along_axis(scores, e_idx, axis=1)
    wts = wts / jnp.sum(wts, axis=-1, keepdims=True)
    combine = jnp.sum(jax.nn.one_hot(e_idx, N_EXPERTS, dtype=jnp.float32) * wts[..., None], axis=1).astype(t.dtype)
    hid = jax.nn.silu(jnp.einsum('td,edf->tef', t, w1)) * jnp.einsum('td,edf->tef', t, w3)
    hid = hid * combine[:, :, None]
    return jnp.einsum('tef,efd->td', hid, w2).reshape(bsz, s, d)


def setup_inputs(seed: int = 0) -> dict:
    key = jax.random.key(seed)
    ks = iter(jax.random.split(key, 40))
    nrm = lambda shape, sc: jax.random.normal(next(ks), shape, jnp.float32) * sc
    D = D_MODEL
    return {
        "x_prompt": nrm((BATCH, SEQ, D), 1.0),
        "x_sample": nrm((DEC_BATCH, DEC_SEQ, D), 1.0),
        "cache_ckv": nrm((DEC_BATCH, N_EVEN, PAST_LEN, KV_RANK), 1.0),
        "cache_krope": nrm((DEC_BATCH, N_EVEN, PAST_LEN, ROPE_DIM), 1.0),
        "c": nrm((DEC_BATCH, D), 1.0),
        "c_ctx": nrm((D,), 1.0),
        "w_ada": nrm((DEPTH, D, 6 * D), 0.5 * D ** -0.5),
        "b_ada": nrm((DEPTH, 6 * D), 0.02),
        "g_mix": 1.0 + nrm((DEPTH, D), 0.02),
        "g_ffn": 1.0 + nrm((DEPTH, D), 0.02),
        "g_final": 1.0 + nrm((D,), 0.02),
        "ev_w_in": nrm((N_EVEN, D, EV_IN), D ** -0.5),
        "conv_w": nrm((N_EVEN, CONV_W, D_CONV), CONV_W ** -0.5),
        "q_norm_g": 1.0 + nrm((N_EVEN, Q_RANK), 0.02),
        "w_q_up": nrm((N_EVEN, Q_RANK, MLA_HEADS * QK_DIM), Q_RANK ** -0.5),
        "kv_norm_g": 1.0 + nrm((N_EVEN, KV_RANK), 0.02),
        "w_kv_up": nrm((N_EVEN, KV_RANK, MLA_HEADS * (NOPE_DIM + V_DIM)), KV_RANK ** -0.5),
        "ev_w_out": nrm((N_EVEN, EV_MIX, D), EV_MIX ** -0.5),
        "gm_w_in": nrm((N_ODD, D, 2 * D_GM), D ** -0.5),
        "gm_v_g": 1.0 + nrm((N_ODD, D_GM), 0.02),
        "gm_w_s": nrm((N_ODD, GM_GROUPS, CHUNK, CHUNK), CHUNK ** -0.5),
        "gm_b_s": 1.0 + nrm((N_ODD, GM_GROUPS, CHUNK), 0.02),
        "gm_w_out": nrm((N_ODD, D_GM, D), D_GM ** -0.5),
        "router_w": nrm((D, N_EXPERTS), D ** -0.5),
        "router_b": nrm((N_EXPERTS,), 0.01),
        "moe_w1": nrm((DEPTH, N_EXPERTS, D, D_EXPERT), D ** -0.5),
        "moe_w3": nrm((DEPTH, N_EXPERTS, D, D_EXPERT), D ** -0.5),
        "moe_w2": nrm((DEPTH, N_EXPERTS, D_EXPERT, D), D_EXPERT ** -0.5),
    }


def reference(x_prompt, x_sample, cache_ckv, cache_krope, c, c_ctx, w_ada, b_ada, g_mix, g_ffn, g_final,
              ev_w_in, conv_w, q_norm_g, w_q_up, kv_norm_g, w_kv_up, ev_w_out,
              gm_w_in, gm_v_g, gm_w_s, gm_b_s, gm_w_out, router_w, router_b, moe_w1, moe_w3, moe_w2):
    cos, sin = axial_rope_tables(x_sample.shape[1], x_sample.dtype)
    xp, xs = x_prompt, x_sample
    ckv_states, kr_states = [], []
    for l in range(DEPTH):
        mp = jnp.split((jax.nn.silu(c_ctx) @ w_ada[l] + b_ada[l])[None, None, :], 6, axis=-1)
        ms = jnp.split((jax.nn.silu(c) @ w_ada[l] + b_ada[l])[:, None, :], 6, axis=-1)
        hp = modulate(rmsnorm(xp, g_mix[l]), mp[0], mp[1])
        hs = modulate(rmsnorm(xs, g_mix[l]), ms[0], ms[1])
        i = l // 2
        if l % 2 == 0:
            ev = (ev_w_in[i], conv_w[i], q_norm_g[i], w_q_up[i], kv_norm_g[i], w_kv_up[i], ev_w_out[i])
            op, ckv, kr = even_mixer(hp, *ev)
            os_, _, _ = even_mixer(hs, *ev, rope=(cos, sin), ctx=(cache_ckv[:, i], cache_krope[:, i]))
            ckv_states.append(ckv)
            kr_states.append(kr)
        else:
            od = (gm_w_in[i], gm_v_g[i], gm_w_s[i], gm_b_s[i], gm_w_out[i])
            op = odd_mixer(hp, *od)
            os_ = odd_mixer(hs, *od)
        xp = xp + mp[2] * op
        xs = xs + ms[2] * os_
        hp = modulate(rmsnorm(xp, g_ffn[l]), mp[3], mp[4])
        hs = modulate(rmsnorm(xs, g_ffn[l]), ms[3], ms[4])
        xp = xp + mp[5] * moe(hp, router_w, router_b, moe_w1[l], moe_w3[l], moe_w2[l])
        xs = xs + ms[5] * moe(hs, router_w, router_b, moe_w1[l], moe_w3[l], moe_w2[l])
    y_prompt = rmsnorm(xp, g_final)
    y_sample = rmsnorm(xs, g_final)
    state_ckv = jnp.stack(ckv_states, axis=1)
    state_krope = jnp.stack(kr_states, axis=1)
    return (y_prompt, y_sample, state_ckv, state_krope)
```

```python
import functools
import math

import jax
import jax.numpy as jnp
from jax import lax
from jax.experimental import pallas as pl
from jax.experimental.pallas import tpu as pltpu

D = 1024
BATCH, SEQ = 32, 256
DEC_BATCH, DEC_SEQ = 8, 2048
PAST = 256
DEPTH = 4
GRID_W = 64
D_CONV = 512
HEADS = 8
NOPE, ROPE, V_DIM = 64, 32, 64
QK_DIM = NOPE + ROPE
Q_RANK, KV_RANK = 384, 256
AX_FREQS = ROPE // 4
ROPE_BASE = 10000.0
CHUNK = 128
GM_GROUPS = 8
N_EXPERTS, N_GROUPS, EPG = 16, 4, 4
D_EXPERT = 256
EPS = 1e-6

NP_TOK = BATCH * SEQ
NS_TOK = DEC_BATCH * DEC_SEQ
T = NP_TOK + NS_TOK
MOD_ROWS = 16
HEAD_PAD = 128
HALO = 16

TM = 512
TM_MOE = 1024
TQ = 256
VMEM_LIMIT = 56 * 1024 * 1024

BF = jnp.bfloat16
F32 = jnp.float32


def _cp(n_axes):
    return pltpu.CompilerParams(dimension_semantics=("arbitrary",) * n_axes,
                                vmem_limit_bytes=VMEM_LIMIT)


def _mod_row(t, tm):
    n_prompt_tiles = NP_TOK // tm
    per_seq = DEC_SEQ // tm
    return jnp.where(t < n_prompt_tiles, DEC_BATCH, (t - n_prompt_tiles) // per_seq)


def _rms(x, g):
    return x * lax.rsqrt(jnp.mean(x * x, axis=-1, keepdims=True) + EPS) * g


def _dot(a, b):
    return jnp.dot(a, b, preferred_element_type=F32)


def _dot_nt(a, b, precision=None):
    return lax.dot_general(a, b, (((1,), (1,)), ((), ())), precision=precision,
                           preferred_element_type=F32)


ADA_TN = 1536


def _ada_kernel(cc_ref, w_ref, b_ref, o_ref):
    cc = cc_ref[...]
    s = (cc / (1.0 + jnp.exp(-cc))).astype(BF)
    o_ref[0] = _dot(s, w_ref[0].astype(BF)) + b_ref[0]


def _ada(cc, w_ada, b_ada):
    n = 6 * D
    return pl.pallas_call(
        _ada_kernel,
        grid=(DEPTH, n // ADA_TN),
        in_specs=[
            pl.BlockSpec((MOD_ROWS, D), lambda l, j: (0, 0)),
            pl.BlockSpec((1, D, ADA_TN), lambda l, j: (l, 0, j)),
            pl.BlockSpec((1, 1, ADA_TN), lambda l, j: (l, 0, j)),
        ],
        out_specs=pl.BlockSpec((1, MOD_ROWS, ADA_TN), lambda l, j: (l, 0, j)),
        out_shape=jax.ShapeDtypeStruct((DEPTH, MOD_ROWS, n), F32),
        compiler_params=_cp(2),
        name="ada",
    )(cc, w_ada, b_ada.reshape(DEPTH, 1, n))


def _route(logits_t, rb):
    sc = 1.0 / (1.0 + jnp.exp(-logits_t))
    sel = sc + rb
    rows = [sel[e:e + 1, :] for e in range(N_EXPERTS)]
    srows = [sc[e:e + 1, :] for e in range(N_EXPERTS)]

    def top2sum(a, b, c, d):
        hi1, lo1 = jnp.maximum(a, b), jnp.minimum(a, b)
        hi2, lo2 = jnp.maximum(c, d), jnp.minimum(c, d)
        return jnp.maximum(hi1, hi2) + jnp.maximum(jnp.minimum(hi1, hi2), jnp.maximum(lo1, lo2))

    gs = [top2sum(*rows[EPG * g:EPG * (g + 1)]) for g in range(N_GROUPS)]
    best = gs[0]
    gidx = jnp.zeros_like(best, dtype=jnp.int32)
    for g in range(1, N_GROUPS):
        upd = gs[g] > best
        best = jnp.where(upd, gs[g], best)
        gidx = jnp.where(upd, g, gidx)

    picked = []
    for g in range(N_GROUPS):
        grp = rows[EPG * g:EPG * (g + 1)]
        in_g = gidx == g
        for j in range(EPG):
            rank = jnp.zeros_like(gidx)
            for k in range(EPG):
                if k == j:
                    continue
                ahead = grp[k] > grp[j]
                if k < j:
                    ahead = ahead | (grp[k] == grp[j])
                rank = rank + ahead.astype(jnp.int32)
            picked.append(in_g & (rank < 2))
    w = [jnp.where(picked[e], srows[e], 0.0) for e in range(N_EXPERTS)]
    wsum = w[0]
    for e in range(1, N_EXPERTS):
        wsum = wsum + w[e]
    inv = 1.0 / wsum
    return [w[e] * inv for e in range(N_EXPERTS)]


def _ffn_prep(x_new, mod_ref, gf_ref, rwt_ref, rb_ref, h2_ref, comb_ref, ct_scr):
    shift = mod_ref[0, :, 3 * D:4 * D]
    scale = mod_ref[0, :, 4 * D:5 * D]
    h2 = _rms(x_new, gf_ref[...]) * (1.0 + scale) + shift
    h2_ref[...] = h2.astype(BF)
    logits_t = _dot_nt(rwt_ref[...], h2, precision=lax.Precision.HIGHEST)
    comb = _route(logits_t, rb_ref[...])
    ct_scr[...] = jnp.zeros_like(ct_scr)
    for e in range(N_EXPERTS):
        ct_scr[e:e + 1, :] = comb[e]
    comb_ref[...] = ct_scr[...].T


EV_EXT = 3 * D_CONV + Q_RANK + KV_RANK + 3 * HEAD_PAD


def _even_in_kernel(x_ref, mod_ref, g_ref, win_ref, qg_ref, wq1_ref, wq2_ref, kg_ref, wk_ref, wv_ref,
                    rc_ref, rs_ref,
                    bg_ref, cv_ref, q_ref, k_ref, v_ref, ckv_ref, kr_ref):
    x = x_ref[...]
    shift = mod_ref[0, :, 0:D]
    scale = mod_ref[0, :, D:2 * D]
    h = (_rms(x, g_ref[...]) * (1.0 + scale) + shift).astype(BF)
    proj = _dot(h, win_ref[...])
    o = 0
    b_g = proj[:, o:o + D_CONV]; o += D_CONV
    c_g = proj[:, o:o + D_CONV]; o += D_CONV
    v_in = proj[:, o:o + D_CONV]; o += D_CONV
    q_a = proj[:, o:o + Q_RANK]; o += Q_RANK
    kv_a = proj[:, o:o + KV_RANK]; o += KV_RANK
    kr_raw = proj[:, o:o + HEAD_PAD]; o += HEAD_PAD
    kr_cat = proj[:, o:o + HEAD_PAD]; o += HEAD_PAD
    kr_sw = proj[:, o:o + HEAD_PAD]

    bg_ref[...] = b_g.astype(BF)
    cv_ref[...] = (c_g * v_in).astype(BF)
    kr_ref[...] = kr_raw

    rc = rc_ref[...]
    rs = rs_ref[...]
    qn = _rms(q_a, qg_ref[...]).astype(BF)
    q1 = _dot(qn, wq1_ref[...])
    q2 = _dot(qn, wq2_ref[...])
    qscale = QK_DIM ** -0.5
    for hd in range(HEADS):
        sl = slice(hd * HEAD_PAD, (hd + 1) * HEAD_PAD)
        q_ref[:, sl] = ((q1[:, sl] * rc + q2[:, sl] * rs) * qscale).astype(BF)

    ckv = _rms(kv_a, kg_ref[...])
    ckv_ref[...] = ckv
    ckv_b = ckv.astype(BF)
    kk = _dot(ckv_b, wk_ref[...])
    kr = kr_cat * rc + kr_sw * rs
    for hd in range(HEADS):
        sl = slice(hd * HEAD_PAD, (hd + 1) * HEAD_PAD)
        k_ref[:, sl] = (kk[:, sl] + kr).astype(BF)
    v_ref[...] = _dot(ckv_b, wv_ref[...]).astype(BF)


def _even_in(x, mod, l, g_mix, win, qg, wq1, wq2, kg, wk, wv, rope_c, rope_s):
    nt = T // TM
    npt = NP_TOK // TM
    per_seq = DEC_SEQ // TM
    ident_blk = DEC_SEQ // TM

    def rope_idx(t):
        return (jnp.where(t < npt, ident_blk, (t - npt) % per_seq), 0)

    full = lambda shape: pl.BlockSpec(shape, lambda t: (0,) * len(shape))
    tok = lambda w: pl.BlockSpec((TM, w), lambda t: (t, 0))
    return pl.pallas_call(
        _even_in_kernel,
        grid=(nt,),
        in_specs=[
            tok(D),
            pl.BlockSpec((1, 1, 6 * D), lambda t: (l * MOD_ROWS + _mod_row(t, TM), 0, 0)),
            full((1, D)), full((D, EV_EXT)), full((1, Q_RANK)),
            full((Q_RANK, HEADS * HEAD_PAD)), full((Q_RANK, HEADS * HEAD_PAD)),
            full((1, KV_RANK)), full((KV_RANK, HEADS * HEAD_PAD)), full((KV_RANK, HEADS * V_DIM)),
            pl.BlockSpec((TM, HEAD_PAD), rope_idx), pl.BlockSpec((TM, HEAD_PAD), rope_idx),
        ],
        out_specs=[tok(D_CONV), tok(D_CONV), tok(HEADS * HEAD_PAD), tok(HEADS * HEAD_PAD),
                   tok(HEADS * V_DIM), tok(KV_RANK), tok(HEAD_PAD)],
        out_shape=[
            jax.ShapeDtypeStruct((T, D_CONV), BF), jax.ShapeDtypeStruct((T, D_CONV), BF),
            jax.ShapeDtypeStruct((T, HEADS * HEAD_PAD), BF), jax.ShapeDtypeStruct((T, HEADS * HEAD_PAD), BF),
            jax.ShapeDtypeStruct((T, HEADS * V_DIM), BF),
            jax.ShapeDtypeStruct((T, KV_RANK), F32), jax.ShapeDtypeStruct((T, HEAD_PAD), F32),
        ],
        compiler_params=_cp(1),
        name="even_in",
    )(x, mod, g_mix, win, qg, wq1, wq2, kg, wk, wv, rope_c, rope_s)


CTX_TM = 512


def _ctx_kv_kernel(ckv_ref, kr_ref, wk_ref, wv_ref, k_ref, v_ref):
    ckv_b = ckv_ref[0].astype(BF)
    kk = _dot(ckv_b, wk_ref[0])
    kr = kr_ref[0]
    for hd in range(HEADS):
        sl = slice(hd * HEAD_PAD, (hd + 1) * HEAD_PAD)
        k_ref[0, :, sl] = (kk[:, sl] + kr).astype(BF)
    v_ref[0] = _dot(ckv_b, wv_ref[0]).astype(BF)


def _ctx_kv(ckv_all, kr_all, wk_all, wv_all):
    n_even = ckv_all.shape[0]
    rows = DEC_BATCH * PAST
    return pl.pallas_call(
        _ctx_kv_kernel,
        grid=(n_even, rows // CTX_TM),
        in_specs=[
            pl.BlockSpec((1, CTX_TM, KV_RANK), lambda i, t: (i, t, 0)),
            pl.BlockSpec((1, CTX_TM, HEAD_PAD), lambda i, t: (i, t, 0)),
            pl.BlockSpec((1, KV_RANK, HEADS * HEAD_PAD), lambda i, t: (i, 0, 0)),
            pl.BlockSpec((1, KV_RANK, HEADS * V_DIM), lambda i, t: (i, 0, 0)),
        ],
        out_specs=[
            pl.BlockSpec((1, CTX_TM, HEADS * HEAD_PAD), lambda i, t: (i, t, 0)),
            pl.BlockSpec((1, CTX_TM, HEADS * V_DIM), lambda i, t: (i, t, 0)),
        ],
        out_shape=[jax.ShapeDtypeStruct((n_even, rows, HEADS * HEAD_PAD), BF),
                   jax.ShapeDtypeStruct((n_even, rows, HEADS * V_DIM), BF)],
        compiler_params=_cp(2),
        name="ctx_kv",
    )(ckv_all, kr_all, wk_all, wv_all)


def _attn_body(n_pairs, has_ctx, q_ref, k_ref, v_ref, *rest):
    if has_ctx:
        kc_ref, vc_ref, o_ref = rest
    else:
        (o_ref,) = rest
    lane = lax.broadcasted_iota(jnp.int32, (1, 2 * V_DIM), 1)
    for pr in range(n_pairs):
        vsl = slice(pr * 2 * V_DIM, (pr + 1) * 2 * V_DIM)
        v = v_ref[:, vsl]
        outs = []
        for sub in range(2):
            hsl = slice((2 * pr + sub) * HEAD_PAD, (2 * pr + sub + 1) * HEAD_PAD)
            q = q_ref[:, hsl]
            s1 = _dot_nt(q, k_ref[:, hsl])
            m = jnp.max(s1, axis=-1, keepdims=True)
            if has_ctx:
                s2 = _dot_nt(q, kc_ref[0, :, hsl])
                m = jnp.maximum(m, jnp.max(s2, axis=-1, keepdims=True))
            p1 = jnp.exp(s1 - m)
            den = jnp.sum(p1, axis=-1, keepdims=True)
            o = _dot(p1.astype(BF), v)
            if has_ctx:
                p2 = jnp.exp(s2 - m)
                den = den + jnp.sum(p2, axis=-1, keepdims=True)
                o = o + _dot(p2.astype(BF), vc_ref[0, :, vsl])
            outs.append(o * (1.0 / den))
        o_ref[:, vsl] = jnp.where(lane < V_DIM, outs[0], outs[1]).astype(BF)


def _attn_prompt(q, k, v):
    return pl.pallas_call(
        functools.partial(_attn_body, HEADS // 2, False),
        grid=(BATCH,),
        in_specs=[
            pl.BlockSpec((SEQ, HEADS * HEAD_PAD), lambda b: (b, 0)),
            pl.BlockSpec((SEQ, HEADS * HEAD_PAD), lambda b: (b, 0)),
            pl.BlockSpec((SEQ, HEADS * V_DIM), lambda b: (b, 0)),
        ],
        out_specs=pl.BlockSpec((SEQ, HEADS * V_DIM), lambda b: (b, 0)),
        out_shape=jax.ShapeDtypeStruct((NP_TOK, HEADS * V_DIM), BF),
        compiler_params=_cp(1),
        name="attn_prompt",
    )(q, k, v)


def _attn_sample(q, k, v, kc, vc, i):
    s_blk0 = NP_TOK // DEC_SEQ
    q_blk0 = NP_TOK // TQ
    nq = DEC_SEQ // TQ
    return pl.pallas_call(
        functools.partial(_attn_body, 1, True),
        grid=(DEC_BATCH, HEADS // 2, nq),
        in_specs=[
            pl.BlockSpec((TQ, 2 * HEAD_PAD), lambda b, hp, j: (q_blk0 + b * nq + j, hp)),
            pl.BlockSpec((DEC_SEQ, 2 * HEAD_PAD), lambda b, hp, j: (s_blk0 + b, hp)),
            pl.BlockSpec((DEC_SEQ, 2 * V_DIM), lambda b, hp, j: (s_blk0 + b, hp)),
            pl.BlockSpec((1, PAST, 2 * HEAD_PAD), lambda b, hp, j: (i, b, hp)),
            pl.BlockSpec((1, PAST, 2 * V_DIM), lambda b, hp, j: (i, b, hp)),
        ],
        out_specs=pl.BlockSpec((TQ, 2 * V_DIM), lambda b, hp, j: (b * nq + j, hp)),
        out_shape=jax.ShapeDtypeStruct((NS_TOK, HEADS * V_DIM), BF),
        compiler_params=_cp(3),
        name="attn_sample",
    )(q, k, v, kc, vc)


def _even_out_kernel(x_ref, mod_ref, bg_ref, cv_ref, cvp_ref, cvn_ref, at_ref, cw_ref, wo_ref,
                     gf_ref, rwt_ref, rb_ref,
                     xo_ref, h2_ref, comb_ref, ct_scr):
    t = pl.program_id(0)
    npt = NP_TOK // TM
    per_seq = DEC_SEQ // TM
    cv = cv_ref[...].astype(F32)
    r = lax.broadcasted_iota(jnp.int32, (TM, 1), 0)
    is_prompt = t < npt
    tile_in_seq = (t - npt) % per_seq
    first_row = jnp.where(is_prompt, 0, jnp.where(tile_in_seq == 0, 0, -1))
    last_row = jnp.where(is_prompt, SEQ - 1, jnp.where(tile_in_seq == per_seq - 1, TM - 1, -1))
    period_mask = jnp.where(is_prompt, SEQ - 1, TM - 1)
    first = (r & period_mask) == first_row
    last = (r & period_mask) == last_row
    prev_row = cvp_ref[HALO - 1:HALO, :].astype(F32)
    next_row = cvn_ref[0:1, :].astype(F32)
    prev = jnp.where(r == 0, prev_row, pltpu.roll(cv, 1, 0))
    prev = jnp.where(first, 0.0, prev)
    nxt = jnp.where(r == TM - 1, next_row, pltpu.roll(cv, TM - 1, 0))
    nxt = jnp.where(last, 0.0, nxt)
    cw = cw_ref[...]
    conv = prev * cw[0:1, :] + cv * cw[1:2, :] + nxt * cw[2:3, :]
    yc = (bg_ref[...].astype(F32) * conv).astype(BF)
    out = _dot(yc, wo_ref[0:D_CONV, :]) + _dot(at_ref[...], wo_ref[D_CONV:2 * D_CONV, :])
    gate = mod_ref[0, :, 2 * D:3 * D]
    x_new = x_ref[...] + gate * out
    xo_ref[...] = x_new
    _ffn_prep(x_new, mod_ref, gf_ref, rwt_ref, rb_ref, h2_ref, comb_ref, ct_scr)


def _even_out(x, mod, l, bg, cv, attn, conv_w, w_out, g_ffn, rwt, rb):
    nt = T // TM
    hb = TM // HALO
    nhb = T // HALO
    full = lambda shape: pl.BlockSpec(shape, lambda t: (0,) * len(shape))
    tok = lambda w: pl.BlockSpec((TM, w), lambda t: (t, 0))
    return pl.pallas_call(
        _even_out_kernel,
        grid=(nt,),
        in_specs=[
            tok(D),
            pl.BlockSpec((1, 1, 6 * D), lambda t: (l * MOD_ROWS + _mod_row(t, TM), 0, 0)),
            tok(D_CONV), tok(D_CONV),
            pl.BlockSpec((HALO, D_CONV), lambda t: (jnp.maximum(t * hb - 1, 0), 0)),
            pl.BlockSpec((HALO, D_CONV), lambda t: (jnp.minimum((t + 1) * hb, nhb - 1), 0)),
            tok(HEADS * V_DIM),
            full((3, D_CONV)), full((2 * D_CONV, D)), full((1, D)),
            full((N_EXPERTS, D)), full((N_EXPERTS, 1)),
        ],
        out_specs=[tok(D), tok(D), tok(128)],
        out_shape=[jax.ShapeDtypeStruct((T, D), F32), jax.ShapeDtypeStruct((T, D), BF),
                   jax.ShapeDtypeStruct((T, 128), F32)],
        scratch_shapes=[pltpu.VMEM((128, TM), F32)],
        compiler_params=_cp(1),
        name="even_out",
    )(x, mod, bg, cv, cv, cv, attn, conv_w, w_out, g_ffn, rwt, rb)


def _odd_kernel(x_ref, mod_ref, g_ref, win_ref, vg_ref, ws_ref, bst_ref, wo_ref,
                gf_ref, rwt_ref, rb_ref,
                xo_ref, h2_ref, comb_ref, gated_scr, ct_scr):
    x = x_ref[...]
    shift = mod_ref[0, :, 0:D]
    scale = mod_ref[0, :, D:2 * D]
    h = (_rms(x, g_ref[...]) * (1.0 + scale) + shift).astype(BF)
    zl = _dot(h, win_ref[...])
    z = 0.5 * zl * (1.0 + lax.erf(zl * math.sqrt(0.5)))
    u = z[:, 0:D]
    v = _rms(z[:, D:2 * D], vg_ref[...]).astype(BF)
    n_chunks = TM // CHUNK
    gch = D // GM_GROUPS
    for g in range(GM_GROUPS):
        csl = slice(g * gch, (g + 1) * gch)
        vg = jnp.concatenate([v[n * CHUNK:(n + 1) * CHUNK, csl] for n in range(n_chunks)], axis=1)
        sg = _dot(ws_ref[g], vg) + bst_ref[:, g:g + 1]
        for n in range(n_chunks):
            rsl = slice(n * CHUNK, (n + 1) * CHUNK)
            gated_scr[rsl, csl] = (u[rsl, csl] * sg[:, n * gch:(n + 1) * gch]).astype(BF)
    out = _dot(gated_scr[...], wo_ref[...])
    gate = mod_ref[0, :, 2 * D:3 * D]
    x_new = x + gate * out
    xo_ref[...] = x_new
    _ffn_prep(x_new, mod_ref, gf_ref, rwt_ref, rb_ref, h2_ref, comb_ref, ct_scr)


def _odd(x, mod, l, g_mix, win, v_g, w_s, b_st, w_out, g_ffn, rwt, rb):
    nt = T // TM
    full = lambda shape: pl.BlockSpec(shape, lambda t: (0,) * len(shape))
    tok = lambda w: pl.BlockSpec((TM, w), lambda t: (t, 0))
    return pl.pallas_call(
        _odd_kernel,
        grid=(nt,),
        in_specs=[
            tok(D),
            pl.BlockSpec((1, 1, 6 * D), lambda t: (l * MOD_ROWS + _mod_row(t, TM), 0, 0)),
            full((1, D)), full((D, 2 * D)), full((1, D)),
            full((GM_GROUPS, CHUNK, CHUNK)), full((CHUNK, GM_GROUPS)), full((D, D)),
            full((1, D)), full((N_EXPERTS, D)), full((N_EXPERTS, 1)),
        ],
        out_specs=[tok(D), tok(D), tok(128)],
        out_shape=[jax.ShapeDtypeStruct((T, D), F32), jax.ShapeDtypeStruct((T, D), BF),
                   jax.ShapeDtypeStruct((T, 128), F32)],
        scratch_shapes=[pltpu.VMEM((TM, D), BF), pltpu.VMEM((128, TM), F32)],
        compiler_params=_cp(1),
        name="odd",
    )(x, mod, g_mix, win, v_g, w_s, b_st, w_out, g_ffn, rwt, rb)


def _moe_kernel(final, x_ref, mod_ref, h2_ref, comb_ref, w1_ref, w3_ref, w2_ref, gfin_ref,
                o_ref, acc_scr):
    e = pl.program_id(1)

    @pl.when(e == 0)
    def _():
        acc_scr[...] = jnp.zeros_like(acc_scr)

    h2 = h2_ref[...]
    a = _dot(h2, w1_ref[0])
    b = _dot(h2, w3_ref[0])
    lane = lax.broadcasted_iota(jnp.int32, (1, 128), 1)
    c = jnp.sum(jnp.where(lane == e, comb_ref[...], 0.0), axis=-1, keepdims=True)
    hid = (a * (1.0 / (1.0 + jnp.exp(-a))) * b * c).astype(BF)
    acc_scr[...] += _dot(hid, w2_ref[0])

    @pl.when(e == N_EXPERTS - 1)
    def _():
        gate = mod_ref[0, :, 5 * D:6 * D]
        x_new = x_ref[...] + gate * acc_scr[...]
        if final:
            x_new = _rms(x_new, gfin_ref[...])
        o_ref[...] = x_new


def _moe(x, mod, l, h2, comb, w1, w3, w2, g_final, final):
    nt = T // TM_MOE
    tok = lambda w: pl.BlockSpec((TM_MOE, w), lambda t, e: (t, 0))
    return pl.pallas_call(
        functools.partial(_moe_kernel, final),
        grid=(nt, N_EXPERTS),
        in_specs=[
            tok(D),
            pl.BlockSpec((1, 1, 6 * D), lambda t, e: (l * MOD_ROWS + _mod_row(t, TM_MOE), 0, 0)),
            tok(D), tok(128),
            pl.BlockSpec((1, D, D_EXPERT), lambda t, e: (e, 0, 0)),
            pl.BlockSpec((1, D, D_EXPERT), lambda t, e: (e, 0, 0)),
            pl.BlockSpec((1, D_EXPERT, D), lambda t, e: (e, 0, 0)),
            pl.BlockSpec((1, D), lambda t, e: (0, 0)),
        ],
        out_specs=tok(D),
        out_shape=jax.ShapeDtypeStruct((T, D), F32),
        scratch_shapes=[pltpu.VMEM((TM_MOE, D), F32)],
        compiler_params=_cp(2),
        name="moe",
    )(x, mod, h2, comb, w1, w3, w2, g_final)


def _rope_tables():
    pos = jnp.arange(DEC_SEQ)
    r = (pos // GRID_W).astype(F32)
    col = (pos % GRID_W).astype(F32)
    inv = ROPE_BASE ** (-jnp.arange(AX_FREQS, dtype=F32) / AX_FREQS)
    ang = jnp.stack([r[:, None] * inv, col[:, None] * inv], axis=1)
    cos = jnp.cos(ang)[:, :, None, :]
    sin = jnp.sin(ang)[:, :, None, :]
    c32 = jnp.broadcast_to(cos, (DEC_SEQ, 2, 2, AX_FREQS)).reshape(DEC_SEQ, ROPE)
    s32 = jnp.concatenate([-sin, sin], axis=2).reshape(DEC_SEQ, ROPE)
    pad = HEAD_PAD - NOPE - ROPE
    c = jnp.concatenate([jnp.ones((DEC_SEQ, NOPE), F32), c32, jnp.zeros((DEC_SEQ, pad), F32)], axis=1)
    s = jnp.concatenate([jnp.zeros((DEC_SEQ, NOPE), F32), s32, jnp.zeros((DEC_SEQ, pad), F32)], axis=1)
    c_id = jnp.concatenate([jnp.ones((TM, NOPE + ROPE), F32), jnp.zeros((TM, pad), F32)], axis=1)
    s_id = jnp.zeros((TM, HEAD_PAD), F32)
    return jnp.concatenate([c, c_id], axis=0), jnp.concatenate([s, s_id], axis=0)


def _swap_halves(w):
    lead = w.shape[:-1]
    return w.reshape(lead + (2, 2, AX_FREQS))[..., ::-1, :].reshape(lead + (ROPE,))


def _even_weights(w_in, w_q_up, w_kv_up):
    k_in = w_in.shape[0]
    base = 3 * D_CONV + Q_RANK + KV_RANK
    w_kr = w_in[:, base:base + ROPE]
    z = lambda n: jnp.zeros((k_in, n), F32)
    win_ext = jnp.concatenate([
        w_in[:, :base],
        w_kr, z(HEAD_PAD - ROPE),
        z(NOPE), w_kr, z(HEAD_PAD - NOPE - ROPE),
        z(NOPE), _swap_halves(w_kr), z(HEAD_PAD - NOPE - ROPE),
    ], axis=1).astype(BF)
    wq = w_q_up.reshape(Q_RANK, HEADS, QK_DIM)
    zq = lambda n: jnp.zeros((Q_RANK, HEADS, n), F32)
    wq1 = jnp.concatenate([wq, zq(HEAD_PAD - QK_DIM)], axis=2).reshape(Q_RANK, HEADS * HEAD_PAD).astype(BF)
    wq2 = jnp.concatenate([zq(NOPE), _swap_halves(wq[:, :, NOPE:]), zq(HEAD_PAD - QK_DIM)],
                          axis=2).reshape(Q_RANK, HEADS * HEAD_PAD).astype(BF)
    wkv = w_kv_up.reshape(KV_RANK, HEADS, NOPE + V_DIM)
    wk = jnp.concatenate([wkv[:, :, :NOPE], jnp.zeros((KV_RANK, HEADS, HEAD_PAD - NOPE), F32)],
                         axis=2).reshape(KV_RANK, HEADS * HEAD_PAD).astype(BF)
    wv = wkv[:, :, NOPE:].reshape(KV_RANK, HEADS * V_DIM).astype(BF)
    return win_ext, wq1, wq2, wk, wv


def kernel(x_prompt, x_sample, cache_ckv, cache_krope, c, c_ctx, w_ada, b_ada, g_mix, g_ffn, g_final,
           ev_w_in, conv_w, q_norm_g, w_q_up, kv_norm_g, w_kv_up, ev_w_out,
           gm_w_in, gm_v_g, gm_w_s, gm_b_s, gm_w_out, router_w, router_b, moe_w1, moe_w3, moe_w2):
    n_even = ev_w_in.shape[0]
    x = jnp.concatenate([x_prompt.reshape(NP_TOK, D), x_sample.reshape(NS_TOK, D)], axis=0)
    cc = jnp.concatenate([c, c_ctx[None, :], jnp.zeros((MOD_ROWS - DEC_BATCH - 1, D), F32)], axis=0)
    mod = _ada(cc, w_ada, b_ada).reshape(DEPTH * MOD_ROWS, 1, 6 * D)

    rope_c, rope_s = _rope_tables()
    rwt = router_w.T
    rb = router_b.reshape(N_EXPERTS, 1)
    ev = [_even_weights(ev_w_in[i], w_q_up[i], w_kv_up[i]) for i in range(n_even)]
    kr_ctx = jnp.pad(cache_krope, ((0, 0), (0, 0), (0, 0), (NOPE, HEAD_PAD - NOPE - ROPE)))
    kr_ctx = kr_ctx.transpose(1, 0, 2, 3).reshape(n_even, DEC_BATCH * PAST, HEAD_PAD)
    ckv_ctx = cache_ckv.transpose(1, 0, 2, 3).reshape(n_even, DEC_BATCH * PAST, KV_RANK)
    kc, vc = _ctx_kv(ckv_ctx, kr_ctx, jnp.stack([e[3] for e in ev]), jnp.stack([e[4] for e in ev]))

    ckv_states, kr_states = [], []
    for l in range(DEPTH):
        i = l // 2
        if l % 2 == 0:
            win_ext, wq1, wq2, wk, wv = ev[i]
            bg, cv, q, k, v, ckv, kr = _even_in(
                x, mod, l, g_mix[l][None, :], win_ext, q_norm_g[i][None, :], wq1, wq2,
                kv_norm_g[i][None, :], wk, wv, rope_c, rope_s)
            attn = jnp.concatenate([_attn_prompt(q, k, v), _attn_sample(q, k, v, kc, vc, i)], axis=0)
            x, h2, comb = _even_out(x, mod, l, bg, cv, attn, conv_w[i], ev_w_out[i].astype(BF),
                                    g_ffn[l][None, :], rwt, rb)
            ckv_states.append(ckv[:NP_TOK].reshape(BATCH, SEQ, KV_RANK))
            kr_states.append(kr[:NP_TOK, :ROPE].reshape(BATCH, SEQ, ROPE))
        else:
            x, h2, comb = _odd(x, mod, l, g_mix[l][None, :], gm_w_in[i].astype(BF), gm_v_g[i][None, :],
                               gm_w_s[i].astype(BF), gm_b_s[i].T, gm_w_out[i].astype(BF),
                               g_ffn[l][None, :], rwt, rb)
        x = _moe(x, mod, l, h2, comb, moe_w1[l].astype(BF), moe_w3[l].astype(BF), moe_w2[l].astype(BF),
                 g_final[None, :], l == DEPTH - 1)

    y_prompt = x[:NP_TOK].reshape(BATCH, SEQ, D)
    y_sample = x[NP_TOK:].reshape(DEC_BATCH, DEC_SEQ, D)
    return (y_prompt, y_sample, jnp.stack(ckv_states, axis=1), jnp.stack(kr_states, axis=1))
```

```python
import functools
import math

import jax
import jax.numpy as jnp
from jax import lax
from jax.experimental import pallas as pl
from jax.experimental.pallas import tpu as pltpu

D = 1024
BATCH, SEQ = 32, 256
DEC_BATCH, DEC_SEQ = 8, 2048
PAST = 256
DEPTH = 4
GRID_W = 64
D_CONV = 512
HEADS = 8
NOPE, ROPE, V_DIM = 64, 32, 64
QK_DIM = NOPE + ROPE
Q_RANK, KV_RANK = 384, 256
AX_FREQS = ROPE // 4
ROPE_BASE = 10000.0
CHUNK = 128
GM_GROUPS = 8
N_EXPERTS, N_GROUPS, EPG = 16, 4, 4
D_EXPERT = 256
EPS = 1e-6

NP_TOK = BATCH * SEQ
NS_TOK = DEC_BATCH * DEC_SEQ
T = NP_TOK + NS_TOK
MOD_ROWS = 16
HEAD_PAD = 128
HALO = 16

TM = 512
TMM = 512
MAX_STEPS = T // TMM + N_GROUPS
SORTED_ROWS = (MAX_STEPS + 1) * TMM
LANES = 128
Y_SUB = D // LANES
X_SUB = 2 * Y_SUB
ISSUE_UNROLL = 8
TQ = 256
VMEM_LIMIT = 56 * 1024 * 1024

BF = jnp.bfloat16
F32 = jnp.float32


def _cp(n_axes):
    return pltpu.CompilerParams(dimension_semantics=("arbitrary",) * n_axes,
                                vmem_limit_bytes=VMEM_LIMIT)


def _mod_row(t, tm):
    n_prompt_tiles = NP_TOK // tm
    per_seq = DEC_SEQ // tm
    return jnp.where(t < n_prompt_tiles, DEC_BATCH, (t - n_prompt_tiles) // per_seq)


def _rms(x, g):
    return x * lax.rsqrt(jnp.mean(x * x, axis=-1, keepdims=True) + EPS) * g


def _dot(a, b):
    return jnp.dot(a, b, preferred_element_type=F32)


def _dot_nt(a, b, precision=None):
    return lax.dot_general(a, b, (((1,), (1,)), ((), ())), precision=precision,
                           preferred_element_type=F32)


ADA_TN = 1536


def _ada_kernel(cc_ref, w_ref, b_ref, o_ref):
    cc = cc_ref[...]
    s = (cc / (1.0 + jnp.exp(-cc))).astype(BF)
    o_ref[0] = _dot(s, w_ref[0].astype(BF)) + b_ref[0]


def _ada(cc, w_ada, b_ada):
    n = 6 * D
    return pl.pallas_call(
        _ada_kernel,
        grid=(DEPTH, n // ADA_TN),
        in_specs=[
            pl.BlockSpec((MOD_ROWS, D), lambda l, j: (0, 0)),
            pl.BlockSpec((1, D, ADA_TN), lambda l, j: (l, 0, j)),
            pl.BlockSpec((1, 1, ADA_TN), lambda l, j: (l, 0, j)),
        ],
        out_specs=pl.BlockSpec((1, MOD_ROWS, ADA_TN), lambda l, j: (l, 0, j)),
        out_shape=jax.ShapeDtypeStruct((DEPTH, MOD_ROWS, n), F32),
        compiler_params=_cp(2),
        name="ada",
    )(cc, w_ada, b_ada.reshape(DEPTH, 1, n))


def _route(logits_t, rb):
    sc = 1.0 / (1.0 + jnp.exp(-logits_t))
    sel = sc + rb
    rows = [sel[e:e + 1, :] for e in range(N_EXPERTS)]
    srows = [sc[e:e + 1, :] for e in range(N_EXPERTS)]

    def top2sum(a, b, c, d):
        hi1, lo1 = jnp.maximum(a, b), jnp.minimum(a, b)
        hi2, lo2 = jnp.maximum(c, d), jnp.minimum(c, d)
        return jnp.maximum(hi1, hi2) + jnp.maximum(jnp.minimum(hi1, hi2), jnp.maximum(lo1, lo2))

    gs = [top2sum(*rows[EPG * g:EPG * (g + 1)]) for g in range(N_GROUPS)]
    best = gs[0]
    gidx = jnp.zeros_like(best, dtype=jnp.int32)
    for g in range(1, N_GROUPS):
        upd = gs[g] > best
        best = jnp.where(upd, gs[g], best)
        gidx = jnp.where(upd, g, gidx)

    picked = []
    for g in range(N_GROUPS):
        grp = rows[EPG * g:EPG * (g + 1)]
        in_g = gidx == g
        for j in range(EPG):
            rank = jnp.zeros_like(gidx)
            for k in range(EPG):
                if k == j:
                    continue
                ahead = grp[k] > grp[j]
                if k < j:
                    ahead = ahead | (grp[k] == grp[j])
                rank = rank + ahead.astype(jnp.int32)
            picked.append(in_g & (rank < 2))
    w = [jnp.where(picked[e], srows[e], 0.0) for e in range(N_EXPERTS)]
    wsum = w[0]
    for e in range(1, N_EXPERTS):
        wsum = wsum + w[e]
    inv = 1.0 / wsum
    wg = []
    for j in range(EPG):
        acc = w[j]
        for g in range(1, N_GROUPS):
            acc = acc + w[EPG * g + j]
        wg.append(acc * inv)
    return wg, gidx


def _ffn_prep(x_new, mod_ref, gf_ref, rwt_ref, rb_ref, h2c_ref, gidx_ref, grank_ref, cnt_ref,
              ct_scr, run_scr):
    t = pl.program_id(0)
    tm = x_new.shape[0]
    shift = mod_ref[0, :, 3 * D:4 * D]
    scale = mod_ref[0, :, 4 * D:5 * D]
    h2 = _rms(x_new, gf_ref[...]) * (1.0 + scale) + shift
    logits_t = _dot_nt(rwt_ref[...], h2, precision=lax.Precision.HIGHEST)
    wg, gidx = _route(logits_t, rb_ref[...])
    ct_scr[...] = jnp.zeros_like(ct_scr)
    for j in range(EPG):
        ct_scr[j:j + 1, :] = wg[j]
    cw = ct_scr[...].T

    for k in range(Y_SUB):
        h2c_ref[pl.ds(k, tm, stride=X_SUB), :] = h2[:, k * LANES:(k + 1) * LANES]
    h2c_ref[pl.ds(Y_SUB, tm, stride=X_SUB), :] = cw
    for k in range(Y_SUB + 1, X_SUB):
        h2c_ref[pl.ds(k, tm, stride=X_SUB), :] = jnp.zeros((tm, LANES), F32)

    @pl.when(t == 0)
    def _():
        run_scr[...] = jnp.zeros_like(run_scr)

    onehot = (lax.broadcasted_iota(jnp.int32, (8, tm), 0) == gidx).astype(F32)
    earlier = (lax.broadcasted_iota(jnp.int32, (tm, tm), 0)
               < lax.broadcasted_iota(jnp.int32, (tm, tm), 1)).astype(BF)
    rank = _dot(onehot.astype(BF), earlier)
    run = run_scr[:, 0:1]
    grank = jnp.sum(onehot * (rank + run), axis=0, keepdims=True)
    gidx_ref[0] = gidx
    grank_ref[0] = grank.astype(jnp.int32)
    run_scr[...] = run_scr[...] + jnp.sum(onehot, axis=1, keepdims=True)
    cnt_ref[...] = run_scr[...]


def _prep_out_specs():
    tok = lambda w: pl.BlockSpec((TM, w), lambda t: (t, 0))
    lanes = pl.BlockSpec((1, 1, TM), lambda t: (t, 0, 0))
    rows = pl.BlockSpec((TM * X_SUB, LANES), lambda t: (t, 0))
    return [tok(D), rows, lanes, lanes, pl.BlockSpec((8, 128), lambda t: (0, 0))]


def _prep_out_shapes():
    nt = T // TM
    return [jax.ShapeDtypeStruct((T, D), F32), jax.ShapeDtypeStruct((T * X_SUB, LANES), F32),
            jax.ShapeDtypeStruct((nt, 1, TM), jnp.int32), jax.ShapeDtypeStruct((nt, 1, TM), jnp.int32),
            jax.ShapeDtypeStruct((8, 128), F32)]


EV_EXT = 3 * D_CONV + Q_RANK + KV_RANK + 3 * HEAD_PAD


def _even_in_kernel(x_ref, mod_ref, g_ref, win_ref, qg_ref, wq1_ref, wq2_ref, kg_ref, wk_ref, wv_ref,
                    rc_ref, rs_ref,
                    bg_ref, cv_ref, q_ref, k_ref, v_ref, ckv_ref, kr_ref):
    x = x_ref[...]
    shift = mod_ref[0, :, 0:D]
    scale = mod_ref[0, :, D:2 * D]
    h = (_rms(x, g_ref[...]) * (1.0 + scale) + shift).astype(BF)
    proj = _dot(h, win_ref[...])
    o = 0
    b_g = proj[:, o:o + D_CONV]; o += D_CONV
    c_g = proj[:, o:o + D_CONV]; o += D_CONV
    v_in = proj[:, o:o + D_CONV]; o += D_CONV
    q_a = proj[:, o:o + Q_RANK]; o += Q_RANK
    kv_a = proj[:, o:o + KV_RANK]; o += KV_RANK
    kr_raw = proj[:, o:o + HEAD_PAD]; o += HEAD_PAD
    kr_cat = proj[:, o:o + HEAD_PAD]; o += HEAD_PAD
    kr_sw = proj[:, o:o + HEAD_PAD]

    bg_ref[...] = b_g.astype(BF)
    cv_ref[...] = (c_g * v_in).astype(BF)
    kr_ref[...] = kr_raw

    rc = rc_ref[...]
    rs = rs_ref[...]
    qn = _rms(q_a, qg_ref[...]).astype(BF)
    q1 = _dot(qn, wq1_ref[...])
    q2 = _dot(qn, wq2_ref[...])
    qscale = QK_DIM ** -0.5
    for hd in range(HEADS):
        sl = slice(hd * HEAD_PAD, (hd + 1) * HEAD_PAD)
        q_ref[:, sl] = ((q1[:, sl] * rc + q2[:, sl] * rs) * qscale).astype(BF)

    ckv = _rms(kv_a, kg_ref[...])
    ckv_ref[...] = ckv
    ckv_b = ckv.astype(BF)
    kk = _dot(ckv_b, wk_ref[...])
    kr = kr_cat * rc + kr_sw * rs
    for hd in range(HEADS):
        sl = slice(hd * HEAD_PAD, (hd + 1) * HEAD_PAD)
        k_ref[:, sl] = (kk[:, sl] + kr).astype(BF)
    v_ref[...] = _dot(ckv_b, wv_ref[...]).astype(BF)


def _even_in(x, mod, l, g_mix, win, qg, wq1, wq2, kg, wk, wv, rope_c, rope_s):
    nt = T // TM
    npt = NP_TOK // TM
    per_seq = DEC_SEQ // TM
    ident_blk = DEC_SEQ // TM

    def rope_idx(t):
        return (jnp.where(t < npt, ident_blk, (t - npt) % per_seq), 0)

    full = lambda shape: pl.BlockSpec(shape, lambda t: (0,) * len(shape))
    tok = lambda w: pl.BlockSpec((TM, w), lambda t: (t, 0))
    return pl.pallas_call(
        _even_in_kernel,
        grid=(nt,),
        in_specs=[
            tok(D),
            pl.BlockSpec((1, 1, 6 * D), lambda t: (l * MOD_ROWS + _mod_row(t, TM), 0, 0)),
            full((1, D)), full((D, EV_EXT)), full((1, Q_RANK)),
            full((Q_RANK, HEADS * HEAD_PAD)), full((Q_RANK, HEADS * HEAD_PAD)),
            full((1, KV_RANK)), full((KV_RANK, HEADS * HEAD_PAD)), full((KV_RANK, HEADS * V_DIM)),
            pl.BlockSpec((TM, HEAD_PAD), rope_idx), pl.BlockSpec((TM, HEAD_PAD), rope_idx),
        ],
        out_specs=[tok(D_CONV), tok(D_CONV), tok(HEADS * HEAD_PAD), tok(HEADS * HEAD_PAD),
                   tok(HEADS * V_DIM), tok(KV_RANK), tok(HEAD_PAD)],
        out_shape=[
            jax.ShapeDtypeStruct((T, D_CONV), BF), jax.ShapeDtypeStruct((T, D_CONV), BF),
            jax.ShapeDtypeStruct((T, HEADS * HEAD_PAD), BF), jax.ShapeDtypeStruct((T, HEADS * HEAD_PAD), BF),
            jax.ShapeDtypeStruct((T, HEADS * V_DIM), BF),
            jax.ShapeDtypeStruct((T, KV_RANK), F32), jax.ShapeDtypeStruct((T, HEAD_PAD), F32),
        ],
        compiler_params=_cp(1),
        name="even_in",
    )(x, mod, g_mix, win, qg, wq1, wq2, kg, wk, wv, rope_c, rope_s)


CTX_TM = 512


def _ctx_kv_kernel(ckv_ref, kr_ref, wk_ref, wv_ref, k_ref, v_ref):
    ckv_b = ckv_ref[0].astype(BF)
    kk = _dot(ckv_b, wk_ref[0])
    kr = kr_ref[0]
    for hd in range(HEADS):
        sl = slice(hd * HEAD_PAD, (hd + 1) * HEAD_PAD)
        k_ref[0, :, sl] = (kk[:, sl] + kr).astype(BF)
    v_ref[0] = _dot(ckv_b, wv_ref[0]).astype(BF)


def _ctx_kv(ckv_all, kr_all, wk_all, wv_all):
    n_even = ckv_all.shape[0]
    rows = DEC_BATCH * PAST
    return pl.pallas_call(
        _ctx_kv_kernel,
        grid=(n_even, rows // CTX_TM),
        in_specs=[
            pl.BlockSpec((1, CTX_TM, KV_RANK), lambda i, t: (i, t, 0)),
            pl.BlockSpec((1, CTX_TM, HEAD_PAD), lambda i, t: (i, t, 0)),
            pl.BlockSpec((1, KV_RANK, HEADS * HEAD_PAD), lambda i, t: (i, 0, 0)),
            pl.BlockSpec((1, KV_RANK, HEADS * V_DIM), lambda i, t: (i, 0, 0)),
        ],
        out_specs=[
            pl.BlockSpec((1, CTX_TM, HEADS * HEAD_PAD), lambda i, t: (i, t, 0)),
            pl.BlockSpec((1, CTX_TM, HEADS * V_DIM), lambda i, t: (i, t, 0)),
        ],
        out_shape=[jax.ShapeDtypeStruct((n_even, rows, HEADS * HEAD_PAD), BF),
                   jax.ShapeDtypeStruct((n_even, rows, HEADS * V_DIM), BF)],
        compiler_params=_cp(2),
        name="ctx_kv",
    )(ckv_all, kr_all, wk_all, wv_all)


def _attn_body(n_pairs, has_ctx, q_ref, k_ref, v_ref, *rest):
    if has_ctx:
        kc_ref, vc_ref, o_ref = rest
    else:
        (o_ref,) = rest
    lane = lax.broadcasted_iota(jnp.int32, (1, 2 * V_DIM), 1)
    for pr in range(n_pairs):
        vsl = slice(pr * 2 * V_DIM, (pr + 1) * 2 * V_DIM)
        v = v_ref[:, vsl]
        outs = []
        for sub in range(2):
            hsl = slice((2 * pr + sub) * HEAD_PAD, (2 * pr + sub + 1) * HEAD_PAD)
            q = q_ref[:, hsl]
            s1 = _dot_nt(q, k_ref[:, hsl])
            m = jnp.max(s1, axis=-1, keepdims=True)
            if has_ctx:
                s2 = _dot_nt(q, kc_ref[0, :, hsl])
                m = jnp.maximum(m, jnp.max(s2, axis=-1, keepdims=True))
            p1 = jnp.exp(s1 - m)
            den = jnp.sum(p1, axis=-1, keepdims=True)
            o = _dot(p1.astype(BF), v)
            if has_ctx:
                p2 = jnp.exp(s2 - m)
                den = den + jnp.sum(p2, axis=-1, keepdims=True)
                o = o + _dot(p2.astype(BF), vc_ref[0, :, vsl])
            outs.append(o * (1.0 / den))
        o_ref[:, vsl] = jnp.where(lane < V_DIM, outs[0], outs[1]).astype(BF)


def _attn_prompt(q, k, v):
    return pl.pallas_call(
        functools.partial(_attn_body, HEADS // 2, False),
        grid=(BATCH,),
        in_specs=[
            pl.BlockSpec((SEQ, HEADS * HEAD_PAD), lambda b: (b, 0)),
            pl.BlockSpec((SEQ, HEADS * HEAD_PAD), lambda b: (b, 0)),
            pl.BlockSpec((SEQ, HEADS * V_DIM), lambda b: (b, 0)),
        ],
        out_specs=pl.BlockSpec((SEQ, HEADS * V_DIM), lambda b: (b, 0)),
        out_shape=jax.ShapeDtypeStruct((NP_TOK, HEADS * V_DIM), BF),
        compiler_params=_cp(1),
        name="attn_prompt",
    )(q, k, v)


def _attn_sample(q, k, v, kc, vc, i):
    s_blk0 = NP_TOK // DEC_SEQ
    q_blk0 = NP_TOK // TQ
    nq = DEC_SEQ // TQ
    return pl.pallas_call(
        functools.partial(_attn_body, 1, True),
        grid=(DEC_BATCH, HEADS // 2, nq),
        in_specs=[
            pl.BlockSpec((TQ, 2 * HEAD_PAD), lambda b, hp, j: (q_blk0 + b * nq + j, hp)),
            pl.BlockSpec((DEC_SEQ, 2 * HEAD_PAD), lambda b, hp, j: (s_blk0 + b, hp)),
            pl.BlockSpec((DEC_SEQ, 2 * V_DIM), lambda b, hp, j: (s_blk0 + b, hp)),
            pl.BlockSpec((1, PAST, 2 * HEAD_PAD), lambda b, hp, j: (i, b, hp)),
            pl.BlockSpec((1, PAST, 2 * V_DIM), lambda b, hp, j: (i, b, hp)),
        ],
        out_specs=pl.BlockSpec((TQ, 2 * V_DIM), lambda b, hp, j: (b * nq + j, hp)),
        out_shape=jax.ShapeDtypeStruct((NS_TOK, HEADS * V_DIM), BF),
        compiler_params=_cp(3),
        name="attn_sample",
    )(q, k, v, kc, vc)


def _even_out_kernel(x_ref, mod_ref, bg_ref, cv_ref, cvp_ref, cvn_ref, at_ref, cw_ref, wo_ref,
                     gf_ref, rwt_ref, rb_ref,
                     xo_ref, h2c_ref, gidx_ref, grank_ref, cnt_ref, ct_scr, run_scr):
    t = pl.program_id(0)
    npt = NP_TOK // TM
    per_seq = DEC_SEQ // TM
    cv = cv_ref[...].astype(F32)
    r = lax.broadcasted_iota(jnp.int32, (TM, 1), 0)
    is_prompt = t < npt
    tile_in_seq = (t - npt) % per_seq
    first_row = jnp.where(is_prompt, 0, jnp.where(tile_in_seq == 0, 0, -1))
    last_row = jnp.where(is_prompt, SEQ - 1, jnp.where(tile_in_seq == per_seq - 1, TM - 1, -1))
    period_mask = jnp.where(is_prompt, SEQ - 1, TM - 1)
    first = (r & period_mask) == first_row
    last = (r & period_mask) == last_row
    prev_row = cvp_ref[HALO - 1:HALO, :].astype(F32)
    next_row = cvn_ref[0:1, :].astype(F32)
    prev = jnp.where(r == 0, prev_row, pltpu.roll(cv, 1, 0))
    prev = jnp.where(first, 0.0, prev)
    nxt = jnp.where(r == TM - 1, next_row, pltpu.roll(cv, TM - 1, 0))
    nxt = jnp.where(last, 0.0, nxt)
    cw = cw_ref[...]
    conv = prev * cw[0:1, :] + cv * cw[1:2, :] + nxt * cw[2:3, :]
    yc = (bg_ref[...].astype(F32) * conv).astype(BF)
    out = _dot(yc, wo_ref[0:D_CONV, :]) + _dot(at_ref[...], wo_ref[D_CONV:2 * D_CONV, :])
    gate = mod_ref[0, :, 2 * D:3 * D]
    x_new = x_ref[...] + gate * out
    xo_ref[...] = x_new
    _ffn_prep(x_new, mod_ref, gf_ref, rwt_ref, rb_ref, h2c_ref, gidx_ref, grank_ref, cnt_ref,
              ct_scr, run_scr)


def _even_out(x, mod, l, bg, cv, attn, conv_w, w_out, g_ffn, rwt, rb):
    nt = T // TM
    hb = TM // HALO
    nhb = T // HALO
    full = lambda shape: pl.BlockSpec(shape, lambda t: (0,) * len(shape))
    tok = lambda w: pl.BlockSpec((TM, w), lambda t: (t, 0))
    return pl.pallas_call(
        _even_out_kernel,
        grid=(nt,),
        in_specs=[
            tok(D),
            pl.BlockSpec((1, 1, 6 * D), lambda t: (l * MOD_ROWS + _mod_row(t, TM), 0, 0)),
            tok(D_CONV), tok(D_CONV),
            pl.BlockSpec((HALO, D_CONV), lambda t: (jnp.maximum(t * hb - 1, 0), 0)),
            pl.BlockSpec((HALO, D_CONV), lambda t: (jnp.minimum((t + 1) * hb, nhb - 1), 0)),
            tok(HEADS * V_DIM),
            full((3, D_CONV)), full((2 * D_CONV, D)), full((1, D)),
            full((N_EXPERTS, D)), full((N_EXPERTS, 1)),
        ],
        out_specs=_prep_out_specs(),
        out_shape=_prep_out_shapes(),
        scratch_shapes=[pltpu.VMEM((128, TM), F32), pltpu.VMEM((8, 128), F32)],
        compiler_params=_cp(1),
        name="even_out",
    )(x, mod, bg, cv, cv, cv, attn, conv_w, w_out, g_ffn, rwt, rb)


def _odd_kernel(x_ref, mod_ref, g_ref, win_ref, vg_ref, ws_ref, bst_ref, wo_ref,
                gf_ref, rwt_ref, rb_ref,
                xo_ref, h2c_ref, gidx_ref, grank_ref, cnt_ref, gated_scr, ct_scr, run_scr):
    x = x_ref[...]
    shift = mod_ref[0, :, 0:D]
    scale = mod_ref[0, :, D:2 * D]
    h = (_rms(x, g_ref[...]) * (1.0 + scale) + shift).astype(BF)
    zl = _dot(h, win_ref[...])
    z = 0.5 * zl * (1.0 + lax.erf(zl * math.sqrt(0.5)))
    u = z[:, 0:D]
    v = _rms(z[:, D:2 * D], vg_ref[...]).astype(BF)
    n_chunks = TM // CHUNK
    gch = D // GM_GROUPS
    for g in range(GM_GROUPS):
        csl = slice(g * gch, (g + 1) * gch)
        vg = jnp.concatenate([v[n * CHUNK:(n + 1) * CHUNK, csl] for n in range(n_chunks)], axis=1)
        sg = _dot(ws_ref[g], vg) + bst_ref[:, g:g + 1]
        for n in range(n_chunks):
            rsl = slice(n * CHUNK, (n + 1) * CHUNK)
            gated_scr[rsl, csl] = (u[rsl, csl] * sg[:, n * gch:(n + 1) * gch]).astype(BF)
    out = _dot(gated_scr[...], wo_ref[...])
    gate = mod_ref[0, :, 2 * D:3 * D]
    x_new = x + gate * out
    xo_ref[...] = x_new
    _ffn_prep(x_new, mod_ref, gf_ref, rwt_ref, rb_ref, h2c_ref, gidx_ref, grank_ref, cnt_ref,
              ct_scr, run_scr)


def _odd(x, mod, l, g_mix, win, v_g, w_s, b_st, w_out, g_ffn, rwt, rb):
    nt = T // TM
    full = lambda shape: pl.BlockSpec(shape, lambda t: (0,) * len(shape))
    tok = lambda w: pl.BlockSpec((TM, w), lambda t: (t, 0))
    return pl.pallas_call(
        _odd_kernel,
        grid=(nt,),
        in_specs=[
            tok(D),
            pl.BlockSpec((1, 1, 6 * D), lambda t: (l * MOD_ROWS + _mod_row(t, TM), 0, 0)),
            full((1, D)), full((D, 2 * D)), full((1, D)),
            full((GM_GROUPS, CHUNK, CHUNK)), full((CHUNK, GM_GROUPS)), full((D, D)),
            full((1, D)), full((N_EXPERTS, D)), full((N_EXPERTS, 1)),
        ],
        out_specs=_prep_out_specs(),
        out_shape=_prep_out_shapes(),
        scratch_shapes=[pltpu.VMEM((TM, D), BF), pltpu.VMEM((128, TM), F32), pltpu.VMEM((8, 128), F32)],
        compiler_params=_cp(1),
        name="odd",
    )(x, mod, g_mix, win, v_g, w_s, b_st, w_out, g_ffn, rwt, rb)


def _routing_plan(gidx, grank, cnt):
    counts = cnt[:N_GROUPS, 0].astype(jnp.int32)
    padded = (counts + (TMM - 1)) // TMM * TMM
    ends = jnp.cumsum(padded)
    base = ends - padded
    g = gidx.reshape(T)
    slot = grank.reshape(T)
    for k in range(N_GROUPS):
        slot = slot + jnp.where(g == k, base[k], 0)
    n_steps = ends[N_GROUPS - 1] // TMM
    starts = jnp.minimum(jnp.arange(MAX_STEPS, dtype=jnp.int32) * TMM, ends[N_GROUPS - 1] - TMM)
    group_of_step = jnp.zeros((MAX_STEPS,), jnp.int32)
    for k in range(N_GROUPS - 1):
        group_of_step = group_of_step + (starts >= ends[k]).astype(jnp.int32)
    sched = jnp.concatenate([group_of_step, n_steps[None]]).astype(jnp.int32)
    fill_plan = jnp.concatenate([base + counts, n_steps[None]]).astype(jnp.int32)
    return slot.astype(jnp.int32), sched, fill_plan


def _row_tile(ref, row, sub):
    return ref.at[pl.ds(pl.multiple_of(row * sub, sub), sub), :]


def _row_copies(n_rows, make_copy):
    def body(r, carry):
        make_copy(r).start()
        return carry
    lax.fori_loop(0, n_rows, body, 0, unroll=ISSUE_UNROLL)


def _dispatch_kernel(pad_ref, slot_ref, h_ref, o_hbm, zero_scr, zsem, sem):
    t = pl.program_id(0)

    @pl.when(t == 0)
    def _():
        zero_scr[...] = jnp.zeros_like(zero_scr)
        fills = [pltpu.make_async_copy(
            zero_scr,
            o_hbm.at[pl.ds(pl.multiple_of(pad_ref[g] * X_SUB, X_SUB), TMM * X_SUB), :],
            zsem.at[g]) for g in range(N_GROUPS)]
        for f in fills:
            f.start()
        for f in fills:
            f.wait()
        for s in range(T // TMM, MAX_STEPS + 1):
            @pl.when(s >= pad_ref[N_GROUPS])
            def _():
                tail = pltpu.make_async_copy(
                    zero_scr, o_hbm.at[pl.ds(s * TMM * X_SUB, TMM * X_SUB), :], zsem.at[0])
                tail.start()
                tail.wait()

    _row_copies(TM, lambda r: pltpu.make_async_copy(
        _row_tile(h_ref, r, X_SUB), _row_tile(o_hbm, slot_ref[r], X_SUB), sem))
    pltpu.make_async_copy(h_ref, o_hbm.at[pl.ds(0, TM * X_SUB), :], sem).wait()


def _dispatch(pad_start, slot, h2c):
    nt = T // TM
    return pl.pallas_call(
        _dispatch_kernel,
        grid_spec=pltpu.PrefetchScalarGridSpec(
            num_scalar_prefetch=1,
            grid=(nt,),
            in_specs=[
                pl.BlockSpec((TM,), lambda t, pad: (t,), memory_space=pltpu.SMEM),
                pl.BlockSpec((TM * X_SUB, LANES), lambda t, pad: (t, 0)),
            ],
            out_specs=pl.BlockSpec(memory_space=pl.ANY),
            scratch_shapes=[pltpu.VMEM((TMM * X_SUB, LANES), F32),
                            pltpu.SemaphoreType.DMA((N_GROUPS,)), pltpu.SemaphoreType.DMA(())],
        ),
        out_shape=jax.ShapeDtypeStruct((SORTED_ROWS * X_SUB, LANES), F32),
        compiler_params=_cp(1),
        name="dispatch",
    )(pad_start, slot, h2c)


def _experts_kernel(sched_ref, xs_ref, w1_ref, w3_ref, w2_ref, o_ref):
    s = pl.program_id(0)

    @pl.when(s < sched_ref[MAX_STEPS])
    def _():
        h = jnp.concatenate([xs_ref[pl.ds(k, TMM, stride=X_SUB), :] for k in range(Y_SUB)],
                            axis=1).astype(BF)
        cw = xs_ref[pl.ds(Y_SUB, TMM, stride=X_SUB), :]
        acc = None
        for j in range(EPG):
            a = _dot(h, w1_ref[j])
            b = _dot(h, w3_ref[j])
            hid = (a * (1.0 / (1.0 + jnp.exp(-a))) * b * cw[:, j:j + 1]).astype(BF)
            y = _dot(hid, w2_ref[j])
            acc = y if acc is None else acc + y
        for k in range(D // LANES):
            o_ref[pl.ds(k, TMM, stride=Y_SUB), :] = acc[:, k * LANES:(k + 1) * LANES]

    @pl.when(s >= sched_ref[MAX_STEPS])
    def _():
        o_ref[...] = jnp.zeros_like(o_ref)


def _experts(sched, xs, w1, w3, w2):
    row_blk = lambda s, sched: (jnp.minimum(s, sched[MAX_STEPS] - 1), 0)
    grp_blk = lambda s, sched: (sched[s], 0, 0)
    return pl.pallas_call(
        _experts_kernel,
        grid_spec=pltpu.PrefetchScalarGridSpec(
            num_scalar_prefetch=1,
            grid=(MAX_STEPS,),
            in_specs=[
                pl.BlockSpec((TMM * X_SUB, LANES), row_blk),
                pl.BlockSpec((EPG, D, D_EXPERT), grp_blk),
                pl.BlockSpec((EPG, D, D_EXPERT), grp_blk),
                pl.BlockSpec((EPG, D_EXPERT, D), grp_blk),
            ],
            out_specs=pl.BlockSpec((TMM * Y_SUB, LANES), lambda s, sched: (s, 0)),
        ),
        out_shape=jax.ShapeDtypeStruct((MAX_STEPS * TMM * Y_SUB, LANES), F32),
        compiler_params=_cp(1),
        name="experts",
    )(sched, xs, w1, w3, w2)


def _combine_kernel(final, slot_ref, slot_next_ref, x_ref, mod_ref, gfin_ref, ys_hbm, o_ref, buf, sem):
    t = pl.program_id(0)
    nt = pl.num_programs(0)
    cur = t % 2

    def gather(idx_ref, b):
        _row_copies(TM, lambda r: pltpu.make_async_copy(
            _row_tile(ys_hbm, idx_ref[r], Y_SUB), _row_tile(buf.at[b], r, Y_SUB), sem.at[b]))

    @pl.when(t == 0)
    def _():
        gather(slot_ref, 0)

    @pl.when(t + 1 < nt)
    def _():
        gather(slot_next_ref, 1 - cur)

    pltpu.make_async_copy(ys_hbm.at[pl.ds(0, TM * Y_SUB), :], buf.at[cur], sem.at[cur]).wait()
    gate = mod_ref[0, :, 5 * D:6 * D]
    y = jnp.concatenate([buf[cur, pl.ds(k, TM, stride=Y_SUB), :] for k in range(Y_SUB)], axis=1)
    x_new = x_ref[...] + gate * y
    if final:
        x_new = _rms(x_new, gfin_ref[...])
    o_ref[...] = x_new


def _combine(x, mod, l, slot, ys, g_final, final):
    nt = T // TM
    return pl.pallas_call(
        functools.partial(_combine_kernel, final),
        grid=(nt,),
        in_specs=[
            pl.BlockSpec((TM,), lambda t: (t,), memory_space=pltpu.SMEM),
            pl.BlockSpec((TM,), lambda t: (jnp.minimum(t + 1, nt - 1),), memory_space=pltpu.SMEM),
            pl.BlockSpec((TM, D), lambda t: (t, 0)),
            pl.BlockSpec((1, 1, 6 * D), lambda t: (l * MOD_ROWS + _mod_row(t, TM), 0, 0)),
            pl.BlockSpec((1, D), lambda t: (0, 0)),
            pl.BlockSpec(memory_space=pl.ANY),
        ],
        out_specs=pl.BlockSpec((TM, D), lambda t: (t, 0)),
        out_shape=jax.ShapeDtypeStruct((T, D), F32),
        scratch_shapes=[pltpu.VMEM((2, TM * Y_SUB, LANES), F32), pltpu.SemaphoreType.DMA((2,))],
        compiler_params=_cp(1),
        name="combine",
    )(slot, slot, x, mod, g_final, ys)


def _moe(x, mod, l, h2c, gidx, grank, cnt, w1, w3, w2, g_final, final):
    slot, sched, pad_start = _routing_plan(gidx, grank, cnt)
    xs = _dispatch(pad_start, slot, h2c)
    ys = _experts(sched, xs, w1, w3, w2)
    return _combine(x, mod, l, slot, ys, g_final, final)


def _rope_tables():
    pos = jnp.arange(DEC_SEQ)
    r = (pos // GRID_W).astype(F32)
    col = (pos % GRID_W).astype(F32)
    inv = ROPE_BASE ** (-jnp.arange(AX_FREQS, dtype=F32) / AX_FREQS)
    ang = jnp.stack([r[:, None] * inv, col[:, None] * inv], axis=1)
    cos = jnp.cos(ang)[:, :, None, :]
    sin = jnp.sin(ang)[:, :, None, :]
    c32 = jnp.broadcast_to(cos, (DEC_SEQ, 2, 2, AX_FREQS)).reshape(DEC_SEQ, ROPE)
    s32 = jnp.concatenate([-sin, sin], axis=2).reshape(DEC_SEQ, ROPE)
    pad = HEAD_PAD - NOPE - ROPE
    c = jnp.concatenate([jnp.ones((DEC_SEQ, NOPE), F32), c32, jnp.zeros((DEC_SEQ, pad), F32)], axis=1)
    s = jnp.concatenate([jnp.zeros((DEC_SEQ, NOPE), F32), s32, jnp.zeros((DEC_SEQ, pad), F32)], axis=1)
    c_id = jnp.concatenate([jnp.ones((TM, NOPE + ROPE), F32), jnp.zeros((TM, pad), F32)], axis=1)
    s_id = jnp.zeros((TM, HEAD_PAD), F32)
    return jnp.concatenate([c, c_id], axis=0), jnp.concatenate([s, s_id], axis=0)


def _swap_halves(w):
    lead = w.shape[:-1]
    return w.reshape(lead + (2, 2, AX_FREQS))[..., ::-1, :].reshape(lead + (ROPE,))


def _even_weights(w_in, w_q_up, w_kv_up):
    k_in = w_in.shape[0]
    base = 3 * D_CONV + Q_RANK + KV_RANK
    w_kr = w_in[:, base:base + ROPE]
    z = lambda n: jnp.zeros((k_in, n), F32)
    win_ext = jnp.concatenate([
        w_in[:, :base],
        w_kr, z(HEAD_PAD - ROPE),
        z(NOPE), w_kr, z(HEAD_PAD - NOPE - ROPE),
        z(NOPE), _swap_halves(w_kr), z(HEAD_PAD - NOPE - ROPE),
    ], axis=1).astype(BF)
    wq = w_q_up.reshape(Q_RANK, HEADS, QK_DIM)
    zq = lambda n: jnp.zeros((Q_RANK, HEADS, n), F32)
    wq1 = jnp.concatenate([wq, zq(HEAD_PAD - QK_DIM)], axis=2).reshape(Q_RANK, HEADS * HEAD_PAD).astype(BF)
    wq2 = jnp.concatenate([zq(NOPE), _swap_halves(wq[:, :, NOPE:]), zq(HEAD_PAD - QK_DIM)],
                          axis=2).reshape(Q_RANK, HEADS * HEAD_PAD).astype(BF)
    wkv = w_kv_up.reshape(KV_RANK, HEADS, NOPE + V_DIM)
    wk = jnp.concatenate([wkv[:, :, :NOPE], jnp.zeros((KV_RANK, HEADS, HEAD_PAD - NOPE), F32)],
                         axis=2).reshape(KV_RANK, HEADS * HEAD_PAD).astype(BF)
    wv = wkv[:, :, NOPE:].reshape(KV_RANK, HEADS * V_DIM).astype(BF)
    return win_ext, wq1, wq2, wk, wv


def kernel(x_prompt, x_sample, cache_ckv, cache_krope, c, c_ctx, w_ada, b_ada, g_mix, g_ffn, g_final,
           ev_w_in, conv_w, q_norm_g, w_q_up, kv_norm_g, w_kv_up, ev_w_out,
           gm_w_in, gm_v_g, gm_w_s, gm_b_s, gm_w_out, router_w, router_b, moe_w1, moe_w3, moe_w2):
    n_even = ev_w_in.shape[0]
    x = jnp.concatenate([x_prompt.reshape(NP_TOK, D), x_sample.reshape(NS_TOK, D)], axis=0)
    cc = jnp.concatenate([c, c_ctx[None, :], jnp.zeros((MOD_ROWS - DEC_BATCH - 1, D), F32)], axis=0)
    mod = _ada(cc, w_ada, b_ada).reshape(DEPTH * MOD_ROWS, 1, 6 * D)

    rope_c, rope_s = _rope_tables()
    rwt = router_w.T
    rb = router_b.reshape(N_EXPERTS, 1)
    ev = [_even_weights(ev_w_in[i], w_q_up[i], w_kv_up[i]) for i in range(n_even)]
    kr_ctx = jnp.pad(cache_krope, ((0, 0), (0, 0), (0, 0), (NOPE, HEAD_PAD - NOPE - ROPE)))
    kr_ctx = kr_ctx.transpose(1, 0, 2, 3).reshape(n_even, DEC_BATCH * PAST, HEAD_PAD)
    ckv_ctx = cache_ckv.transpose(1, 0, 2, 3).reshape(n_even, DEC_BATCH * PAST, KV_RANK)
    kc, vc = _ctx_kv(ckv_ctx, kr_ctx, jnp.stack([e[3] for e in ev]), jnp.stack([e[4] for e in ev]))

    ckv_states, kr_states = [], []
    for l in range(DEPTH):
        i = l // 2
        if l % 2 == 0:
            win_ext, wq1, wq2, wk, wv = ev[i]
            bg, cv, q, k, v, ckv, kr = _even_in(
                x, mod, l, g_mix[l][None, :], win_ext, q_norm_g[i][None, :], wq1, wq2,
                kv_norm_g[i][None, :], wk, wv, rope_c, rope_s)
            attn = jnp.concatenate([_attn_prompt(q, k, v), _attn_sample(q, k, v, kc, vc, i)], axis=0)
            x, h2c, gidx, grank, cnt = _even_out(x, mod, l, bg, cv, attn, conv_w[i], ev_w_out[i].astype(BF),
                                                 g_ffn[l][None, :], rwt, rb)
            ckv_states.append(ckv[:NP_TOK].reshape(BATCH, SEQ, KV_RANK))
            kr_states.append(kr[:NP_TOK, :ROPE].reshape(BATCH, SEQ, ROPE))
        else:
            x, h2c, gidx, grank, cnt = _odd(x, mod, l, g_mix[l][None, :], gm_w_in[i].astype(BF),
                                            gm_v_g[i][None, :], gm_w_s[i].astype(BF), gm_b_s[i].T,
                                            gm_w_out[i].astype(BF), g_ffn[l][None, :], rwt, rb)
        x = _moe(x, mod, l, h2c, gidx, grank, cnt, moe_w1[l].astype(BF), moe_w3[l].astype(BF),
                 moe_w2[l].astype(BF), g_final[None, :], l == DEPTH - 1)

    y_prompt = x[:NP_TOK].reshape(BATCH, SEQ, D)
    y_sample = x[NP_TOK:].reshape(DEC_BATCH, DEC_SEQ, D)
    return (y_prompt, y_sample, jnp.stack(ckv_states, axis=1), jnp.stack(kr_states, axis=1))
```

```python
import functools
import math

import jax
import jax.numpy as jnp
from jax import lax
from jax.experimental import pallas as pl
from jax.experimental.pallas import tpu as pltpu

D = 1024
BATCH, SEQ = 32, 256
DEC_BATCH, DEC_SEQ = 8, 2048
PAST = 256
DEPTH = 4
GRID_W = 64
D_CONV = 512
HEADS = 8
NOPE, ROPE, V_DIM = 64, 32, 64
QK_DIM = NOPE + ROPE
Q_RANK, KV_RANK = 384, 256
AX_FREQS = ROPE // 4
ROPE_BASE = 10000.0
CHUNK = 128
GM_GROUPS = 8
N_EXPERTS, N_GROUPS, EPG = 16, 4, 4
D_EXPERT = 256
EPS = 1e-6

NP_TOK = BATCH * SEQ
NS_TOK = DEC_BATCH * DEC_SEQ
T = NP_TOK + NS_TOK
MOD_ROWS = 16
HEAD_PAD = 128
HALO = 16

TM = 512
TMM = 512
MAX_STEPS = T // TMM + N_GROUPS
SORTED_ROWS = (MAX_STEPS + 1) * TMM
LANES = 128
Y_SUB = D // LANES
X_SUB = 2 * Y_SUB
ISSUE_UNROLL = 8
TQ = 256
VMEM_LIMIT = 56 * 1024 * 1024

BF = jnp.bfloat16
F32 = jnp.float32


def _cp(n_axes):
    return pltpu.CompilerParams(dimension_semantics=("arbitrary",) * n_axes,
                                vmem_limit_bytes=VMEM_LIMIT)


def _mod_row(t, tm):
    n_prompt_tiles = NP_TOK // tm
    per_seq = DEC_SEQ // tm
    return jnp.where(t < n_prompt_tiles, DEC_BATCH, (t - n_prompt_tiles) // per_seq)


def _rms(x, g):
    return x * lax.rsqrt(jnp.mean(x * x, axis=-1, keepdims=True) + EPS) * g


def _dot(a, b):
    return jnp.dot(a, b, preferred_element_type=F32)


def _dot_nt(a, b, precision=None):
    return lax.dot_general(a, b, (((1,), (1,)), ((), ())), precision=precision,
                           preferred_element_type=F32)


ADA_TN = 1536


def _ada_kernel(cc_ref, w_ref, b_ref, o_ref):
    cc = cc_ref[...]
    s = (cc / (1.0 + jnp.exp(-cc))).astype(BF)
    o_ref[0] = _dot(s, w_ref[0].astype(BF)) + b_ref[0]


def _ada(cc, w_ada, b_ada):
    n = 6 * D
    return pl.pallas_call(
        _ada_kernel,
        grid=(DEPTH, n // ADA_TN),
        in_specs=[
            pl.BlockSpec((MOD_ROWS, D), lambda l, j: (0, 0)),
            pl.BlockSpec((1, D, ADA_TN), lambda l, j: (l, 0, j)),
            pl.BlockSpec((1, 1, ADA_TN), lambda l, j: (l, 0, j)),
        ],
        out_specs=pl.BlockSpec((1, MOD_ROWS, ADA_TN), lambda l, j: (l, 0, j)),
        out_shape=jax.ShapeDtypeStruct((DEPTH, MOD_ROWS, n), F32),
        compiler_params=_cp(2),
        name="ada",
    )(cc, w_ada, b_ada.reshape(DEPTH, 1, n))


def _route(logits_t, rb):
    sc = 1.0 / (1.0 + jnp.exp(-logits_t))
    sel = sc + rb
    rows = [sel[e:e + 1, :] for e in range(N_EXPERTS)]
    srows = [sc[e:e + 1, :] for e in range(N_EXPERTS)]

    def top2sum(a, b, c, d):
        hi1, lo1 = jnp.maximum(a, b), jnp.minimum(a, b)
        hi2, lo2 = jnp.maximum(c, d), jnp.minimum(c, d)
        return jnp.maximum(hi1, hi2) + jnp.maximum(jnp.minimum(hi1, hi2), jnp.maximum(lo1, lo2))

    gs = [top2sum(*rows[EPG * g:EPG * (g + 1)]) for g in range(N_GROUPS)]
    best = gs[0]
    gidx = jnp.zeros_like(best, dtype=jnp.int32)
    for g in range(1, N_GROUPS):
        upd = gs[g] > best
        best = jnp.where(upd, gs[g], best)
        gidx = jnp.where(upd, g, gidx)

    picked = []
    for g in range(N_GROUPS):
        grp = rows[EPG * g:EPG * (g + 1)]
        in_g = gidx == g
        for j in range(EPG):
            rank = jnp.zeros_like(gidx)
            for k in range(EPG):
                if k == j:
                    continue
                ahead = grp[k] > grp[j]
                if k < j:
                    ahead = ahead | (grp[k] == grp[j])
                rank = rank + ahead.astype(jnp.int32)
            picked.append(in_g & (rank < 2))
    w = [jnp.where(picked[e], srows[e], 0.0) for e in range(N_EXPERTS)]
    wsum = w[0]
    for e in range(1, N_EXPERTS):
        wsum = wsum + w[e]
    inv = 1.0 / wsum
    wg = []
    for j in range(EPG):
        acc = w[j]
        for g in range(1, N_GROUPS):
            acc = acc + w[EPG * g + j]
        wg.append(acc * inv)
    return wg, gidx


def _ffn_prep(x_new, mod_ref, gf_ref, rwt_ref, rb_ref, h2c_ref, gidx_ref, grank_ref, cnt_ref,
              ct_scr, run_scr):
    t = pl.program_id(0)
    tm = x_new.shape[0]
    shift = mod_ref[0, :, 3 * D:4 * D]
    scale = mod_ref[0, :, 4 * D:5 * D]
    h2 = _rms(x_new, gf_ref[...]) * (1.0 + scale) + shift
    logits_t = _dot_nt(rwt_ref[...], h2, precision=lax.Precision.HIGHEST)
    wg, gidx = _route(logits_t, rb_ref[...])
    ct_scr[...] = jnp.zeros_like(ct_scr)
    for j in range(EPG):
        ct_scr[j:j + 1, :] = wg[j]
    cw = ct_scr[...].T

    for k in range(Y_SUB):
        h2c_ref[pl.ds(k, tm, stride=X_SUB), :] = h2[:, k * LANES:(k + 1) * LANES]
    h2c_ref[pl.ds(Y_SUB, tm, stride=X_SUB), :] = cw
    for k in range(Y_SUB + 1, X_SUB):
        h2c_ref[pl.ds(k, tm, stride=X_SUB), :] = jnp.zeros((tm, LANES), F32)

    @pl.when(t == 0)
    def _():
        run_scr[...] = jnp.zeros_like(run_scr)

    onehot = (lax.broadcasted_iota(jnp.int32, (8, tm), 0) == gidx).astype(F32)
    earlier = (lax.broadcasted_iota(jnp.int32, (tm, tm), 0)
               < lax.broadcasted_iota(jnp.int32, (tm, tm), 1)).astype(BF)
    rank = _dot(onehot.astype(BF), earlier)
    run = run_scr[:, 0:1]
    grank = jnp.sum(onehot * (rank + run), axis=0, keepdims=True)
    gidx_ref[0] = gidx
    grank_ref[0] = grank.astype(jnp.int32)
    run_scr[...] = run_scr[...] + jnp.sum(onehot, axis=1, keepdims=True)
    cnt_ref[...] = run_scr[...]


def _prep_out_specs():
    tok = lambda w: pl.BlockSpec((TM, w), lambda t: (t, 0))
    lanes = pl.BlockSpec((1, 1, TM), lambda t: (t, 0, 0))
    rows = pl.BlockSpec((TM * X_SUB, LANES), lambda t: (t, 0))
    return [tok(D), rows, lanes, lanes, pl.BlockSpec((8, 128), lambda t: (0, 0))]


def _prep_out_shapes():
    nt = T // TM
    return [jax.ShapeDtypeStruct((T, D), F32), jax.ShapeDtypeStruct((T * X_SUB, LANES), F32),
            jax.ShapeDtypeStruct((nt, 1, TM), jnp.int32), jax.ShapeDtypeStruct((nt, 1, TM), jnp.int32),
            jax.ShapeDtypeStruct((8, 128), F32)]


EV_EXT = 3 * D_CONV + Q_RANK + KV_RANK + 3 * HEAD_PAD


def _tok_specs(parts, width):
    npt = NP_TOK // TM
    if len(parts) == 1:
        return [pl.BlockSpec((TM, width), lambda t: (t, 0))]
    return [pl.BlockSpec((TM, width), lambda t: (jnp.minimum(t, npt - 1), 0)),
            pl.BlockSpec((TM, width), lambda t: (jnp.maximum(t - npt, 0), 0))]


def _tok_load(refs):
    if len(refs) == 1:
        return refs[0][...]
    return jnp.where(pl.program_id(0) < NP_TOK // TM, refs[0][...], refs[1][...])


def _even_in_kernel(n_x, *refs):
    x_refs, refs = refs[:n_x], refs[n_x:]
    (mod_ref, g_ref, win_ref, qg_ref, wq1_ref, wq2_ref, kg_ref, wk_ref, wv_ref, rc_ref, rs_ref,
     bg_ref, cv_ref, q_ref, k_ref, v_ref, ckv_ref, kr_ref) = refs
    x = _tok_load(x_refs)
    shift = mod_ref[0, :, 0:D]
    scale = mod_ref[0, :, D:2 * D]
    h = (_rms(x, g_ref[...]) * (1.0 + scale) + shift).astype(BF)
    proj = _dot(h, win_ref[...])
    o = 0
    b_g = proj[:, o:o + D_CONV]; o += D_CONV
    c_g = proj[:, o:o + D_CONV]; o += D_CONV
    v_in = proj[:, o:o + D_CONV]; o += D_CONV
    q_a = proj[:, o:o + Q_RANK]; o += Q_RANK
    kv_a = proj[:, o:o + KV_RANK]; o += KV_RANK
    kr_raw = proj[:, o:o + HEAD_PAD]; o += HEAD_PAD
    kr_cat = proj[:, o:o + HEAD_PAD]; o += HEAD_PAD
    kr_sw = proj[:, o:o + HEAD_PAD]

    bg_ref[...] = b_g.astype(BF)
    cv_ref[...] = (c_g * v_in).astype(BF)
    kr_ref[...] = kr_raw

    rc = rc_ref[...]
    rs = rs_ref[...]
    qn = _rms(q_a, qg_ref[...]).astype(BF)
    q1 = _dot(qn, wq1_ref[...])
    q2 = _dot(qn, wq2_ref[...])
    qscale = QK_DIM ** -0.5
    for hd in range(HEADS):
        sl = slice(hd * HEAD_PAD, (hd + 1) * HEAD_PAD)
        q_ref[:, sl] = ((q1[:, sl] * rc + q2[:, sl] * rs) * qscale).astype(BF)

    ckv = _rms(kv_a, kg_ref[...])
    ckv_ref[...] = ckv
    ckv_b = ckv.astype(BF)
    kk = _dot(ckv_b, wk_ref[...])
    kr = kr_cat * rc + kr_sw * rs
    for hd in range(HEADS):
        sl = slice(hd * HEAD_PAD, (hd + 1) * HEAD_PAD)
        k_ref[:, sl] = (kk[:, sl] + kr).astype(BF)
    v_ref[...] = _dot(ckv_b, wv_ref[...]).astype(BF)


def _even_in(x_parts, mod, l, g_mix, win, qg, wq1, wq2, kg, wk, wv, rope_c, rope_s):
    nt = T // TM
    npt = NP_TOK // TM
    per_seq = DEC_SEQ // TM
    ident_blk = DEC_SEQ // TM

    def rope_idx(t):
        return (jnp.where(t < npt, ident_blk, (t - npt) % per_seq), 0)

    full = lambda shape: pl.BlockSpec(shape, lambda t: (0,) * len(shape))
    tok = lambda w: pl.BlockSpec((TM, w), lambda t: (t, 0))
    return pl.pallas_call(
        functools.partial(_even_in_kernel, len(x_parts)),
        grid=(nt,),
        in_specs=_tok_specs(x_parts, D) + [
            pl.BlockSpec((1, 1, 6 * D), lambda t: (l * MOD_ROWS + _mod_row(t, TM), 0, 0)),
            full((1, D)), full((D, EV_EXT)), full((1, Q_RANK)),
            full((Q_RANK, HEADS * HEAD_PAD)), full((Q_RANK, HEADS * HEAD_PAD)),
            full((1, KV_RANK)), full((KV_RANK, HEADS * HEAD_PAD)), full((KV_RANK, HEADS * V_DIM)),
            pl.BlockSpec((TM, HEAD_PAD), rope_idx), pl.BlockSpec((TM, HEAD_PAD), rope_idx),
        ],
        out_specs=[tok(D_CONV), tok(D_CONV), tok(HEADS * HEAD_PAD), tok(HEADS * HEAD_PAD),
                   tok(HEADS * V_DIM), tok(KV_RANK), tok(HEAD_PAD)],
        out_shape=[
            jax.ShapeDtypeStruct((T, D_CONV), BF), jax.ShapeDtypeStruct((T, D_CONV), BF),
            jax.ShapeDtypeStruct((T, HEADS * HEAD_PAD), BF), jax.ShapeDtypeStruct((T, HEADS * HEAD_PAD), BF),
            jax.ShapeDtypeStruct((T, HEADS * V_DIM), BF),
            jax.ShapeDtypeStruct((T, KV_RANK), F32), jax.ShapeDtypeStruct((T, HEAD_PAD), F32),
        ],
        compiler_params=_cp(1),
        name="even_in",
    )(*x_parts, mod, g_mix, win, qg, wq1, wq2, kg, wk, wv, rope_c, rope_s)


CTX_TM = 512


def _ctx_kv_kernel(ckv_ref, kr_ref, wk_ref, wv_ref, k_ref, v_ref):
    ckv_b = ckv_ref[0].astype(BF)
    kk = _dot(ckv_b, wk_ref[0])
    kr = kr_ref[0]
    for hd in range(HEADS):
        sl = slice(hd * HEAD_PAD, (hd + 1) * HEAD_PAD)
        k_ref[0, :, sl] = (kk[:, sl] + kr).astype(BF)
    v_ref[0] = _dot(ckv_b, wv_ref[0]).astype(BF)


def _ctx_kv(ckv_all, kr_all, wk_all, wv_all):
    n_even = ckv_all.shape[0]
    rows = DEC_BATCH * PAST
    return pl.pallas_call(
        _ctx_kv_kernel,
        grid=(n_even, rows // CTX_TM),
        in_specs=[
            pl.BlockSpec((1, CTX_TM, KV_RANK), lambda i, t: (i, t, 0)),
            pl.BlockSpec((1, CTX_TM, HEAD_PAD), lambda i, t: (i, t, 0)),
            pl.BlockSpec((1, KV_RANK, HEADS * HEAD_PAD), lambda i, t: (i, 0, 0)),
            pl.BlockSpec((1, KV_RANK, HEADS * V_DIM), lambda i, t: (i, 0, 0)),
        ],
        out_specs=[
            pl.BlockSpec((1, CTX_TM, HEADS * HEAD_PAD), lambda i, t: (i, t, 0)),
            pl.BlockSpec((1, CTX_TM, HEADS * V_DIM), lambda i, t: (i, t, 0)),
        ],
        out_shape=[jax.ShapeDtypeStruct((n_even, rows, HEADS * HEAD_PAD), BF),
                   jax.ShapeDtypeStruct((n_even, rows, HEADS * V_DIM), BF)],
        compiler_params=_cp(2),
        name="ctx_kv",
    )(ckv_all, kr_all, wk_all, wv_all)


def _attn_body(n_pairs, has_ctx, q_ref, k_ref, v_ref, *rest):
    if has_ctx:
        kc_ref, vc_ref, o_ref = rest
    else:
        (o_ref,) = rest
    lane = lax.broadcasted_iota(jnp.int32, (1, 2 * V_DIM), 1)
    for pr in range(n_pairs):
        vsl = slice(pr * 2 * V_DIM, (pr + 1) * 2 * V_DIM)
        v = v_ref[:, vsl]
        outs = []
        for sub in range(2):
            hsl = slice((2 * pr + sub) * HEAD_PAD, (2 * pr + sub + 1) * HEAD_PAD)
            q = q_ref[:, hsl]
            s1 = _dot_nt(q, k_ref[:, hsl])
            m = jnp.max(s1, axis=-1, keepdims=True)
            if has_ctx:
                s2 = _dot_nt(q, kc_ref[0, :, hsl])
                m = jnp.maximum(m, jnp.max(s2, axis=-1, keepdims=True))
            p1 = jnp.exp(s1 - m)
            den = jnp.sum(p1, axis=-1, keepdims=True)
            o = _dot(p1.astype(BF), v)
            if has_ctx:
                p2 = jnp.exp(s2 - m)
                den = den + jnp.sum(p2, axis=-1, keepdims=True)
                o = o + _dot(p2.astype(BF), vc_ref[0, :, vsl])
            outs.append(o * (1.0 / den))
        o_ref[:, vsl] = jnp.where(lane < V_DIM, outs[0], outs[1]).astype(BF)


def _attn_prompt(q, k, v):
    return pl.pallas_call(
        functools.partial(_attn_body, HEADS // 2, False),
        grid=(BATCH,),
        in_specs=[
            pl.BlockSpec((SEQ, HEADS * HEAD_PAD), lambda b: (b, 0)),
            pl.BlockSpec((SEQ, HEADS * HEAD_PAD), lambda b: (b, 0)),
            pl.BlockSpec((SEQ, HEADS * V_DIM), lambda b: (b, 0)),
        ],
        out_specs=pl.BlockSpec((SEQ, HEADS * V_DIM), lambda b: (b, 0)),
        out_shape=jax.ShapeDtypeStruct((NP_TOK, HEADS * V_DIM), BF),
        compiler_params=_cp(1),
        name="attn_prompt",
    )(q, k, v)


def _attn_sample(q, k, v, kc, vc, i):
    s_blk0 = NP_TOK // DEC_SEQ
    q_blk0 = NP_TOK // TQ
    nq = DEC_SEQ // TQ
    return pl.pallas_call(
        functools.partial(_attn_body, 1, True),
        grid=(DEC_BATCH, HEADS // 2, nq),
        in_specs=[
            pl.BlockSpec((TQ, 2 * HEAD_PAD), lambda b, hp, j: (q_blk0 + b * nq + j, hp)),
            pl.BlockSpec((DEC_SEQ, 2 * HEAD_PAD), lambda b, hp, j: (s_blk0 + b, hp)),
            pl.BlockSpec((DEC_SEQ, 2 * V_DIM), lambda b, hp, j: (s_blk0 + b, hp)),
            pl.BlockSpec((1, PAST, 2 * HEAD_PAD), lambda b, hp, j: (i, b, hp)),
            pl.BlockSpec((1, PAST, 2 * V_DIM), lambda b, hp, j: (i, b, hp)),
        ],
        out_specs=pl.BlockSpec((TQ, 2 * V_DIM), lambda b, hp, j: (b * nq + j, hp)),
        out_shape=jax.ShapeDtypeStruct((NS_TOK, HEADS * V_DIM), BF),
        compiler_params=_cp(3),
        name="attn_sample",
    )(q, k, v, kc, vc)


def _even_out_kernel(n_x, *refs):
    x_refs, refs = refs[:n_x], refs[n_x:]
    (mod_ref, bg_ref, cv_ref, cvp_ref, cvn_ref, atp_ref, ats_ref, cw_ref, wo_ref, gf_ref, rwt_ref, rb_ref,
     xo_ref, h2c_ref, gidx_ref, grank_ref, cnt_ref, ct_scr, run_scr) = refs
    t = pl.program_id(0)
    npt = NP_TOK // TM
    per_seq = DEC_SEQ // TM
    cv = cv_ref[...].astype(F32)
    r = lax.broadcasted_iota(jnp.int32, (TM, 1), 0)
    is_prompt = t < npt
    tile_in_seq = (t - npt) % per_seq
    first_row = jnp.where(is_prompt, 0, jnp.where(tile_in_seq == 0, 0, -1))
    last_row = jnp.where(is_prompt, SEQ - 1, jnp.where(tile_in_seq == per_seq - 1, TM - 1, -1))
    period_mask = jnp.where(is_prompt, SEQ - 1, TM - 1)
    first = (r & period_mask) == first_row
    last = (r & period_mask) == last_row
    prev_row = cvp_ref[HALO - 1:HALO, :].astype(F32)
    next_row = cvn_ref[0:1, :].astype(F32)
    prev = jnp.where(r == 0, prev_row, pltpu.roll(cv, 1, 0))
    prev = jnp.where(first, 0.0, prev)
    nxt = jnp.where(r == TM - 1, next_row, pltpu.roll(cv, TM - 1, 0))
    nxt = jnp.where(last, 0.0, nxt)
    cw = cw_ref[...]
    conv = prev * cw[0:1, :] + cv * cw[1:2, :] + nxt * cw[2:3, :]
    yc = (bg_ref[...].astype(F32) * conv).astype(BF)
    attn = _tok_load((atp_ref, ats_ref))
    out = _dot(yc, wo_ref[0:D_CONV, :]) + _dot(attn, wo_ref[D_CONV:2 * D_CONV, :])
    gate = mod_ref[0, :, 2 * D:3 * D]
    x_new = _tok_load(x_refs) + gate * out
    xo_ref[...] = x_new
    _ffn_prep(x_new, mod_ref, gf_ref, rwt_ref, rb_ref, h2c_ref, gidx_ref, grank_ref, cnt_ref,
              ct_scr, run_scr)


def _even_out(x_parts, mod, l, bg, cv, attn_parts, conv_w, w_out, g_ffn, rwt, rb):
    nt = T // TM
    hb = TM // HALO
    nhb = T // HALO
    full = lambda shape: pl.BlockSpec(shape, lambda t: (0,) * len(shape))
    tok = lambda w: pl.BlockSpec((TM, w), lambda t: (t, 0))
    return pl.pallas_call(
        functools.partial(_even_out_kernel, len(x_parts)),
        grid=(nt,),
        in_specs=_tok_specs(x_parts, D) + [
            pl.BlockSpec((1, 1, 6 * D), lambda t: (l * MOD_ROWS + _mod_row(t, TM), 0, 0)),
            tok(D_CONV), tok(D_CONV),
            pl.BlockSpec((HALO, D_CONV), lambda t: (jnp.maximum(t * hb - 1, 0), 0)),
            pl.BlockSpec((HALO, D_CONV), lambda t: (jnp.minimum((t + 1) * hb, nhb - 1), 0)),
        ] + _tok_specs(attn_parts, HEADS * V_DIM) + [
            full((3, D_CONV)), full((2 * D_CONV, D)), full((1, D)),
            full((N_EXPERTS, D)), full((N_EXPERTS, 1)),
        ],
        out_specs=_prep_out_specs(),
        out_shape=_prep_out_shapes(),
        scratch_shapes=[pltpu.VMEM((128, TM), F32), pltpu.VMEM((8, 128), F32)],
        compiler_params=_cp(1),
        name="even_out",
    )(*x_parts, mod, bg, cv, cv, cv, *attn_parts, conv_w, w_out, g_ffn, rwt, rb)


def _odd_kernel(x_ref, mod_ref, g_ref, win_ref, vg_ref, ws_ref, bst_ref, wo_ref,
                gf_ref, rwt_ref, rb_ref,
                xo_ref, h2c_ref, gidx_ref, grank_ref, cnt_ref, gated_scr, ct_scr, run_scr):
    x = x_ref[...]
    shift = mod_ref[0, :, 0:D]
    scale = mod_ref[0, :, D:2 * D]
    h = (_rms(x, g_ref[...]) * (1.0 + scale) + shift).astype(BF)
    zl = _dot(h, win_ref[...])
    z = 0.5 * zl * (1.0 + lax.erf(zl * math.sqrt(0.5)))
    u = z[:, 0:D]
    v = _rms(z[:, D:2 * D], vg_ref[...]).astype(BF)
    n_chunks = TM // CHUNK
    gch = D // GM_GROUPS
    for g in range(GM_GROUPS):
        csl = slice(g * gch, (g + 1) * gch)
        vg = jnp.concatenate([v[n * CHUNK:(n + 1) * CHUNK, csl] for n in range(n_chunks)], axis=1)
        sg = _dot(ws_ref[g], vg) + bst_ref[:, g:g + 1]
        for n in range(n_chunks):
            rsl = slice(n * CHUNK, (n + 1) * CHUNK)
            gated_scr[rsl, csl] = (u[rsl, csl] * sg[:, n * gch:(n + 1) * gch]).astype(BF)
    out = _dot(gated_scr[...], wo_ref[...])
    gate = mod_ref[0, :, 2 * D:3 * D]
    x_new = x + gate * out
    xo_ref[...] = x_new
    _ffn_prep(x_new, mod_ref, gf_ref, rwt_ref, rb_ref, h2c_ref, gidx_ref, grank_ref, cnt_ref,
              ct_scr, run_scr)


def _odd(x, mod, l, g_mix, win, v_g, w_s, b_st, w_out, g_ffn, rwt, rb):
    nt = T // TM
    full = lambda shape: pl.BlockSpec(shape, lambda t: (0,) * len(shape))
    tok = lambda w: pl.BlockSpec((TM, w), lambda t: (t, 0))
    return pl.pallas_call(
        _odd_kernel,
        grid=(nt,),
        in_specs=[
            tok(D),
            pl.BlockSpec((1, 1, 6 * D), lambda t: (l * MOD_ROWS + _mod_row(t, TM), 0, 0)),
            full((1, D)), full((D, 2 * D)), full((1, D)),
            full((GM_GROUPS, CHUNK, CHUNK)), full((CHUNK, GM_GROUPS)), full((D, D)),
            full((1, D)), full((N_EXPERTS, D)), full((N_EXPERTS, 1)),
        ],
        out_specs=_prep_out_specs(),
        out_shape=_prep_out_shapes(),
        scratch_shapes=[pltpu.VMEM((TM, D), BF), pltpu.VMEM((128, TM), F32), pltpu.VMEM((8, 128), F32)],
        compiler_params=_cp(1),
        name="odd",
    )(x, mod, g_mix, win, v_g, w_s, b_st, w_out, g_ffn, rwt, rb)


def _routing_plan(gidx, grank, cnt):
    counts = cnt[:N_GROUPS, 0].astype(jnp.int32)
    padded = (counts + (TMM - 1)) // TMM * TMM
    ends = jnp.cumsum(padded)
    base = ends - padded
    g = gidx.reshape(T)
    slot = grank.reshape(T)
    for k in range(N_GROUPS):
        slot = slot + jnp.where(g == k, base[k], 0)
    n_steps = ends[N_GROUPS - 1] // TMM
    starts = jnp.minimum(jnp.arange(MAX_STEPS, dtype=jnp.int32) * TMM, ends[N_GROUPS - 1] - TMM)
    group_of_step = jnp.zeros((MAX_STEPS,), jnp.int32)
    for k in range(N_GROUPS - 1):
        group_of_step = group_of_step + (starts >= ends[k]).astype(jnp.int32)
    sched = jnp.concatenate([group_of_step, n_steps[None]]).astype(jnp.int32)
    fill_plan = jnp.concatenate([base + counts, n_steps[None]]).astype(jnp.int32)
    return slot.astype(jnp.int32), sched, fill_plan


def _row_tile(ref, row, sub):
    return ref.at[pl.ds(pl.multiple_of(row * sub, sub), sub), :]


def _row_copies(n_rows, make_copy):
    def body(i, carry):
        for j in range(ISSUE_UNROLL):
            make_copy(i * ISSUE_UNROLL + j).start(priority=j % 2)
        return carry
    lax.fori_loop(0, n_rows // ISSUE_UNROLL, body, 0)


def _dispatch_kernel(pad_ref, slot_ref, h_ref, o_hbm, zero_scr, zsem, sem):
    t = pl.program_id(0)

    @pl.when(t == 0)
    def _():
        zero_scr[...] = jnp.zeros_like(zero_scr)
        fills = [pltpu.make_async_copy(
            zero_scr,
            o_hbm.at[pl.ds(pl.multiple_of(pad_ref[g] * X_SUB, X_SUB), TMM * X_SUB), :],
            zsem.at[g]) for g in range(N_GROUPS)]
        for f in fills:
            f.start()
        for f in fills:
            f.wait()
        for s in range(T // TMM, MAX_STEPS + 1):
            @pl.when(s >= pad_ref[N_GROUPS])
            def _():
                tail = pltpu.make_async_copy(
                    zero_scr, o_hbm.at[pl.ds(s * TMM * X_SUB, TMM * X_SUB), :], zsem.at[0])
                tail.start()
                tail.wait()

    _row_copies(TM, lambda r: pltpu.make_async_copy(
        _row_tile(h_ref, r, X_SUB), _row_tile(o_hbm, slot_ref[r], X_SUB), sem))
    pltpu.make_async_copy(h_ref, o_hbm.at[pl.ds(0, TM * X_SUB), :], sem).wait()


def _dispatch(pad_start, slot, h2c):
    nt = T // TM
    return pl.pallas_call(
        _dispatch_kernel,
        grid_spec=pltpu.PrefetchScalarGridSpec(
            num_scalar_prefetch=1,
            grid=(nt,),
            in_specs=[
                pl.BlockSpec((TM,), lambda t, pad: (t,), memory_space=pltpu.SMEM),
                pl.BlockSpec((TM * X_SUB, LANES), lambda t, pad: (t, 0)),
            ],
            out_specs=pl.BlockSpec(memory_space=pl.ANY),
            scratch_shapes=[pltpu.VMEM((TMM * X_SUB, LANES), F32),
                            pltpu.SemaphoreType.DMA((N_GROUPS,)), pltpu.SemaphoreType.DMA(())],
        ),
        out_shape=jax.ShapeDtypeStruct((SORTED_ROWS * X_SUB, LANES), F32),
        compiler_params=_cp(1),
        name="dispatch",
    )(pad_start, slot, h2c)


def _experts_kernel(sched_ref, xs_ref, w1_ref, w3_ref, w2_ref, o_ref, w1b, w3b, w2b):
    s = pl.program_id(0)

    @pl.when(jnp.logical_or(s == 0, sched_ref[s] != sched_ref[jnp.maximum(s - 1, 0)]))
    def _():
        w1b[...] = w1_ref[...].astype(BF)
        w3b[...] = w3_ref[...].astype(BF)
        w2b[...] = w2_ref[...].astype(BF)

    @pl.when(s < sched_ref[MAX_STEPS])
    def _():
        h = jnp.concatenate([xs_ref[pl.ds(k, TMM, stride=X_SUB), :] for k in range(Y_SUB)],
                            axis=1).astype(BF)
        cw = xs_ref[pl.ds(Y_SUB, TMM, stride=X_SUB), :]
        acc = None
        for j in range(EPG):
            a = _dot(h, w1b[j])
            b = _dot(h, w3b[j])
            hid = (a * (1.0 / (1.0 + jnp.exp(-a))) * b * cw[:, j:j + 1]).astype(BF)
            y = _dot(hid, w2b[j])
            acc = y if acc is None else acc + y
        for k in range(D // LANES):
            o_ref[pl.ds(k, TMM, stride=Y_SUB), :] = acc[:, k * LANES:(k + 1) * LANES]

    @pl.when(s >= sched_ref[MAX_STEPS])
    def _():
        o_ref[...] = jnp.zeros_like(o_ref)


def _experts(sched, xs, l, w1, w3, w2):
    row_blk = lambda s, sched: (jnp.maximum(jnp.minimum(s, sched[MAX_STEPS] - 1), 0), 0)
    grp_blk = lambda s, sched: (l, sched[s], 0, 0)
    return pl.pallas_call(
        _experts_kernel,
        grid_spec=pltpu.PrefetchScalarGridSpec(
            num_scalar_prefetch=1,
            grid=(MAX_STEPS,),
            in_specs=[
                pl.BlockSpec((TMM * X_SUB, LANES), row_blk),
                pl.BlockSpec((None, EPG, D, D_EXPERT), grp_blk),
                pl.BlockSpec((None, EPG, D, D_EXPERT), grp_blk),
                pl.BlockSpec((None, EPG, D_EXPERT, D), grp_blk),
            ],
            out_specs=pl.BlockSpec((TMM * Y_SUB, LANES), lambda s, sched: (s, 0)),
            scratch_shapes=[pltpu.VMEM((EPG, D, D_EXPERT), BF), pltpu.VMEM((EPG, D, D_EXPERT), BF),
                            pltpu.VMEM((EPG, D_EXPERT, D), BF)],
        ),
        out_shape=jax.ShapeDtypeStruct((MAX_STEPS * TMM * Y_SUB, LANES), F32),
        compiler_params=_cp(1),
        name="experts",
    )(sched, xs, w1, w3, w2)


def _combine_kernel(final, slot_ref, slot_next_ref, x_ref, mod_ref, gfin_ref, ys_hbm, *rest):
    o_refs, (buf, sem) = rest[:-2], rest[-2:]
    t = pl.program_id(0)
    nt = pl.num_programs(0)
    cur = t % 2

    def gather(idx_ref, b):
        _row_copies(TM, lambda r: pltpu.make_async_copy(
            _row_tile(ys_hbm, idx_ref[r], Y_SUB), _row_tile(buf.at[b], r, Y_SUB), sem.at[b]))

    @pl.when(t == 0)
    def _():
        gather(slot_ref, 0)

    @pl.when(t + 1 < nt)
    def _():
        gather(slot_next_ref, 1 - cur)

    pltpu.make_async_copy(ys_hbm.at[pl.ds(0, TM * Y_SUB), :], buf.at[cur], sem.at[cur]).wait()
    gate = mod_ref[0, :, 5 * D:6 * D]
    y = jnp.concatenate([buf[cur, pl.ds(k, TM, stride=Y_SUB), :] for k in range(Y_SUB)], axis=1)
    x_new = x_ref[...] + gate * y
    if not final:
        o_refs[0][...] = x_new
        return
    y_out = _rms(x_new, gfin_ref[...])
    npt = NP_TOK // TM

    @pl.when(t < npt)
    def _():
        o_refs[0][...] = y_out

    @pl.when(t >= npt)
    def _():
        o_refs[1][...] = y_out


def _combine(x, mod, l, slot, ys, g_final, final):
    nt = T // TM
    npt = NP_TOK // TM
    if final:
        out_specs = [pl.BlockSpec((TM, D), lambda t: (jnp.minimum(t, npt - 1), 0)),
                     pl.BlockSpec((TM, D), lambda t: (jnp.maximum(t - npt, 0), 0))]
        out_shape = [jax.ShapeDtypeStruct((NP_TOK, D), F32), jax.ShapeDtypeStruct((NS_TOK, D), F32)]
    else:
        out_specs = [pl.BlockSpec((TM, D), lambda t: (t, 0))]
        out_shape = [jax.ShapeDtypeStruct((T, D), F32)]
    return pl.pallas_call(
        functools.partial(_combine_kernel, final),
        grid=(nt,),
        in_specs=[
            pl.BlockSpec((TM,), lambda t: (t,), memory_space=pltpu.SMEM),
            pl.BlockSpec((TM,), lambda t: (jnp.minimum(t + 1, nt - 1),), memory_space=pltpu.SMEM),
            pl.BlockSpec((TM, D), lambda t: (t, 0)),
            pl.BlockSpec((1, 1, 6 * D), lambda t: (l * MOD_ROWS + _mod_row(t, TM), 0, 0)),
            pl.BlockSpec((1, D), lambda t: (0, 0)),
            pl.BlockSpec(memory_space=pl.ANY),
        ],
        out_specs=out_specs,
        out_shape=out_shape,
        scratch_shapes=[pltpu.VMEM((2, TM * Y_SUB, LANES), F32), pltpu.SemaphoreType.DMA((2,))],
        compiler_params=_cp(1),
        name="combine",
    )(slot, slot, x, mod, g_final, ys)


def _moe(x, mod, l, h2c, gidx, grank, cnt, w1, w3, w2, g_final, final):
    slot, sched, pad_start = _routing_plan(gidx, grank, cnt)
    xs = _dispatch(pad_start, slot, h2c)
    ys = _experts(sched, xs, l, w1, w3, w2)
    return _combine(x, mod, l, slot, ys, g_final, final)


def _rope_tables():
    pos = jnp.arange(DEC_SEQ)
    r = (pos // GRID_W).astype(F32)
    col = (pos % GRID_W).astype(F32)
    inv = ROPE_BASE ** (-jnp.arange(AX_FREQS, dtype=F32) / AX_FREQS)
    ang = jnp.stack([r[:, None] * inv, col[:, None] * inv], axis=1)
    cos = jnp.cos(ang)[:, :, None, :]
    sin = jnp.sin(ang)[:, :, None, :]
    c32 = jnp.broadcast_to(cos, (DEC_SEQ, 2, 2, AX_FREQS)).reshape(DEC_SEQ, ROPE)
    s32 = jnp.concatenate([-sin, sin], axis=2).reshape(DEC_SEQ, ROPE)
    pad = HEAD_PAD - NOPE - ROPE
    c = jnp.concatenate([jnp.ones((DEC_SEQ, NOPE), F32), c32, jnp.zeros((DEC_SEQ, pad), F32)], axis=1)
    s = jnp.concatenate([jnp.zeros((DEC_SEQ, NOPE), F32), s32, jnp.zeros((DEC_SEQ, pad), F32)], axis=1)
    c_id = jnp.concatenate([jnp.ones((TM, NOPE + ROPE), F32), jnp.zeros((TM, pad), F32)], axis=1)
    s_id = jnp.zeros((TM, HEAD_PAD), F32)
    return jnp.concatenate([c, c_id], axis=0), jnp.concatenate([s, s_id], axis=0)


def _swap_halves(w):
    lead = w.shape[:-1]
    return w.reshape(lead + (2, 2, AX_FREQS))[..., ::-1, :].reshape(lead + (ROPE,))


def _even_weights(w_in, w_q_up, w_kv_up):
    k_in = w_in.shape[0]
    base = 3 * D_CONV + Q_RANK + KV_RANK
    w_kr = w_in[:, base:base + ROPE]
    z = lambda n: jnp.zeros((k_in, n), F32)
    win_ext = jnp.concatenate([
        w_in[:, :base],
        w_kr, z(HEAD_PAD - ROPE),
        z(NOPE), w_kr, z(HEAD_PAD - NOPE - ROPE),
        z(NOPE), _swap_halves(w_kr), z(HEAD_PAD - NOPE - ROPE),
    ], axis=1).astype(BF)
    wq = w_q_up.reshape(Q_RANK, HEADS, QK_DIM)
    zq = lambda n: jnp.zeros((Q_RANK, HEADS, n), F32)
    wq1 = jnp.concatenate([wq, zq(HEAD_PAD - QK_DIM)], axis=2).reshape(Q_RANK, HEADS * HEAD_PAD).astype(BF)
    wq2 = jnp.concatenate([zq(NOPE), _swap_halves(wq[:, :, NOPE:]), zq(HEAD_PAD - QK_DIM)],
                          axis=2).reshape(Q_RANK, HEADS * HEAD_PAD).astype(BF)
    wkv = w_kv_up.reshape(KV_RANK, HEADS, NOPE + V_DIM)
    wk = jnp.concatenate([wkv[:, :, :NOPE], jnp.zeros((KV_RANK, HEADS, HEAD_PAD - NOPE), F32)],
                         axis=2).reshape(KV_RANK, HEADS * HEAD_PAD).astype(BF)
    wv = wkv[:, :, NOPE:].reshape(KV_RANK, HEADS * V_DIM).astype(BF)
    return win_ext, wq1, wq2, wk, wv


def kernel(x_prompt, x_sample, cache_ckv, cache_krope, c, c_ctx, w_ada, b_ada, g_mix, g_ffn, g_final,
           ev_w_in, conv_w, q_norm_g, w_q_up, kv_norm_g, w_kv_up, ev_w_out,
           gm_w_in, gm_v_g, gm_w_s, gm_b_s, gm_w_out, router_w, router_b, moe_w1, moe_w3, moe_w2):
    n_even = ev_w_in.shape[0]
    x_parts = (x_prompt.reshape(NP_TOK, D), x_sample.reshape(NS_TOK, D))
    cc =jnp.concatenate([c, c_ctx[None, :], jnp.zeros((MOD_ROWS - DEC_BATCH - 1, D), F32)], axis=0)
    mod = _ada(cc, w_ada, b_ada).reshape(DEPTH * MOD_ROWS, 1, 6 * D)

    rope_c, rope_s = _rope_tables()
    rwt = router_w.T
    rb = router_b.reshape(N_EXPERTS, 1)
    ev = [_even_weights(ev_w_in[i], w_q_up[i], w_kv_up[i]) for i in range(n_even)]
    kr_ctx = jnp.pad(cache_krope, ((0, 0), (0, 0), (0, 0), (NOPE, HEAD_PAD - NOPE - ROPE)))
    kr_ctx = kr_ctx.transpose(1, 0, 2, 3).reshape(n_even, DEC_BATCH * PAST, HEAD_PAD)
    ckv_ctx = cache_ckv.transpose(1, 0, 2, 3).reshape(n_even, DEC_BATCH * PAST, KV_RANK)
    kc, vc = _ctx_kv(ckv_ctx, kr_ctx, jnp.stack([e[3] for e in ev]), jnp.stack([e[4] for e in ev]))

    ckv_states, kr_states = [], []
    for l in range(DEPTH):
        i = l // 2
        if l % 2 == 0:
            win_ext, wq1, wq2, wk, wv = ev[i]
            bg, cv, q, k, v, ckv, kr = _even_in(
                x_parts, mod, l, g_mix[l][None, :], win_ext, q_norm_g[i][None, :], wq1, wq2,
                kv_norm_g[i][None, :], wk, wv, rope_c, rope_s)
            attn_parts = (_attn_prompt(q, k, v), _attn_sample(q, k, v, kc, vc, i))
            x, h2c, gidx, grank, cnt = _even_out(x_parts, mod, l, bg, cv, attn_parts, conv_w[i],
                                                 ev_w_out[i].astype(BF), g_ffn[l][None, :], rwt, rb)
            ckv_states.append(ckv[:NP_TOK].reshape(BATCH, SEQ, KV_RANK))
            kr_states.append(kr[:NP_TOK, :ROPE].reshape(BATCH, SEQ, ROPE))
        else:
            x, h2c, gidx, grank, cnt = _odd(x_parts[0], mod, l, g_mix[l][None, :], gm_w_in[i].astype(BF),
                                            gm_v_g[i][None, :], gm_w_s[i].astype(BF), gm_b_s[i].T,
                                            gm_w_out[i].astype(BF), g_ffn[l][None, :], rwt, rb)
        x_parts = _moe(x, mod, l, h2c, gidx, grank, cnt, moe_w1, moe_w3, moe_w2, g_final[None, :],
                       l == DEPTH - 1)

    y_prompt = x_parts[0].reshape(BATCH, SEQ, D)
    y_sample = x_parts[1].reshape(DEC_BATCH, DEC_SEQ, D)
    return (y_prompt, y_sample, jnp.stack(ckv_states, axis=1), jnp.stack(kr_states, axis=1))
```

```python
import functools
import math

import jax
import jax.numpy as jnp
from jax import lax
from jax.experimental import pallas as pl
from jax.experimental.pallas import tpu as pltpu

D = 1024
BATCH, SEQ = 32, 256
DEC_BATCH, DEC_SEQ = 8, 2048
PAST = 256
DEPTH = 4
GRID_W = 64
D_CONV = 512
HEADS = 8
NOPE, ROPE, V_DIM = 64, 32, 64
QK_DIM = NOPE + ROPE
Q_RANK, KV_RANK = 384, 256
AX_FREQS = ROPE // 4
ROPE_BASE = 10000.0
CHUNK = 128
GM_GROUPS = 8
N_EXPERTS, N_GROUPS, EPG = 16, 4, 4
D_EXPERT = 256
EPS = 1e-6

NP_TOK = BATCH * SEQ
NS_TOK = DEC_BATCH * DEC_SEQ
T = NP_TOK + NS_TOK
MOD_ROWS = 16
HEAD_PAD = 128
HALO = 16

TM = 512
TMM = 512
MAX_STEPS = T // TMM + N_GROUPS
SORTED_ROWS = (MAX_STEPS + 1) * TMM
LANES = 128
Y_SUB = D // LANES
X_SUB = 2 * Y_SUB
ISSUE_UNROLL = 8
TQ = 256
VMEM_LIMIT = 56 * 1024 * 1024

BF = jnp.bfloat16
F32 = jnp.float32


def _cp(n_axes):
    return pltpu.CompilerParams(dimension_semantics=("arbitrary",) * n_axes,
                                vmem_limit_bytes=VMEM_LIMIT)


def _mod_row(t, tm):
    n_prompt_tiles = NP_TOK // tm
    per_seq = DEC_SEQ // tm
    return jnp.where(t < n_prompt_tiles, DEC_BATCH, (t - n_prompt_tiles) // per_seq)


def _rms(x, g):
    return x * lax.rsqrt(jnp.mean(x * x, axis=-1, keepdims=True) + EPS) * g


def _norm_mod(x, g, shift, scale):
    return x * lax.rsqrt(jnp.mean(x * x, axis=-1, keepdims=True) + EPS) * (g * (1.0 + scale)) + shift


def _dot(a, b):
    return jnp.dot(a, b, preferred_element_type=F32)


def _dot_nt(a, b, precision=None):
    return lax.dot_general(a, b, (((1,), (1,)), ((), ())), precision=precision,
                           preferred_element_type=F32)


ADA_TN = 1536


def _ada_kernel(cc_ref, w_ref, b_ref, o_ref):
    cc = cc_ref[...]
    s = (cc / (1.0 + jnp.exp(-cc))).astype(BF)
    o_ref[0] = _dot(s, w_ref[0].astype(BF)) + b_ref[0]


def _ada(cc, w_ada, b_ada):
    n = 6 * D
    return pl.pallas_call(
        _ada_kernel,
        grid=(DEPTH, n // ADA_TN),
        in_specs=[
            pl.BlockSpec((MOD_ROWS, D), lambda l, j: (0, 0)),
            pl.BlockSpec((1, D, ADA_TN), lambda l, j: (l, 0, j)),
            pl.BlockSpec((1, 1, ADA_TN), lambda l, j: (l, 0, j)),
        ],
        out_specs=pl.BlockSpec((1, MOD_ROWS, ADA_TN), lambda l, j: (l, 0, j)),
        out_shape=jax.ShapeDtypeStruct((DEPTH, MOD_ROWS, n), F32),
        compiler_params=_cp(2),
        name="ada",
    )(cc, w_ada, b_ada.reshape(DEPTH, 1, n))


def _route(logits_t, rb):
    sc = 1.0 / (1.0 + jnp.exp(-logits_t))
    sel = sc + rb
    rows = [sel[e:e + 1, :] for e in range(N_EXPERTS)]
    srows = [sc[e:e + 1, :] for e in range(N_EXPERTS)]

    def top2sum(a, b, c, d):
        hi1, lo1 = jnp.maximum(a, b), jnp.minimum(a, b)
        hi2, lo2 = jnp.maximum(c, d), jnp.minimum(c, d)
        return jnp.maximum(hi1, hi2) + jnp.maximum(jnp.minimum(hi1, hi2), jnp.maximum(lo1, lo2))

    gs = [top2sum(*rows[EPG * g:EPG * (g + 1)]) for g in range(N_GROUPS)]
    best = gs[0]
    gidx = jnp.zeros_like(best, dtype=jnp.int32)
    for g in range(1, N_GROUPS):
        upd = gs[g] > best
        best = jnp.where(upd, gs[g], best)
        gidx = jnp.where(upd, g, gidx)

    picked = []
    for g in range(N_GROUPS):
        grp = rows[EPG * g:EPG * (g + 1)]
        in_g = gidx == g
        for j in range(EPG):
            rank = jnp.zeros_like(gidx)
            for k in range(EPG):
                if k == j:
                    continue
                ahead = grp[k] > grp[j]
                if k < j:
                    ahead = ahead | (grp[k] == grp[j])
                rank = rank + ahead.astype(jnp.int32)
            picked.append(in_g & (rank < 2))
    w = [jnp.where(picked[e], srows[e], 0.0) for e in range(N_EXPERTS)]
    wsum = w[0]
    for e in range(1, N_EXPERTS):
        wsum = wsum + w[e]
    inv = 1.0 / wsum
    wg = []
    for j in range(EPG):
        acc = w[j]
        for g in range(1, N_GROUPS):
            acc = acc + w[EPG * g + j]
        wg.append(acc * inv)
    return wg, gidx


def _ffn_prep(x_new, mod_ref, gf_ref, rwt_ref, rb_ref, h2c_ref, gidx_ref, grank_ref, cnt_ref,
              ct_scr, run_scr):
    t = pl.program_id(0)
    tm = x_new.shape[0]
    shift = mod_ref[0, :, 3 * D:4 * D]
    scale = mod_ref[0, :, 4 * D:5 * D]
    h2 = _norm_mod(x_new, gf_ref[...], shift, scale)
    logits_t = _dot_nt(rwt_ref[...], h2, precision=lax.Precision.HIGHEST)
    wg, gidx = _route(logits_t, rb_ref[...])
    ct_scr[...] = jnp.zeros_like(ct_scr)
    for j in range(EPG):
        ct_scr[j:j + 1, :] = wg[j]
    cw = ct_scr[...].T

    for k in range(Y_SUB):
        h2c_ref[pl.ds(k, tm, stride=X_SUB), :] = h2[:, k * LANES:(k + 1) * LANES]
    h2c_ref[pl.ds(Y_SUB, tm, stride=X_SUB), :] = cw
    for k in range(Y_SUB + 1, X_SUB):
        h2c_ref[pl.ds(k, tm, stride=X_SUB), :] = jnp.zeros((tm, LANES), F32)

    @pl.when(t == 0)
    def _():
        run_scr[...] = jnp.zeros_like(run_scr)

    onehot = (lax.broadcasted_iota(jnp.int32, (8, tm), 0) == gidx).astype(F32)
    earlier = (lax.broadcasted_iota(jnp.int32, (tm, tm), 0)
               < lax.broadcasted_iota(jnp.int32, (tm, tm), 1)).astype(BF)
    rank = _dot(onehot.astype(BF), earlier)
    run = run_scr[:, 0:1]
    grank = jnp.sum(onehot * (rank + run), axis=0, keepdims=True)
    gidx_ref[0] = gidx
    grank_ref[0] = grank.astype(jnp.int32)
    run_scr[...] = run_scr[...] + jnp.sum(onehot, axis=1, keepdims=True)
    cnt_ref[...] = run_scr[...]


def _prep_out_specs():
    tok = lambda w: pl.BlockSpec((TM, w), lambda t: (t, 0))
    lanes = pl.BlockSpec((1, 1, TM), lambda t: (t, 0, 0))
    rows = pl.BlockSpec((TM * X_SUB, LANES), lambda t: (t, 0))
    return [tok(D), rows, lanes, lanes, pl.BlockSpec((8, 128), lambda t: (0, 0))]


def _prep_out_shapes():
    nt = T // TM
    return [jax.ShapeDtypeStruct((T, D), F32), jax.ShapeDtypeStruct((T * X_SUB, LANES), F32),
            jax.ShapeDtypeStruct((nt, 1, TM), jnp.int32), jax.ShapeDtypeStruct((nt, 1, TM), jnp.int32),
            jax.ShapeDtypeStruct((8, 128), F32)]


EV_EXT = 3 * D_CONV + Q_RANK + KV_RANK + 3 * HEAD_PAD


def _tok_specs(parts, width):
    npt = NP_TOK // TM
    if len(parts) == 1:
        return [pl.BlockSpec((TM, width), lambda t: (t, 0))]
    return [pl.BlockSpec((TM, width), lambda t: (jnp.minimum(t, npt - 1), 0)),
            pl.BlockSpec((TM, width), lambda t: (jnp.maximum(t - npt, 0), 0))]


def _tok_load(refs):
    if len(refs) == 1:
        return refs[0][...]
    return jnp.where(pl.program_id(0) < NP_TOK // TM, refs[0][...], refs[1][...])


def _even_in_kernel(n_x, *refs):
    if n_x:
        x_refs, refs = refs[:n_x], refs[n_x:]
        x = _tok_load(x_refs)
    else:
        pend, refs, scratch = refs[:N_PENDING], refs[N_PENDING:-2], refs[-2:]
        refs, xo_ref = refs[:-1], refs[-1]
        x = _apply_pending(*pend, *scratch)
        xo_ref[...] = x
    (mod_ref, g_ref, win_ref, qg_ref, wq1_ref, wq2_ref, kg_ref, wk_ref, wv_ref, rc_ref, rs_ref,
     bg_ref, cv_ref, q_ref, k_ref, v_ref, ckv_ref, kr_ref) = refs
    shift = mod_ref[0, :, 0:D]
    scale = mod_ref[0, :, D:2 * D]
    h = _norm_mod(x, g_ref[...], shift, scale).astype(BF)
    proj = _dot(h, win_ref[...])
    o = 0
    b_g = proj[:, o:o + D_CONV]; o += D_CONV
    c_g = proj[:, o:o + D_CONV]; o += D_CONV
    v_in = proj[:, o:o + D_CONV]; o += D_CONV
    q_a = proj[:, o:o + Q_RANK]; o += Q_RANK
    kv_a = proj[:, o:o + KV_RANK]; o += KV_RANK
    kr_raw = proj[:, o:o + HEAD_PAD]; o += HEAD_PAD
    kr_cat = proj[:, o:o + HEAD_PAD]; o += HEAD_PAD
    kr_sw = proj[:, o:o + HEAD_PAD]

    bg_ref[...] = b_g.astype(BF)
    cv_ref[...] = (c_g * v_in).astype(BF)
    kr_ref[...] = kr_raw

    rc = rc_ref[...]
    rs = rs_ref[...]
    qn = _rms(q_a, qg_ref[...]).astype(BF)
    q1 = _dot(qn, wq1_ref[...])
    q2 = _dot(qn, wq2_ref[...])
    qscale = QK_DIM ** -0.5
    for hd in range(HEADS):
        sl = slice(hd * HEAD_PAD, (hd + 1) * HEAD_PAD)
        q_ref[:, sl] = ((q1[:, sl] * rc + q2[:, sl] * rs) * qscale).astype(BF)

    ckv = _rms(kv_a, kg_ref[...])
    ckv_ref[...] = ckv
    ckv_b = ckv.astype(BF)
    kk = _dot(ckv_b, wk_ref[...])
    kr = kr_cat * rc + kr_sw * rs
    for hd in range(HEADS):
        sl = slice(hd * HEAD_PAD, (hd + 1) * HEAD_PAD)
        k_ref[:, sl] = (kk[:, sl] + kr).astype(BF)
    v_ref[...] = _dot(ckv_b, wv_ref[...]).astype(BF)


def _even_in(x_parts, pending, mod, l, g_mix, win, qg, wq1, wq2, kg, wk, wv, rope_c, rope_s):
    nt = T // TM
    npt = NP_TOK // TM
    per_seq = DEC_SEQ // TM
    ident_blk = DEC_SEQ // TM

    def rope_idx(t):
        return (jnp.where(t < npt, ident_blk, (t - npt) % per_seq), 0)

    full = lambda shape: pl.BlockSpec(shape, lambda t: (0,) * len(shape))
    tok = lambda w: pl.BlockSpec((TM, w), lambda t: (t, 0))
    out_specs = [tok(D_CONV), tok(D_CONV), tok(HEADS * HEAD_PAD), tok(HEADS * HEAD_PAD),
                 tok(HEADS * V_DIM), tok(KV_RANK), tok(HEAD_PAD)]
    out_shape = [
        jax.ShapeDtypeStruct((T, D_CONV), BF), jax.ShapeDtypeStruct((T, D_CONV), BF),
        jax.ShapeDtypeStruct((T, HEADS * HEAD_PAD), BF), jax.ShapeDtypeStruct((T, HEADS * HEAD_PAD), BF),
        jax.ShapeDtypeStruct((T, HEADS * V_DIM), BF),
        jax.ShapeDtypeStruct((T, KV_RANK), F32), jax.ShapeDtypeStruct((T, HEAD_PAD), F32),
    ]
    if pending is None:
        lead_specs, lead_args, scratch = _tok_specs(x_parts, D), tuple(x_parts), []
    else:
        lead_specs, lead_args, scratch = _pending_specs(pending["layer"]), _pending_args(pending), PENDING_SCRATCH
        out_specs.append(tok(D))
        out_shape.append(jax.ShapeDtypeStruct((T, D), F32))
    return pl.pallas_call(
        functools.partial(_even_in_kernel, 0 if pending is not None else len(x_parts)),
        grid=(nt,),
        in_specs=lead_specs + [
            pl.BlockSpec((1, 1, 6 * D), lambda t: (l * MOD_ROWS + _mod_row(t, TM), 0, 0)),
            full((1, D)), full((D, EV_EXT)), full((1, Q_RANK)),
            full((Q_RANK, HEADS * HEAD_PAD)), full((Q_RANK, HEADS * HEAD_PAD)),
            full((1, KV_RANK)), full((KV_RANK, HEADS * HEAD_PAD)), full((KV_RANK, HEADS * V_DIM)),
            pl.BlockSpec((TM, HEAD_PAD), rope_idx), pl.BlockSpec((TM, HEAD_PAD), rope_idx),
        ],
        out_specs=out_specs,
        out_shape=out_shape,
        scratch_shapes=scratch,
        compiler_params=_cp(1),
        name="even_in",
    )(*lead_args, mod, g_mix, win, qg, wq1, wq2, kg, wk, wv, rope_c, rope_s)


CTX_TM = 512


def _ctx_kv_kernel(ckv_ref, kr_ref, wk_ref, wv_ref, k_ref, v_ref):
    ckv_b = ckv_ref[0].astype(BF)
    kk = _dot(ckv_b, wk_ref[0])
    kr = kr_ref[0]
    for hd in range(HEADS):
        sl = slice(hd * HEAD_PAD, (hd + 1) * HEAD_PAD)
        k_ref[0, :, sl] = (kk[:, sl] + kr).astype(BF)
    v_ref[0] = _dot(ckv_b, wv_ref[0]).astype(BF)


def _ctx_kv(ckv_all, kr_all, wk_all, wv_all):
    n_even = ckv_all.shape[0]
    rows = DEC_BATCH * PAST
    return pl.pallas_call(
        _ctx_kv_kernel,
        grid=(n_even, rows // CTX_TM),
        in_specs=[
            pl.BlockSpec((1, CTX_TM, KV_RANK), lambda i, t: (i, t, 0)),
            pl.BlockSpec((1, CTX_TM, HEAD_PAD), lambda i, t: (i, t, 0)),
            pl.BlockSpec((1, KV_RANK, HEADS * HEAD_PAD), lambda i, t: (i, 0, 0)),
            pl.BlockSpec((1, KV_RANK, HEADS * V_DIM), lambda i, t: (i, 0, 0)),
        ],
        out_specs=[
            pl.BlockSpec((1, CTX_TM, HEADS * HEAD_PAD), lambda i, t: (i, t, 0)),
            pl.BlockSpec((1, CTX_TM, HEADS * V_DIM), lambda i, t: (i, t, 0)),
        ],
        out_shape=[jax.ShapeDtypeStruct((n_even, rows, HEADS * HEAD_PAD), BF),
                   jax.ShapeDtypeStruct((n_even, rows, HEADS * V_DIM), BF)],
        compiler_params=_cp(2),
        name="ctx_kv",
    )(ckv_all, kr_all, wk_all, wv_all)


def _attn_body(n_pairs, has_ctx, q_ref, k_ref, v_ref, *rest):
    if has_ctx:
        kc_ref, vc_ref, o_ref = rest
    else:
        (o_ref,) = rest
    lane = lax.broadcasted_iota(jnp.int32, (1, 2 * V_DIM), 1)
    for pr in range(n_pairs):
        vsl = slice(pr * 2 * V_DIM, (pr + 1) * 2 * V_DIM)
        v = v_ref[:, vsl]
        outs = []
        for sub in range(2):
            hsl = slice((2 * pr + sub) * HEAD_PAD, (2 * pr + sub + 1) * HEAD_PAD)
            q = q_ref[:, hsl]
            s1 = _dot_nt(q, k_ref[:, hsl])
            m = jnp.max(s1, axis=-1, keepdims=True)
            if has_ctx:
                s2 = _dot_nt(q, kc_ref[0, :, hsl])
                m = jnp.maximum(m, jnp.max(s2, axis=-1, keepdims=True))
            p1 = jnp.exp(s1 - m)
            den = jnp.sum(p1, axis=-1, keepdims=True)
            o = _dot(p1.astype(BF), v)
            if has_ctx:
                p2 = jnp.exp(s2 - m)
                den = den + jnp.sum(p2, axis=-1, keepdims=True)
                o = o + _dot(p2.astype(BF), vc_ref[0, :, vsl])
            outs.append(o * (1.0 / den))
        o_ref[:, vsl] = jnp.where(lane < V_DIM, outs[0], outs[1]).astype(BF)


def _attn_prompt(q, k, v):
    return pl.pallas_call(
        functools.partial(_attn_body, HEADS // 2, False),
        grid=(BATCH,),
        in_specs=[
            pl.BlockSpec((SEQ, HEADS * HEAD_PAD), lambda b: (b, 0)),
            pl.BlockSpec((SEQ, HEADS * HEAD_PAD), lambda b: (b, 0)),
            pl.BlockSpec((SEQ, HEADS * V_DIM), lambda b: (b, 0)),
        ],
        out_specs=pl.BlockSpec((SEQ, HEADS * V_DIM), lambda b: (b, 0)),
        out_shape=jax.ShapeDtypeStruct((NP_TOK, HEADS * V_DIM), BF),
        compiler_params=_cp(1),
        name="attn_prompt",
    )(q, k, v)


def _attn_sample(q, k, v, kc, vc, i):
    s_blk0 = NP_TOK // DEC_SEQ
    q_blk0 = NP_TOK // TQ
    nq = DEC_SEQ // TQ
    return pl.pallas_call(
        functools.partial(_attn_body, 1, True),
        grid=(DEC_BATCH, HEADS // 2, nq),
        in_specs=[
            pl.BlockSpec((TQ, 2 * HEAD_PAD), lambda b, hp, j: (q_blk0 + b * nq + j, hp)),
            pl.BlockSpec((DEC_SEQ, 2 * HEAD_PAD), lambda b, hp, j: (s_blk0 + b, hp)),
            pl.BlockSpec((DEC_SEQ, 2 * V_DIM), lambda b, hp, j: (s_blk0 + b, hp)),
            pl.BlockSpec((1, PAST, 2 * HEAD_PAD), lambda b, hp, j: (i, b, hp)),
            pl.BlockSpec((1, PAST, 2 * V_DIM), lambda b, hp, j: (i, b, hp)),
        ],
        out_specs=pl.BlockSpec((TQ, 2 * V_DIM), lambda b, hp, j: (b * nq + j, hp)),
        out_shape=jax.ShapeDtypeStruct((NS_TOK, HEADS * V_DIM), BF),
        compiler_params=_cp(3),
        name="attn_sample",
    )(q, k, v, kc, vc)


def _even_out_kernel(n_x, *refs):
    x_refs, refs = refs[:n_x], refs[n_x:]
    (mod_ref, bg_ref, cv_ref, cvp_ref, cvn_ref, atp_ref, ats_ref, cw_ref, wo_ref, gf_ref, rwt_ref, rb_ref,
     xo_ref, h2c_ref, gidx_ref, grank_ref, cnt_ref, ct_scr, run_scr) = refs
    t = pl.program_id(0)
    npt = NP_TOK // TM
    per_seq = DEC_SEQ // TM
    cv = cv_ref[...].astype(F32)
    r = lax.broadcasted_iota(jnp.int32, (TM, 1), 0)
    is_prompt = t < npt
    tile_in_seq = (t - npt) % per_seq
    first_row = jnp.where(is_prompt, 0, jnp.where(tile_in_seq == 0, 0, -1))
    last_row = jnp.where(is_prompt, SEQ - 1, jnp.where(tile_in_seq == per_seq - 1, TM - 1, -1))
    period_mask = jnp.where(is_prompt, SEQ - 1, TM - 1)
    first = (r & period_mask) == first_row
    last = (r & period_mask) == last_row
    prev_row = cvp_ref[HALO - 1:HALO, :].astype(F32)
    next_row = cvn_ref[0:1, :].astype(F32)
    prev = jnp.where(r == 0, prev_row, pltpu.roll(cv, 1, 0))
    prev = jnp.where(first, 0.0, prev)
    nxt = jnp.where(r == TM - 1, next_row, pltpu.roll(cv, TM - 1, 0))
    nxt = jnp.where(last, 0.0, nxt)
    cw = cw_ref[...]
    conv = prev * cw[0:1, :] + cv * cw[1:2, :] + nxt * cw[2:3, :]
    yc = (bg_ref[...].astype(F32) * conv).astype(BF)
    attn = _tok_load((atp_ref, ats_ref))
    out = _dot(yc, wo_ref[0:D_CONV, :]) + _dot(attn, wo_ref[D_CONV:2 * D_CONV, :])
    gate = mod_ref[0, :, 2 * D:3 * D]
    x_new = _tok_load(x_refs) + gate * out
    xo_ref[...] = x_new
    _ffn_prep(x_new, mod_ref, gf_ref, rwt_ref, rb_ref, h2c_ref, gidx_ref, grank_ref, cnt_ref,
              ct_scr, run_scr)


def _even_out(x_parts, mod, l, bg, cv, attn_parts, conv_w, w_out, g_ffn, rwt, rb):
    nt = T // TM
    hb = TM // HALO
    nhb = T // HALO
    full = lambda shape: pl.BlockSpec(shape, lambda t: (0,) * len(shape))
    tok = lambda w: pl.BlockSpec((TM, w), lambda t: (t, 0))
    return pl.pallas_call(
        functools.partial(_even_out_kernel, len(x_parts)),
        grid=(nt,),
        in_specs=_tok_specs(x_parts, D) + [
            pl.BlockSpec((1, 1, 6 * D), lambda t: (l * MOD_ROWS + _mod_row(t, TM), 0, 0)),
            tok(D_CONV), tok(D_CONV),
            pl.BlockSpec((HALO, D_CONV), lambda t: (jnp.maximum(t * hb - 1, 0), 0)),
            pl.BlockSpec((HALO, D_CONV), lambda t: (jnp.minimum((t + 1) * hb, nhb - 1), 0)),
        ] + _tok_specs(attn_parts, HEADS * V_DIM) + [
            full((3, D_CONV)), full((2 * D_CONV, D)), full((1, D)),
            full((N_EXPERTS, D)), full((N_EXPERTS, 1)),
        ],
        out_specs=_prep_out_specs(),
        out_shape=_prep_out_shapes(),
        scratch_shapes=[pltpu.VMEM((128, TM), F32), pltpu.VMEM((8, 128), F32)],
        compiler_params=_cp(1),
        name="even_out",
    )(*x_parts, mod, bg, cv, cv, cv, *attn_parts, conv_w, w_out, g_ffn, rwt, rb)


def _odd_kernel(*refs):
    pend, refs, scratch = refs[:N_PENDING], refs[N_PENDING:-2], refs[-2:]
    (mod_ref, g_ref, win_ref, vg_ref, ws_ref, bst_ref, wo_ref, gf_ref, rwt_ref, rb_ref,
     xo_ref, h2c_ref, gidx_ref, grank_ref, cnt_ref, gated_scr, ct_scr, run_scr) = refs
    x = _apply_pending(*pend, *scratch)
    shift = mod_ref[0, :, 0:D]
    scale = mod_ref[0, :, D:2 * D]
    h = _norm_mod(x, g_ref[...], shift, scale).astype(BF)
    zl = _dot(h, win_ref[...])
    z = 0.5 * zl * (1.0 + lax.erf(zl * math.sqrt(0.5)))
    u = z[:, 0:D]
    v = _rms(z[:, D:2 * D], vg_ref[...]).astype(BF)
    n_chunks = TM // CHUNK
    gch = D // GM_GROUPS
    for g in range(GM_GROUPS):
        csl = slice(g * gch, (g + 1) * gch)
        vg = jnp.concatenate([v[n * CHUNK:(n + 1) * CHUNK, csl] for n in range(n_chunks)], axis=1)
        sg = _dot(ws_ref[g], vg) + bst_ref[:, g:g + 1]
        for n in range(n_chunks):
            rsl = slice(n * CHUNK, (n + 1) * CHUNK)
            gated_scr[rsl, csl] = (u[rsl, csl] * sg[:, n * gch:(n + 1) * gch]).astype(BF)
    out = _dot(gated_scr[...], wo_ref[...])
    gate = mod_ref[0, :, 2 * D:3 * D]
    x_new = x + gate * out
    xo_ref[...] = x_new
    _ffn_prep(x_new, mod_ref, gf_ref, rwt_ref, rb_ref, h2c_ref, gidx_ref, grank_ref, cnt_ref,
              ct_scr, run_scr)


def _odd(pending, mod, l, g_mix, win, v_g, w_s, b_st, w_out, g_ffn, rwt, rb):
    nt = T // TM
    full = lambda shape: pl.BlockSpec(shape, lambda t: (0,) * len(shape))
    return pl.pallas_call(
        _odd_kernel,
        grid=(nt,),
        in_specs=_pending_specs(pending["layer"]) + [
            pl.BlockSpec((1, 1, 6 * D), lambda t: (l * MOD_ROWS + _mod_row(t, TM), 0, 0)),
            full((1, D)), full((D, 2 * D)), full((1, D)),
            full((GM_GROUPS, CHUNK, CHUNK)), full((CHUNK, GM_GROUPS)), full((D, D)),
            full((1, D)), full((N_EXPERTS, D)), full((N_EXPERTS, 1)),
        ],
        out_specs=_prep_out_specs(),
        out_shape=_prep_out_shapes(),
        scratch_shapes=[pltpu.VMEM((TM, D), BF), pltpu.VMEM((128, TM), F32), pltpu.VMEM((8, 128), F32)]
        + PENDING_SCRATCH,
        compiler_params=_cp(1),
        name="odd",
    )(*_pending_args(pending), mod, g_mix, win, v_g, w_s, b_st, w_out, g_ffn, rwt, rb)


def _routing_plan(gidx, grank, cnt):
    counts = cnt[:N_GROUPS, 0].astype(jnp.int32)
    padded = (counts + (TMM - 1)) // TMM * TMM
    ends = jnp.cumsum(padded)
    base = ends - padded
    g = gidx.reshape(T)
    slot = grank.reshape(T)
    for k in range(N_GROUPS):
        slot = slot + jnp.where(g == k, base[k], 0)
    n_steps = ends[N_GROUPS - 1] // TMM
    starts = jnp.minimum(jnp.arange(MAX_STEPS, dtype=jnp.int32) * TMM, ends[N_GROUPS - 1] - TMM)
    group_of_step = jnp.zeros((MAX_STEPS,), jnp.int32)
    for k in range(N_GROUPS - 1):
        group_of_step = group_of_step + (starts >= ends[k]).astype(jnp.int32)
    sched = jnp.concatenate([group_of_step, n_steps[None]]).astype(jnp.int32)
    fill_plan = jnp.concatenate([base + counts, n_steps[None]]).astype(jnp.int32)
    return slot.astype(jnp.int32), sched, fill_plan


def _row_tile(ref, row, sub):
    return ref.at[pl.ds(pl.multiple_of(row * sub, sub), sub), :]


def _row_copies(n_rows, make_copy):
    def body(i, carry):
        for j in range(ISSUE_UNROLL):
            make_copy(i * ISSUE_UNROLL + j).start(priority=j % 2)
        return carry
    lax.fori_loop(0, n_rows // ISSUE_UNROLL, body, 0)


def _dispatch_kernel(pad_ref, slot_ref, h_ref, o_hbm, zero_scr, zsem, sem):
    t = pl.program_id(0)

    @pl.when(t == 0)
    def _():
        zero_scr[...] = jnp.zeros_like(zero_scr)
        fills = [pltpu.make_async_copy(
            zero_scr,
            o_hbm.at[pl.ds(pl.multiple_of(pad_ref[g] * X_SUB, X_SUB), TMM * X_SUB), :],
            zsem.at[g]) for g in range(N_GROUPS)]
        for f in fills:
            f.start()
        for f in fills:
            f.wait()
        for s in range(T // TMM, MAX_STEPS + 1):
            @pl.when(s >= pad_ref[N_GROUPS])
            def _():
                tail = pltpu.make_async_copy(
                    zero_scr, o_hbm.at[pl.ds(s * TMM * X_SUB, TMM * X_SUB), :], zsem.at[0])
                tail.start()
                tail.wait()

    _row_copies(TM, lambda r: pltpu.make_async_copy(
        _row_tile(h_ref, r, X_SUB), _row_tile(o_hbm, slot_ref[r], X_SUB), sem))
    pltpu.make_async_copy(h_ref, o_hbm.at[pl.ds(0, TM * X_SUB), :], sem).wait()


def _dispatch(pad_start, slot, h2c):
    nt = T // TM
    return pl.pallas_call(
        _dispatch_kernel,
        grid_spec=pltpu.PrefetchScalarGridSpec(
            num_scalar_prefetch=1,
            grid=(nt,),
            in_specs=[
                pl.BlockSpec((TM,), lambda t, pad: (t,), memory_space=pltpu.SMEM),
                pl.BlockSpec((TM * X_SUB, LANES), lambda t, pad: (t, 0)),
            ],
            out_specs=pl.BlockSpec(memory_space=pl.ANY),
            scratch_shapes=[pltpu.VMEM((TMM * X_SUB, LANES), F32),
                            pltpu.SemaphoreType.DMA((N_GROUPS,)), pltpu.SemaphoreType.DMA(())],
        ),
        out_shape=jax.ShapeDtypeStruct((SORTED_ROWS * X_SUB, LANES), F32),
        compiler_params=_cp(1),
        name="dispatch",
    )(pad_start, slot, h2c)


def _experts_kernel(sched_ref, xs_ref, w1_ref, w3_ref, w2_ref, o_ref, w1b, w3b, w2b):
    s = pl.program_id(0)

    @pl.when(jnp.logical_or(s == 0, sched_ref[s] != sched_ref[jnp.maximum(s - 1, 0)]))
    def _():
        w1b[...] = w1_ref[...].astype(BF)
        w3b[...] = w3_ref[...].astype(BF)
        w2b[...] = w2_ref[...].astype(BF)

    @pl.when(s < sched_ref[MAX_STEPS])
    def _():
        h = jnp.concatenate([xs_ref[pl.ds(k, TMM, stride=X_SUB), :] for k in range(Y_SUB)],
                            axis=1).astype(BF)
        cw = xs_ref[pl.ds(Y_SUB, TMM, stride=X_SUB), :]
        acc = None
        for j in range(EPG):
            a = _dot(h, w1b[j])
            b = _dot(h, w3b[j])
            hid = (a * (1.0 / (1.0 + jnp.exp(-a))) * b * cw[:, j:j + 1]).astype(BF)
            y = _dot(hid, w2b[j])
            acc = y if acc is None else acc + y
        for k in range(D // LANES):
            o_ref[pl.ds(k, TMM, stride=Y_SUB), :] = acc[:, k * LANES:(k + 1) * LANES]

    @pl.when(s >= sched_ref[MAX_STEPS])
    def _():
        o_ref[...] = jnp.zeros_like(o_ref)


def _experts(sched, xs, l, w1, w3, w2):
    row_blk = lambda s, sched: (jnp.maximum(jnp.minimum(s, sched[MAX_STEPS] - 1), 0), 0)
    grp_blk = lambda s, sched: (l, sched[s], 0, 0)
    return pl.pallas_call(
        _experts_kernel,
        grid_spec=pltpu.PrefetchScalarGridSpec(
            num_scalar_prefetch=1,
            grid=(MAX_STEPS,),
            in_specs=[
                pl.BlockSpec((TMM * X_SUB, LANES), row_blk),
                pl.BlockSpec((None, EPG, D, D_EXPERT), grp_blk),
                pl.BlockSpec((None, EPG, D, D_EXPERT), grp_blk),
                pl.BlockSpec((None, EPG, D_EXPERT, D), grp_blk),
            ],
            out_specs=pl.BlockSpec((TMM * Y_SUB, LANES), lambda s, sched: (s, 0)),
            scratch_shapes=[pltpu.VMEM((EPG, D, D_EXPERT), BF), pltpu.VMEM((EPG, D, D_EXPERT), BF),
                            pltpu.VMEM((EPG, D_EXPERT, D), BF)],
        ),
        out_shape=jax.ShapeDtypeStruct((MAX_STEPS * TMM * Y_SUB, LANES), F32),
        compiler_params=_cp(1),
        name="experts",
    )(sched, xs, w1, w3, w2)


def _pending_specs(l):
    nt = T // TM
    return [
        pl.BlockSpec((TM,), lambda t: (t,), memory_space=pltpu.SMEM),
        pl.BlockSpec((TM,), lambda t: (jnp.minimum(t + 1, nt - 1),), memory_space=pltpu.SMEM),
        pl.BlockSpec(memory_space=pl.ANY),
        pl.BlockSpec((TM, D), lambda t: (t, 0)),
        pl.BlockSpec((1, 1, 6 * D), lambda t: (l * MOD_ROWS + _mod_row(t, TM), 0, 0)),
    ]


def _pending_args(pending):
    return (pending["slot"], pending["slot"], pending["ys"], pending["x"], pending["mod"])


PENDING_SCRATCH = [pltpu.VMEM((2, TM * Y_SUB, LANES), F32), pltpu.SemaphoreType.DMA((2,))]
N_PENDING = 5


def _apply_pending(slot_ref, slot_next_ref, ys_hbm, x_ref, mod_ref, buf, sem):
    t = pl.program_id(0)
    nt = pl.num_programs(0)
    cur = t % 2

    def gather(idx_ref, b):
        _row_copies(TM, lambda r: pltpu.make_async_copy(
            _row_tile(ys_hbm, idx_ref[r], Y_SUB), _row_tile(buf.at[b], r, Y_SUB), sem.at[b]))

    @pl.when(t == 0)
    def _():
        gather(slot_ref, 0)

    @pl.when(t + 1 < nt)
    def _():
        gather(slot_next_ref, 1 - cur)

    pltpu.make_async_copy(ys_hbm.at[pl.ds(0, TM * Y_SUB), :], buf.at[cur], sem.at[cur]).wait()
    gate = mod_ref[0, :, 5 * D:6 * D]
    y = jnp.concatenate([buf[cur, pl.ds(k, TM, stride=Y_SUB), :] for k in range(Y_SUB)], axis=1)
    return x_ref[...] + gate * y


def _final_kernel(slot_ref, slot_next_ref, ys_hbm, x_ref, mod_ref, gfin_ref, op_ref, os_ref, buf, sem):
    t = pl.program_id(0)
    x_new = _apply_pending(slot_ref, slot_next_ref, ys_hbm, x_ref, mod_ref, buf, sem)
    y_out = _rms(x_new, gfin_ref[...])
    npt = NP_TOK // TM

    @pl.when(t < npt)
    def _():
        op_ref[...] = y_out

    @pl.when(t >= npt)
    def _():
        os_ref[...] = y_out


def _final(pending, g_final):
    nt = T // TM
    npt = NP_TOK // TM
    return pl.pallas_call(
        _final_kernel,
        grid=(nt,),
        in_specs=_pending_specs(pending["layer"]) + [pl.BlockSpec((1, D), lambda t: (0, 0))],
        out_specs=[pl.BlockSpec((TM, D), lambda t: (jnp.minimum(t, npt - 1), 0)),
                   pl.BlockSpec((TM, D), lambda t: (jnp.maximum(t - npt, 0), 0))],
        out_shape=[jax.ShapeDtypeStruct((NP_TOK, D), F32), jax.ShapeDtypeStruct((NS_TOK, D), F32)],
        scratch_shapes=PENDING_SCRATCH,
        compiler_params=_cp(1),
        name="final",
    )(*_pending_args(pending), g_final)


def _moe(x, mod, l, h2c, gidx, grank, cnt, w1, w3, w2):
    slot, sched, pad_start = _routing_plan(gidx, grank, cnt)
    xs = _dispatch(pad_start, slot, h2c)
    ys = _experts(sched, xs, l, w1, w3, w2)
    return {"slot": slot, "ys": ys, "x": x, "mod": mod, "layer": l}


def _rope_tables():
    pos = jnp.arange(DEC_SEQ)
    r = (pos // GRID_W).astype(F32)
    col = (pos % GRID_W).astype(F32)
    inv = ROPE_BASE ** (-jnp.arange(AX_FREQS, dtype=F32) / AX_FREQS)
    ang = jnp.stack([r[:, None] * inv, col[:, None] * inv], axis=1)
    cos = jnp.cos(ang)[:, :, None, :]
    sin = jnp.sin(ang)[:, :, None, :]
    c32 = jnp.broadcast_to(cos, (DEC_SEQ, 2, 2, AX_FREQS)).reshape(DEC_SEQ, ROPE)
    s32 = jnp.concatenate([-sin, sin], axis=2).reshape(DEC_SEQ, ROPE)
    pad = HEAD_PAD - NOPE - ROPE
    c = jnp.concatenate([jnp.ones((DEC_SEQ, NOPE), F32), c32, jnp.zeros((DEC_SEQ, pad), F32)], axis=1)
    s = jnp.concatenate([jnp.zeros((DEC_SEQ, NOPE), F32), s32, jnp.zeros((DEC_SEQ, pad), F32)], axis=1)
    c_id = jnp.concatenate([jnp.ones((TM, NOPE + ROPE), F32), jnp.zeros((TM, pad), F32)], axis=1)
    s_id = jnp.zeros((TM, HEAD_PAD), F32)
    return jnp.concatenate([c, c_id], axis=0), jnp.concatenate([s, s_id], axis=0)


def _swap_halves(w):
    lead = w.shape[:-1]
    return w.reshape(lead + (2, 2, AX_FREQS))[..., ::-1, :].reshape(lead + (ROPE,))


def _even_weights(w_in, w_q_up, w_kv_up):
    k_in = w_in.shape[0]
    base = 3 * D_CONV + Q_RANK + KV_RANK
    w_kr = w_in[:, base:base + ROPE]
    z = lambda n: jnp.zeros((k_in, n), F32)
    win_ext = jnp.concatenate([
        w_in[:, :base],
        w_kr, z(HEAD_PAD - ROPE),
        z(NOPE), w_kr, z(HEAD_PAD - NOPE - ROPE),
        z(NOPE), _swap_halves(w_kr), z(HEAD_PAD - NOPE - ROPE),
    ], axis=1).astype(BF)
    wq = w_q_up.reshape(Q_RANK, HEADS, QK_DIM)
    zq = lambda n: jnp.zeros((Q_RANK, HEADS, n), F32)
    wq1 = jnp.concatenate([wq, zq(HEAD_PAD - QK_DIM)], axis=2).reshape(Q_RANK, HEADS * HEAD_PAD).astype(BF)
    wq2 = jnp.concatenate([zq(NOPE), _swap_halves(wq[:, :, NOPE:]), zq(HEAD_PAD - QK_DIM)],
                          axis=2).reshape(Q_RANK, HEADS * HEAD_PAD).astype(BF)
    wkv = w_kv_up.reshape(KV_RANK, HEADS, NOPE + V_DIM)
    wk = jnp.concatenate([wkv[:, :, :NOPE], jnp.zeros((KV_RANK, HEADS, HEAD_PAD - NOPE), F32)],
                         axis=2).reshape(KV_RANK, HEADS * HEAD_PAD).astype(BF)
    wv = wkv[:, :, NOPE:].reshape(KV_RANK, HEADS * V_DIM).astype(BF)
    return win_ext, wq1, wq2, wk, wv


def kernel(x_prompt, x_sample, cache_ckv, cache_krope, c, c_ctx, w_ada, b_ada, g_mix, g_ffn, g_final,
           ev_w_in, conv_w, q_norm_g, w_q_up, kv_norm_g, w_kv_up, ev_w_out,
           gm_w_in, gm_v_g, gm_w_s, gm_b_s, gm_w_out, router_w, router_b, moe_w1, moe_w3, moe_w2):
    n_even = ev_w_in.shape[0]
    x_parts = (x_prompt.reshape(NP_TOK, D), x_sample.reshape(NS_TOK, D))
    cc =jnp.concatenate([c, c_ctx[None, :], jnp.zeros((MOD_ROWS - DEC_BATCH - 1, D), F32)], axis=0)
    mod = _ada(cc, w_ada, b_ada).reshape(DEPTH * MOD_ROWS, 1, 6 * D)

    rope_c, rope_s = _rope_tables()
    rwt = router_w.T
    rb = router_b.reshape(N_EXPERTS, 1)
    ev = [_even_weights(ev_w_in[i], w_q_up[i], w_kv_up[i]) for i in range(n_even)]
    kr_ctx = jnp.pad(cache_krope, ((0, 0), (0, 0), (0, 0), (NOPE, HEAD_PAD - NOPE - ROPE)))
    kr_ctx = kr_ctx.transpose(1, 0, 2, 3).reshape(n_even, DEC_BATCH * PAST, HEAD_PAD)
    ckv_ctx = cache_ckv.transpose(1, 0, 2, 3).reshape(n_even, DEC_BATCH * PAST, KV_RANK)
    kc, vc = _ctx_kv(ckv_ctx, kr_ctx, jnp.stack([e[3] for e in ev]), jnp.stack([e[4] for e in ev]))

    ckv_states, kr_states = [], []
    pending = None
    for l in range(DEPTH):
        i = l // 2
        if l % 2 == 0:
            win_ext, wq1, wq2, wk, wv = ev[i]
            outs = _even_in(
                None if pending else x_parts, pending, mod, l, g_mix[l][None, :], win_ext,
                q_norm_g[i][None, :], wq1, wq2, kv_norm_g[i][None, :], wk, wv, rope_c, rope_s)
            bg, cv, q, k, v, ckv, kr = outs[:7]
            if pending:
                x_parts = (outs[7],)
            attn_parts = (_attn_prompt(q, k, v), _attn_sample(q, k, v, kc, vc, i))
            x, h2c, gidx, grank, cnt = _even_out(x_parts, mod, l, bg, cv, attn_parts, conv_w[i],
                                                 ev_w_out[i].astype(BF), g_ffn[l][None, :], rwt, rb)
            ckv_states.append(ckv[:NP_TOK].reshape(BATCH, SEQ, KV_RANK))
            kr_states.append(kr[:NP_TOK, :ROPE].reshape(BATCH, SEQ, ROPE))
        else:
            x, h2c, gidx, grank, cnt = _odd(pending, mod, l, g_mix[l][None, :], gm_w_in[i].astype(BF),
                                            gm_v_g[i][None, :], gm_w_s[i].astype(BF), gm_b_s[i].T,
                                            gm_w_out[i].astype(BF), g_ffn[l][None, :], rwt, rb)
        pending = _moe(x, mod, l, h2c, gidx, grank, cnt, moe_w1, moe_w3, moe_w2)

    y_prompt, y_sample = _final(pending, g_final[None, :])
    y_prompt = y_prompt.reshape(BATCH, SEQ, D)
    y_sample = y_sample.reshape(DEC_BATCH, DEC_SEQ, D)
    return (y_prompt, y_sample, jnp.stack(ckv_states, axis=1), jnp.stack(kr_states, axis=1))
```

```python
import functools
import math

import jax
import jax.numpy as jnp
from jax import lax
from jax.experimental import pallas as pl
from jax.experimental.pallas import tpu as pltpu

D = 1024
BATCH, SEQ = 32, 256
DEC_BATCH, DEC_SEQ = 8, 2048
PAST = 256
DEPTH = 4
GRID_W = 64
D_CONV = 512
HEADS = 8
NOPE, ROPE, V_DIM = 64, 32, 64
QK_DIM = NOPE + ROPE
Q_RANK, KV_RANK = 384, 256
AX_FREQS = ROPE // 4
ROPE_BASE = 10000.0
CHUNK = 128
GM_GROUPS = 8
N_EXPERTS, N_GROUPS, EPG = 16, 4, 4
D_EXPERT = 256
EPS = 1e-6

NP_TOK = BATCH * SEQ
NS_TOK = DEC_BATCH * DEC_SEQ
T = NP_TOK + NS_TOK
MOD_ROWS = 16
HEAD_PAD = 128
HALO = 16

TM = 512
TMM = 512
MAX_STEPS = T // TMM + N_GROUPS
SORTED_ROWS = (MAX_STEPS + 1) * TMM
LANES = 128
Y_SUB = D // LANES
X_SUB = 2 * Y_SUB
ISSUE_UNROLL = 8
TQ = 256
ATTN_PAIRS = 4
VMEM_LIMIT = 56 * 1024 * 1024

BF = jnp.bfloat16
F32 = jnp.float32


def _cp(n_axes):
    return pltpu.CompilerParams(dimension_semantics=("arbitrary",) * n_axes,
                                vmem_limit_bytes=VMEM_LIMIT)


def _mod_row(t, tm):
    n_prompt_tiles = NP_TOK // tm
    per_seq = DEC_SEQ // tm
    return jnp.where(t < n_prompt_tiles, DEC_BATCH, (t - n_prompt_tiles) // per_seq)


def _rms(x, g):
    return x * lax.rsqrt(jnp.mean(x * x, axis=-1, keepdims=True) + EPS) * g


def _norm_mod(x, g, shift, scale):
    return x * lax.rsqrt(jnp.mean(x * x, axis=-1, keepdims=True) + EPS) * (g * (1.0 + scale)) + shift


def _dot(a, b):
    return jnp.dot(a, b, preferred_element_type=F32)


def _dot_nt(a, b, precision=None):
    return lax.dot_general(a, b, (((1,), (1,)), ((), ())), precision=precision,
                           preferred_element_type=F32)


ADA_TN = 1536


def _ada_kernel(cc_ref, w_ref, b_ref, o_ref):
    cc = cc_ref[...]
    s = (cc / (1.0 + jnp.exp(-cc))).astype(BF)
    o_ref[0] = _dot(s, w_ref[0].astype(BF)) + b_ref[0]


def _ada(cc, w_ada, b_ada):
    n = 6 * D
    return pl.pallas_call(
        _ada_kernel,
        grid=(DEPTH, n // ADA_TN),
        in_specs=[
            pl.BlockSpec((MOD_ROWS, D), lambda l, j: (0, 0)),
            pl.BlockSpec((1, D, ADA_TN), lambda l, j: (l, 0, j)),
            pl.BlockSpec((1, 1, ADA_TN), lambda l, j: (l, 0, j)),
        ],
        out_specs=pl.BlockSpec((1, MOD_ROWS, ADA_TN), lambda l, j: (l, 0, j)),
        out_shape=jax.ShapeDtypeStruct((DEPTH, MOD_ROWS, n), F32),
        compiler_params=_cp(2),
        name="ada",
    )(cc, w_ada, b_ada.reshape(DEPTH, 1, n))


def _route(logits_t, rb):
    sc = 1.0 / (1.0 + jnp.exp(-logits_t))
    sel = sc + rb
    rows = [sel[e:e + 1, :] for e in range(N_EXPERTS)]
    srows = [sc[e:e + 1, :] for e in range(N_EXPERTS)]

    def top2sum(a, b, c, d):
        hi1, lo1 = jnp.maximum(a, b), jnp.minimum(a, b)
        hi2, lo2 = jnp.maximum(c, d), jnp.minimum(c, d)
        return jnp.maximum(hi1, hi2) + jnp.maximum(jnp.minimum(hi1, hi2), jnp.maximum(lo1, lo2))

    gs = [top2sum(*rows[EPG * g:EPG * (g + 1)]) for g in range(N_GROUPS)]
    best = gs[0]
    gidx = jnp.zeros_like(best, dtype=jnp.int32)
    for g in range(1, N_GROUPS):
        upd = gs[g] > best
        best = jnp.where(upd, gs[g], best)
        gidx = jnp.where(upd, g, gidx)

    picked = []
    for g in range(N_GROUPS):
        grp = rows[EPG * g:EPG * (g + 1)]
        in_g = gidx == g
        for j in range(EPG):
            rank = jnp.zeros_like(gidx)
            for k in range(EPG):
                if k == j:
                    continue
                ahead = grp[k] > grp[j]
                if k < j:
                    ahead = ahead | (grp[k] == grp[j])
                rank = rank + ahead.astype(jnp.int32)
            picked.append(in_g & (rank < 2))
    w = [jnp.where(picked[e], srows[e], 0.0) for e in range(N_EXPERTS)]
    wsum = w[0]
    for e in range(1, N_EXPERTS):
        wsum = wsum + w[e]
    inv = 1.0 / wsum
    wg = []
    for j in range(EPG):
        acc = w[j]
        for g in range(1, N_GROUPS):
            acc = acc + w[EPG * g + j]
        wg.append(acc * inv)
    return wg, gidx


def _ffn_prep(x_new, mod_ref, gf_ref, rwt_ref, rb_ref, h2c_ref, gidx_ref, grank_ref, cnt_ref,
              ct_scr, run_scr):
    t = pl.program_id(0)
    tm = x_new.shape[0]
    shift = mod_ref[0, :, 3 * D:4 * D]
    scale = mod_ref[0, :, 4 * D:5 * D]
    h2 = _norm_mod(x_new, gf_ref[...], shift, scale)
    logits_t = _dot_nt(rwt_ref[...], h2, precision=lax.Precision.HIGHEST)
    wg, gidx = _route(logits_t, rb_ref[...])
    ct_scr[...] = jnp.zeros_like(ct_scr)
    for j in range(EPG):
        ct_scr[j:j + 1, :] = wg[j]
    cw = ct_scr[...].T

    for k in range(Y_SUB):
        h2c_ref[pl.ds(k, tm, stride=X_SUB), :] = h2[:, k * LANES:(k + 1) * LANES]
    h2c_ref[pl.ds(Y_SUB, tm, stride=X_SUB), :] = cw
    for k in range(Y_SUB + 1, X_SUB):
        h2c_ref[pl.ds(k, tm, stride=X_SUB), :] = jnp.zeros((tm, LANES), F32)

    @pl.when(t == 0)
    def _():
        run_scr[...] = jnp.zeros_like(run_scr)

    onehot = (lax.broadcasted_iota(jnp.int32, (8, tm), 0) == gidx).astype(F32)
    earlier = (lax.broadcasted_iota(jnp.int32, (tm, tm), 0)
               < lax.broadcasted_iota(jnp.int32, (tm, tm), 1)).astype(BF)
    rank = _dot(onehot.astype(BF), earlier)
    run = run_scr[:, 0:1]
    grank = jnp.sum(onehot * (rank + run), axis=0, keepdims=True)
    gidx_ref[0] = gidx
    grank_ref[0] = grank.astype(jnp.int32)
    run_scr[...] = run_scr[...] + jnp.sum(onehot, axis=1, keepdims=True)
    cnt_ref[...] = run_scr[...]


def _prep_out_specs():
    tok = lambda w: pl.BlockSpec((TM, w), lambda t: (t, 0))
    lanes = pl.BlockSpec((1, 1, TM), lambda t: (t, 0, 0))
    rows = pl.BlockSpec((TM * X_SUB, LANES), lambda t: (t, 0))
    return [tok(D), rows, lanes, lanes, pl.BlockSpec((8, 128), lambda t: (0, 0))]


def _prep_out_shapes():
    nt = T // TM
    return [jax.ShapeDtypeStruct((T, D), F32), jax.ShapeDtypeStruct((T * X_SUB, LANES), F32),
            jax.ShapeDtypeStruct((nt, 1, TM), jnp.int32), jax.ShapeDtypeStruct((nt, 1, TM), jnp.int32),
            jax.ShapeDtypeStruct((8, 128), F32)]


EV_EXT = 3 * D_CONV + Q_RANK + KV_RANK + 3 * HEAD_PAD


def _tok_specs(parts, width):
    npt = NP_TOK // TM
    if len(parts) == 1:
        return [pl.BlockSpec((TM, width), lambda t: (t, 0))]
    return [pl.BlockSpec((TM, width), lambda t: (jnp.minimum(t, npt - 1), 0)),
            pl.BlockSpec((TM, width), lambda t: (jnp.maximum(t - npt, 0), 0))]


def _tok_load(refs):
    if len(refs) == 1:
        return refs[0][...]
    return jnp.where(pl.program_id(0) < NP_TOK // TM, refs[0][...], refs[1][...])


def _even_in_kernel(n_x, *refs):
    if n_x:
        x_refs, refs = refs[:n_x], refs[n_x:]
        x = _tok_load(x_refs)
    else:
        pend, refs, scratch = refs[:N_PENDING], refs[N_PENDING:-2], refs[-2:]
        refs, xo_ref = refs[:-1], refs[-1]
        x = _apply_pending(*pend, *scratch)
        xo_ref[...] = x
    (mod_ref, g_ref, win_ref, qg_ref, wq1_ref, wq2_ref, kg_ref, wk_ref, wv_ref, rc_ref, rs_ref,
     bg_ref, cv_ref, q_ref, k_ref, v_ref, ckv_ref, kr_ref) = refs
    shift = mod_ref[0, :, 0:D]
    scale = mod_ref[0, :, D:2 * D]
    h = _norm_mod(x, g_ref[...], shift, scale).astype(BF)
    proj = _dot(h, win_ref[...])
    o = 0
    b_g = proj[:, o:o + D_CONV]; o += D_CONV
    c_g = proj[:, o:o + D_CONV]; o += D_CONV
    v_in = proj[:, o:o + D_CONV]; o += D_CONV
    q_a = proj[:, o:o + Q_RANK]; o += Q_RANK
    kv_a = proj[:, o:o + KV_RANK]; o += KV_RANK
    kr_raw = proj[:, o:o + HEAD_PAD]; o += HEAD_PAD
    kr_cat = proj[:, o:o + HEAD_PAD]; o += HEAD_PAD
    kr_sw = proj[:, o:o + HEAD_PAD]

    bg_ref[...] = b_g.astype(BF)
    cv_ref[...] = (c_g * v_in).astype(BF)
    kr_ref[...] = kr_raw

    rc = rc_ref[...]
    rs = rs_ref[...]
    qn = _rms(q_a, qg_ref[...]).astype(BF)
    q1 = _dot(qn, wq1_ref[...])
    q2 = _dot(qn, wq2_ref[...])
    qscale = QK_DIM ** -0.5
    for hd in range(HEADS):
        sl = slice(hd * HEAD_PAD, (hd + 1) * HEAD_PAD)
        q_ref[:, sl] = ((q1[:, sl] * rc + q2[:, sl] * rs) * qscale).astype(BF)

    ckv = _rms(kv_a, kg_ref[...])
    ckv_ref[...] = ckv
    ckv_b = ckv.astype(BF)
    kk = _dot(ckv_b, wk_ref[...])
    kr = kr_cat * rc + kr_sw * rs
    for hd in range(HEADS):
        sl = slice(hd * HEAD_PAD, (hd + 1) * HEAD_PAD)
        k_ref[:, sl] = (kk[:, sl] + kr).astype(BF)
    v_ref[...] = _dot(ckv_b, wv_ref[...]).astype(BF)


def _even_in(x_parts, pending, mod, l, g_mix, win, qg, wq1, wq2, kg, wk, wv, rope_c, rope_s):
    nt = T // TM
    npt = NP_TOK // TM
    per_seq = DEC_SEQ // TM
    ident_blk = DEC_SEQ // TM

    def rope_idx(t):
        return (jnp.where(t < npt, ident_blk, (t - npt) % per_seq), 0)

    full = lambda shape: pl.BlockSpec(shape, lambda t: (0,) * len(shape))
    tok = lambda w: pl.BlockSpec((TM, w), lambda t: (t, 0))
    out_specs = [tok(D_CONV), tok(D_CONV), tok(HEADS * HEAD_PAD), tok(HEADS * HEAD_PAD),
                 tok(HEADS * V_DIM), tok(KV_RANK), tok(HEAD_PAD)]
    out_shape = [
        jax.ShapeDtypeStruct((T, D_CONV), BF), jax.ShapeDtypeStruct((T, D_CONV), BF),
        jax.ShapeDtypeStruct((T, HEADS * HEAD_PAD), BF), jax.ShapeDtypeStruct((T, HEADS * HEAD_PAD), BF),
        jax.ShapeDtypeStruct((T, HEADS * V_DIM), BF),
        jax.ShapeDtypeStruct((T, KV_RANK), F32), jax.ShapeDtypeStruct((T, HEAD_PAD), F32),
    ]
    if pending is None:
        lead_specs, lead_args, scratch = _tok_specs(x_parts, D), tuple(x_parts), []
    else:
        lead_specs, lead_args, scratch = _pending_specs(pending["layer"]), _pending_args(pending), PENDING_SCRATCH
        out_specs.append(tok(D))
        out_shape.append(jax.ShapeDtypeStruct((T, D), F32))
    return pl.pallas_call(
        functools.partial(_even_in_kernel, 0 if pending is not None else len(x_parts)),
        grid=(nt,),
        in_specs=lead_specs + [
            pl.BlockSpec((1, 1, 6 * D), lambda t: (l * MOD_ROWS + _mod_row(t, TM), 0, 0)),
            full((1, D)), full((D, EV_EXT)), full((1, Q_RANK)),
            full((Q_RANK, HEADS * HEAD_PAD)), full((Q_RANK, HEADS * HEAD_PAD)),
            full((1, KV_RANK)), full((KV_RANK, HEADS * HEAD_PAD)), full((KV_RANK, HEADS * V_DIM)),
            pl.BlockSpec((TM, HEAD_PAD), rope_idx), pl.BlockSpec((TM, HEAD_PAD), rope_idx),
        ],
        out_specs=out_specs,
        out_shape=out_shape,
        scratch_shapes=scratch,
        compiler_params=_cp(1),
        name="even_in",
    )(*lead_args, mod, g_mix, win, qg, wq1, wq2, kg, wk, wv, rope_c, rope_s)


CTX_TM = 512


def _ctx_kv_kernel(ckv_ref, kr_ref, wk_ref, wv_ref, k_ref, v_ref):
    ckv_b = ckv_ref[0].astype(BF)
    kk = _dot(ckv_b, wk_ref[0])
    kr = kr_ref[0]
    for hd in range(HEADS):
        sl = slice(hd * HEAD_PAD, (hd + 1) * HEAD_PAD)
        k_ref[0, :, sl] = (kk[:, sl] + kr).astype(BF)
    v_ref[0] = _dot(ckv_b, wv_ref[0]).astype(BF)


def _ctx_kv(ckv_all, kr_all, wk_all, wv_all):
    n_even = ckv_all.shape[0]
    rows = DEC_BATCH * PAST
    return pl.pallas_call(
        _ctx_kv_kernel,
        grid=(n_even, rows // CTX_TM),
        in_specs=[
            pl.BlockSpec((1, CTX_TM, KV_RANK), lambda i, t: (i, t, 0)),
            pl.BlockSpec((1, CTX_TM, HEAD_PAD), lambda i, t: (i, t, 0)),
            pl.BlockSpec((1, KV_RANK, HEADS * HEAD_PAD), lambda i, t: (i, 0, 0)),
            pl.BlockSpec((1, KV_RANK, HEADS * V_DIM), lambda i, t: (i, 0, 0)),
        ],
        out_specs=[
            pl.BlockSpec((1, CTX_TM, HEADS * HEAD_PAD), lambda i, t: (i, t, 0)),
            pl.BlockSpec((1, CTX_TM, HEADS * V_DIM), lambda i, t: (i, t, 0)),
        ],
        out_shape=[jax.ShapeDtypeStruct((n_even, rows, HEADS * HEAD_PAD), BF),
                   jax.ShapeDtypeStruct((n_even, rows, HEADS * V_DIM), BF)],
        compiler_params=_cp(2),
        name="ctx_kv",
    )(ckv_all, kr_all, wk_all, wv_all)


def _attn_body(n_pairs, has_ctx, q_ref, k_ref, v_ref, *rest):
    if has_ctx:
        kc_ref, vc_ref, o_ref = rest
    else:
        (o_ref,) = rest
    lane = lax.broadcasted_iota(jnp.int32, (1, 2 * V_DIM), 1)
    w = 2 * V_DIM
    ones = jnp.ones((k_ref.shape[0], w), BF)
    ones_ctx = jnp.ones((PAST, w), BF)
    for pr in range(n_pairs):
        vsl = slice(pr * w, (pr + 1) * w)
        v = jnp.concatenate([v_ref[:, vsl], ones], axis=1)
        if has_ctx:
            vc = jnp.concatenate([vc_ref[0, :, vsl], ones_ctx], axis=1)
        outs = []
        for sub in range(2):
            hsl = slice((2 * pr + sub) * HEAD_PAD, (2 * pr + sub + 1) * HEAD_PAD)
            q = q_ref[:, hsl]
            s1 = _dot_nt(q, k_ref[:, hsl])
            m = jnp.max(s1, axis=-1, keepdims=True)
            if has_ctx:
                s2 = _dot_nt(q, kc_ref[0, :, hsl])
                m = jnp.maximum(m, jnp.max(s2, axis=-1, keepdims=True))
            acc = _dot(jnp.exp(s1 - m).astype(BF), v)
            if has_ctx:
                acc = acc + _dot(jnp.exp(s2 - m).astype(BF), vc)
            outs.append(acc[:, 0:w] * (1.0 / acc[:, w:2 * w]))
        o_ref[:, vsl] = jnp.where(lane < V_DIM, outs[0], outs[1]).astype(BF)


def _attn_prompt(q, k, v):
    return pl.pallas_call(
        functools.partial(_attn_body, HEADS // 2, False),
        grid=(BATCH,),
        in_specs=[
            pl.BlockSpec((SEQ, HEADS * HEAD_PAD), lambda b: (b, 0)),
            pl.BlockSpec((SEQ, HEADS * HEAD_PAD), lambda b: (b, 0)),
            pl.BlockSpec((SEQ, HEADS * V_DIM), lambda b: (b, 0)),
        ],
        out_specs=pl.BlockSpec((SEQ, HEADS * V_DIM), lambda b: (b, 0)),
        out_shape=jax.ShapeDtypeStruct((NP_TOK, HEADS * V_DIM), BF),
        compiler_params=_cp(1),
        name="attn_prompt",
    )(q, k, v)


def _attn_sample(q, k, v, kc, vc, i):
    s_blk0 = NP_TOK // DEC_SEQ
    q_blk0 = NP_TOK // TQ
    nq = DEC_SEQ // TQ
    hw = ATTN_PAIRS * 2 * HEAD_PAD
    vw = ATTN_PAIRS * 2 * V_DIM
    return pl.pallas_call(
        functools.partial(_attn_body, ATTN_PAIRS, True),
        grid=(DEC_BATCH, HEADS // (2 * ATTN_PAIRS), nq),
        in_specs=[
            pl.BlockSpec((TQ, hw), lambda b, hp, j: (q_blk0 + b * nq + j, hp)),
            pl.BlockSpec((DEC_SEQ, hw), lambda b, hp, j: (s_blk0 + b, hp)),
            pl.BlockSpec((DEC_SEQ, vw), lambda b, hp, j: (s_blk0 + b, hp)),
            pl.BlockSpec((1, PAST, hw), lambda b, hp, j: (i, b, hp)),
            pl.BlockSpec((1, PAST, vw), lambda b, hp, j: (i, b, hp)),
        ],
        out_specs=pl.BlockSpec((TQ, vw), lambda b, hp, j: (b * nq + j, hp)),
        out_shape=jax.ShapeDtypeStruct((NS_TOK, HEADS * V_DIM), BF),
        compiler_params=_cp(3),
        name="attn_sample",
    )(q, k, v, kc, vc)


def _even_out_kernel(n_x, *refs):
    x_refs, refs = refs[:n_x], refs[n_x:]
    (mod_ref, bg_ref, cv_ref, cvp_ref, cvn_ref, atp_ref, ats_ref, cw_ref, wo_ref, gf_ref, rwt_ref, rb_ref,
     xo_ref, h2c_ref, gidx_ref, grank_ref, cnt_ref, ct_scr, run_scr) = refs
    t = pl.program_id(0)
    npt = NP_TOK // TM
    per_seq = DEC_SEQ // TM
    cv = cv_ref[...].astype(F32)
    r = lax.broadcasted_iota(jnp.int32, (TM, 1), 0)
    is_prompt = t < npt
    tile_in_seq = (t - npt) % per_seq
    first_row = jnp.where(is_prompt, 0, jnp.where(tile_in_seq == 0, 0, -1))
    last_row = jnp.where(is_prompt, SEQ - 1, jnp.where(tile_in_seq == per_seq - 1, TM - 1, -1))
    period_mask = jnp.where(is_prompt, SEQ - 1, TM - 1)
    first = (r & period_mask) == first_row
    last = (r & period_mask) == last_row
    prev_row = cvp_ref[HALO - 1:HALO, :].astype(F32)
    next_row = cvn_ref[0:1, :].astype(F32)
    prev = jnp.where(r == 0, prev_row, pltpu.roll(cv, 1, 0))
    prev = jnp.where(first, 0.0, prev)
    nxt = jnp.where(r == TM - 1, next_row, pltpu.roll(cv, TM - 1, 0))
    nxt = jnp.where(last, 0.0, nxt)
    cw = cw_ref[...]
    conv = prev * cw[0:1, :] + cv * cw[1:2, :] + nxt * cw[2:3, :]
    yc = (bg_ref[...].astype(F32) * conv).astype(BF)
    attn = _tok_load((atp_ref, ats_ref))
    out = _dot(yc, wo_ref[0:D_CONV, :]) + _dot(attn, wo_ref[D_CONV:2 * D_CONV, :])
    gate = mod_ref[0, :, 2 * D:3 * D]
    x_new = _tok_load(x_refs) + gate * out
    xo_ref[...] = x_new
    _ffn_prep(x_new, mod_ref, gf_ref, rwt_ref, rb_ref, h2c_ref, gidx_ref, grank_ref, cnt_ref,
              ct_scr, run_scr)


def _even_out(x_parts, mod, l, bg, cv, attn_parts, conv_w, w_out, g_ffn, rwt, rb):
    nt = T // TM
    hb = TM // HALO
    nhb = T // HALO
    full = lambda shape: pl.BlockSpec(shape, lambda t: (0,) * len(shape))
    tok = lambda w: pl.BlockSpec((TM, w), lambda t: (t, 0))
    return pl.pallas_call(
        functools.partial(_even_out_kernel, len(x_parts)),
        grid=(nt,),
        in_specs=_tok_specs(x_parts, D) + [
            pl.BlockSpec((1, 1, 6 * D), lambda t: (l * MOD_ROWS + _mod_row(t, TM), 0, 0)),
            tok(D_CONV), tok(D_CONV),
            pl.BlockSpec((HALO, D_CONV), lambda t: (jnp.maximum(t * hb - 1, 0), 0)),
            pl.BlockSpec((HALO, D_CONV), lambda t: (jnp.minimum((t + 1) * hb, nhb - 1), 0)),
        ] + _tok_specs(attn_parts, HEADS * V_DIM) + [
            full((3, D_CONV)), full((2 * D_CONV, D)), full((1, D)),
            full((N_EXPERTS, D)), full((N_EXPERTS, 1)),
        ],
        out_specs=_prep_out_specs(),
        out_shape=_prep_out_shapes(),
        scratch_shapes=[pltpu.VMEM((128, TM), F32), pltpu.VMEM((8, 128), F32)],
        compiler_params=_cp(1),
        name="even_out",
    )(*x_parts, mod, bg, cv, cv, cv, *attn_parts, conv_w, w_out, g_ffn, rwt, rb)


def _odd_kernel(*refs):
    pend, refs, scratch = refs[:N_PENDING], refs[N_PENDING:-2], refs[-2:]
    (mod_ref, g_ref, win_ref, vg_ref, ws_ref, bst_ref, wo_ref, gf_ref, rwt_ref, rb_ref,
     xo_ref, h2c_ref, gidx_ref, grank_ref, cnt_ref, gated_scr, ct_scr, run_scr) = refs
    x = _apply_pending(*pend, *scratch)
    shift = mod_ref[0, :, 0:D]
    scale = mod_ref[0, :, D:2 * D]
    h = _norm_mod(x, g_ref[...], shift, scale).astype(BF)
    zl = _dot(h, win_ref[...])
    z = 0.5 * zl * (1.0 + lax.erf(zl * math.sqrt(0.5)))
    u = z[:, 0:D]
    v = _rms(z[:, D:2 * D], vg_ref[...]).astype(BF)
    n_chunks = TM // CHUNK
    gch = D // GM_GROUPS
    for g in range(GM_GROUPS):
        csl = slice(g * gch, (g + 1) * gch)
        vg = jnp.concatenate([v[n * CHUNK:(n + 1) * CHUNK, csl] for n in range(n_chunks)], axis=1)
        sg = _dot(ws_ref[g], vg) + bst_ref[:, g:g + 1]
        for n in range(n_chunks):
            rsl = slice(n * CHUNK, (n + 1) * CHUNK)
            gated_scr[rsl, csl] = (u[rsl, csl] * sg[:, n * gch:(n + 1) * gch]).astype(BF)
    out = _dot(gated_scr[...], wo_ref[...])
    gate = mod_ref[0, :, 2 * D:3 * D]
    x_new = x + gate * out
    xo_ref[...] = x_new
    _ffn_prep(x_new, mod_ref, gf_ref, rwt_ref, rb_ref, h2c_ref, gidx_ref, grank_ref, cnt_ref,
              ct_scr, run_scr)


def _odd(pending, mod, l, g_mix, win, v_g, w_s, b_st, w_out, g_ffn, rwt, rb):
    nt = T // TM
    full = lambda shape: pl.BlockSpec(shape, lambda t: (0,) * len(shape))
    return pl.pallas_call(
        _odd_kernel,
        grid=(nt,),
        in_specs=_pending_specs(pending["layer"]) + [
            pl.BlockSpec((1, 1, 6 * D), lambda t: (l * MOD_ROWS + _mod_row(t, TM), 0, 0)),
            full((1, D)), full((D, 2 * D)), full((1, D)),
            full((GM_GROUPS, CHUNK, CHUNK)), full((CHUNK, GM_GROUPS)), full((D, D)),
            full((1, D)), full((N_EXPERTS, D)), full((N_EXPERTS, 1)),
        ],
        out_specs=_prep_out_specs(),
        out_shape=_prep_out_shapes(),
        scratch_shapes=[pltpu.VMEM((TM, D), BF), pltpu.VMEM((128, TM), F32), pltpu.VMEM((8, 128), F32)]
        + PENDING_SCRATCH,
        compiler_params=_cp(1),
        name="odd",
    )(*_pending_args(pending), mod, g_mix, win, v_g, w_s, b_st, w_out, g_ffn, rwt, rb)


def _routing_plan(gidx, grank, cnt):
    counts = cnt[:N_GROUPS, 0].astype(jnp.int32)
    padded = (counts + (TMM - 1)) // TMM * TMM
    ends = jnp.cumsum(padded)
    base = ends - padded
    g = gidx.reshape(T)
    slot = grank.reshape(T)
    for k in range(N_GROUPS):
        slot = slot + jnp.where(g == k, base[k], 0)
    n_steps = ends[N_GROUPS - 1] // TMM
    starts = jnp.minimum(jnp.arange(MAX_STEPS, dtype=jnp.int32) * TMM, ends[N_GROUPS - 1] - TMM)
    group_of_step = jnp.zeros((MAX_STEPS,), jnp.int32)
    for k in range(N_GROUPS - 1):
        group_of_step = group_of_step + (starts >= ends[k]).astype(jnp.int32)
    sched = jnp.concatenate([group_of_step, n_steps[None]]).astype(jnp.int32)
    fill_plan = jnp.concatenate([base + counts, n_steps[None]]).astype(jnp.int32)
    return slot.astype(jnp.int32), sched, fill_plan


def _row_tile(ref, row, sub):
    return ref.at[pl.ds(pl.multiple_of(row * sub, sub), sub), :]


def _row_copies(n_rows, make_copy):
    def body(i, carry):
        for j in range(ISSUE_UNROLL):
            make_copy(i * ISSUE_UNROLL + j).start(priority=j % 2)
        return carry
    lax.fori_loop(0, n_rows // ISSUE_UNROLL, body, 0)


def _dispatch_kernel(pad_ref, slot_ref, h_ref, o_hbm, zero_scr, zsem, sem):
    t = pl.program_id(0)

    @pl.when(t == 0)
    def _():
        zero_scr[...] = jnp.zeros_like(zero_scr)
        fills = [pltpu.make_async_copy(
            zero_scr,
            o_hbm.at[pl.ds(pl.multiple_of(pad_ref[g] * X_SUB, X_SUB), TMM * X_SUB), :],
            zsem.at[g]) for g in range(N_GROUPS)]
        for f in fills:
            f.start()
        for f in fills:
            f.wait()
        for s in range(T // TMM, MAX_STEPS + 1):
            @pl.when(s >= pad_ref[N_GROUPS])
            def _():
                tail = pltpu.make_async_copy(
                    zero_scr, o_hbm.at[pl.ds(s * TMM * X_SUB, TMM * X_SUB), :], zsem.at[0])
                tail.start()
                tail.wait()

    _row_copies(TM, lambda r: pltpu.make_async_copy(
        _row_tile(h_ref, r, X_SUB), _row_tile(o_hbm, slot_ref[r], X_SUB), sem))
    pltpu.make_async_copy(h_ref, o_hbm.at[pl.ds(0, TM * X_SUB), :], sem).wait()


def _dispatch(pad_start, slot, h2c):
    nt = T // TM
    return pl.pallas_call(
        _dispatch_kernel,
        grid_spec=pltpu.PrefetchScalarGridSpec(
            num_scalar_prefetch=1,
            grid=(nt,),
            in_specs=[
                pl.BlockSpec((TM,), lambda t, pad: (t,), memory_space=pltpu.SMEM),
                pl.BlockSpec((TM * X_SUB, LANES), lambda t, pad: (t, 0)),
            ],
            out_specs=pl.BlockSpec(memory_space=pl.ANY),
            scratch_shapes=[pltpu.VMEM((TMM * X_SUB, LANES), F32),
                            pltpu.SemaphoreType.DMA((N_GROUPS,)), pltpu.SemaphoreType.DMA(())],
        ),
        out_shape=jax.ShapeDtypeStruct((SORTED_ROWS * X_SUB, LANES), F32),
        compiler_params=_cp(1),
        name="dispatch",
    )(pad_start, slot, h2c)


def _experts_kernel(sched_ref, xs_ref, w1_ref, w3_ref, w2_ref, o_ref, w1b, w3b, w2b):
    s = pl.program_id(0)

    @pl.when(jnp.logical_or(s == 0, sched_ref[s] != sched_ref[jnp.maximum(s - 1, 0)]))
    def _():
        w1b[...] = w1_ref[...].astype(BF)
        w3b[...] = w3_ref[...].astype(BF)
        w2b[...] = w2_ref[...].astype(BF)

    @pl.when(s < sched_ref[MAX_STEPS])
    def _():
        h = jnp.concatenate([xs_ref[pl.ds(k, TMM, stride=X_SUB), :] for k in range(Y_SUB)],
                            axis=1).astype(BF)
        cw = xs_ref[pl.ds(Y_SUB, TMM, stride=X_SUB), :]
        acc = None
        for j in range(EPG):
            a = _dot(h, w1b[j])
            b = _dot(h, w3b[j])
            hid = (a * (1.0 / (1.0 + jnp.exp(-a))) * b * cw[:, j:j + 1]).astype(BF)
            y = _dot(hid, w2b[j])
            acc = y if acc is None else acc + y
        for k in range(D // LANES):
            o_ref[pl.ds(k, TMM, stride=Y_SUB), :] = acc[:, k * LANES:(k + 1) * LANES]

    @pl.when(s >= sched_ref[MAX_STEPS])
    def _():
        o_ref[...] = jnp.zeros_like(o_ref)


def _experts(sched, xs, l, w1, w3, w2):
    row_blk = lambda s, sched: (jnp.maximum(jnp.minimum(s, sched[MAX_STEPS] - 1), 0), 0)
    grp_blk = lambda s, sched: (l, sched[s], 0, 0)
    return pl.pallas_call(
        _experts_kernel,
        grid_spec=pltpu.PrefetchScalarGridSpec(
            num_scalar_prefetch=1,
            grid=(MAX_STEPS,),
            in_specs=[
                pl.BlockSpec((TMM * X_SUB, LANES), row_blk),
                pl.BlockSpec((None, EPG, D, D_EXPERT), grp_blk),
                pl.BlockSpec((None, EPG, D, D_EXPERT), grp_blk),
                pl.BlockSpec((None, EPG, D_EXPERT, D), grp_blk),
            ],
            out_specs=pl.BlockSpec((TMM * Y_SUB, LANES), lambda s, sched: (s, 0)),
            scratch_shapes=[pltpu.VMEM((EPG, D, D_EXPERT), BF), pltpu.VMEM((EPG, D, D_EXPERT), BF),
                            pltpu.VMEM((EPG, D_EXPERT, D), BF)],
        ),
        out_shape=jax.ShapeDtypeStruct((MAX_STEPS * TMM * Y_SUB, LANES), F32),
        compiler_params=_cp(1),
        name="experts",
    )(sched, xs, w1, w3, w2)


def _pending_specs(l):
    nt = T // TM
    return [
        pl.BlockSpec((TM,), lambda t: (t,), memory_space=pltpu.SMEM),
        pl.BlockSpec((TM,), lambda t: (jnp.minimum(t + 1, nt - 1),), memory_space=pltpu.SMEM),
        pl.BlockSpec(memory_space=pl.ANY),
        pl.BlockSpec((TM, D), lambda t: (t, 0)),
        pl.BlockSpec((1, 1, 6 * D), lambda t: (l * MOD_ROWS + _mod_row(t, TM), 0, 0)),
    ]


def _pending_args(pending):
    return (pending["slot"], pending["slot"], pending["ys"], pending["x"], pending["mod"])


PENDING_SCRATCH = [pltpu.VMEM((2, TM * Y_SUB, LANES), F32), pltpu.SemaphoreType.DMA((2,))]
N_PENDING = 5


def _apply_pending(slot_ref, slot_next_ref, ys_hbm, x_ref, mod_ref, buf, sem):
    t = pl.program_id(0)
    nt = pl.num_programs(0)
    cur = t % 2

    def gather(idx_ref, b):
        _row_copies(TM, lambda r: pltpu.make_async_copy(
            _row_tile(ys_hbm, idx_ref[r], Y_SUB), _row_tile(buf.at[b], r, Y_SUB), sem.at[b]))

    @pl.when(t == 0)
    def _():
        gather(slot_ref, 0)

    @pl.when(t + 1 < nt)
    def _():
        gather(slot_next_ref, 1 - cur)

    pltpu.make_async_copy(ys_hbm.at[pl.ds(0, TM * Y_SUB), :], buf.at[cur], sem.at[cur]).wait()
    gate = mod_ref[0, :, 5 * D:6 * D]
    y = jnp.concatenate([buf[cur, pl.ds(k, TM, stride=Y_SUB), :] for k in range(Y_SUB)], axis=1)
    return x_ref[...] + gate * y


def _final_kernel(slot_ref, slot_next_ref, ys_hbm, x_ref, mod_ref, gfin_ref, op_ref, os_ref, buf, sem):
    t = pl.program_id(0)
    x_new = _apply_pending(slot_ref, slot_next_ref, ys_hbm, x_ref, mod_ref, buf, sem)
    y_out = _rms(x_new, gfin_ref[...])
    npt = NP_TOK // TM

    @pl.when(t < npt)
    def _():
        op_ref[...] = y_out

    @pl.when(t >= npt)
    def _():
        os_ref[...] = y_out


def _final(pending, g_final):
    nt = T // TM
    npt = NP_TOK // TM
    return pl.pallas_call(
        _final_kernel,
        grid=(nt,),
        in_specs=_pending_specs(pending["layer"]) + [pl.BlockSpec((1, D), lambda t: (0, 0))],
        out_specs=[pl.BlockSpec((TM, D), lambda t: (jnp.minimum(t, npt - 1), 0)),
                   pl.BlockSpec((TM, D), lambda t: (jnp.maximum(t - npt, 0), 0))],
        out_shape=[jax.ShapeDtypeStruct((NP_TOK, D), F32), jax.ShapeDtypeStruct((NS_TOK, D), F32)],
        scratch_shapes=PENDING_SCRATCH,
        compiler_params=_cp(1),
        name="final",
    )(*_pending_args(pending), g_final)


def _moe(x, mod, l, h2c, gidx, grank, cnt, w1, w3, w2):
    slot, sched, pad_start = _routing_plan(gidx, grank, cnt)
    xs = _dispatch(pad_start, slot, h2c)
    ys = _experts(sched, xs, l, w1, w3, w2)
    return {"slot": slot, "ys": ys, "x": x, "mod": mod, "layer": l}


def _rope_tables():
    pos = jnp.arange(DEC_SEQ)
    r = (pos // GRID_W).astype(F32)
    col = (pos % GRID_W).astype(F32)
    inv = ROPE_BASE ** (-jnp.arange(AX_FREQS, dtype=F32) / AX_FREQS)
    ang = jnp.stack([r[:, None] * inv, col[:, None] * inv], axis=1)
    cos = jnp.cos(ang)[:, :, None, :]
    sin = jnp.sin(ang)[:, :, None, :]
    c32 = jnp.broadcast_to(cos, (DEC_SEQ, 2, 2, AX_FREQS)).reshape(DEC_SEQ, ROPE)
    s32 = jnp.concatenate([-sin, sin], axis=2).reshape(DEC_SEQ, ROPE)
    pad = HEAD_PAD - NOPE - ROPE
    c = jnp.concatenate([jnp.ones((DEC_SEQ, NOPE), F32), c32, jnp.zeros((DEC_SEQ, pad), F32)], axis=1)
    s = jnp.concatenate([jnp.zeros((DEC_SEQ, NOPE), F32), s32, jnp.zeros((DEC_SEQ, pad), F32)], axis=1)
    c_id = jnp.concatenate([jnp.ones((TM, NOPE + ROPE), F32), jnp.zeros((TM, pad), F32)], axis=1)
    s_id = jnp.zeros((TM, HEAD_PAD), F32)
    return jnp.concatenate([c, c_id], axis=0), jnp.concatenate([s, s_id], axis=0)


def _swap_halves(w):
    lead = w.shape[:-1]
    return w.reshape(lead + (2, 2, AX_FREQS))[..., ::-1, :].reshape(lead + (ROPE,))


def _even_weights(w_in, w_q_up, w_kv_up):
    k_in = w_in.shape[0]
    base = 3 * D_CONV + Q_RANK + KV_RANK
    w_kr = w_in[:, base:base + ROPE]
    z = lambda n: jnp.zeros((k_in, n), F32)
    win_ext = jnp.concatenate([
        w_in[:, :base],
        w_kr, z(HEAD_PAD - ROPE),
        z(NOPE), w_kr, z(HEAD_PAD - NOPE - ROPE),
        z(NOPE), _swap_halves(w_kr), z(HEAD_PAD - NOPE - ROPE),
    ], axis=1).astype(BF)
    wq = w_q_up.reshape(Q_RANK, HEADS, QK_DIM)
    zq = lambda n: jnp.zeros((Q_RANK, HEADS, n), F32)
    wq1 = jnp.concatenate([wq, zq(HEAD_PAD - QK_DIM)], axis=2).reshape(Q_RANK, HEADS * HEAD_PAD).astype(BF)
    wq2 = jnp.concatenate([zq(NOPE), _swap_halves(wq[:, :, NOPE:]), zq(HEAD_PAD - QK_DIM)],
                          axis=2).reshape(Q_RANK, HEADS * HEAD_PAD).astype(BF)
    wkv = w_kv_up.reshape(KV_RANK, HEADS, NOPE + V_DIM)
    wk = jnp.concatenate([wkv[:, :, :NOPE], jnp.zeros((KV_RANK, HEADS, HEAD_PAD - NOPE), F32)],
                         axis=2).reshape(KV_RANK, HEADS * HEAD_PAD).astype(BF)
    wv = wkv[:, :, NOPE:].reshape(KV_RANK, HEADS * V_DIM).astype(BF)
    return win_ext, wq1, wq2, wk, wv


def kernel(x_prompt, x_sample, cache_ckv, cache_krope, c, c_ctx, w_ada, b_ada, g_mix, g_ffn, g_final,
           ev_w_in, conv_w, q_norm_g, w_q_up, kv_norm_g, w_kv_up, ev_w_out,
           gm_w_in, gm_v_g, gm_w_s, gm_b_s, gm_w_out, router_w, router_b, moe_w1, moe_w3, moe_w2):
    n_even = ev_w_in.shape[0]
    x_parts = (x_prompt.reshape(NP_TOK, D), x_sample.reshape(NS_TOK, D))
    cc =jnp.concatenate([c, c_ctx[None, :], jnp.zeros((MOD_ROWS - DEC_BATCH - 1, D), F32)], axis=0)
    mod = _ada(cc, w_ada, b_ada).reshape(DEPTH * MOD_ROWS, 1, 6 * D)

    rope_c, rope_s = _rope_tables()
    rwt = router_w.T
    rb = router_b.reshape(N_EXPERTS, 1)
    ev = [_even_weights(ev_w_in[i], w_q_up[i], w_kv_up[i]) for i in range(n_even)]
    kr_ctx = jnp.pad(cache_krope, ((0, 0), (0, 0), (0, 0), (NOPE, HEAD_PAD - NOPE - ROPE)))
    kr_ctx = kr_ctx.transpose(1, 0, 2, 3).reshape(n_even, DEC_BATCH * PAST, HEAD_PAD)
    ckv_ctx = cache_ckv.transpose(1, 0, 2, 3).reshape(n_even, DEC_BATCH * PAST, KV_RANK)
    kc, vc = _ctx_kv(ckv_ctx, kr_ctx, jnp.stack([e[3] for e in ev]), jnp.stack([e[4] for e in ev]))

    ckv_states, kr_states = [], []
    pending = None
    for l in range(DEPTH):
        i = l // 2
        if l % 2 == 0:
            win_ext, wq1, wq2, wk, wv = ev[i]
            outs = _even_in(
                None if pending else x_parts, pending, mod, l, g_mix[l][None, :], win_ext,
                q_norm_g[i][None, :], wq1, wq2, kv_norm_g[i][None, :], wk, wv, rope_c, rope_s)
            bg, cv, q, k, v, ckv, kr = outs[:7]
            if pending:
                x_parts = (outs[7],)
            attn_parts = (_attn_prompt(q, k, v), _attn_sample(q, k, v, kc, vc, i))
            x, h2c, gidx, grank, cnt = _even_out(x_parts, mod, l, bg, cv, attn_parts, conv_w[i],
                                                 ev_w_out[i].astype(BF), g_ffn[l][None, :], rwt, rb)
            ckv_states.append(ckv[:NP_TOK].reshape(BATCH, SEQ, KV_RANK))
            kr_states.append(kr[:NP_TOK, :ROPE].reshape(BATCH, SEQ, ROPE))
        else:
            x, h2c, gidx, grank, cnt = _odd(pending, mod, l, g_mix[l][None, :], gm_w_in[i].astype(BF),
                                            gm_v_g[i][None, :], gm_w_s[i].astype(BF), gm_b_s[i].T,
                                            gm_w_out[i].astype(BF), g_ffn[l][None, :], rwt, rb)
        pending = _moe(x, mod, l, h2c, gidx, grank, cnt, moe_w1, moe_w3, moe_w2)

    y_prompt, y_sample = _final(pending, g_final[None, :])
    y_prompt = y_prompt.reshape(BATCH, SEQ, D)
    y_sample = y_sample.reshape(DEC_BATCH, DEC_SEQ, D)
    return (y_prompt, y_sample, jnp.stack(ckv_states, axis=1), jnp.stack(kr_states, axis=1))
```

```python
import functools
import math

import jax
import jax.numpy as jnp
from jax import lax
from jax.experimental import pallas as pl
from jax.experimental.pallas import tpu as pltpu

D = 1024
BATCH, SEQ = 32, 256
DEC_BATCH, DEC_SEQ = 8, 2048
PAST = 256
DEPTH = 4
GRID_W = 64
D_CONV = 512
HEADS = 8
NOPE, ROPE, V_DIM = 64, 32, 64
QK_DIM = NOPE + ROPE
Q_RANK, KV_RANK = 384, 256
AX_FREQS = ROPE // 4
ROPE_BASE = 10000.0
CHUNK = 128
GM_GROUPS = 8
N_EXPERTS, N_GROUPS, EPG = 16, 4, 4
D_EXPERT = 256
EPS = 1e-6

NP_TOK = BATCH * SEQ
NS_TOK = DEC_BATCH * DEC_SEQ
T = NP_TOK + NS_TOK
MOD_ROWS = 16
HEAD_PAD = 128
HALO = 16

TM = 512
TMM = 512
MAX_STEPS = T // TMM + N_GROUPS
SORTED_ROWS = (MAX_STEPS + 1) * TMM
LANES = 128
Y_SUB = D // LANES
X_SUB = 2 * Y_SUB
ISSUE_UNROLL = 8
TQ = 256
ATTN_PAIRS = 4
VMEM_LIMIT = 56 * 1024 * 1024

BF = jnp.bfloat16
F32 = jnp.float32


def _cp(n_axes):
    return pltpu.CompilerParams(dimension_semantics=("arbitrary",) * n_axes,
                                vmem_limit_bytes=VMEM_LIMIT)


def _mod_row(t, tm):
    n_prompt_tiles = NP_TOK // tm
    per_seq = DEC_SEQ // tm
    return jnp.where(t < n_prompt_tiles, DEC_BATCH, (t - n_prompt_tiles) // per_seq)


def _rms(x, g):
    return x * lax.rsqrt(jnp.mean(x * x, axis=-1, keepdims=True) + EPS) * g


def _norm_mod(x, g, shift, scale):
    return x * lax.rsqrt(jnp.mean(x * x, axis=-1, keepdims=True) + EPS) * (g * (1.0 + scale)) + shift


def _dot(a, b):
    return jnp.dot(a, b, preferred_element_type=F32)


def _dot_nt(a, b, precision=None):
    return lax.dot_general(a, b, (((1,), (1,)), ((), ())), precision=precision,
                           preferred_element_type=F32)


ADA_TN = 1536


def _ada_kernel(cc_ref, w_ref, b_ref, o_ref):
    cc = cc_ref[...]
    s = (cc / (1.0 + jnp.exp(-cc))).astype(BF)
    o_ref[0] = _dot(s, w_ref[0].astype(BF)) + b_ref[0]


def _ada(cc, w_ada, b_ada):
    n = 6 * D
    return pl.pallas_call(
        _ada_kernel,
        grid=(DEPTH, n // ADA_TN),
        in_specs=[
            pl.BlockSpec((MOD_ROWS, D), lambda l, j: (0, 0)),
            pl.BlockSpec((1, D, ADA_TN), lambda l, j: (l, 0, j)),
            pl.BlockSpec((1, 1, ADA_TN), lambda l, j: (l, 0, j)),
        ],
        out_specs=pl.BlockSpec((1, MOD_ROWS, ADA_TN), lambda l, j: (l, 0, j)),
        out_shape=jax.ShapeDtypeStruct((DEPTH, MOD_ROWS, n), F32),
        compiler_params=_cp(2),
        name="ada",
    )(cc, w_ada, b_ada.reshape(DEPTH, 1, n))


def _route(logits_t, rb):
    sc = 1.0 / (1.0 + jnp.exp(-logits_t))
    sel = sc + rb
    rows = [sel[e:e + 1, :] for e in range(N_EXPERTS)]
    srows = [sc[e:e + 1, :] for e in range(N_EXPERTS)]

    def top2sum(a, b, c, d):
        hi1, lo1 = jnp.maximum(a, b), jnp.minimum(a, b)
        hi2, lo2 = jnp.maximum(c, d), jnp.minimum(c, d)
        return jnp.maximum(hi1, hi2) + jnp.maximum(jnp.minimum(hi1, hi2), jnp.maximum(lo1, lo2))

    gs = [top2sum(*rows[EPG * g:EPG * (g + 1)]) for g in range(N_GROUPS)]
    best = gs[0]
    gidx = jnp.zeros_like(best, dtype=jnp.int32)
    for g in range(1, N_GROUPS):
        upd = gs[g] > best
        best = jnp.where(upd, gs[g], best)
        gidx = jnp.where(upd, g, gidx)

    picked = []
    for g in range(N_GROUPS):
        grp = rows[EPG * g:EPG * (g + 1)]
        in_g = gidx == g
        for j in range(EPG):
            rank = jnp.zeros_like(gidx)
            for k in range(EPG):
                if k == j:
                    continue
                ahead = grp[k] > grp[j]
                if k < j:
                    ahead = ahead | (grp[k] == grp[j])
                rank = rank + ahead.astype(jnp.int32)
            picked.append(in_g & (rank < 2))
    w = [jnp.where(picked[e], srows[e], 0.0) for e in range(N_EXPERTS)]
    wsum = w[0]
    for e in range(1, N_EXPERTS):
        wsum = wsum + w[e]
    inv = 1.0 / wsum
    wg = []
    for j in range(EPG):
        acc = w[j]
        for g in range(1, N_GROUPS):
            acc = acc + w[EPG * g + j]
        wg.append(acc * inv)
    return wg, gidx


def _ffn_prep(x_new, mod_ref, gf_ref, rwt_ref, rb_ref, h2c_ref, gidx_ref, grank_ref, cnt_ref,
              ct_scr, run_scr):
    t = pl.program_id(0)
    tm = x_new.shape[0]
    shift = mod_ref[0, :, 3 * D:4 * D]
    scale = mod_ref[0, :, 4 * D:5 * D]
    h2 = _norm_mod(x_new, gf_ref[...], shift, scale)
    logits_t = _dot_nt(rwt_ref[...], h2, precision=lax.Precision.HIGHEST)
    wg, gidx = _route(logits_t, rb_ref[...])
    ct_scr[...] = jnp.zeros_like(ct_scr)
    for j in range(EPG):
        ct_scr[j:j + 1, :] = wg[j]
    cw = ct_scr[...].T

    for k in range(Y_SUB):
        h2c_ref[pl.ds(k, tm, stride=X_SUB), :] = h2[:, k * LANES:(k + 1) * LANES]
    h2c_ref[pl.ds(Y_SUB, tm, stride=X_SUB), :] = cw
    for k in range(Y_SUB + 1, X_SUB):
        h2c_ref[pl.ds(k, tm, stride=X_SUB), :] = jnp.zeros((tm, LANES), F32)

    @pl.when(t == 0)
    def _():
        run_scr[...] = jnp.zeros_like(run_scr)

    onehot = (lax.broadcasted_iota(jnp.int32, (8, tm), 0) == gidx).astype(F32)
    earlier = (lax.broadcasted_iota(jnp.int32, (tm, tm), 0)
               < lax.broadcasted_iota(jnp.int32, (tm, tm), 1)).astype(BF)
    rank = _dot(onehot.astype(BF), earlier)
    run = run_scr[:, 0:1]
    grank = jnp.sum(onehot * (rank + run), axis=0, keepdims=True)
    gidx_ref[0] = gidx
    grank_ref[0] = grank.astype(jnp.int32)
    run_scr[...] = run_scr[...] + jnp.sum(onehot, axis=1, keepdims=True)
    cnt_ref[...] = run_scr[...]


def _prep_out_specs():
    tok = lambda w: pl.BlockSpec((TM, w), lambda t: (t, 0))
    lanes = pl.BlockSpec((1, 1, TM), lambda t: (t, 0, 0))
    rows = pl.BlockSpec((TM * X_SUB, LANES), lambda t: (t, 0))
    return [tok(D), rows, lanes, lanes, pl.BlockSpec((8, 128), lambda t: (0, 0))]


def _prep_out_shapes():
    nt = T // TM
    return [jax.ShapeDtypeStruct((T, D), F32), jax.ShapeDtypeStruct((T * X_SUB, LANES), F32),
            jax.ShapeDtypeStruct((nt, 1, TM), jnp.int32), jax.ShapeDtypeStruct((nt, 1, TM), jnp.int32),
            jax.ShapeDtypeStruct((8, 128), F32)]


EV_EXT = 3 * D_CONV + Q_RANK + KV_RANK + 3 * HEAD_PAD


def _tok_specs(parts, width):
    npt = NP_TOK // TM
    if len(parts) == 1:
        return [pl.BlockSpec((TM, width), lambda t: (t, 0))]
    return [pl.BlockSpec((TM, width), lambda t: (jnp.minimum(t, npt - 1), 0)),
            pl.BlockSpec((TM, width), lambda t: (jnp.maximum(t - npt, 0), 0))]


def _tok_load(refs):
    if len(refs) == 1:
        return refs[0][...]
    return jnp.where(pl.program_id(0) < NP_TOK // TM, refs[0][...], refs[1][...])


def _even_in_kernel(n_x, *refs):
    if n_x:
        x_refs, refs = refs[:n_x], refs[n_x:]
        x = _tok_load(x_refs)
    else:
        pend, refs, scratch = refs[:N_PENDING], refs[N_PENDING:-2], refs[-2:]
        refs, xo_ref = refs[:-1], refs[-1]
        x = _apply_pending(*pend, *scratch)
        xo_ref[...] = x
    (mod_ref, g_ref, win_ref, qg_ref, wq1_ref, wq2_ref, kg_ref, wk_ref, wv_ref, rc_ref, rs_ref,
     bg_ref, cv_ref, q_ref, k_ref, v_ref, ckv_ref, kr_ref) = refs
    shift = mod_ref[0, :, 0:D]
    scale = mod_ref[0, :, D:2 * D]
    h = _norm_mod(x, g_ref[...], shift, scale).astype(BF)
    proj = _dot(h, win_ref[...])
    o = 0
    b_g = proj[:, o:o + D_CONV]; o += D_CONV
    c_g = proj[:, o:o + D_CONV]; o += D_CONV
    v_in = proj[:, o:o + D_CONV]; o += D_CONV
    q_a = proj[:, o:o + Q_RANK]; o += Q_RANK
    kv_a = proj[:, o:o + KV_RANK]; o += KV_RANK
    kr_raw = proj[:, o:o + HEAD_PAD]; o += HEAD_PAD
    kr_cat = proj[:, o:o + HEAD_PAD]; o += HEAD_PAD
    kr_sw = proj[:, o:o + HEAD_PAD]

    bg_ref[...] = b_g.astype(BF)
    cv_ref[...] = (c_g * v_in).astype(BF)
    kr_ref[...] = kr_raw

    rc = rc_ref[...]
    rs = rs_ref[...]
    qn = _rms(q_a, qg_ref[...]).astype(BF)
    q1 = _dot(qn, wq1_ref[...])
    q2 = _dot(qn, wq2_ref[...])
    qscale = QK_DIM ** -0.5
    for hd in range(HEADS):
        sl = slice(hd * HEAD_PAD, (hd + 1) * HEAD_PAD)
        q_ref[:, sl] = ((q1[:, sl] * rc + q2[:, sl] * rs) * qscale).astype(BF)

    ckv = _rms(kv_a, kg_ref[...])
    ckv_ref[...] = ckv
    ckv_b = ckv.astype(BF)
    kk = _dot(ckv_b, wk_ref[...])
    kr = kr_cat * rc + kr_sw * rs
    for hd in range(HEADS):
        sl = slice(hd * HEAD_PAD, (hd + 1) * HEAD_PAD)
        k_ref[:, sl] = (kk[:, sl] + kr).astype(BF)
    v_ref[...] = _dot(ckv_b, wv_ref[...]).astype(BF)


def _even_in(x_parts, pending, mod, l, g_mix, win, qg, wq1, wq2, kg, wk, wv, rope_c, rope_s):
    nt = T // TM
    npt = NP_TOK // TM
    per_seq = DEC_SEQ // TM
    ident_blk = DEC_SEQ // TM

    def rope_idx(t):
        return (jnp.where(t < npt, ident_blk, (t - npt) % per_seq), 0)

    full = lambda shape: pl.BlockSpec(shape, lambda t: (0,) * len(shape))
    tok = lambda w: pl.BlockSpec((TM, w), lambda t: (t, 0))
    out_specs = [tok(D_CONV), tok(D_CONV), tok(HEADS * HEAD_PAD), tok(HEADS * HEAD_PAD),
                 tok(HEADS * V_DIM), tok(KV_RANK), tok(HEAD_PAD)]
    out_shape = [
        jax.ShapeDtypeStruct((T, D_CONV), BF), jax.ShapeDtypeStruct((T, D_CONV), BF),
        jax.ShapeDtypeStruct((T, HEADS * HEAD_PAD), BF), jax.ShapeDtypeStruct((T, HEADS * HEAD_PAD), BF),
        jax.ShapeDtypeStruct((T, HEADS * V_DIM), BF),
        jax.ShapeDtypeStruct((T, KV_RANK), F32), jax.ShapeDtypeStruct((T, HEAD_PAD), F32),
    ]
    if pending is None:
        lead_specs, lead_args, scratch = _tok_specs(x_parts, D), tuple(x_parts), []
    else:
        lead_specs, lead_args, scratch = _pending_specs(pending["layer"]), _pending_args(pending), PENDING_SCRATCH
        out_specs.append(tok(D))
        out_shape.append(jax.ShapeDtypeStruct((T, D), F32))
    return pl.pallas_call(
        functools.partial(_even_in_kernel, 0 if pending is not None else len(x_parts)),
        grid=(nt,),
        in_specs=lead_specs + [
            pl.BlockSpec((1, 1, 6 * D), lambda t: (l * MOD_ROWS + _mod_row(t, TM), 0, 0)),
            full((1, D)), full((D, EV_EXT)), full((1, Q_RANK)),
            full((Q_RANK, HEADS * HEAD_PAD)), full((Q_RANK, HEADS * HEAD_PAD)),
            full((1, KV_RANK)), full((KV_RANK, HEADS * HEAD_PAD)), full((KV_RANK, HEADS * V_DIM)),
            pl.BlockSpec((TM, HEAD_PAD), rope_idx), pl.BlockSpec((TM, HEAD_PAD), rope_idx),
        ],
        out_specs=out_specs,
        out_shape=out_shape,
        scratch_shapes=scratch,
        compiler_params=_cp(1),
        name="even_in",
    )(*lead_args, mod, g_mix, win, qg, wq1, wq2, kg, wk, wv, rope_c, rope_s)


CTX_TM = 512


def _ctx_kv_kernel(ckv_ref, kr_ref, wk_ref, wv_ref, k_ref, v_ref):
    ckv_b = ckv_ref[0].astype(BF)
    kk = _dot(ckv_b, wk_ref[0])
    kr = kr_ref[0]
    for hd in range(HEADS):
        sl = slice(hd * HEAD_PAD, (hd + 1) * HEAD_PAD)
        k_ref[0, :, sl] = (kk[:, sl] + kr).astype(BF)
    v_ref[0] = _dot(ckv_b, wv_ref[0]).astype(BF)


def _ctx_kv(ckv_all, kr_all, wk_all, wv_all):
    n_even = ckv_all.shape[0]
    rows = DEC_BATCH * PAST
    return pl.pallas_call(
        _ctx_kv_kernel,
        grid=(n_even, rows // CTX_TM),
        in_specs=[
            pl.BlockSpec((1, CTX_TM, KV_RANK), lambda i, t: (i, t, 0)),
            pl.BlockSpec((1, CTX_TM, HEAD_PAD), lambda i, t: (i, t, 0)),
            pl.BlockSpec((1, KV_RANK, HEADS * HEAD_PAD), lambda i, t: (i, 0, 0)),
            pl.BlockSpec((1, KV_RANK, HEADS * V_DIM), lambda i, t: (i, 0, 0)),
        ],
        out_specs=[
            pl.BlockSpec((1, CTX_TM, HEADS * HEAD_PAD), lambda i, t: (i, t, 0)),
            pl.BlockSpec((1, CTX_TM, HEADS * V_DIM), lambda i, t: (i, t, 0)),
        ],
        out_shape=[jax.ShapeDtypeStruct((n_even, rows, HEADS * HEAD_PAD), BF),
                   jax.ShapeDtypeStruct((n_even, rows, HEADS * V_DIM), BF)],
        compiler_params=_cp(2),
        name="ctx_kv",
    )(ckv_all, kr_all, wk_all, wv_all)


def _attn_body(n_pairs, has_ctx, q_ref, k_ref, v_ref, *rest):
    if has_ctx:
        kc_ref, vc_ref, o_ref = rest
    else:
        (o_ref,) = rest
    lane = lax.broadcasted_iota(jnp.int32, (1, 2 * V_DIM), 1)
    w = 2 * V_DIM
    ones = jnp.ones((k_ref.shape[0], w), BF)
    ones_ctx = jnp.ones((PAST, w), BF)
    for pr in range(n_pairs):
        vsl = slice(pr * w, (pr + 1) * w)
        v = jnp.concatenate([v_ref[:, vsl], ones], axis=1)
        if has_ctx:
            vc = jnp.concatenate([vc_ref[0, :, vsl], ones_ctx], axis=1)
        outs = []
        for sub in range(2):
            hsl = slice((2 * pr + sub) * HEAD_PAD, (2 * pr + sub + 1) * HEAD_PAD)
            q = q_ref[:, hsl]
            s1 = _dot_nt(q, k_ref[:, hsl])
            m = jnp.max(s1, axis=-1, keepdims=True)
            if has_ctx:
                s2 = _dot_nt(q, kc_ref[0, :, hsl])
                m = jnp.maximum(m, jnp.max(s2, axis=-1, keepdims=True))
            acc = _dot(jnp.exp(s1 - m).astype(BF), v)
            if has_ctx:
                acc = acc + _dot(jnp.exp(s2 - m).astype(BF), vc)
            outs.append(acc[:, 0:w] * (1.0 / acc[:, w:2 * w]))
        o_ref[:, vsl] = jnp.where(lane < V_DIM, outs[0], outs[1]).astype(BF)


def _attn_prompt(q, k, v):
    return pl.pallas_call(
        functools.partial(_attn_body, HEADS // 2, False),
        grid=(BATCH,),
        in_specs=[
            pl.BlockSpec((SEQ, HEADS * HEAD_PAD), lambda b: (b, 0)),
            pl.BlockSpec((SEQ, HEADS * HEAD_PAD), lambda b: (b, 0)),
            pl.BlockSpec((SEQ, HEADS * V_DIM), lambda b: (b, 0)),
        ],
        out_specs=pl.BlockSpec((SEQ, HEADS * V_DIM), lambda b: (b, 0)),
        out_shape=jax.ShapeDtypeStruct((NP_TOK, HEADS * V_DIM), BF),
        compiler_params=_cp(1),
        name="attn_prompt",
    )(q, k, v)


def _attn_sample(q, k, v, kc, vc, i):
    s_blk0 = NP_TOK // DEC_SEQ
    q_blk0 = NP_TOK // TQ
    nq = DEC_SEQ // TQ
    hw = ATTN_PAIRS * 2 * HEAD_PAD
    vw = ATTN_PAIRS * 2 * V_DIM
    return pl.pallas_call(
        functools.partial(_attn_body, ATTN_PAIRS, True),
        grid=(DEC_BATCH, HEADS // (2 * ATTN_PAIRS), nq),
        in_specs=[
            pl.BlockSpec((TQ, hw), lambda b, hp, j: (q_blk0 + b * nq + j, hp)),
            pl.BlockSpec((DEC_SEQ, hw), lambda b, hp, j: (s_blk0 + b, hp)),
            pl.BlockSpec((DEC_SEQ, vw), lambda b, hp, j: (s_blk0 + b, hp)),
            pl.BlockSpec((1, PAST, hw), lambda b, hp, j: (i, b, hp)),
            pl.BlockSpec((1, PAST, vw), lambda b, hp, j: (i, b, hp)),
        ],
        out_specs=pl.BlockSpec((TQ, vw), lambda b, hp, j: (b * nq + j, hp)),
        out_shape=jax.ShapeDtypeStruct((NS_TOK, HEADS * V_DIM), BF),
        compiler_params=_cp(3),
        name="attn_sample",
    )(q, k, v, kc, vc)


def _even_out_kernel(n_x, *refs):
    x_refs, refs = refs[:n_x], refs[n_x:]
    (mod_ref, bg_ref, cv_ref, cvp_ref, cvn_ref, atp_ref, ats_ref, cw_ref, wo_ref, gf_ref, rwt_ref, rb_ref,
     xo_ref, h2c_ref, gidx_ref, grank_ref, cnt_ref, ct_scr, run_scr) = refs
    t = pl.program_id(0)
    npt = NP_TOK // TM
    per_seq = DEC_SEQ // TM
    cv = cv_ref[...].astype(F32)
    r = lax.broadcasted_iota(jnp.int32, (TM, 1), 0)
    is_prompt = t < npt
    tile_in_seq = (t - npt) % per_seq
    first_row = jnp.where(is_prompt, 0, jnp.where(tile_in_seq == 0, 0, -1))
    last_row = jnp.where(is_prompt, SEQ - 1, jnp.where(tile_in_seq == per_seq - 1, TM - 1, -1))
    period_mask = jnp.where(is_prompt, SEQ - 1, TM - 1)
    first = (r & period_mask) == first_row
    last = (r & period_mask) == last_row
    prev_row = cvp_ref[HALO - 1:HALO, :].astype(F32)
    next_row = cvn_ref[0:1, :].astype(F32)
    prev = jnp.where(r == 0, prev_row, pltpu.roll(cv, 1, 0))
    prev = jnp.where(first, 0.0, prev)
    nxt = jnp.where(r == TM - 1, next_row, pltpu.roll(cv, TM - 1, 0))
    nxt = jnp.where(last, 0.0, nxt)
    cw = cw_ref[...]
    conv = prev * cw[0:1, :] + cv * cw[1:2, :] + nxt * cw[2:3, :]
    yc = (bg_ref[...].astype(F32) * conv).astype(BF)
    attn = _tok_load((atp_ref, ats_ref))
    out = _dot(yc, wo_ref[0:D_CONV, :]) + _dot(attn, wo_ref[D_CONV:2 * D_CONV, :])
    gate = mod_ref[0, :, 2 * D:3 * D]
    x_new = _tok_load(x_refs) + gate * out
    xo_ref[...] = x_new
    _ffn_prep(x_new, mod_ref, gf_ref, rwt_ref, rb_ref, h2c_ref, gidx_ref, grank_ref, cnt_ref,
              ct_scr, run_scr)


def _even_out(x_parts, mod, l, bg, cv, attn_parts, conv_w, w_out, g_ffn, rwt, rb):
    nt = T // TM
    hb = TM // HALO
    nhb = T // HALO
    full = lambda shape: pl.BlockSpec(shape, lambda t: (0,) * len(shape))
    tok = lambda w: pl.BlockSpec((TM, w), lambda t: (t, 0))
    return pl.pallas_call(
        functools.partial(_even_out_kernel, len(x_parts)),
        grid=(nt,),
        in_specs=_tok_specs(x_parts, D) + [
            pl.BlockSpec((1, 1, 6 * D), lambda t: (l * MOD_ROWS + _mod_row(t, TM), 0, 0)),
            tok(D_CONV), tok(D_CONV),
            pl.BlockSpec((HALO, D_CONV), lambda t: (jnp.maximum(t * hb - 1, 0), 0)),
            pl.BlockSpec((HALO, D_CONV), lambda t: (jnp.minimum((t + 1) * hb, nhb - 1), 0)),
        ] + _tok_specs(attn_parts, HEADS * V_DIM) + [
            full((3, D_CONV)), full((2 * D_CONV, D)), full((1, D)),
            full((N_EXPERTS, D)), full((N_EXPERTS, 1)),
        ],
        out_specs=_prep_out_specs(),
        out_shape=_prep_out_shapes(),
        scratch_shapes=[pltpu.VMEM((128, TM), F32), pltpu.VMEM((8, 128), F32)],
        compiler_params=_cp(1),
        name="even_out",
    )(*x_parts, mod, bg, cv, cv, cv, *attn_parts, conv_w, w_out, g_ffn, rwt, rb)


def _odd_kernel(*refs):
    pend, refs, scratch = refs[:N_PENDING], refs[N_PENDING:-2], refs[-2:]
    (mod_ref, g_ref, win_ref, vg_ref, ws_ref, bst_ref, wo_ref, gf_ref, rwt_ref, rb_ref,
     xo_ref, h2c_ref, gidx_ref, grank_ref, cnt_ref, gated_scr, ct_scr, run_scr) = refs
    x = _apply_pending(*pend, *scratch)
    shift = mod_ref[0, :, 0:D]
    scale = mod_ref[0, :, D:2 * D]
    h = _norm_mod(x, g_ref[...], shift, scale).astype(BF)
    zl = _dot(h, win_ref[...])
    z = 0.5 * zl * (1.0 + lax.erf(zl * math.sqrt(0.5)))
    u = z[:, 0:D]
    v = _rms(z[:, D:2 * D], vg_ref[...]).astype(BF)
    n_chunks = TM // CHUNK
    gch = D // GM_GROUPS
    for g in range(GM_GROUPS):
        csl = slice(g * gch, (g + 1) * gch)
        vg = jnp.concatenate([v[n * CHUNK:(n + 1) * CHUNK, csl] for n in range(n_chunks)], axis=1)
        sg = _dot(ws_ref[g], vg) + bst_ref[:, g:g + 1]
        for n in range(n_chunks):
            rsl = slice(n * CHUNK, (n + 1) * CHUNK)
            gated_scr[rsl, csl] = (u[rsl, csl] * sg[:, n * gch:(n + 1) * gch]).astype(BF)
    out = _dot(gated_scr[...], wo_ref[...])
    gate = mod_ref[0, :, 2 * D:3 * D]
    x_new = x + gate * out
    xo_ref[...] = x_new
    _ffn_prep(x_new, mod_ref, gf_ref, rwt_ref, rb_ref, h2c_ref, gidx_ref, grank_ref, cnt_ref,
              ct_scr, run_scr)


def _odd(pending, mod, l, g_mix, win, v_g, w_s, b_st, w_out, g_ffn, rwt, rb):
    nt = T // TM
    full = lambda shape: pl.BlockSpec(shape, lambda t: (0,) * len(shape))
    return pl.pallas_call(
        _odd_kernel,
        grid=(nt,),
        in_specs=_pending_specs(pending["layer"]) + [
            pl.BlockSpec((1, 1, 6 * D), lambda t: (l * MOD_ROWS + _mod_row(t, TM), 0, 0)),
            full((1, D)), full((D, 2 * D)), full((1, D)),
            full((GM_GROUPS, CHUNK, CHUNK)), full((CHUNK, GM_GROUPS)), full((D, D)),
            full((1, D)), full((N_EXPERTS, D)), full((N_EXPERTS, 1)),
        ],
        out_specs=_prep_out_specs(),
        out_shape=_prep_out_shapes(),
        scratch_shapes=[pltpu.VMEM((TM, D), BF), pltpu.VMEM((128, TM), F32), pltpu.VMEM((8, 128), F32)]
        + PENDING_SCRATCH,
        compiler_params=_cp(1),
        name="odd",
    )(*_pending_args(pending), mod, g_mix, win, v_g, w_s, b_st, w_out, g_ffn, rwt, rb)


def _routing_plan(gidx, grank, cnt):
    counts = cnt[:N_GROUPS, 0].astype(jnp.int32)
    padded = (counts + (TMM - 1)) // TMM * TMM
    ends = jnp.cumsum(padded)
    base = ends - padded
    g = gidx.reshape(T)
    slot = grank.reshape(T)
    for k in range(N_GROUPS):
        slot = slot + jnp.where(g == k, base[k], 0)
    n_steps = ends[N_GROUPS - 1] // TMM
    starts = jnp.minimum(jnp.arange(MAX_STEPS, dtype=jnp.int32) * TMM, ends[N_GROUPS - 1] - TMM)
    group_of_step = jnp.zeros((MAX_STEPS,), jnp.int32)
    for k in range(N_GROUPS - 1):
        group_of_step = group_of_step + (starts >= ends[k]).astype(jnp.int32)
    sched = jnp.concatenate([group_of_step, n_steps[None]]).astype(jnp.int32)
    fill_plan = jnp.concatenate([base + counts, n_steps[None]]).astype(jnp.int32)
    return slot.astype(jnp.int32), sched, fill_plan


def _row_tile(ref, row, sub):
    return ref.at[pl.ds(pl.multiple_of(row * sub, sub), sub), :]


def _row_copies(n_rows, make_copy):
    def body(i, carry):
        for j in range(ISSUE_UNROLL):
            make_copy(i * ISSUE_UNROLL + j).start(priority=j % 2)
        return carry
    lax.fori_loop(0, n_rows // ISSUE_UNROLL, body, 0)


def _dispatch_kernel(pad_ref, slot_ref, h_hbm, o_hbm, zero_scr, zsem, sem):
    t = pl.program_id(0)
    nt = pl.num_programs(0)

    @pl.when(t == 0)
    def _():
        zero_scr[...] = jnp.zeros_like(zero_scr)
        fills = [pltpu.make_async_copy(
            zero_scr,
            o_hbm.at[pl.ds(pl.multiple_of(pad_ref[g] * X_SUB, X_SUB), TMM * X_SUB), :],
            zsem.at[g]) for g in range(N_GROUPS)]
        for f in fills:
            f.start()
        for f in fills:
            f.wait()
        for s in range(T // TMM, MAX_STEPS + 1):
            @pl.when(s >= pad_ref[N_GROUPS])
            def _():
                tail = pltpu.make_async_copy(
                    zero_scr, o_hbm.at[pl.ds(s * TMM * X_SUB, TMM * X_SUB), :], zsem.at[0])
                tail.start()
                tail.wait()

    def batch_wait(b):
        pltpu.make_async_copy(h_hbm.at[pl.ds(0, TM * X_SUB), :], o_hbm.at[pl.ds(0, TM * X_SUB), :],
                              sem.at[b]).wait()

    cur = t % 2
    _row_copies(TM, lambda r: pltpu.make_async_copy(
        _row_tile(h_hbm, t * TM + r, X_SUB), _row_tile(o_hbm, slot_ref[r], X_SUB), sem.at[cur]))

    @pl.when(t > 0)
    def _():
        batch_wait(1 - cur)

    @pl.when(t == nt - 1)
    def _():
        batch_wait(cur)


def _dispatch(pad_start, slot, h2c):
    nt = T // TM
    return pl.pallas_call(
        _dispatch_kernel,
        grid_spec=pltpu.PrefetchScalarGridSpec(
            num_scalar_prefetch=1,
            grid=(nt,),
            in_specs=[
                pl.BlockSpec((TM,), lambda t, pad: (t,), memory_space=pltpu.SMEM),
                pl.BlockSpec(memory_space=pl.ANY),
            ],
            out_specs=pl.BlockSpec(memory_space=pl.ANY),
            scratch_shapes=[pltpu.VMEM((TMM * X_SUB, LANES), F32),
                            pltpu.SemaphoreType.DMA((N_GROUPS,)), pltpu.SemaphoreType.DMA((2,))],
        ),
        out_shape=jax.ShapeDtypeStruct((SORTED_ROWS * X_SUB, LANES), F32),
        compiler_params=_cp(1),
        name="dispatch",
    )(pad_start, slot, h2c)


def _experts_kernel(sched_ref, xs_ref, w1_ref, w3_ref, w2_ref, o_ref, w1b, w3b, w2b):
    s = pl.program_id(0)

    @pl.when(jnp.logical_or(s == 0, sched_ref[s] != sched_ref[jnp.maximum(s - 1, 0)]))
    def _():
        w1b[...] = w1_ref[...].astype(BF)
        w3b[...] = w3_ref[...].astype(BF)
        w2b[...] = w2_ref[...].astype(BF)

    @pl.when(s < sched_ref[MAX_STEPS])
    def _():
        h = jnp.concatenate([xs_ref[pl.ds(k, TMM, stride=X_SUB), :] for k in range(Y_SUB)],
                            axis=1).astype(BF)
        cw = xs_ref[pl.ds(Y_SUB, TMM, stride=X_SUB), :]
        acc = None
        for j in range(EPG):
            a = _dot(h, w1b[j])
            b = _dot(h, w3b[j])
            hid = (a * (1.0 / (1.0 + jnp.exp(-a))) * b * cw[:, j:j + 1]).astype(BF)
            y = _dot(hid, w2b[j])
            acc = y if acc is None else acc + y
        for k in range(D // LANES):
            o_ref[pl.ds(k, TMM, stride=Y_SUB), :] = acc[:, k * LANES:(k + 1) * LANES]

    @pl.when(s >= sched_ref[MAX_STEPS])
    def _():
        o_ref[...] = jnp.zeros_like(o_ref)


def _experts(sched, xs, l, w1, w3, w2):
    row_blk = lambda s, sched: (jnp.maximum(jnp.minimum(s, sched[MAX_STEPS] - 1), 0), 0)
    grp_blk = lambda s, sched: (l, sched[s], 0, 0)
    return pl.pallas_call(
        _experts_kernel,
        grid_spec=pltpu.PrefetchScalarGridSpec(
            num_scalar_prefetch=1,
            grid=(MAX_STEPS,),
            in_specs=[
                pl.BlockSpec((TMM * X_SUB, LANES), row_blk),
                pl.BlockSpec((None, EPG, D, D_EXPERT), grp_blk),
                pl.BlockSpec((None, EPG, D, D_EXPERT), grp_blk),
                pl.BlockSpec((None, EPG, D_EXPERT, D), grp_blk),
            ],
            out_specs=pl.BlockSpec((TMM * Y_SUB, LANES), lambda s, sched: (s, 0)),
            scratch_shapes=[pltpu.VMEM((EPG, D, D_EXPERT), BF), pltpu.VMEM((EPG, D, D_EXPERT), BF),
                            pltpu.VMEM((EPG, D_EXPERT, D), BF)],
        ),
        out_shape=jax.ShapeDtypeStruct((MAX_STEPS * TMM * Y_SUB, LANES), F32),
        compiler_params=_cp(1),
        name="experts",
    )(sched, xs, w1, w3, w2)


def _pending_specs(l):
    nt = T // TM
    return [
        pl.BlockSpec((TM,), lambda t: (t,), memory_space=pltpu.SMEM),
        pl.BlockSpec((TM,), lambda t: (jnp.minimum(t + 1, nt - 1),), memory_space=pltpu.SMEM),
        pl.BlockSpec(memory_space=pl.ANY),
        pl.BlockSpec((TM, D), lambda t: (t, 0)),
        pl.BlockSpec((1, 1, 6 * D), lambda t: (l * MOD_ROWS + _mod_row(t, TM), 0, 0)),
    ]


def _pending_args(pending):
    return (pending["slot"], pending["slot"], pending["ys"], pending["x"], pending["mod"])


PENDING_SCRATCH = [pltpu.VMEM((2, TM * Y_SUB, LANES), F32), pltpu.SemaphoreType.DMA((2,))]
N_PENDING = 5


def _apply_pending(slot_ref, slot_next_ref, ys_hbm, x_ref, mod_ref, buf, sem):
    t = pl.program_id(0)
    nt = pl.num_programs(0)
    cur = t % 2

    def gather(idx_ref, b):
        _row_copies(TM, lambda r: pltpu.make_async_copy(
            _row_tile(ys_hbm, idx_ref[r], Y_SUB), _row_tile(buf.at[b], r, Y_SUB), sem.at[b]))

    @pl.when(t == 0)
    def _():
        gather(slot_ref, 0)

    @pl.when(t + 1 < nt)
    def _():
        gather(slot_next_ref, 1 - cur)

    pltpu.make_async_copy(ys_hbm.at[pl.ds(0, TM * Y_SUB), :], buf.at[cur], sem.at[cur]).wait()
    gate = mod_ref[0, :, 5 * D:6 * D]
    y = jnp.concatenate([buf[cur, pl.ds(k, TM, stride=Y_SUB), :] for k in range(Y_SUB)], axis=1)
    return x_ref[...] + gate * y


def _final_kernel(slot_ref, slot_next_ref, ys_hbm, x_ref, mod_ref, gfin_ref, op_ref, os_ref, buf, sem):
    t = pl.program_id(0)
    x_new = _apply_pending(slot_ref, slot_next_ref, ys_hbm, x_ref, mod_ref, buf, sem)
    y_out = _rms(x_new, gfin_ref[...])
    npt = NP_TOK // TM

    @pl.when(t < npt)
    def _():
        op_ref[...] = y_out

    @pl.when(t >= npt)
    def _():
        os_ref[...] = y_out


def _final(pending, g_final):
    nt = T // TM
    npt = NP_TOK // TM
    return pl.pallas_call(
        _final_kernel,
        grid=(nt,),
        in_specs=_pending_specs(pending["layer"]) + [pl.BlockSpec((1, D), lambda t: (0, 0))],
        out_specs=[pl.BlockSpec((TM, D), lambda t: (jnp.minimum(t, npt - 1), 0)),
                   pl.BlockSpec((TM, D), lambda t: (jnp.maximum(t - npt, 0), 0))],
        out_shape=[jax.ShapeDtypeStruct((NP_TOK, D), F32), jax.ShapeDtypeStruct((NS_TOK, D), F32)],
        scratch_shapes=PENDING_SCRATCH,
        compiler_params=_cp(1),
        name="final",
    )(*_pending_args(pending), g_final)


def _moe(x, mod, l, h2c, gidx, grank, cnt, w1, w3, w2):
    slot, sched, pad_start = _routing_plan(gidx, grank, cnt)
    xs = _dispatch(pad_start, slot, h2c)
    ys = _experts(sched, xs, l, w1, w3, w2)
    return {"slot": slot, "ys": ys, "x": x, "mod": mod, "layer": l}


def _rope_tables():
    pos = jnp.arange(DEC_SEQ)
    r = (pos // GRID_W).astype(F32)
    col = (pos % GRID_W).astype(F32)
    inv = ROPE_BASE ** (-jnp.arange(AX_FREQS, dtype=F32) / AX_FREQS)
    ang = jnp.stack([r[:, None] * inv, col[:, None] * inv], axis=1)
    cos = jnp.cos(ang)[:, :, None, :]
    sin = jnp.sin(ang)[:, :, None, :]
    c32 = jnp.broadcast_to(cos, (DEC_SEQ, 2, 2, AX_FREQS)).reshape(DEC_SEQ, ROPE)
    s32 = jnp.concatenate([-sin, sin], axis=2).reshape(DEC_SEQ, ROPE)
    pad = HEAD_PAD - NOPE - ROPE
    c = jnp.concatenate([jnp.ones((DEC_SEQ, NOPE), F32), c32, jnp.zeros((DEC_SEQ, pad), F32)], axis=1)
    s = jnp.concatenate([jnp.zeros((DEC_SEQ, NOPE), F32), s32, jnp.zeros((DEC_SEQ, pad), F32)], axis=1)
    c_id = jnp.concatenate([jnp.ones((TM, NOPE + ROPE), F32), jnp.zeros((TM, pad), F32)], axis=1)
    s_id = jnp.zeros((TM, HEAD_PAD), F32)
    return jnp.concatenate([c, c_id], axis=0), jnp.concatenate([s, s_id], axis=0)


def _swap_halves(w):
    lead = w.shape[:-1]
    return w.reshape(lead + (2, 2, AX_FREQS))[..., ::-1, :].reshape(lead + (ROPE,))


def _even_weights(w_in, w_q_up, w_kv_up):
    k_in = w_in.shape[0]
    base = 3 * D_CONV + Q_RANK + KV_RANK
    w_kr = w_in[:, base:base + ROPE]
    z = lambda n: jnp.zeros((k_in, n), F32)
    win_ext = jnp.concatenate([
        w_in[:, :base],
        w_kr, z(HEAD_PAD - ROPE),
        z(NOPE), w_kr, z(HEAD_PAD - NOPE - ROPE),
        z(NOPE), _swap_halves(w_kr), z(HEAD_PAD - NOPE - ROPE),
    ], axis=1).astype(BF)
    wq = w_q_up.reshape(Q_RANK, HEADS, QK_DIM)
    zq = lambda n: jnp.zeros((Q_RANK, HEADS, n), F32)
    wq1 = jnp.concatenate([wq, zq(HEAD_PAD - QK_DIM)], axis=2).reshape(Q_RANK, HEADS * HEAD_PAD).astype(BF)
    wq2 = jnp.concatenate([zq(NOPE), _swap_halves(wq[:, :, NOPE:]), zq(HEAD_PAD - QK_DIM)],
                          axis=2).reshape(Q_RANK, HEADS * HEAD_PAD).astype(BF)
    wkv = w_kv_up.reshape(KV_RANK, HEADS, NOPE + V_DIM)
    wk = jnp.concatenate([wkv[:, :, :NOPE], jnp.zeros((KV_RANK, HEADS, HEAD_PAD - NOPE), F32)],
                         axis=2).reshape(KV_RANK, HEADS * HEAD_PAD).astype(BF)
    wv = wkv[:, :, NOPE:].reshape(KV_RANK, HEADS * V_DIM).astype(BF)
    return win_ext, wq1, wq2, wk, wv


def kernel(x_prompt, x_sample, cache_ckv, cache_krope, c, c_ctx, w_ada, b_ada, g_mix, g_ffn, g_final,
           ev_w_in, conv_w, q_norm_g, w_q_up, kv_norm_g, w_kv_up, ev_w_out,
           gm_w_in, gm_v_g, gm_w_s, gm_b_s, gm_w_out, router_w, router_b, moe_w1, moe_w3, moe_w2):
    n_even = ev_w_in.shape[0]
    x_parts = (x_prompt.reshape(NP_TOK, D), x_sample.reshape(NS_TOK, D))
    cc =jnp.concatenate([c, c_ctx[None, :], jnp.zeros((MOD_ROWS - DEC_BATCH - 1, D), F32)], axis=0)
    mod = _ada(cc, w_ada, b_ada).reshape(DEPTH * MOD_ROWS, 1, 6 * D)

    rope_c, rope_s = _rope_tables()
    rwt = router_w.T
    rb = router_b.reshape(N_EXPERTS, 1)
    ev = [_even_weights(ev_w_in[i], w_q_up[i], w_kv_up[i]) for i in range(n_even)]
    kr_ctx = jnp.pad(cache_krope, ((0, 0), (0, 0), (0, 0), (NOPE, HEAD_PAD - NOPE - ROPE)))
    kr_ctx = kr_ctx.transpose(1, 0, 2, 3).reshape(n_even, DEC_BATCH * PAST, HEAD_PAD)
    ckv_ctx = cache_ckv.transpose(1, 0, 2, 3).reshape(n_even, DEC_BATCH * PAST, KV_RANK)
    kc, vc = _ctx_kv(ckv_ctx, kr_ctx, jnp.stack([e[3] for e in ev]), jnp.stack([e[4] for e in ev]))

    ckv_states, kr_states = [], []
    pending = None
    for l in range(DEPTH):
        i = l // 2
        if l % 2 == 0:
            win_ext, wq1, wq2, wk, wv = ev[i]
            outs = _even_in(
                None if pending else x_parts, pending, mod, l, g_mix[l][None, :], win_ext,
                q_norm_g[i][None, :], wq1, wq2, kv_norm_g[i][None, :], wk, wv, rope_c, rope_s)
            bg, cv, q, k, v, ckv, kr = outs[:7]
            if pending:
                x_parts = (outs[7],)
            attn_parts = (_attn_prompt(q, k, v), _attn_sample(q, k, v, kc, vc, i))
            x, h2c, gidx, grank, cnt = _even_out(x_parts, mod, l, bg, cv, attn_parts, conv_w[i],
                                                 ev_w_out[i].astype(BF), g_ffn[l][None, :], rwt, rb)
            ckv_states.append(ckv[:NP_TOK].reshape(BATCH, SEQ, KV_RANK))
            kr_states.append(kr[:NP_TOK, :ROPE].reshape(BATCH, SEQ, ROPE))
        else:
            x, h2c, gidx, grank, cnt = _odd(pending, mod, l, g_mix[l][None, :], gm_w_in[i].astype(BF),
                                            gm_v_g[i][None, :], gm_w_s[i].astype(BF), gm_b_s[i].T,
                                            gm_w_out[i].astype(BF), g_ffn[l][None, :], rwt, rb)
        pending = _moe(x, mod, l, h2c, gidx, grank, cnt, moe_w1, moe_w3, moe_w2)

    y_prompt, y_sample = _final(pending, g_final[None, :])
    y_prompt = y_prompt.reshape(BATCH, SEQ, D)
    y_sample = y_sample.reshape(DEC_BATCH, DEC_SEQ, D)
    return (y_prompt, y_sample, jnp.stack(ckv_states, axis=1), jnp.stack(kr_states, axis=1))
```

```python
import functools
import math

import jax
import jax.numpy as jnp
from jax import lax
from jax.experimental import pallas as pl
from jax.experimental.pallas import tpu as pltpu

D = 1024
BATCH, SEQ = 32, 256
DEC_BATCH, DEC_SEQ = 8, 2048
PAST = 256
DEPTH = 4
GRID_W = 64
D_CONV = 512
HEADS = 8
NOPE, ROPE, V_DIM = 64, 32, 64
QK_DIM = NOPE + ROPE
Q_RANK, KV_RANK = 384, 256
AX_FREQS = ROPE // 4
ROPE_BASE = 10000.0
CHUNK = 128
GM_GROUPS = 8
N_EXPERTS, N_GROUPS, EPG = 16, 4, 4
D_EXPERT = 256
EPS = 1e-6

NP_TOK = BATCH * SEQ
NS_TOK = DEC_BATCH * DEC_SEQ
T = NP_TOK + NS_TOK
MOD_ROWS = 16
HEAD_PAD = 128
HALO = 16

TM = 512
PAIRS = ((0, 1), (0, 2), (0, 3), (1, 2), (1, 3), (2, 3))
NB = N_GROUPS * len(PAIRS)
NB_PAD = 32
TMM = 512
FILL = 128
MAX_STEPS = T // TMM + NB
SORTED_ROWS = (MAX_STEPS + 1) * TMM
LANES = 128
Y_SUB = D // LANES
X_SUB = 2 * Y_SUB
ISSUE_UNROLL = 8
TQ = 256
ATTN_PAIRS = 4
VMEM_LIMIT = 56 * 1024 * 1024

BF = jnp.bfloat16
F32 = jnp.float32


def _cp(n_axes):
    return pltpu.CompilerParams(dimension_semantics=("arbitrary",) * n_axes,
                                vmem_limit_bytes=VMEM_LIMIT)


def _mod_row(t, tm):
    n_prompt_tiles = NP_TOK // tm
    per_seq = DEC_SEQ // tm
    return jnp.where(t < n_prompt_tiles, DEC_BATCH, (t - n_prompt_tiles) // per_seq)


def _rms(x, g):
    return x * lax.rsqrt(jnp.mean(x * x, axis=-1, keepdims=True) + EPS) * g


def _norm_mod(x, g, shift, scale):
    return x * lax.rsqrt(jnp.mean(x * x, axis=-1, keepdims=True) + EPS) * (g * (1.0 + scale)) + shift


def _dot(a, b):
    return jnp.dot(a, b, preferred_element_type=F32)


def _dot_nt(a, b, precision=None):
    return lax.dot_general(a, b, (((1,), (1,)), ((), ())), precision=precision,
                           preferred_element_type=F32)


ADA_TN = 1536


def _ada_kernel(cc_ref, w_ref, b_ref, o_ref):
    cc = cc_ref[...]
    s = (cc / (1.0 + jnp.exp(-cc))).astype(BF)
    o_ref[0] = _dot(s, w_ref[0].astype(BF)) + b_ref[0]


def _ada(cc, w_ada, b_ada):
    n = 6 * D
    return pl.pallas_call(
        _ada_kernel,
        grid=(DEPTH, n // ADA_TN),
        in_specs=[
            pl.BlockSpec((MOD_ROWS, D), lambda l, j: (0, 0)),
            pl.BlockSpec((1, D, ADA_TN), lambda l, j: (l, 0, j)),
            pl.BlockSpec((1, 1, ADA_TN), lambda l, j: (l, 0, j)),
        ],
        out_specs=pl.BlockSpec((1, MOD_ROWS, ADA_TN), lambda l, j: (l, 0, j)),
        out_shape=jax.ShapeDtypeStruct((DEPTH, MOD_ROWS, n), F32),
        compiler_params=_cp(2),
        name="ada",
    )(cc, w_ada, b_ada.reshape(DEPTH, 1, n))


def _route(logits_t, rb):
    sc = 1.0 / (1.0 + jnp.exp(-logits_t))
    sel = sc + rb
    rows = [sel[e:e + 1, :] for e in range(N_EXPERTS)]
    srows = [sc[e:e + 1, :] for e in range(N_EXPERTS)]

    def top2sum(a, b, c, d):
        hi1, lo1 = jnp.maximum(a, b), jnp.minimum(a, b)
        hi2, lo2 = jnp.maximum(c, d), jnp.minimum(c, d)
        return jnp.maximum(hi1, hi2) + jnp.maximum(jnp.minimum(hi1, hi2), jnp.maximum(lo1, lo2))

    gs = [top2sum(*rows[EPG * g:EPG * (g + 1)]) for g in range(N_GROUPS)]
    best = gs[0]
    gidx = jnp.zeros_like(best, dtype=jnp.int32)
    for g in range(1, N_GROUPS):
        upd = gs[g] > best
        best = jnp.where(upd, gs[g], best)
        gidx = jnp.where(upd, g, gidx)

    picked = []
    for g in range(N_GROUPS):
        grp = rows[EPG * g:EPG * (g + 1)]
        in_g = gidx == g
        for j in range(EPG):
            rank = jnp.zeros_like(gidx)
            for k in range(EPG):
                if k == j:
                    continue
                ahead = grp[k] > grp[j]
                if k < j:
                    ahead = ahead | (grp[k] == grp[j])
                rank = rank + ahead.astype(jnp.int32)
            picked.append(in_g & (rank < 2))
    w = [jnp.where(picked[e], srows[e], 0.0) for e in range(N_EXPERTS)]
    wsum = w[0]
    for e in range(1, N_EXPERTS):
        wsum = wsum + w[e]
    inv = 1.0 / wsum
    pj, wj = [], []
    for j in range(EPG):
        hit, acc = picked[j], w[j]
        for g in range(1, N_GROUPS):
            hit = hit | picked[EPG * g + j]
            acc = acc + w[EPG * g + j]
        pj.append(hit)
        wj.append(acc * inv)
    pair = jnp.full_like(gidx, len(PAIRS) - 1)
    for idx in range(len(PAIRS) - 2, -1, -1):
        a, b = PAIRS[idx]
        pair = jnp.where(pj[a] & pj[b], idx, pair)
    w_a = jnp.where(pj[0], wj[0], jnp.where(pj[1], wj[1], wj[2]))
    w_b = jnp.where(pj[3], wj[3], jnp.where(pj[2], wj[2], wj[1]))
    return (w_a, w_b), gidx * len(PAIRS) + pair


def _ffn_prep(x_new, mod_ref, gf_ref, rwt_ref, rb_ref, h2c_ref, gidx_ref, grank_ref, cnt_ref,
              ct_scr, run_scr):
    t = pl.program_id(0)
    tm = x_new.shape[0]
    shift = mod_ref[0, :, 3 * D:4 * D]
    scale = mod_ref[0, :, 4 * D:5 * D]
    h2 = _norm_mod(x_new, gf_ref[...], shift, scale)
    logits_t = _dot_nt(rwt_ref[...], h2, precision=lax.Precision.HIGHEST)
    wab, gidx = _route(logits_t, rb_ref[...])
    ct_scr[...] = jnp.zeros_like(ct_scr)
    for j in range(2):
        ct_scr[j:j + 1, :] = wab[j]
    cw = ct_scr[...].T

    for k in range(Y_SUB):
        h2c_ref[pl.ds(k, tm, stride=X_SUB), :] = h2[:, k * LANES:(k + 1) * LANES]
    h2c_ref[pl.ds(Y_SUB, tm, stride=X_SUB), :] = cw
    for k in range(Y_SUB + 1, X_SUB):
        h2c_ref[pl.ds(k, tm, stride=X_SUB), :] = jnp.zeros((tm, LANES), F32)

    @pl.when(t == 0)
    def _():
        run_scr[...] = jnp.zeros_like(run_scr)

    onehot = (lax.broadcasted_iota(jnp.int32, (NB_PAD, tm), 0) == gidx).astype(F32)
    earlier = (lax.broadcasted_iota(jnp.int32, (tm, tm), 0)
               < lax.broadcasted_iota(jnp.int32, (tm, tm), 1)).astype(BF)
    rank = _dot(onehot.astype(BF), earlier)
    run = run_scr[:, 0:1]
    grank = jnp.sum(onehot * (rank + run), axis=0, keepdims=True)
    gidx_ref[0] = gidx
    grank_ref[0] = grank.astype(jnp.int32)
    run_scr[...] = run_scr[...] + jnp.sum(onehot, axis=1, keepdims=True)
    cnt_ref[...] = run_scr[...]


def _prep_out_specs():
    tok = lambda w: pl.BlockSpec((TM, w), lambda t: (t, 0))
    lanes = pl.BlockSpec((1, 1, TM), lambda t: (t, 0, 0))
    rows = pl.BlockSpec((TM * X_SUB, LANES), lambda t: (t, 0))
    return [tok(D), rows, lanes, lanes, pl.BlockSpec((NB_PAD, 128), lambda t: (0, 0))]


def _prep_out_shapes():
    nt = T // TM
    return [jax.ShapeDtypeStruct((T, D), F32), jax.ShapeDtypeStruct((T * X_SUB, LANES), F32),
            jax.ShapeDtypeStruct((nt, 1, TM), jnp.int32), jax.ShapeDtypeStruct((nt, 1, TM), jnp.int32),
            jax.ShapeDtypeStruct((NB_PAD, 128), F32)]


EV_EXT = 3 * D_CONV + Q_RANK + KV_RANK + 3 * HEAD_PAD


def _tok_specs(parts, width):
    npt = NP_TOK // TM
    if len(parts) == 1:
        return [pl.BlockSpec((TM, width), lambda t: (t, 0))]
    return [pl.BlockSpec((TM, width), lambda t: (jnp.minimum(t, npt - 1), 0)),
            pl.BlockSpec((TM, width), lambda t: (jnp.maximum(t - npt, 0), 0))]


def _tok_load(refs):
    if len(refs) == 1:
        return refs[0][...]
    return jnp.where(pl.program_id(0) < NP_TOK // TM, refs[0][...], refs[1][...])


def _even_in_kernel(n_x, *refs):
    if n_x:
        x_refs, refs = refs[:n_x], refs[n_x:]
        x = _tok_load(x_refs)
    else:
        pend, refs, scratch = refs[:N_PENDING], refs[N_PENDING:-2], refs[-2:]
        refs, xo_ref = refs[:-1], refs[-1]
        x = _apply_pending(*pend, *scratch)
        xo_ref[...] = x
    (mod_ref, g_ref, win_ref, qg_ref, wq1_ref, wq2_ref, kg_ref, wk_ref, wv_ref, rc_ref, rs_ref,
     bg_ref, cv_ref, q_ref, k_ref, v_ref, ckv_ref, kr_ref) = refs
    shift = mod_ref[0, :, 0:D]
    scale = mod_ref[0, :, D:2 * D]
    h = _norm_mod(x, g_ref[...], shift, scale).astype(BF)
    proj = _dot(h, win_ref[...])
    o = 0
    b_g = proj[:, o:o + D_CONV]; o += D_CONV
    c_g = proj[:, o:o + D_CONV]; o += D_CONV
    v_in = proj[:, o:o + D_CONV]; o += D_CONV
    q_a = proj[:, o:o + Q_RANK]; o += Q_RANK
    kv_a = proj[:, o:o + KV_RANK]; o += KV_RANK
    kr_raw = proj[:, o:o + HEAD_PAD]; o += HEAD_PAD
    kr_cat = proj[:, o:o + HEAD_PAD]; o += HEAD_PAD
    kr_sw = proj[:, o:o + HEAD_PAD]

    bg_ref[...] = b_g.astype(BF)
    cv_ref[...] = (c_g * v_in).astype(BF)
    kr_ref[...] = kr_raw

    rc = rc_ref[...]
    rs = rs_ref[...]
    qn = _rms(q_a, qg_ref[...]).astype(BF)
    q1 = _dot(qn, wq1_ref[...])
    q2 = _dot(qn, wq2_ref[...])
    qscale = QK_DIM ** -0.5
    for hd in range(HEADS):
        sl = slice(hd * HEAD_PAD, (hd + 1) * HEAD_PAD)
        q_ref[:, sl] = ((q1[:, sl] * rc + q2[:, sl] * rs) * qscale).astype(BF)

    ckv = _rms(kv_a, kg_ref[...])
    ckv_ref[...] = ckv
    ckv_b = ckv.astype(BF)
    kk = _dot(ckv_b, wk_ref[...])
    kr = kr_cat * rc + kr_sw * rs
    for hd in range(HEADS):
        sl = slice(hd * HEAD_PAD, (hd + 1) * HEAD_PAD)
        k_ref[:, sl] = (kk[:, sl] + kr).astype(BF)
    v_ref[...] = _dot(ckv_b, wv_ref[...]).astype(BF)


def _even_in(x_parts, pending, mod, l, g_mix, win, qg, wq1, wq2, kg, wk, wv, rope_c, rope_s):
    nt = T // TM
    npt = NP_TOK // TM
    per_seq = DEC_SEQ // TM
    ident_blk = DEC_SEQ // TM

    def rope_idx(t):
        return (jnp.where(t < npt, ident_blk, (t - npt) % per_seq), 0)

    full = lambda shape: pl.BlockSpec(shape, lambda t: (0,) * len(shape))
    tok = lambda w: pl.BlockSpec((TM, w), lambda t: (t, 0))
    out_specs = [tok(D_CONV), tok(D_CONV), tok(HEADS * HEAD_PAD), tok(HEADS * HEAD_PAD),
                 tok(HEADS * V_DIM), tok(KV_RANK), tok(HEAD_PAD)]
    out_shape = [
        jax.ShapeDtypeStruct((T, D_CONV), BF), jax.ShapeDtypeStruct((T, D_CONV), BF),
        jax.ShapeDtypeStruct((T, HEADS * HEAD_PAD), BF), jax.ShapeDtypeStruct((T, HEADS * HEAD_PAD), BF),
        jax.ShapeDtypeStruct((T, HEADS * V_DIM), BF),
        jax.ShapeDtypeStruct((T, KV_RANK), F32), jax.ShapeDtypeStruct((T, HEAD_PAD), F32),
    ]
    if pending is None:
        lead_specs, lead_args, scratch = _tok_specs(x_parts, D), tuple(x_parts), []
    else:
        lead_specs, lead_args, scratch = _pending_specs(pending["layer"]), _pending_args(pending), PENDING_SCRATCH
        out_specs.append(tok(D))
        out_shape.append(jax.ShapeDtypeStruct((T, D), F32))
    return pl.pallas_call(
        functools.partial(_even_in_kernel, 0 if pending is not None else len(x_parts)),
        grid=(nt,),
        in_specs=lead_specs + [
            pl.BlockSpec((1, 1, 6 * D), lambda t: (l * MOD_ROWS + _mod_row(t, TM), 0, 0)),
            full((1, D)), full((D, EV_EXT)), full((1, Q_RANK)),
            full((Q_RANK, HEADS * HEAD_PAD)), full((Q_RANK, HEADS * HEAD_PAD)),
            full((1, KV_RANK)), full((KV_RANK, HEADS * HEAD_PAD)), full((KV_RANK, HEADS * V_DIM)),
            pl.BlockSpec((TM, HEAD_PAD), rope_idx), pl.BlockSpec((TM, HEAD_PAD), rope_idx),
        ],
        out_specs=out_specs,
        out_shape=out_shape,
        scratch_shapes=scratch,
        compiler_params=_cp(1),
        name="even_in",
    )(*lead_args, mod, g_mix, win, qg, wq1, wq2, kg, wk, wv, rope_c, rope_s)


CTX_TM = 512


def _ctx_kv_kernel(ckv_ref, kr_ref, wk_ref, wv_ref, k_ref, v_ref):
    ckv_b = ckv_ref[0].astype(BF)
    kk = _dot(ckv_b, wk_ref[0])
    kr = kr_ref[0]
    for hd in range(HEADS):
        sl = slice(hd * HEAD_PAD, (hd + 1) * HEAD_PAD)
        k_ref[0, :, sl] = (kk[:, sl] + kr).astype(BF)
    v_ref[0] = _dot(ckv_b, wv_ref[0]).astype(BF)


def _ctx_kv(ckv_all, kr_all, wk_all, wv_all):
    n_even = ckv_all.shape[0]
    rows = DEC_BATCH * PAST
    return pl.pallas_call(
        _ctx_kv_kernel,
        grid=(n_even, rows // CTX_TM),
        in_specs=[
            pl.BlockSpec((1, CTX_TM, KV_RANK), lambda i, t: (i, t, 0)),
            pl.BlockSpec((1, CTX_TM, HEAD_PAD), lambda i, t: (i, t, 0)),
            pl.BlockSpec((1, KV_RANK, HEADS * HEAD_PAD), lambda i, t: (i, 0, 0)),
            pl.BlockSpec((1, KV_RANK, HEADS * V_DIM), lambda i, t: (i, 0, 0)),
        ],
        out_specs=[
            pl.BlockSpec((1, CTX_TM, HEADS * HEAD_PAD), lambda i, t: (i, t, 0)),
            pl.BlockSpec((1, CTX_TM, HEADS * V_DIM), lambda i, t: (i, t, 0)),
        ],
        out_shape=[jax.ShapeDtypeStruct((n_even, rows, HEADS * HEAD_PAD), BF),
                   jax.ShapeDtypeStruct((n_even, rows, HEADS * V_DIM), BF)],
        compiler_params=_cp(2),
        name="ctx_kv",
    )(ckv_all, kr_all, wk_all, wv_all)


def _attn_body(n_pairs, has_ctx, q_ref, k_ref, v_ref, *rest):
    if has_ctx:
        kc_ref, vc_ref, o_ref = rest
    else:
        (o_ref,) = rest
    lane = lax.broadcasted_iota(jnp.int32, (1, 2 * V_DIM), 1)
    w = 2 * V_DIM
    ones = jnp.ones((k_ref.shape[0], w), BF)
    ones_ctx = jnp.ones((PAST, w), BF)
    for pr in range(n_pairs):
        vsl = slice(pr * w, (pr + 1) * w)
        v = jnp.concatenate([v_ref[:, vsl], ones], axis=1)
        if has_ctx:
            vc = jnp.concatenate([vc_ref[0, :, vsl], ones_ctx], axis=1)
        outs = []
        for sub in range(2):
            hsl = slice((2 * pr + sub) * HEAD_PAD, (2 * pr + sub + 1) * HEAD_PAD)
            q = q_ref[:, hsl]
            s1 = _dot_nt(q, k_ref[:, hsl])
            m = jnp.max(s1, axis=-1, keepdims=True)
            if has_ctx:
                s2 = _dot_nt(q, kc_ref[0, :, hsl])
                m = jnp.maximum(m, jnp.max(s2, axis=-1, keepdims=True))
            acc = _dot(jnp.exp(s1 - m).astype(BF), v)
            if has_ctx:
                acc = acc + _dot(jnp.exp(s2 - m).astype(BF), vc)
            outs.append(acc[:, 0:w] * (1.0 / acc[:, w:2 * w]))
        o_ref[:, vsl] = jnp.where(lane < V_DIM, outs[0], outs[1]).astype(BF)


def _attn_prompt(q, k, v):
    return pl.pallas_call(
        functools.partial(_attn_body, HEADS // 2, False),
        grid=(BATCH,),
        in_specs=[
            pl.BlockSpec((SEQ, HEADS * HEAD_PAD), lambda b: (b, 0)),
            pl.BlockSpec((SEQ, HEADS * HEAD_PAD), lambda b: (b, 0)),
            pl.BlockSpec((SEQ, HEADS * V_DIM), lambda b: (b, 0)),
        ],
        out_specs=pl.BlockSpec((SEQ, HEADS * V_DIM), lambda b: (b, 0)),
        out_shape=jax.ShapeDtypeStruct((NP_TOK, HEADS * V_DIM), BF),
        compiler_params=_cp(1),
        name="attn_prompt",
    )(q, k, v)


def _attn_sample(q, k, v, kc, vc, i):
    s_blk0 = NP_TOK // DEC_SEQ
    q_blk0 = NP_TOK // TQ
    nq = DEC_SEQ // TQ
    hw = ATTN_PAIRS * 2 * HEAD_PAD
    vw = ATTN_PAIRS * 2 * V_DIM
    return pl.pallas_call(
        functools.partial(_attn_body, ATTN_PAIRS, True),
        grid=(DEC_BATCH, HEADS // (2 * ATTN_PAIRS), nq),
        in_specs=[
            pl.BlockSpec((TQ, hw), lambda b, hp, j: (q_blk0 + b * nq + j, hp)),
            pl.BlockSpec((DEC_SEQ, hw), lambda b, hp, j: (s_blk0 + b, hp)),
            pl.BlockSpec((DEC_SEQ, vw), lambda b, hp, j: (s_blk0 + b, hp)),
            pl.BlockSpec((1, PAST, hw), lambda b, hp, j: (i, b, hp)),
            pl.BlockSpec((1, PAST, vw), lambda b, hp, j: (i, b, hp)),
        ],
        out_specs=pl.BlockSpec((TQ, vw), lambda b, hp, j: (b * nq + j, hp)),
        out_shape=jax.ShapeDtypeStruct((NS_TOK, HEADS * V_DIM), BF),
        compiler_params=_cp(3),
        name="attn_sample",
    )(q, k, v, kc, vc)


def _even_out_kernel(n_x, *refs):
    x_refs, refs = refs[:n_x], refs[n_x:]
    (mod_ref, bg_ref, cv_ref, cvp_ref, cvn_ref, atp_ref, ats_ref, cw_ref, wo_ref, gf_ref, rwt_ref, rb_ref,
     xo_ref, h2c_ref, gidx_ref, grank_ref, cnt_ref, ct_scr, run_scr) = refs
    t = pl.program_id(0)
    npt = NP_TOK // TM
    per_seq = DEC_SEQ // TM
    cv = cv_ref[...].astype(F32)
    r = lax.broadcasted_iota(jnp.int32, (TM, 1), 0)
    is_prompt = t < npt
    tile_in_seq = (t - npt) % per_seq
    first_row = jnp.where(is_prompt, 0, jnp.where(tile_in_seq == 0, 0, -1))
    last_row = jnp.where(is_prompt, SEQ - 1, jnp.where(tile_in_seq == per_seq - 1, TM - 1, -1))
    period_mask = jnp.where(is_prompt, SEQ - 1, TM - 1)
    first = (r & period_mask) == first_row
    last = (r & period_mask) == last_row
    prev_row = cvp_ref[HALO - 1:HALO, :].astype(F32)
    next_row = cvn_ref[0:1, :].astype(F32)
    prev = jnp.where(r == 0, prev_row, pltpu.roll(cv, 1, 0))
    prev = jnp.where(first, 0.0, prev)
    nxt = jnp.where(r == TM - 1, next_row, pltpu.roll(cv, TM - 1, 0))
    nxt = jnp.where(last, 0.0, nxt)
    cw = cw_ref[...]
    conv = prev * cw[0:1, :] + cv * cw[1:2, :] + nxt * cw[2:3, :]
    yc = (bg_ref[...].astype(F32) * conv).astype(BF)
    attn = _tok_load((atp_ref, ats_ref))
    out = _dot(yc, wo_ref[0:D_CONV, :]) + _dot(attn, wo_ref[D_CONV:2 * D_CONV, :])
    gate = mod_ref[0, :, 2 * D:3 * D]
    x_new = _tok_load(x_refs) + gate * out
    xo_ref[...] = x_new
    _ffn_prep(x_new, mod_ref, gf_ref, rwt_ref, rb_ref, h2c_ref, gidx_ref, grank_ref, cnt_ref,
              ct_scr, run_scr)


def _even_out(x_parts, mod, l, bg, cv, attn_parts, conv_w, w_out, g_ffn, rwt, rb):
    nt = T // TM
    hb = TM // HALO
    nhb = T // HALO
    full = lambda shape: pl.BlockSpec(shape, lambda t: (0,) * len(shape))
    tok = lambda w: pl.BlockSpec((TM, w), lambda t: (t, 0))
    return pl.pallas_call(
        functools.partial(_even_out_kernel, len(x_parts)),
        grid=(nt,),
        in_specs=_tok_specs(x_parts, D) + [
            pl.BlockSpec((1, 1, 6 * D), lambda t: (l * MOD_ROWS + _mod_row(t, TM), 0, 0)),
            tok(D_CONV), tok(D_CONV),
            pl.BlockSpec((HALO, D_CONV), lambda t: (jnp.maximum(t * hb - 1, 0), 0)),
            pl.BlockSpec((HALO, D_CONV), lambda t: (jnp.minimum((t + 1) * hb, nhb - 1), 0)),
        ] + _tok_specs(attn_parts, HEADS * V_DIM) + [
            full((3, D_CONV)), full((2 * D_CONV, D)), full((1, D)),
            full((N_EXPERTS, D)), full((N_EXPERTS, 1)),
        ],
        out_specs=_prep_out_specs(),
        out_shape=_prep_out_shapes(),
        scratch_shapes=[pltpu.VMEM((128, TM), F32), pltpu.VMEM((NB_PAD, 128), F32)],
        compiler_params=_cp(1),
        name="even_out",
    )(*x_parts, mod, bg, cv, cv, cv, *attn_parts, conv_w, w_out, g_ffn, rwt, rb)


def _odd_kernel(*refs):
    pend, refs, scratch = refs[:N_PENDING], refs[N_PENDING:-2], refs[-2:]
    (mod_ref, g_ref, win_ref, vg_ref, ws_ref, bst_ref, wo_ref, gf_ref, rwt_ref, rb_ref,
     xo_ref, h2c_ref, gidx_ref, grank_ref, cnt_ref, gated_scr, ct_scr, run_scr) = refs
    x = _apply_pending(*pend, *scratch)
    shift = mod_ref[0, :, 0:D]
    scale = mod_ref[0, :, D:2 * D]
    h = _norm_mod(x, g_ref[...], shift, scale).astype(BF)
    zl = _dot(h, win_ref[...])
    z = 0.5 * zl * (1.0 + lax.erf(zl * math.sqrt(0.5)))
    u = z[:, 0:D]
    v = _rms(z[:, D:2 * D], vg_ref[...]).astype(BF)
    n_chunks = TM // CHUNK
    gch = D // GM_GROUPS
    for g in range(GM_GROUPS):
        csl = slice(g * gch, (g + 1) * gch)
        vg = jnp.concatenate([v[n * CHUNK:(n + 1) * CHUNK, csl] for n in range(n_chunks)], axis=1)
        sg = _dot(ws_ref[g], vg) + bst_ref[:, g:g + 1]
        for n in range(n_chunks):
            rsl = slice(n * CHUNK, (n + 1) * CHUNK)
            gated_scr[rsl, csl] = (u[rsl, csl] * sg[:, n * gch:(n + 1) * gch]).astype(BF)
    out = _dot(gated_scr[...], wo_ref[...])
    gate = mod_ref[0, :, 2 * D:3 * D]
    x_new = x + gate * out
    xo_ref[...] = x_new
    _ffn_prep(x_new, mod_ref, gf_ref, rwt_ref, rb_ref, h2c_ref, gidx_ref, grank_ref, cnt_ref,
              ct_scr, run_scr)


def _odd(pending, mod, l, g_mix, win, v_g, w_s, b_st, w_out, g_ffn, rwt, rb):
    nt = T // TM
    full = lambda shape: pl.BlockSpec(shape, lambda t: (0,) * len(shape))
    return pl.pallas_call(
        _odd_kernel,
        grid=(nt,),
        in_specs=_pending_specs(pending["layer"]) + [
            pl.BlockSpec((1, 1, 6 * D), lambda t: (l * MOD_ROWS + _mod_row(t, TM), 0, 0)),
            full((1, D)), full((D, 2 * D)), full((1, D)),
            full((GM_GROUPS, CHUNK, CHUNK)), full((CHUNK, GM_GROUPS)), full((D, D)),
            full((1, D)), full((N_EXPERTS, D)), full((N_EXPERTS, 1)),
        ],
        out_specs=_prep_out_specs(),
        out_shape=_prep_out_shapes(),
        scratch_shapes=[pltpu.VMEM((TM, D), BF), pltpu.VMEM((128, TM), F32), pltpu.VMEM((NB_PAD, 128), F32)]
        + PENDING_SCRATCH,
        compiler_params=_cp(1),
        name="odd",
    )(*_pending_args(pending), mod, g_mix, win, v_g, w_s, b_st, w_out, g_ffn, rwt, rb)


def _routing_plan(bucket, brank, cnt):
    counts = cnt[:NB, 0].astype(jnp.int32)
    padded = (counts + (TMM - 1)) // TMM * TMM
    ends = jnp.cumsum(padded)
    base = ends - padded
    b = bucket.reshape(T)
    hit = b[:, None] == jnp.arange(NB, dtype=jnp.int32)[None, :]
    slot = brank.reshape(T) + jnp.sum(jnp.where(hit, base[None, :], 0), axis=1)
    n_steps = ends[NB - 1] // TMM
    starts = jnp.minimum(jnp.arange(MAX_STEPS, dtype=jnp.int32) * TMM, ends[NB - 1] - TMM)
    bucket_of_step = jnp.sum((starts[:, None] >= ends[None, :NB - 1]).astype(jnp.int32), axis=1)
    pair_of_step = bucket_of_step % len(PAIRS)
    pairs = jnp.asarray(PAIRS, dtype=jnp.int32)
    sched = jnp.concatenate([bucket_of_step // len(PAIRS), pairs[pair_of_step, 0], pairs[pair_of_step, 1],
                             n_steps[None]]).astype(jnp.int32)
    fill_plan = jnp.concatenate([base + counts, ends, n_steps[None]]).astype(jnp.int32)
    return slot.astype(jnp.int32), sched, fill_plan


def _row_tile(ref, row, sub):
    return ref.at[pl.ds(pl.multiple_of(row * sub, sub), sub), :]


def _row_copies(n_rows, make_copy):
    def body(i, carry):
        for j in range(ISSUE_UNROLL):
            make_copy(i * ISSUE_UNROLL + j).start(priority=j % 2)
        return carry
    lax.fori_loop(0, n_rows // ISSUE_UNROLL, body, 0)


def _dispatch_kernel(pad_ref, slot_ref, h_ref, o_hbm, zero_scr, zsem, sem):
    t = pl.program_id(0)

    @pl.when(t == 0)
    def _():
        zero_scr[...] = jnp.zeros_like(zero_scr)

        def chunk_fill(row):
            return pltpu.make_async_copy(
                zero_scr.at[pl.ds(0, FILL * X_SUB), :],
                o_hbm.at[pl.ds(pl.multiple_of(row * X_SUB, X_SUB), FILL * X_SUB), :], zsem.at[0])

        def tile_fill(s):
            return pltpu.make_async_copy(
                zero_scr, o_hbm.at[pl.ds(s * TMM * X_SUB, TMM * X_SUB), :], zsem.at[1])

        for phase in ("start", "wait"):
            for k in range(NB):
                for c in range(TMM // FILL):
                    top = pad_ref[NB + k] - c * FILL

                    @pl.when(top > pad_ref[k])
                    def _():
                        getattr(chunk_fill(top - FILL), phase)()
        for phase in ("start", "wait"):
            for s in range(T // TMM, MAX_STEPS + 1):
                @pl.when(s >= pad_ref[2 * NB])
                def _():
                    getattr(tile_fill(s), phase)()

    _row_copies(TM, lambda r: pltpu.make_async_copy(
        _row_tile(h_ref, r, X_SUB), _row_tile(o_hbm, slot_ref[r], X_SUB), sem))
    pltpu.make_async_copy(h_ref, o_hbm.at[pl.ds(0, TM * X_SUB), :], sem).wait()


def _dispatch(pad_start, slot, h2c):
    nt = T // TM
    return pl.pallas_call(
        _dispatch_kernel,
        grid_spec=pltpu.PrefetchScalarGridSpec(
            num_scalar_prefetch=1,
            grid=(nt,),
            in_specs=[
                pl.BlockSpec((TM,), lambda t, pad: (t,), memory_space=pltpu.SMEM),
                pl.BlockSpec((TM * X_SUB, LANES), lambda t, pad: (t, 0)),
            ],
            out_specs=pl.BlockSpec(memory_space=pl.ANY),
            scratch_shapes=[pltpu.VMEM((TMM * X_SUB, LANES), F32),
                            pltpu.SemaphoreType.DMA((2,)), pltpu.SemaphoreType.DMA(())],
        ),
        out_shape=jax.ShapeDtypeStruct((SORTED_ROWS * X_SUB, LANES), F32),
        compiler_params=_cp(1),
        name="dispatch",
    )(pad_start, slot, h2c)


def _experts_kernel(sched_ref, xs_ref, w1_ref, w3_ref, w2_ref, o_ref, w1b, w3b, w2b):
    s = pl.program_id(0)

    @pl.when(jnp.logical_or(s == 0, sched_ref[s] != sched_ref[jnp.maximum(s - 1, 0)]))
    def _():
        w1b[...] = w1_ref[...].astype(BF)
        w3b[...] = w3_ref[...].astype(BF)
        w2b[...] = w2_ref[...].astype(BF)

    n_steps = sched_ref[3 * MAX_STEPS]

    @pl.when(s < n_steps)
    def _():
        h = jnp.concatenate([xs_ref[pl.ds(k, TMM, stride=X_SUB), :] for k in range(Y_SUB)],
                            axis=1).astype(BF)
        cw = xs_ref[pl.ds(Y_SUB, TMM, stride=X_SUB), :]
        acc = None
        for j in range(2):
            e = sched_ref[(1 + j) * MAX_STEPS + s]
            a = _dot(h, w1b[e])
            b = _dot(h, w3b[e])
            hid = (a * (1.0 / (1.0 + jnp.exp(-a))) * b * cw[:, j:j + 1]).astype(BF)
            y = _dot(hid, w2b[e])
            acc = y if acc is None else acc + y
        for k in range(D // LANES):
            o_ref[pl.ds(k, TMM, stride=Y_SUB), :] = acc[:, k * LANES:(k + 1) * LANES]

    @pl.when(s >= n_steps)
    def _():
        o_ref[...] = jnp.zeros_like(o_ref)


def _experts(sched, xs, l, w1, w3, w2):
    row_blk = lambda s, sched: (jnp.maximum(jnp.minimum(s, sched[3 * MAX_STEPS] - 1), 0), 0)
    grp_blk = lambda s, sched: (l, sched[s], 0, 0)
    return pl.pallas_call(
        _experts_kernel,
        grid_spec=pltpu.PrefetchScalarGridSpec(
            num_scalar_prefetch=1,
            grid=(MAX_STEPS,),
            in_specs=[
                pl.BlockSpec((TMM * X_SUB, LANES), row_blk),
                pl.BlockSpec((None, EPG, D, D_EXPERT), grp_blk),
                pl.BlockSpec((None, EPG, D, D_EXPERT), grp_blk),
                pl.BlockSpec((None, EPG, D_EXPERT, D), grp_blk),
            ],
            out_specs=pl.BlockSpec((TMM * Y_SUB, LANES), lambda s, sched: (s, 0)),
            scratch_shapes=[pltpu.VMEM((EPG, D, D_EXPERT), BF), pltpu.VMEM((EPG, D, D_EXPERT), BF),
                            pltpu.VMEM((EPG, D_EXPERT, D), BF)],
        ),
        out_shape=jax.ShapeDtypeStruct((MAX_STEPS * TMM * Y_SUB, LANES), F32),
        compiler_params=_cp(1),
        name="experts",
    )(sched, xs, w1, w3, w2)


def _pending_specs(l):
    nt = T // TM
    return [
        pl.BlockSpec((TM,), lambda t: (t,), memory_space=pltpu.SMEM),
        pl.BlockSpec((TM,), lambda t: (jnp.minimum(t + 1, nt - 1),), memory_space=pltpu.SMEM),
        pl.BlockSpec(memory_space=pl.ANY),
        pl.BlockSpec((TM, D), lambda t: (t, 0)),
        pl.BlockSpec((1, 1, 6 * D), lambda t: (l * MOD_ROWS + _mod_row(t, TM), 0, 0)),
    ]


def _pending_args(pending):
    return (pending["slot"], pending["slot"], pending["ys"], pending["x"], pending["mod"])


PENDING_SCRATCH = [pltpu.VMEM((2, TM * Y_SUB, LANES), F32), pltpu.SemaphoreType.DMA((2,))]
N_PENDING = 5


def _apply_pending(slot_ref, slot_next_ref, ys_hbm, x_ref, mod_ref, buf, sem):
    t = pl.program_id(0)
    nt = pl.num_programs(0)
    cur = t % 2

    def gather(idx_ref, b):
        _row_copies(TM, lambda r: pltpu.make_async_copy(
            _row_tile(ys_hbm, idx_ref[r], Y_SUB), _row_tile(buf.at[b], r, Y_SUB), sem.at[b]))

    @pl.when(t == 0)
    def _():
        gather(slot_ref, 0)

    @pl.when(t + 1 < nt)
    def _():
        gather(slot_next_ref, 1 - cur)

    pltpu.make_async_copy(ys_hbm.at[pl.ds(0, TM * Y_SUB), :], buf.at[cur], sem.at[cur]).wait()
    gate = mod_ref[0, :, 5 * D:6 * D]
    y = jnp.concatenate([buf[cur, pl.ds(k, TM, stride=Y_SUB), :] for k in range(Y_SUB)], axis=1)
    return x_ref[...] + gate * y


def _final_kernel(slot_ref, slot_next_ref, ys_hbm, x_ref, mod_ref, gfin_ref, op_ref, os_ref, buf, sem):
    t = pl.program_id(0)
    x_new = _apply_pending(slot_ref, slot_next_ref, ys_hbm, x_ref, mod_ref, buf, sem)
    y_out = _rms(x_new, gfin_ref[...])
    npt = NP_TOK // TM

    @pl.when(t < npt)
    def _():
        op_ref[...] = y_out

    @pl.when(t >= npt)
    def _():
        os_ref[...] = y_out


def _final(pending, g_final):
    nt = T // TM
    npt = NP_TOK // TM
    return pl.pallas_call(
        _final_kernel,
        grid=(nt,),
        in_specs=_pending_specs(pending["layer"]) + [pl.BlockSpec((1, D), lambda t: (0, 0))],
        out_specs=[pl.BlockSpec((TM, D), lambda t: (jnp.minimum(t, npt - 1), 0)),
                   pl.BlockSpec((TM, D), lambda t: (jnp.maximum(t - npt, 0), 0))],
        out_shape=[jax.ShapeDtypeStruct((NP_TOK, D), F32), jax.ShapeDtypeStruct((NS_TOK, D), F32)],
        scratch_shapes=PENDING_SCRATCH,
        compiler_params=_cp(1),
        name="final",
    )(*_pending_args(pending), g_final)


def _moe(x, mod, l, h2c, gidx, grank, cnt, w1, w3, w2):
    slot, sched, pad_start = _routing_plan(gidx, grank, cnt)
    xs = _dispatch(pad_start, slot, h2c)
    ys = _experts(sched, xs, l, w1, w3, w2)
    return {"slot": slot, "ys": ys, "x": x, "mod": mod, "layer": l}


def _rope_tables():
    pos = jnp.arange(DEC_SEQ)
    r = (pos // GRID_W).astype(F32)
    col = (pos % GRID_W).astype(F32)
    inv = ROPE_BASE ** (-jnp.arange(AX_FREQS, dtype=F32) / AX_FREQS)
    ang = jnp.stack([r[:, None] * inv, col[:, None] * inv], axis=1)
    cos = jnp.cos(ang)[:, :, None, :]
    sin = jnp.sin(ang)[:, :, None, :]
    c32 = jnp.broadcast_to(cos, (DEC_SEQ, 2, 2, AX_FREQS)).reshape(DEC_SEQ, ROPE)
    s32 = jnp.concatenate([-sin, sin], axis=2).reshape(DEC_SEQ, ROPE)
    pad = HEAD_PAD - NOPE - ROPE
    c = jnp.concatenate([jnp.ones((DEC_SEQ, NOPE), F32), c32, jnp.zeros((DEC_SEQ, pad), F32)], axis=1)
    s = jnp.concatenate([jnp.zeros((DEC_SEQ, NOPE), F32), s32, jnp.zeros((DEC_SEQ, pad), F32)], axis=1)
    c_id = jnp.concatenate([jnp.ones((TM, NOPE + ROPE), F32), jnp.zeros((TM, pad), F32)], axis=1)
    s_id = jnp.zeros((TM, HEAD_PAD), F32)
    return jnp.concatenate([c, c_id], axis=0), jnp.concatenate([s, s_id], axis=0)


def _swap_halves(w):
    lead = w.shape[:-1]
    return w.reshape(lead + (2, 2, AX_FREQS))[..., ::-1, :].reshape(lead + (ROPE,))


def _even_weights(w_in, w_q_up, w_kv_up):
    k_in = w_in.shape[0]
    base = 3 * D_CONV + Q_RANK + KV_RANK
    w_kr = w_in[:, base:base + ROPE]
    z = lambda n: jnp.zeros((k_in, n), F32)
    win_ext = jnp.concatenate([
        w_in[:, :base],
        w_kr, z(HEAD_PAD - ROPE),
        z(NOPE), w_kr, z(HEAD_PAD - NOPE - ROPE),
        z(NOPE), _swap_halves(w_kr), z(HEAD_PAD - NOPE - ROPE),
    ], axis=1).astype(BF)
    wq = w_q_up.reshape(Q_RANK, HEADS, QK_DIM)
    zq = lambda n: jnp.zeros((Q_RANK, HEADS, n), F32)
    wq1 = jnp.concatenate([wq, zq(HEAD_PAD - QK_DIM)], axis=2).reshape(Q_RANK, HEADS * HEAD_PAD).astype(BF)
    wq2 = jnp.concatenate([zq(NOPE), _swap_halves(wq[:, :, NOPE:]), zq(HEAD_PAD - QK_DIM)],
                          axis=2).reshape(Q_RANK, HEADS * HEAD_PAD).astype(BF)
    wkv = w_kv_up.reshape(KV_RANK, HEADS, NOPE + V_DIM)
    wk = jnp.concatenate([wkv[:, :, :NOPE], jnp.zeros((KV_RANK, HEADS, HEAD_PAD - NOPE), F32)],
                         axis=2).reshape(KV_RANK, HEADS * HEAD_PAD).astype(BF)
    wv = wkv[:, :, NOPE:].reshape(KV_RANK, HEADS * V_DIM).astype(BF)
    return win_ext, wq1, wq2, wk, wv


def kernel(x_prompt, x_sample, cache_ckv, cache_krope, c, c_ctx, w_ada, b_ada, g_mix, g_ffn, g_final,
           ev_w_in, conv_w, q_norm_g, w_q_up, kv_norm_g, w_kv_up, ev_w_out,
           gm_w_in, gm_v_g, gm_w_s, gm_b_s, gm_w_out, router_w, router_b, moe_w1, moe_w3, moe_w2):
    n_even = ev_w_in.shape[0]
    x_parts = (x_prompt.reshape(NP_TOK, D), x_sample.reshape(NS_TOK, D))
    cc =jnp.concatenate([c, c_ctx[None, :], jnp.zeros((MOD_ROWS - DEC_BATCH - 1, D), F32)], axis=0)
    mod = _ada(cc, w_ada, b_ada).reshape(DEPTH * MOD_ROWS, 1, 6 * D)

    rope_c, rope_s = _rope_tables()
    rwt = router_w.T
    rb = router_b.reshape(N_EXPERTS, 1)
    ev = [_even_weights(ev_w_in[i], w_q_up[i], w_kv_up[i]) for i in range(n_even)]
    kr_ctx = jnp.pad(cache_krope, ((0, 0), (0, 0), (0, 0), (NOPE, HEAD_PAD - NOPE - ROPE)))
    kr_ctx = kr_ctx.transpose(1, 0, 2, 3).reshape(n_even, DEC_BATCH * PAST, HEAD_PAD)
    ckv_ctx = cache_ckv.transpose(1, 0, 2, 3).reshape(n_even, DEC_BATCH * PAST, KV_RANK)
    kc, vc = _ctx_kv(ckv_ctx, kr_ctx, jnp.stack([e[3] for e in ev]), jnp.stack([e[4] for e in ev]))

    ckv_states, kr_states = [], []
    pending = None
    for l in range(DEPTH):
        i = l // 2
        if l % 2 == 0:
            win_ext, wq1, wq2, wk, wv = ev[i]
            outs = _even_in(
                None if pending else x_parts, pending, mod, l, g_mix[l][None, :], win_ext,
                q_norm_g[i][None, :], wq1, wq2, kv_norm_g[i][None, :], wk, wv, rope_c, rope_s)
            bg, cv, q, k, v, ckv, kr = outs[:7]
            if pending:
                x_parts = (outs[7],)
            attn_parts = (_attn_prompt(q, k, v), _attn_sample(q, k, v, kc, vc, i))
            x, h2c, gidx, grank, cnt = _even_out(x_parts, mod, l, bg, cv, attn_parts, conv_w[i],
                                                 ev_w_out[i].astype(BF), g_ffn[l][None, :], rwt, rb)
            ckv_states.append(ckv[:NP_TOK].reshape(BATCH, SEQ, KV_RANK))
            kr_states.append(kr[:NP_TOK, :ROPE].reshape(BATCH, SEQ, ROPE))
        else:
            x, h2c, gidx, grank, cnt = _odd(pending, mod, l, g_mix[l][None, :], gm_w_in[i].astype(BF),
                                            gm_v_g[i][None, :], gm_w_s[i].astype(BF), gm_b_s[i].T,
                                            gm_w_out[i].astype(BF), g_ffn[l][None, :], rwt, rb)
        pending = _moe(x, mod, l, h2c, gidx, grank, cnt, moe_w1, moe_w3, moe_w2)

    y_prompt, y_sample = _final(pending, g_final[None, :])
    y_prompt = y_prompt.reshape(BATCH, SEQ, D)
    y_sample = y_sample.reshape(DEC_BATCH, DEC_SEQ, D)
    return (y_prompt, y_sample, jnp.stack(ckv_states, axis=1), jnp.stack(kr_states, axis=1))
```

```python
import functools
import math

import jax
import jax.numpy as jnp
from jax import lax
from jax.experimental import pallas as pl
from jax.experimental.pallas import tpu as pltpu

D = 1024
BATCH, SEQ = 32, 256
DEC_BATCH, DEC_SEQ = 8, 2048
PAST = 256
DEPTH = 4
GRID_W = 64
D_CONV = 512
HEADS = 8
NOPE, ROPE, V_DIM = 64, 32, 64
QK_DIM = NOPE + ROPE
Q_RANK, KV_RANK = 384, 256
AX_FREQS = ROPE // 4
ROPE_BASE = 10000.0
CHUNK = 128
GM_GROUPS = 8
N_EXPERTS, N_GROUPS, EPG = 16, 4, 4
D_EXPERT = 256
EPS = 1e-6

NP_TOK = BATCH * SEQ
NS_TOK = DEC_BATCH * DEC_SEQ
T = NP_TOK + NS_TOK
MOD_ROWS = 16
HEAD_PAD = 128
HALO = 16

TM = 512
PAIRS = ((0, 1), (0, 2), (0, 3), (1, 2), (1, 3), (2, 3))
NB = N_GROUPS * len(PAIRS)
NB_PAD = 32
TMM = 512
FILL = 128
MAX_STEPS = T // TMM + NB
SORTED_TILES = MAX_STEPS + MAX_STEPS % 2
SORTED_ROWS = SORTED_TILES * TMM
LANES = 128
Y_SUB = D // LANES
X_SUB = Y_SUB
ISSUE_UNROLL = 8
TQ = 256
ATTN_PAIRS = 4
VMEM_LIMIT = 56 * 1024 * 1024

BF = jnp.bfloat16
F32 = jnp.float32


def _cp(n_axes):
    return pltpu.CompilerParams(dimension_semantics=("arbitrary",) * n_axes,
                                vmem_limit_bytes=VMEM_LIMIT)


def _mod_row(t, tm):
    n_prompt_tiles = NP_TOK // tm
    per_seq = DEC_SEQ // tm
    return jnp.where(t < n_prompt_tiles, DEC_BATCH, (t - n_prompt_tiles) // per_seq)


def _rms(x, g):
    return x * lax.rsqrt(jnp.mean(x * x, axis=-1, keepdims=True) + EPS) * g


def _norm_mod(x, g, shift, scale):
    return x * lax.rsqrt(jnp.mean(x * x, axis=-1, keepdims=True) + EPS) * (g * (1.0 + scale)) + shift


def _dot(a, b):
    return jnp.dot(a, b, preferred_element_type=F32)


def _dot_nt(a, b, precision=None):
    return lax.dot_general(a, b, (((1,), (1,)), ((), ())), precision=precision,
                           preferred_element_type=F32)


ADA_TN = 1536


def _ada_kernel(cc_ref, w_ref, b_ref, o_ref):
    cc = cc_ref[...]
    s = (cc / (1.0 + jnp.exp(-cc))).astype(BF)
    o_ref[0] = _dot(s, w_ref[0].astype(BF)) + b_ref[0]


def _ada(cc, w_ada, b_ada):
    n = 6 * D
    return pl.pallas_call(
        _ada_kernel,
        grid=(DEPTH, n // ADA_TN),
        in_specs=[
            pl.BlockSpec((MOD_ROWS, D), lambda l, j: (0, 0)),
            pl.BlockSpec((1, D, ADA_TN), lambda l, j: (l, 0, j)),
            pl.BlockSpec((1, 1, ADA_TN), lambda l, j: (l, 0, j)),
        ],
        out_specs=pl.BlockSpec((1, MOD_ROWS, ADA_TN), lambda l, j: (l, 0, j)),
        out_shape=jax.ShapeDtypeStruct((DEPTH, MOD_ROWS, n), F32),
        compiler_params=_cp(2),
        name="ada",
    )(cc, w_ada, b_ada.reshape(DEPTH, 1, n))


def _route(logits_t, rb):
    sc = 1.0 / (1.0 + jnp.exp(-logits_t))
    sel = sc + rb
    rows = [sel[e:e + 1, :] for e in range(N_EXPERTS)]
    srows = [sc[e:e + 1, :] for e in range(N_EXPERTS)]

    def top2sum(a, b, c, d):
        hi1, lo1 = jnp.maximum(a, b), jnp.minimum(a, b)
        hi2, lo2 = jnp.maximum(c, d), jnp.minimum(c, d)
        return jnp.maximum(hi1, hi2) + jnp.maximum(jnp.minimum(hi1, hi2), jnp.maximum(lo1, lo2))

    gs = [top2sum(*rows[EPG * g:EPG * (g + 1)]) for g in range(N_GROUPS)]
    best = gs[0]
    gidx = jnp.zeros_like(best, dtype=jnp.int32)
    for g in range(1, N_GROUPS):
        upd = gs[g] > best
        best = jnp.where(upd, gs[g], best)
        gidx = jnp.where(upd, g, gidx)

    picked = []
    for g in range(N_GROUPS):
        grp = rows[EPG * g:EPG * (g + 1)]
        in_g = gidx == g
        for j in range(EPG):
            rank = jnp.zeros_like(gidx)
            for k in range(EPG):
                if k == j:
                    continue
                ahead = grp[k] > grp[j]
                if k < j:
                    ahead = ahead | (grp[k] == grp[j])
                rank = rank + ahead.astype(jnp.int32)
            picked.append(in_g & (rank < 2))
    w = [jnp.where(picked[e], srows[e], 0.0) for e in range(N_EXPERTS)]
    wsum = w[0]
    for e in range(1, N_EXPERTS):
        wsum = wsum + w[e]
    inv = 1.0 / wsum
    pj, wj = [], []
    for j in range(EPG):
        hit, acc = picked[j], w[j]
        for g in range(1, N_GROUPS):
            hit = hit | picked[EPG * g + j]
            acc = acc + w[EPG * g + j]
        pj.append(hit)
        wj.append(acc * inv)
    pair = jnp.full_like(gidx, len(PAIRS) - 1)
    for idx in range(len(PAIRS) - 2, -1, -1):
        a, b = PAIRS[idx]
        pair = jnp.where(pj[a] & pj[b], idx, pair)
    w_a = jnp.where(pj[0], wj[0], jnp.where(pj[1], wj[1], wj[2]))
    w_b = jnp.where(pj[3], wj[3], jnp.where(pj[2], wj[2], wj[1]))
    return (w_a, w_b), gidx * len(PAIRS) + pair


def _ffn_prep(x_new, mod_ref, gf_ref, rwt_ref, rb_ref, h2c_ref, gidx_ref, grank_ref, wa_ref, cnt_ref,
              run_scr):
    t = pl.program_id(0)
    tm = x_new.shape[0]
    shift = mod_ref[0, :, 3 * D:4 * D]
    scale = mod_ref[0, :, 4 * D:5 * D]
    h2 = _norm_mod(x_new, gf_ref[...], shift, scale)
    logits_t = _dot_nt(rwt_ref[...], h2, precision=lax.Precision.HIGHEST)
    wab, gidx = _route(logits_t, rb_ref[...])
    wa_ref[0] = wab[0]

    for k in range(X_SUB):
        h2c_ref[pl.ds(k, tm, stride=X_SUB), :] = h2[:, k * LANES:(k + 1) * LANES]

    @pl.when(t == 0)
    def _():
        run_scr[...] = jnp.zeros_like(run_scr)

    onehot = (lax.broadcasted_iota(jnp.int32, (NB_PAD, tm), 0) == gidx).astype(F32)
    earlier = (lax.broadcasted_iota(jnp.int32, (tm, tm), 0)
               < lax.broadcasted_iota(jnp.int32, (tm, tm), 1)).astype(BF)
    rank = _dot(onehot.astype(BF), earlier)
    run = run_scr[:, 0:1]
    grank = jnp.sum(onehot * (rank + run), axis=0, keepdims=True)
    gidx_ref[0] = gidx
    grank_ref[0] = grank.astype(jnp.int32)
    run_scr[...] = run_scr[...] + jnp.sum(onehot, axis=1, keepdims=True)
    cnt_ref[...] = run_scr[...]


def _prep_out_specs():
    tok = lambda w: pl.BlockSpec((TM, w), lambda t: (t, 0))
    lanes = pl.BlockSpec((1, 1, TM), lambda t: (t, 0, 0))
    rows = pl.BlockSpec((TM * X_SUB, LANES), lambda t: (t, 0))
    return [tok(D), rows, lanes, lanes, lanes, pl.BlockSpec((NB_PAD, 128), lambda t: (0, 0))]


def _prep_out_shapes():
    nt = T // TM
    return [jax.ShapeDtypeStruct((T, D), F32), jax.ShapeDtypeStruct((T * X_SUB, LANES), F32),
            jax.ShapeDtypeStruct((nt, 1, TM), jnp.int32), jax.ShapeDtypeStruct((nt, 1, TM), jnp.int32),
            jax.ShapeDtypeStruct((nt, 1, TM), F32), jax.ShapeDtypeStruct((NB_PAD, 128), F32)]


EV_EXT = 3 * D_CONV + Q_RANK + KV_RANK + 3 * HEAD_PAD


def _tok_specs(parts, width):
    npt = NP_TOK // TM
    if len(parts) == 1:
        return [pl.BlockSpec((TM, width), lambda t: (t, 0))]
    return [pl.BlockSpec((TM, width), lambda t: (jnp.minimum(t, npt - 1), 0)),
            pl.BlockSpec((TM, width), lambda t: (jnp.maximum(t - npt, 0), 0))]


def _tok_load(refs):
    if len(refs) == 1:
        return refs[0][...]
    return jnp.where(pl.program_id(0) < NP_TOK // TM, refs[0][...], refs[1][...])


def _even_in_kernel(n_x, *refs):
    if n_x:
        x_refs, refs = refs[:n_x], refs[n_x:]
        x = _tok_load(x_refs)
    else:
        pend, refs, scratch = refs[:N_PENDING], refs[N_PENDING:-2], refs[-2:]
        refs, xo_ref = refs[:-1], refs[-1]
        x = _apply_pending(*pend, *scratch)
        xo_ref[...] = x
    (mod_ref, g_ref, win_ref, qg_ref, wq1_ref, wq2_ref, kg_ref, wk_ref, wv_ref, rc_ref, rs_ref,
     bg_ref, cv_ref, q_ref, k_ref, v_ref, ckv_ref, kr_ref) = refs
    shift = mod_ref[0, :, 0:D]
    scale = mod_ref[0, :, D:2 * D]
    h = _norm_mod(x, g_ref[...], shift, scale).astype(BF)
    proj = _dot(h, win_ref[...])
    o = 0
    b_g = proj[:, o:o + D_CONV]; o += D_CONV
    c_g = proj[:, o:o + D_CONV]; o += D_CONV
    v_in = proj[:, o:o + D_CONV]; o += D_CONV
    q_a = proj[:, o:o + Q_RANK]; o += Q_RANK
    kv_a = proj[:, o:o + KV_RANK]; o += KV_RANK
    kr_raw = proj[:, o:o + HEAD_PAD]; o += HEAD_PAD
    kr_cat = proj[:, o:o + HEAD_PAD]; o += HEAD_PAD
    kr_sw = proj[:, o:o + HEAD_PAD]

    bg_ref[...] = b_g.astype(BF)
    cv_ref[...] = (c_g * v_in).astype(BF)
    kr_ref[...] = kr_raw

    rc = rc_ref[...]
    rs = rs_ref[...]
    qn = _rms(q_a, qg_ref[...]).astype(BF)
    q1 = _dot(qn, wq1_ref[...])
    q2 = _dot(qn, wq2_ref[...])
    qscale = QK_DIM ** -0.5
    for hd in range(HEADS):
        sl = slice(hd * HEAD_PAD, (hd + 1) * HEAD_PAD)
        q_ref[:, sl] = ((q1[:, sl] * rc + q2[:, sl] * rs) * qscale).astype(BF)

    ckv = _rms(kv_a, kg_ref[...])
    ckv_ref[...] = ckv
    ckv_b = ckv.astype(BF)
    kk = _dot(ckv_b, wk_ref[...])
    kr = kr_cat * rc + kr_sw * rs
    for hd in range(HEADS):
        sl = slice(hd * HEAD_PAD, (hd + 1) * HEAD_PAD)
        k_ref[:, sl] = (kk[:, sl] + kr).astype(BF)
    v_ref[...] = _dot(ckv_b, wv_ref[...]).astype(BF)


def _even_in(x_parts, pending, mod, l, g_mix, win, qg, wq1, wq2, kg, wk, wv, rope_c, rope_s):
    nt = T // TM
    npt = NP_TOK // TM
    per_seq = DEC_SEQ // TM
    ident_blk = DEC_SEQ // TM

    def rope_idx(t):
        return (jnp.where(t < npt, ident_blk, (t - npt) % per_seq), 0)

    full = lambda shape: pl.BlockSpec(shape, lambda t: (0,) * len(shape))
    tok = lambda w: pl.BlockSpec((TM, w), lambda t: (t, 0))
    out_specs = [tok(D_CONV), tok(D_CONV), tok(HEADS * HEAD_PAD), tok(HEADS * HEAD_PAD),
                 tok(HEADS * V_DIM), tok(KV_RANK), tok(HEAD_PAD)]
    out_shape = [
        jax.ShapeDtypeStruct((T, D_CONV), BF), jax.ShapeDtypeStruct((T, D_CONV), BF),
        jax.ShapeDtypeStruct((T, HEADS * HEAD_PAD), BF), jax.ShapeDtypeStruct((T, HEADS * HEAD_PAD), BF),
        jax.ShapeDtypeStruct((T, HEADS * V_DIM), BF),
        jax.ShapeDtypeStruct((T, KV_RANK), F32), jax.ShapeDtypeStruct((T, HEAD_PAD), F32),
    ]
    if pending is None:
        lead_specs, lead_args, scratch = _tok_specs(x_parts, D), tuple(x_parts), []
    else:
        lead_specs, lead_args, scratch = _pending_specs(pending["layer"]), _pending_args(pending), PENDING_SCRATCH
        out_specs.append(tok(D))
        out_shape.append(jax.ShapeDtypeStruct((T, D), F32))
    return pl.pallas_call(
        functools.partial(_even_in_kernel, 0 if pending is not None else len(x_parts)),
        grid=(nt,),
        in_specs=lead_specs + [
            pl.BlockSpec((1, 1, 6 * D), lambda t: (l * MOD_ROWS + _mod_row(t, TM), 0, 0)),
            full((1, D)), full((D, EV_EXT)), full((1, Q_RANK)),
            full((Q_RANK, HEADS * HEAD_PAD)), full((Q_RANK, HEADS * HEAD_PAD)),
            full((1, KV_RANK)), full((KV_RANK, HEADS * HEAD_PAD)), full((KV_RANK, HEADS * V_DIM)),
            pl.BlockSpec((TM, HEAD_PAD), rope_idx), pl.BlockSpec((TM, HEAD_PAD), rope_idx),
        ],
        out_specs=out_specs,
        out_shape=out_shape,
        scratch_shapes=scratch,
        compiler_params=_cp(1),
        name="even_in",
    )(*lead_args, mod, g_mix, win, qg, wq1, wq2, kg, wk, wv, rope_c, rope_s)


CTX_TM = 512


def _ctx_kv_kernel(ckv_ref, kr_ref, wk_ref, wv_ref, k_ref, v_ref):
    ckv_b = ckv_ref[0].astype(BF)
    kk = _dot(ckv_b, wk_ref[0])
    kr = kr_ref[0]
    for hd in range(HEADS):
        sl = slice(hd * HEAD_PAD, (hd + 1) * HEAD_PAD)
        k_ref[0, :, sl] = (kk[:, sl] + kr).astype(BF)
    v_ref[0] = _dot(ckv_b, wv_ref[0]).astype(BF)


def _ctx_kv(ckv_all, kr_all, wk_all, wv_all):
    n_even = ckv_all.shape[0]
    rows = DEC_BATCH * PAST
    return pl.pallas_call(
        _ctx_kv_kernel,
        grid=(n_even, rows // CTX_TM),
        in_specs=[
            pl.BlockSpec((1, CTX_TM, KV_RANK), lambda i, t: (i, t, 0)),
            pl.BlockSpec((1, CTX_TM, HEAD_PAD), lambda i, t: (i, t, 0)),
            pl.BlockSpec((1, KV_RANK, HEADS * HEAD_PAD), lambda i, t: (i, 0, 0)),
            pl.BlockSpec((1, KV_RANK, HEADS * V_DIM), lambda i, t: (i, 0, 0)),
        ],
        out_specs=[
            pl.BlockSpec((1, CTX_TM, HEADS * HEAD_PAD), lambda i, t: (i, t, 0)),
            pl.BlockSpec((1, CTX_TM, HEADS * V_DIM), lambda i, t: (i, t, 0)),
        ],
        out_shape=[jax.ShapeDtypeStruct((n_even, rows, HEADS * HEAD_PAD), BF),
                   jax.ShapeDtypeStruct((n_even, rows, HEADS * V_DIM), BF)],
        compiler_params=_cp(2),
        name="ctx_kv",
    )(ckv_all, kr_all, wk_all, wv_all)


def _attn_body(n_pairs, has_ctx, q_ref, k_ref, v_ref, *rest):
    if has_ctx:
        kc_ref, vc_ref, o_ref = rest
    else:
        (o_ref,) = rest
    lane = lax.broadcasted_iota(jnp.int32, (1, 2 * V_DIM), 1)
    w = 2 * V_DIM
    ones = jnp.ones((k_ref.shape[0], w), BF)
    ones_ctx = jnp.ones((PAST, w), BF)
    for pr in range(n_pairs):
        vsl = slice(pr * w, (pr + 1) * w)
        v = jnp.concatenate([v_ref[:, vsl], ones], axis=1)
        if has_ctx:
            vc = jnp.concatenate([vc_ref[0, :, vsl], ones_ctx], axis=1)
        outs = []
        for sub in range(2):
            hsl = slice((2 * pr + sub) * HEAD_PAD, (2 * pr + sub + 1) * HEAD_PAD)
            q = q_ref[:, hsl]
            s1 = _dot_nt(q, k_ref[:, hsl])
            m = jnp.max(s1, axis=-1, keepdims=True)
            if has_ctx:
                s2 = _dot_nt(q, kc_ref[0, :, hsl])
                m = jnp.maximum(m, jnp.max(s2, axis=-1, keepdims=True))
            acc = _dot(jnp.exp(s1 - m).astype(BF), v)
            if has_ctx:
                acc = acc + _dot(jnp.exp(s2 - m).astype(BF), vc)
            outs.append(acc[:, 0:w] * (1.0 / acc[:, w:2 * w]))
        o_ref[:, vsl] = jnp.where(lane < V_DIM, outs[0], outs[1]).astype(BF)


def _attn_prompt(q, k, v):
    return pl.pallas_call(
        functools.partial(_attn_body, HEADS // 2, False),
        grid=(BATCH,),
        in_specs=[
            pl.BlockSpec((SEQ, HEADS * HEAD_PAD), lambda b: (b, 0)),
            pl.BlockSpec((SEQ, HEADS * HEAD_PAD), lambda b: (b, 0)),
            pl.BlockSpec((SEQ, HEADS * V_DIM), lambda b: (b, 0)),
        ],
        out_specs=pl.BlockSpec((SEQ, HEADS * V_DIM), lambda b: (b, 0)),
        out_shape=jax.ShapeDtypeStruct((NP_TOK, HEADS * V_DIM), BF),
        compiler_params=_cp(1),
        name="attn_prompt",
    )(q, k, v)


def _attn_sample(q, k, v, kc, vc, i):
    s_blk0 = NP_TOK // DEC_SEQ
    q_blk0 = NP_TOK // TQ
    nq = DEC_SEQ // TQ
    hw = ATTN_PAIRS * 2 * HEAD_PAD
    vw = ATTN_PAIRS * 2 * V_DIM
    return pl.pallas_call(
        functools.partial(_attn_body, ATTN_PAIRS, True),
        grid=(DEC_BATCH, HEADS // (2 * ATTN_PAIRS), nq),
        in_specs=[
            pl.BlockSpec((TQ, hw), lambda b, hp, j: (q_blk0 + b * nq + j, hp)),
            pl.BlockSpec((DEC_SEQ, hw), lambda b, hp, j: (s_blk0 + b, hp)),
            pl.BlockSpec((DEC_SEQ, vw), lambda b, hp, j: (s_blk0 + b, hp)),
            pl.BlockSpec((1, PAST, hw), lambda b, hp, j: (i, b, hp)),
            pl.BlockSpec((1, PAST, vw), lambda b, hp, j: (i, b, hp)),
        ],
        out_specs=pl.BlockSpec((TQ, vw), lambda b, hp, j: (b * nq + j, hp)),
        out_shape=jax.ShapeDtypeStruct((NS_TOK, HEADS * V_DIM), BF),
        compiler_params=_cp(3),
        name="attn_sample",
    )(q, k, v, kc, vc)


def _even_out_kernel(n_x, *refs):
    x_refs, refs = refs[:n_x], refs[n_x:]
    (mod_ref, bg_ref, cv_ref, cvp_ref, cvn_ref, atp_ref, ats_ref, cw_ref, wo_ref, gf_ref, rwt_ref, rb_ref,
     xo_ref, h2c_ref, gidx_ref, grank_ref, wa_ref, cnt_ref, run_scr) = refs
    t = pl.program_id(0)
    npt = NP_TOK // TM
    per_seq = DEC_SEQ // TM
    cv = cv_ref[...].astype(F32)
    r = lax.broadcasted_iota(jnp.int32, (TM, 1), 0)
    is_prompt = t < npt
    tile_in_seq = (t - npt) % per_seq
    first_row = jnp.where(is_prompt, 0, jnp.where(tile_in_seq == 0, 0, -1))
    last_row = jnp.where(is_prompt, SEQ - 1, jnp.where(tile_in_seq == per_seq - 1, TM - 1, -1))
    period_mask = jnp.where(is_prompt, SEQ - 1, TM - 1)
    first = (r & period_mask) == first_row
    last = (r & period_mask) == last_row
    prev_row = cvp_ref[HALO - 1:HALO, :].astype(F32)
    next_row = cvn_ref[0:1, :].astype(F32)
    prev = jnp.where(r == 0, prev_row, pltpu.roll(cv, 1, 0))
    prev = jnp.where(first, 0.0, prev)
    nxt = jnp.where(r == TM - 1, next_row, pltpu.roll(cv, TM - 1, 0))
    nxt = jnp.where(last, 0.0, nxt)
    cw = cw_ref[...]
    conv = prev * cw[0:1, :] + cv * cw[1:2, :] + nxt * cw[2:3, :]
    yc = (bg_ref[...].astype(F32) * conv).astype(BF)
    attn = _tok_load((atp_ref, ats_ref))
    out = _dot(yc, wo_ref[0:D_CONV, :]) + _dot(attn, wo_ref[D_CONV:2 * D_CONV, :])
    gate = mod_ref[0, :, 2 * D:3 * D]
    x_new = _tok_load(x_refs) + gate * out
    xo_ref[...] = x_new
    _ffn_prep(x_new, mod_ref, gf_ref, rwt_ref, rb_ref, h2c_ref, gidx_ref, grank_ref, wa_ref, cnt_ref,
              run_scr)


def _even_out(x_parts, mod, l, bg, cv, attn_parts, conv_w, w_out, g_ffn, rwt, rb):
    nt = T // TM
    hb = TM // HALO
    nhb = T // HALO
    full = lambda shape: pl.BlockSpec(shape, lambda t: (0,) * len(shape))
    tok = lambda w: pl.BlockSpec((TM, w), lambda t: (t, 0))
    return pl.pallas_call(
        functools.partial(_even_out_kernel, len(x_parts)),
        grid=(nt,),
        in_specs=_tok_specs(x_parts, D) + [
            pl.BlockSpec((1, 1, 6 * D), lambda t: (l * MOD_ROWS + _mod_row(t, TM), 0, 0)),
            tok(D_CONV), tok(D_CONV),
            pl.BlockSpec((HALO, D_CONV), lambda t: (jnp.maximum(t * hb - 1, 0), 0)),
            pl.BlockSpec((HALO, D_CONV), lambda t: (jnp.minimum((t + 1) * hb, nhb - 1), 0)),
        ] + _tok_specs(attn_parts, HEADS * V_DIM) + [
            full((3, D_CONV)), full((2 * D_CONV, D)), full((1, D)),
            full((N_EXPERTS, D)), full((N_EXPERTS, 1)),
        ],
        out_specs=_prep_out_specs(),
        out_shape=_prep_out_shapes(),
        scratch_shapes=[pltpu.VMEM((NB_PAD, 128), F32)],
        compiler_params=_cp(1),
        name="even_out",
    )(*x_parts, mod, bg, cv, cv, cv, *attn_parts, conv_w, w_out, g_ffn, rwt, rb)


def _odd_kernel(*refs):
    pend, refs, scratch = refs[:N_PENDING], refs[N_PENDING:-2], refs[-2:]
    (mod_ref, g_ref, win_ref, vg_ref, ws_ref, bst_ref, wo_ref, gf_ref, rwt_ref, rb_ref,
     xo_ref, h2c_ref, gidx_ref, grank_ref, wa_ref, cnt_ref, gated_scr, run_scr) = refs
    x = _apply_pending(*pend, *scratch)
    shift = mod_ref[0, :, 0:D]
    scale = mod_ref[0, :, D:2 * D]
    h = _norm_mod(x, g_ref[...], shift, scale).astype(BF)
    zl = _dot(h, win_ref[...])
    z = 0.5 * zl * (1.0 + lax.erf(zl * math.sqrt(0.5)))
    u = z[:, 0:D]
    v = _rms(z[:, D:2 * D], vg_ref[...]).astype(BF)
    n_chunks = TM // CHUNK
    gch = D // GM_GROUPS
    for g in range(GM_GROUPS):
        csl = slice(g * gch, (g + 1) * gch)
        vg = jnp.concatenate([v[n * CHUNK:(n + 1) * CHUNK, csl] for n in range(n_chunks)], axis=1)
        sg = _dot(ws_ref[g], vg) + bst_ref[:, g:g + 1]
        for n in range(n_chunks):
            rsl = slice(n * CHUNK, (n + 1) * CHUNK)
            gated_scr[rsl, csl] = (u[rsl, csl] * sg[:, n * gch:(n + 1) * gch]).astype(BF)
    out = _dot(gated_scr[...], wo_ref[...])
    gate = mod_ref[0, :, 2 * D:3 * D]
    x_new = x + gate * out
    xo_ref[...] = x_new
    _ffn_prep(x_new, mod_ref, gf_ref, rwt_ref, rb_ref, h2c_ref, gidx_ref, grank_ref, wa_ref, cnt_ref,
              run_scr)


def _odd(pending, mod, l, g_mix, win, v_g, w_s, b_st, w_out, g_ffn, rwt, rb):
    nt = T // TM
    full = lambda shape: pl.BlockSpec(shape, lambda t: (0,) * len(shape))
    return pl.pallas_call(
        _odd_kernel,
        grid=(nt,),
        in_specs=_pending_specs(pending["layer"]) + [
            pl.BlockSpec((1, 1, 6 * D), lambda t: (l * MOD_ROWS + _mod_row(t, TM), 0, 0)),
            full((1, D)), full((D, 2 * D)), full((1, D)),
            full((GM_GROUPS, CHUNK, CHUNK)), full((CHUNK, GM_GROUPS)), full((D, D)),
            full((1, D)), full((N_EXPERTS, D)), full((N_EXPERTS, 1)),
        ],
        out_specs=_prep_out_specs(),
        out_shape=_prep_out_shapes(),
        scratch_shapes=[pltpu.VMEM((TM, D), BF), pltpu.VMEM((NB_PAD, 128), F32)]
        + PENDING_SCRATCH,
        compiler_params=_cp(1),
        name="odd",
    )(*_pending_args(pending), mod, g_mix, win, v_g, w_s, b_st, w_out, g_ffn, rwt, rb)


def _routing_plan(bucket, brank, cnt):
    counts = cnt[:NB, 0].astype(jnp.int32)
    padded = (counts + (TMM - 1)) // TMM * TMM
    ends = jnp.cumsum(padded)
    base = ends - padded
    b = bucket.reshape(T)
    hit = b[:, None] == jnp.arange(NB, dtype=jnp.int32)[None, :]
    slot = brank.reshape(T) + jnp.sum(jnp.where(hit, base[None, :], 0), axis=1)
    n_steps = ends[NB - 1] // TMM
    starts = jnp.minimum(jnp.arange(MAX_STEPS, dtype=jnp.int32) * TMM, ends[NB - 1] - TMM)
    bucket_of_step = jnp.sum((starts[:, None] >= ends[None, :NB - 1]).astype(jnp.int32), axis=1)
    pair_of_step = bucket_of_step % len(PAIRS)
    pairs = jnp.asarray(PAIRS, dtype=jnp.int32)
    sched = jnp.concatenate([bucket_of_step // len(PAIRS), pairs[pair_of_step, 0], pairs[pair_of_step, 1],
                             n_steps[None]]).astype(jnp.int32)
    fill_plan = jnp.concatenate([base + counts, ends, n_steps[None]]).astype(jnp.int32)
    return slot.astype(jnp.int32), sched, fill_plan


def _row_tile(ref, row, sub):
    return ref.at[pl.ds(pl.multiple_of(row * sub, sub), sub), :]


def _row_copies(n_rows, make_copy):
    def body(i, carry):
        for j in range(ISSUE_UNROLL):
            make_copy(i * ISSUE_UNROLL + j).start(priority=j % 2)
        return carry
    lax.fori_loop(0, n_rows // ISSUE_UNROLL, body, 0)


def _dispatch_kernel(pad_ref, slot_ref, wa_ref, h_ref, o_hbm, ws_ref, zero_scr, zvec_scr, zsem, sem):
    t = pl.program_id(0)

    @pl.when(t == 0)
    def _():
        zvec_scr[...] = jnp.zeros_like(zvec_scr)
        clear = pltpu.make_async_copy(zvec_scr, ws_ref, zsem.at[0])
        clear.start()
        clear.wait()

    @pl.when(t == 0)
    def _():
        zero_scr[...] = jnp.zeros_like(zero_scr)

        def chunk_fill(row):
            return pltpu.make_async_copy(
                zero_scr.at[pl.ds(0, FILL * X_SUB), :],
                o_hbm.at[pl.ds(pl.multiple_of(row * X_SUB, X_SUB), FILL * X_SUB), :], zsem.at[0])

        def tile_fill(s):
            return pltpu.make_async_copy(
                zero_scr, o_hbm.at[pl.ds(s * TMM * X_SUB, TMM * X_SUB), :], zsem.at[1])

        for phase in ("start", "wait"):
            for k in range(NB):
                for c in range(TMM // FILL):
                    top = pad_ref[NB + k] - c * FILL

                    @pl.when(top > pad_ref[k])
                    def _():
                        getattr(chunk_fill(top - FILL), phase)()
        for phase in ("start", "wait"):
            for s in range(T // TMM, SORTED_TILES):
                @pl.when(s >= pad_ref[2 * NB])
                def _():
                    getattr(tile_fill(s), phase)()

    _row_copies(TM, lambda r: pltpu.make_async_copy(
        _row_tile(h_ref, r, X_SUB), _row_tile(o_hbm, slot_ref[r], X_SUB), sem))

    def put(r, carry):
        ws_ref[slot_ref[r]] = wa_ref[r]
        return carry
    lax.fori_loop(0, TM, put, 0, unroll=ISSUE_UNROLL)
    pltpu.make_async_copy(h_ref, o_hbm.at[pl.ds(0, TM * X_SUB), :], sem).wait()


def _dispatch(pad_start, slot, wa, h2c):
    nt = T // TM
    return pl.pallas_call(
        _dispatch_kernel,
        grid_spec=pltpu.PrefetchScalarGridSpec(
            num_scalar_prefetch=1,
            grid=(nt,),
            in_specs=[
                pl.BlockSpec((TM,), lambda t, pad: (t,), memory_space=pltpu.SMEM),
                pl.BlockSpec((TM,), lambda t, pad: (t,), memory_space=pltpu.SMEM),
                pl.BlockSpec((TM * X_SUB, LANES), lambda t, pad: (t, 0)),
            ],
            out_specs=[pl.BlockSpec(memory_space=pl.ANY), pl.BlockSpec(memory_space=pltpu.SMEM)],
            scratch_shapes=[pltpu.VMEM((TMM * X_SUB, LANES), F32), pltpu.VMEM((SORTED_ROWS,), F32),
                            pltpu.SemaphoreType.DMA((2,)), pltpu.SemaphoreType.DMA(())],
        ),
        out_shape=[jax.ShapeDtypeStruct((SORTED_ROWS * X_SUB, LANES), F32),
                   jax.ShapeDtypeStruct((SORTED_ROWS,), F32)],
        compiler_params=_cp(1),
        name="dispatch",
    )(pad_start, slot, wa, h2c)


def _experts_kernel(sched_ref, xs_ref, ws_ref, w1_ref, w3_ref, w2_ref, o_ref, w1b, w3b, w2b, wt_scr):
    s = pl.program_id(0)

    @pl.when(jnp.logical_or(s == 0, sched_ref[s] != sched_ref[jnp.maximum(s - 1, 0)]))
    def _():
        w1b[...] = w1_ref[...].astype(BF)
        w3b[...] = w3_ref[...].astype(BF)
        w2b[...] = w2_ref[...].astype(BF)

    n_steps = sched_ref[3 * MAX_STEPS]

    @pl.when(s < n_steps)
    def _():
        h = jnp.concatenate([xs_ref[pl.ds(k, TMM, stride=X_SUB), :] for k in range(X_SUB)],
                            axis=1).astype(BF)
        wt_scr[...] = jnp.zeros_like(wt_scr)
        wt_scr[0:1, :] = ws_ref[...]
        w_first = wt_scr[...].T[:, 0:1]
        weights = (w_first, 1.0 - w_first)
        acc = None
        for j in range(2):
            e = sched_ref[(1 + j) * MAX_STEPS + s]
            a = _dot(h, w1b[e])
            b = _dot(h, w3b[e])
            hid = (a * (1.0 / (1.0 + jnp.exp(-a))) * b * weights[j]).astype(BF)
            y = _dot(hid, w2b[e])
            acc = y if acc is None else acc + y
        for k in range(D // LANES):
            o_ref[pl.ds(k, TMM, stride=Y_SUB), :] = acc[:, k * LANES:(k + 1) * LANES]

    @pl.when(s >= n_steps)
    def _():
        o_ref[...] = jnp.zeros_like(o_ref)


def _experts(sched, xs, ws, l, w1, w3, w2):
    row_blk = lambda s, sched: (jnp.maximum(jnp.minimum(s, sched[3 * MAX_STEPS] - 1), 0), 0)
    grp_blk = lambda s, sched: (l, sched[s], 0, 0)
    return pl.pallas_call(
        _experts_kernel,
        grid_spec=pltpu.PrefetchScalarGridSpec(
            num_scalar_prefetch=1,
            grid=(MAX_STEPS,),
            in_specs=[
                pl.BlockSpec((TMM * X_SUB, LANES), row_blk),
                pl.BlockSpec((None, 1, TMM), lambda s, sched: row_blk(s, sched) + (0,)),
                pl.BlockSpec((None, EPG, D, D_EXPERT), grp_blk),
                pl.BlockSpec((None, EPG, D, D_EXPERT), grp_blk),
                pl.BlockSpec((None, EPG, D_EXPERT, D), grp_blk),
            ],
            out_specs=pl.BlockSpec((TMM * Y_SUB, LANES), lambda s, sched: (s, 0)),
            scratch_shapes=[pltpu.VMEM((EPG, D, D_EXPERT), BF), pltpu.VMEM((EPG, D, D_EXPERT), BF),
                            pltpu.VMEM((EPG, D_EXPERT, D), BF), pltpu.VMEM((LANES, TMM), F32)],
        ),
        out_shape=jax.ShapeDtypeStruct((MAX_STEPS * TMM * Y_SUB, LANES), F32),
        compiler_params=_cp(1),
        name="experts",
    )(sched, xs, ws, w1, w3, w2)


def _pending_specs(l):
    nt = T // TM
    return [
        pl.BlockSpec((TM,), lambda t: (t,), memory_space=pltpu.SMEM),
        pl.BlockSpec((TM,), lambda t: (jnp.minimum(t + 1, nt - 1),), memory_space=pltpu.SMEM),
        pl.BlockSpec(memory_space=pl.ANY),
        pl.BlockSpec((TM, D), lambda t: (t, 0)),
        pl.BlockSpec((1, 1, 6 * D), lambda t: (l * MOD_ROWS + _mod_row(t, TM), 0, 0)),
    ]


def _pending_args(pending):
    return (pending["slot"], pending["slot"], pending["ys"], pending["x"], pending["mod"])


PENDING_SCRATCH = [pltpu.VMEM((2, TM * Y_SUB, LANES), F32), pltpu.SemaphoreType.DMA((2,))]
N_PENDING = 5


def _apply_pending(slot_ref, slot_next_ref, ys_hbm, x_ref, mod_ref, buf, sem):
    t = pl.program_id(0)
    nt = pl.num_programs(0)
    cur = t % 2

    def gather(idx_ref, b):
        _row_copies(TM, lambda r: pltpu.make_async_copy(
            _row_tile(ys_hbm, idx_ref[r], Y_SUB), _row_tile(buf.at[b], r, Y_SUB), sem.at[b]))

    @pl.when(t == 0)
    def _():
        gather(slot_ref, 0)

    @pl.when(t + 1 < nt)
    def _():
        gather(slot_next_ref, 1 - cur)

    pltpu.make_async_copy(ys_hbm.at[pl.ds(0, TM * Y_SUB), :], buf.at[cur], sem.at[cur]).wait()
    gate = mod_ref[0, :, 5 * D:6 * D]
    y = jnp.concatenate([buf[cur, pl.ds(k, TM, stride=Y_SUB), :] for k in range(Y_SUB)], axis=1)
    return x_ref[...] + gate * y


def _final_kernel(slot_ref, slot_next_ref, ys_hbm, x_ref, mod_ref, gfin_ref, op_ref, os_ref, buf, sem):
    t = pl.program_id(0)
    x_new = _apply_pending(slot_ref, slot_next_ref, ys_hbm, x_ref, mod_ref, buf, sem)
    y_out = _rms(x_new, gfin_ref[...])
    npt = NP_TOK // TM

    @pl.when(t < npt)
    def _():
        op_ref[...] = y_out

    @pl.when(t >= npt)
    def _():
        os_ref[...] = y_out


def _final(pending, g_final):
    nt = T // TM
    npt = NP_TOK // TM
    return pl.pallas_call(
        _final_kernel,
        grid=(nt,),
        in_specs=_pending_specs(pending["layer"]) + [pl.BlockSpec((1, D), lambda t: (0, 0))],
        out_specs=[pl.BlockSpec((TM, D), lambda t: (jnp.minimum(t, npt - 1), 0)),
                   pl.BlockSpec((TM, D), lambda t: (jnp.maximum(t - npt, 0), 0))],
        out_shape=[jax.ShapeDtypeStruct((NP_TOK, D), F32), jax.ShapeDtypeStruct((NS_TOK, D), F32)],
        scratch_shapes=PENDING_SCRATCH,
        compiler_params=_cp(1),
        name="final",
    )(*_pending_args(pending), g_final)


def _moe(x, mod, l, rows, bucket, brank, wa, cnt, w1, w3, w2):
    slot, sched, fill_plan = _routing_plan(bucket, brank, cnt)
    xs, ws = _dispatch(fill_plan, slot, wa.reshape(T), rows)
    ys = _experts(sched, xs, ws.reshape(SORTED_TILES, 1, TMM), l, w1, w3, w2)
    return {"slot": slot, "ys": ys, "x": x, "mod": mod, "layer": l}


def _rope_tables():
    pos = jnp.arange(DEC_SEQ)
    r = (pos // GRID_W).astype(F32)
    col = (pos % GRID_W).astype(F32)
    inv = ROPE_BASE ** (-jnp.arange(AX_FREQS, dtype=F32) / AX_FREQS)
    ang = jnp.stack([r[:, None] * inv, col[:, None] * inv], axis=1)
    cos = jnp.cos(ang)[:, :, None, :]
    sin = jnp.sin(ang)[:, :, None, :]
    c32 = jnp.broadcast_to(cos, (DEC_SEQ, 2, 2, AX_FREQS)).reshape(DEC_SEQ, ROPE)
    s32 = jnp.concatenate([-sin, sin], axis=2).reshape(DEC_SEQ, ROPE)
    pad = HEAD_PAD - NOPE - ROPE
    c = jnp.concatenate([jnp.ones((DEC_SEQ, NOPE), F32), c32, jnp.zeros((DEC_SEQ, pad), F32)], axis=1)
    s = jnp.concatenate([jnp.zeros((DEC_SEQ, NOPE), F32), s32, jnp.zeros((DEC_SEQ, pad), F32)], axis=1)
    c_id = jnp.concatenate([jnp.ones((TM, NOPE + ROPE), F32), jnp.zeros((TM, pad), F32)], axis=1)
    s_id = jnp.zeros((TM, HEAD_PAD), F32)
    return jnp.concatenate([c, c_id], axis=0), jnp.concatenate([s, s_id], axis=0)


def _swap_halves(w):
    lead = w.shape[:-1]
    return w.reshape(lead + (2, 2, AX_FREQS))[..., ::-1, :].reshape(lead + (ROPE,))


def _even_weights(w_in, w_q_up, w_kv_up):
    k_in = w_in.shape[0]
    base = 3 * D_CONV + Q_RANK + KV_RANK
    w_kr = w_in[:, base:base + ROPE]
    z = lambda n: jnp.zeros((k_in, n), F32)
    win_ext = jnp.concatenate([
        w_in[:, :base],
        w_kr, z(HEAD_PAD - ROPE),
        z(NOPE), w_kr, z(HEAD_PAD - NOPE - ROPE),
        z(NOPE), _swap_halves(w_kr), z(HEAD_PAD - NOPE - ROPE),
    ], axis=1).astype(BF)
    wq = w_q_up.reshape(Q_RANK, HEADS, QK_DIM)
    zq = lambda n: jnp.zeros((Q_RANK, HEADS, n), F32)
    wq1 = jnp.concatenate([wq, zq(HEAD_PAD - QK_DIM)], axis=2).reshape(Q_RANK, HEADS * HEAD_PAD).astype(BF)
    wq2 = jnp.concatenate([zq(NOPE), _swap_halves(wq[:, :, NOPE:]), zq(HEAD_PAD - QK_DIM)],
                          axis=2).reshape(Q_RANK, HEADS * HEAD_PAD).astype(BF)
    wkv = w_kv_up.reshape(KV_RANK, HEADS, NOPE + V_DIM)
    wk = jnp.concatenate([wkv[:, :, :NOPE], jnp.zeros((KV_RANK, HEADS, HEAD_PAD - NOPE), F32)],
                         axis=2).reshape(KV_RANK, HEADS * HEAD_PAD).astype(BF)
    wv = wkv[:, :, NOPE:].reshape(KV_RANK, HEADS * V_DIM).astype(BF)
    return win_ext, wq1, wq2, wk, wv


def kernel(x_prompt, x_sample, cache_ckv, cache_krope, c, c_ctx, w_ada, b_ada, g_mix, g_ffn, g_final,
           ev_w_in, conv_w, q_norm_g, w_q_up, kv_norm_g, w_kv_up, ev_w_out,
           gm_w_in, gm_v_g, gm_w_s, gm_b_s, gm_w_out, router_w, router_b, moe_w1, moe_w3, moe_w2):
    n_even = ev_w_in.shape[0]
    x_parts = (x_prompt.reshape(NP_TOK, D), x_sample.reshape(NS_TOK, D))
    cc =jnp.concatenate([c, c_ctx[None, :], jnp.zeros((MOD_ROWS - DEC_BATCH - 1, D), F32)], axis=0)
    mod = _ada(cc, w_ada, b_ada).reshape(DEPTH * MOD_ROWS, 1, 6 * D)

    rope_c, rope_s = _rope_tables()
    rwt = router_w.T
    rb = router_b.reshape(N_EXPERTS, 1)
    ev = [_even_weights(ev_w_in[i], w_q_up[i], w_kv_up[i]) for i in range(n_even)]
    kr_ctx = jnp.pad(cache_krope, ((0, 0), (0, 0), (0, 0), (NOPE, HEAD_PAD - NOPE - ROPE)))
    kr_ctx = kr_ctx.transpose(1, 0, 2, 3).reshape(n_even, DEC_BATCH * PAST, HEAD_PAD)
    ckv_ctx = cache_ckv.transpose(1, 0, 2, 3).reshape(n_even, DEC_BATCH * PAST, KV_RANK)
    kc, vc = _ctx_kv(ckv_ctx, kr_ctx, jnp.stack([e[3] for e in ev]), jnp.stack([e[4] for e in ev]))

    ckv_states, kr_states = [], []
    pending = None
    for l in range(DEPTH):
        i = l // 2
        if l % 2 == 0:
            win_ext, wq1, wq2, wk, wv = ev[i]
            outs = _even_in(
                None if pending else x_parts, pending, mod, l, g_mix[l][None, :], win_ext,
                q_norm_g[i][None, :], wq1, wq2, kv_norm_g[i][None, :], wk, wv, rope_c, rope_s)
            bg, cv, q, k, v, ckv, kr = outs[:7]
            if pending:
                x_parts = (outs[7],)
            attn_parts = (_attn_prompt(q, k, v), _attn_sample(q, k, v, kc, vc, i))
            x, *routed = _even_out(x_parts, mod, l, bg, cv, attn_parts, conv_w[i],
                                                 ev_w_out[i].astype(BF), g_ffn[l][None, :], rwt, rb)
            ckv_states.append(ckv[:NP_TOK].reshape(BATCH, SEQ, KV_RANK))
            kr_states.append(kr[:NP_TOK, :ROPE].reshape(BATCH, SEQ, ROPE))
        else:
            x, *routed = _odd(pending, mod, l, g_mix[l][None, :], gm_w_in[i].astype(BF),
                                            gm_v_g[i][None, :], gm_w_s[i].astype(BF), gm_b_s[i].T,
                                            gm_w_out[i].astype(BF), g_ffn[l][None, :], rwt, rb)
        pending = _moe(x, mod, l, *routed, moe_w1, moe_w3, moe_w2)

    y_prompt, y_sample = _final(pending, g_final[None, :])
    y_prompt = y_prompt.reshape(BATCH, SEQ, D)
    y_sample = y_sample.reshape(DEC_BATCH, DEC_SEQ, D)
    return (y_prompt, y_sample, jnp.stack(ckv_states, axis=1), jnp.stack(kr_states, axis=1))
```

```python
import functools
import math

import jax
import jax.numpy as jnp
from jax import lax
from jax.experimental import pallas as pl
from jax.experimental.pallas import tpu as pltpu

D = 1024
BATCH, SEQ = 32, 256
DEC_BATCH, DEC_SEQ = 8, 2048
PAST = 256
DEPTH = 4
GRID_W = 64
D_CONV = 512
HEADS = 8
NOPE, ROPE, V_DIM = 64, 32, 64
QK_DIM = NOPE + ROPE
Q_RANK, KV_RANK = 384, 256
AX_FREQS = ROPE // 4
ROPE_BASE = 10000.0
CHUNK = 128
GM_GROUPS = 8
N_EXPERTS, N_GROUPS, EPG = 16, 4, 4
D_EXPERT = 256
EPS = 1e-6

NP_TOK = BATCH * SEQ
NS_TOK = DEC_BATCH * DEC_SEQ
T = NP_TOK + NS_TOK
MOD_ROWS = 16
HEAD_PAD = 128
HALO = 16

TM = 512
PAIRS = ((0, 1), (0, 2), (0, 3), (1, 2), (1, 3), (2, 3))
NB = N_GROUPS * len(PAIRS)
NB_PAD = 32
TMM = 512
FILL = 128
MAX_STEPS = T // TMM + NB
SORTED_TILES = MAX_STEPS + MAX_STEPS % 2
SORTED_ROWS = SORTED_TILES * TMM
LANES = 128
Y_SUB = D // LANES
X_SUB = Y_SUB
ISSUE_UNROLL = 8
TQ = 256
ATTN_PAIRS = 4
VMEM_LIMIT = 56 * 1024 * 1024

BF = jnp.bfloat16
F32 = jnp.float32


def _cp(n_axes):
    return pltpu.CompilerParams(dimension_semantics=("arbitrary",) * n_axes,
                                vmem_limit_bytes=VMEM_LIMIT)


def _mod_row(t, tm):
    n_prompt_tiles = NP_TOK // tm
    per_seq = DEC_SEQ // tm
    return jnp.where(t < n_prompt_tiles, DEC_BATCH, (t - n_prompt_tiles) // per_seq)


def _rms(x, g):
    return x * lax.rsqrt(jnp.mean(x * x, axis=-1, keepdims=True) + EPS) * g


def _norm_mod(x, g, shift, scale):
    return x * lax.rsqrt(jnp.mean(x * x, axis=-1, keepdims=True) + EPS) * (g * (1.0 + scale)) + shift


def _dot(a, b):
    return jnp.dot(a, b, preferred_element_type=F32)


def _dot_nt(a, b, precision=None):
    return lax.dot_general(a, b, (((1,), (1,)), ((), ())), precision=precision,
                           preferred_element_type=F32)


ADA_TN = 1536


def _ada_kernel(cc_ref, w_ref, b_ref, o_ref):
    cc = cc_ref[...]
    s = (cc / (1.0 + jnp.exp(-cc))).astype(BF)
    o_ref[0] = _dot(s, w_ref[0].astype(BF)) + b_ref[0]


def _ada(cc, w_ada, b_ada):
    n = 6 * D
    return pl.pallas_call(
        _ada_kernel,
        grid=(DEPTH, n // ADA_TN),
        in_specs=[
            pl.BlockSpec((MOD_ROWS, D), lambda l, j: (0, 0)),
            pl.BlockSpec((1, D, ADA_TN), lambda l, j: (l, 0, j)),
            pl.BlockSpec((1, 1, ADA_TN), lambda l, j: (l, 0, j)),
        ],
        out_specs=pl.BlockSpec((1, MOD_ROWS, ADA_TN), lambda l, j: (l, 0, j)),
        out_shape=jax.ShapeDtypeStruct((DEPTH, MOD_ROWS, n), F32),
        compiler_params=_cp(2),
        name="ada",
    )(cc, w_ada, b_ada.reshape(DEPTH, 1, n))


def _route(logits_t, rb):
    sc = 1.0 / (1.0 + jnp.exp(-logits_t))
    sel = sc + rb
    rows = [sel[e:e + 1, :] for e in range(N_EXPERTS)]
    srows = [sc[e:e + 1, :] for e in range(N_EXPERTS)]

    def top2sum(a, b, c, d):
        hi1, lo1 = jnp.maximum(a, b), jnp.minimum(a, b)
        hi2, lo2 = jnp.maximum(c, d), jnp.minimum(c, d)
        return jnp.maximum(hi1, hi2) + jnp.maximum(jnp.minimum(hi1, hi2), jnp.maximum(lo1, lo2))

    gs = [top2sum(*rows[EPG * g:EPG * (g + 1)]) for g in range(N_GROUPS)]
    best = gs[0]
    gidx = jnp.zeros_like(best, dtype=jnp.int32)
    for g in range(1, N_GROUPS):
        upd = gs[g] > best
        best = jnp.where(upd, gs[g], best)
        gidx = jnp.where(upd, g, gidx)

    picked = []
    for g in range(N_GROUPS):
        grp = rows[EPG * g:EPG * (g + 1)]
        in_g = gidx == g
        for j in range(EPG):
            rank = jnp.zeros_like(gidx)
            for k in range(EPG):
                if k == j:
                    continue
                ahead = grp[k] > grp[j]
                if k < j:
                    ahead = ahead | (grp[k] == grp[j])
                rank = rank + ahead.astype(jnp.int32)
            picked.append(in_g & (rank < 2))
    w = [jnp.where(picked[e], srows[e], 0.0) for e in range(N_EXPERTS)]
    wsum = w[0]
    for e in range(1, N_EXPERTS):
        wsum = wsum + w[e]
    inv = 1.0 / wsum
    pj, wj = [], []
    for j in range(EPG):
        hit, acc = picked[j], w[j]
        for g in range(1, N_GROUPS):
            hit = hit | picked[EPG * g + j]
            acc = acc + w[EPG * g + j]
        pj.append(hit)
        wj.append(acc * inv)
    pair = jnp.full_like(gidx, len(PAIRS) - 1)
    for idx in range(len(PAIRS) - 2, -1, -1):
        a, b = PAIRS[idx]
        pair = jnp.where(pj[a] & pj[b], idx, pair)
    w_a = jnp.where(pj[0], wj[0], jnp.where(pj[1], wj[1], wj[2]))
    w_b = jnp.where(pj[3], wj[3], jnp.where(pj[2], wj[2], wj[1]))
    return (w_a, w_b), gidx * len(PAIRS) + pair


def _ffn_prep(x_new, mod_ref, gf_ref, rwt_ref, rb_ref, h2c_ref, gidx_ref, grank_ref, wa_ref, cnt_ref,
              run_scr):
    t = pl.program_id(0)
    tm = x_new.shape[0]
    shift = mod_ref[0, :, 3 * D:4 * D]
    scale = mod_ref[0, :, 4 * D:5 * D]
    h2 = _norm_mod(x_new, gf_ref[...], shift, scale)
    h_hi = h2.astype(BF)
    h_lo = (h2 - h_hi.astype(F32)).astype(BF)
    by_hi = _dot_nt(rwt_ref[...], h_hi)
    by_lo = _dot_nt(rwt_ref[0:N_EXPERTS, :], h_lo)
    logits_t = by_hi[0:N_EXPERTS, :] + by_hi[N_EXPERTS:2 * N_EXPERTS, :] + by_lo
    wab, gidx = _route(logits_t, rb_ref[...])
    wa_ref[0] = wab[0]

    for k in range(X_SUB):
        h2c_ref[pl.ds(k, tm, stride=X_SUB), :] = h2[:, k * LANES:(k + 1) * LANES]

    @pl.when(t == 0)
    def _():
        run_scr[...] = jnp.zeros_like(run_scr)

    onehot = (lax.broadcasted_iota(jnp.int32, (NB_PAD, tm), 0) == gidx).astype(F32)
    earlier = (lax.broadcasted_iota(jnp.int32, (tm, tm), 0)
               < lax.broadcasted_iota(jnp.int32, (tm, tm), 1)).astype(BF)
    rank = _dot(onehot.astype(BF), earlier)
    run = run_scr[:, 0:1]
    grank = jnp.sum(onehot * (rank + run), axis=0, keepdims=True)
    gidx_ref[0] = gidx
    grank_ref[0] = grank.astype(jnp.int32)
    run_scr[...] = run_scr[...] + jnp.sum(onehot, axis=1, keepdims=True)
    cnt_ref[...] = run_scr[...]


def _prep_out_specs():
    tok = lambda w: pl.BlockSpec((TM, w), lambda t: (t, 0))
    lanes = pl.BlockSpec((1, 1, TM), lambda t: (t, 0, 0))
    rows = pl.BlockSpec((TM * X_SUB, LANES), lambda t: (t, 0))
    return [tok(D), rows, lanes, lanes, lanes, pl.BlockSpec((NB_PAD, 128), lambda t: (0, 0))]


def _prep_out_shapes():
    nt = T // TM
    return [jax.ShapeDtypeStruct((T, D), F32), jax.ShapeDtypeStruct((T * X_SUB, LANES), F32),
            jax.ShapeDtypeStruct((nt, 1, TM), jnp.int32), jax.ShapeDtypeStruct((nt, 1, TM), jnp.int32),
            jax.ShapeDtypeStruct((nt, 1, TM), F32), jax.ShapeDtypeStruct((NB_PAD, 128), F32)]


EV_EXT = 3 * D_CONV + Q_RANK + KV_RANK + 3 * HEAD_PAD


def _tok_specs(parts, width):
    npt = NP_TOK // TM
    if len(parts) == 1:
        return [pl.BlockSpec((TM, width), lambda t: (t, 0))]
    return [pl.BlockSpec((TM, width), lambda t: (jnp.minimum(t, npt - 1), 0)),
            pl.BlockSpec((TM, width), lambda t: (jnp.maximum(t - npt, 0), 0))]


def _tok_load(refs):
    if len(refs) == 1:
        return refs[0][...]
    return jnp.where(pl.program_id(0) < NP_TOK // TM, refs[0][...], refs[1][...])


def _even_in_kernel(n_x, *refs):
    if n_x:
        x_refs, refs = refs[:n_x], refs[n_x:]
        x = _tok_load(x_refs)
    else:
        pend, refs, scratch = refs[:N_PENDING], refs[N_PENDING:-2], refs[-2:]
        refs, xo_ref = refs[:-1], refs[-1]
        x = _apply_pending(*pend, *scratch)
        xo_ref[...] = x
    (mod_ref, g_ref, win_ref, qg_ref, wq1_ref, wq2_ref, kg_ref, wk_ref, wv_ref, rc_ref, rs_ref,
     bg_ref, cv_ref, q_ref, k_ref, v_ref, ckv_ref, kr_ref) = refs
    shift = mod_ref[0, :, 0:D]
    scale = mod_ref[0, :, D:2 * D]
    h = _norm_mod(x, g_ref[...], shift, scale).astype(BF)
    proj = _dot(h, win_ref[...])
    o = 0
    b_g = proj[:, o:o + D_CONV]; o += D_CONV
    c_g = proj[:, o:o + D_CONV]; o += D_CONV
    v_in = proj[:, o:o + D_CONV]; o += D_CONV
    q_a = proj[:, o:o + Q_RANK]; o += Q_RANK
    kv_a = proj[:, o:o + KV_RANK]; o += KV_RANK
    kr_raw = proj[:, o:o + HEAD_PAD]; o += HEAD_PAD
    kr_cat = proj[:, o:o + HEAD_PAD]; o += HEAD_PAD
    kr_sw = proj[:, o:o + HEAD_PAD]

    bg_ref[...] = b_g.astype(BF)
    cv_ref[...] = (c_g * v_in).astype(BF)
    kr_ref[...] = kr_raw

    rc = rc_ref[...]
    rs = rs_ref[...]
    qn = _rms(q_a, qg_ref[...]).astype(BF)
    q1 = _dot(qn, wq1_ref[...])
    q2 = _dot(qn, wq2_ref[...])
    qscale = QK_DIM ** -0.5
    for hd in range(HEADS):
        sl = slice(hd * HEAD_PAD, (hd + 1) * HEAD_PAD)
        q_ref[:, sl] = ((q1[:, sl] * rc + q2[:, sl] * rs) * qscale).astype(BF)

    ckv = _rms(kv_a, kg_ref[...])
    ckv_ref[...] = ckv
    ckv_b = ckv.astype(BF)
    kk = _dot(ckv_b, wk_ref[...])
    kr = kr_cat * rc + kr_sw * rs
    for hd in range(HEADS):
        sl = slice(hd * HEAD_PAD, (hd + 1) * HEAD_PAD)
        k_ref[:, sl] = (kk[:, sl] + kr).astype(BF)
    v_ref[...] = _dot(ckv_b, wv_ref[...]).astype(BF)


def _even_in(x_parts, pending, mod, l, g_mix, win, qg, wq1, wq2, kg, wk, wv, rope_c, rope_s):
    nt = T // TM
    npt = NP_TOK // TM
    per_seq = DEC_SEQ // TM
    ident_blk = DEC_SEQ // TM

    def rope_idx(t):
        return (jnp.where(t < npt, ident_blk, (t - npt) % per_seq), 0)

    full = lambda shape: pl.BlockSpec(shape, lambda t: (0,) * len(shape))
    tok = lambda w: pl.BlockSpec((TM, w), lambda t: (t, 0))
    out_specs = [tok(D_CONV), tok(D_CONV), tok(HEADS * HEAD_PAD), tok(HEADS * HEAD_PAD),
                 tok(HEADS * V_DIM), tok(KV_RANK), tok(HEAD_PAD)]
    out_shape = [
        jax.ShapeDtypeStruct((T, D_CONV), BF), jax.ShapeDtypeStruct((T, D_CONV), BF),
        jax.ShapeDtypeStruct((T, HEADS * HEAD_PAD), BF), jax.ShapeDtypeStruct((T, HEADS * HEAD_PAD), BF),
        jax.ShapeDtypeStruct((T, HEADS * V_DIM), BF),
        jax.ShapeDtypeStruct((T, KV_RANK), F32), jax.ShapeDtypeStruct((T, HEAD_PAD), F32),
    ]
    if pending is None:
        lead_specs, lead_args, scratch = _tok_specs(x_parts, D), tuple(x_parts), []
    else:
        lead_specs, lead_args, scratch = _pending_specs(pending["layer"]), _pending_args(pending), PENDING_SCRATCH
        out_specs.append(tok(D))
        out_shape.append(jax.ShapeDtypeStruct((T, D), F32))
    return pl.pallas_call(
        functools.partial(_even_in_kernel, 0 if pending is not None else len(x_parts)),
        grid=(nt,),
        in_specs=lead_specs + [
            pl.BlockSpec((1, 1, 6 * D), lambda t: (l * MOD_ROWS + _mod_row(t, TM), 0, 0)),
            full((1, D)), full((D, EV_EXT)), full((1, Q_RANK)),
            full((Q_RANK, HEADS * HEAD_PAD)), full((Q_RANK, HEADS * HEAD_PAD)),
            full((1, KV_RANK)), full((KV_RANK, HEADS * HEAD_PAD)), full((KV_RANK, HEADS * V_DIM)),
            pl.BlockSpec((TM, HEAD_PAD), rope_idx), pl.BlockSpec((TM, HEAD_PAD), rope_idx),
        ],
        out_specs=out_specs,
        out_shape=out_shape,
        scratch_shapes=scratch,
        compiler_params=_cp(1),
        name="even_in",
    )(*lead_args, mod, g_mix, win, qg, wq1, wq2, kg, wk, wv, rope_c, rope_s)


CTX_TM = 512


def _ctx_kv_kernel(ckv_ref, kr_ref, wk_ref, wv_ref, k_ref, v_ref):
    ckv_b = ckv_ref[0].astype(BF)
    kk = _dot(ckv_b, wk_ref[0])
    kr = kr_ref[0]
    for hd in range(HEADS):
        sl = slice(hd * HEAD_PAD, (hd + 1) * HEAD_PAD)
        k_ref[0, :, sl] = (kk[:, sl] + kr).astype(BF)
    v_ref[0] = _dot(ckv_b, wv_ref[0]).astype(BF)


def _ctx_kv(ckv_all, kr_all, wk_all, wv_all):
    n_even = ckv_all.shape[0]
    rows = DEC_BATCH * PAST
    return pl.pallas_call(
        _ctx_kv_kernel,
        grid=(n_even, rows // CTX_TM),
        in_specs=[
            pl.BlockSpec((1, CTX_TM, KV_RANK), lambda i, t: (i, t, 0)),
            pl.BlockSpec((1, CTX_TM, HEAD_PAD), lambda i, t: (i, t, 0)),
            pl.BlockSpec((1, KV_RANK, HEADS * HEAD_PAD), lambda i, t: (i, 0, 0)),
            pl.BlockSpec((1, KV_RANK, HEADS * V_DIM), lambda i, t: (i, 0, 0)),
        ],
        out_specs=[
            pl.BlockSpec((1, CTX_TM, HEADS * HEAD_PAD), lambda i, t: (i, t, 0)),
            pl.BlockSpec((1, CTX_TM, HEADS * V_DIM), lambda i, t: (i, t, 0)),
        ],
        out_shape=[jax.ShapeDtypeStruct((n_even, rows, HEADS * HEAD_PAD), BF),
                   jax.ShapeDtypeStruct((n_even, rows, HEADS * V_DIM), BF)],
        compiler_params=_cp(2),
        name="ctx_kv",
    )(ckv_all, kr_all, wk_all, wv_all)


def _attn_body(n_pairs, has_ctx, q_ref, k_ref, v_ref, *rest):
    if has_ctx:
        kc_ref, vc_ref, o_ref = rest
    else:
        (o_ref,) = rest
    lane = lax.broadcasted_iota(jnp.int32, (1, 2 * V_DIM), 1)
    w = 2 * V_DIM
    ones = jnp.ones((k_ref.shape[0], w), BF)
    ones_ctx = jnp.ones((PAST, w), BF)
    for pr in range(n_pairs):
        vsl = slice(pr * w, (pr + 1) * w)
        v = jnp.concatenate([v_ref[:, vsl], ones], axis=1)
        if has_ctx:
            vc = jnp.concatenate([vc_ref[0, :, vsl], ones_ctx], axis=1)
        outs = []
        for sub in range(2):
            hsl = slice((2 * pr + sub) * HEAD_PAD, (2 * pr + sub + 1) * HEAD_PAD)
            q = q_ref[:, hsl]
            s1 = _dot_nt(q, k_ref[:, hsl])
            m = jnp.max(s1, axis=-1, keepdims=True)
            if has_ctx:
                s2 = _dot_nt(q, kc_ref[0, :, hsl])
                m = jnp.maximum(m, jnp.max(s2, axis=-1, keepdims=True))
            acc = _dot(jnp.exp(s1 - m).astype(BF), v)
            if has_ctx:
                acc = acc + _dot(jnp.exp(s2 - m).astype(BF), vc)
            outs.append(acc[:, 0:w] * (1.0 / acc[:, w:2 * w]))
        o_ref[:, vsl] = jnp.where(lane < V_DIM, outs[0], outs[1]).astype(BF)


def _attn_prompt(q, k, v):
    return pl.pallas_call(
        functools.partial(_attn_body, HEADS // 2, False),
        grid=(BATCH,),
        in_specs=[
            pl.BlockSpec((SEQ, HEADS * HEAD_PAD), lambda b: (b, 0)),
            pl.BlockSpec((SEQ, HEADS * HEAD_PAD), lambda b: (b, 0)),
            pl.BlockSpec((SEQ, HEADS * V_DIM), lambda b: (b, 0)),
        ],
        out_specs=pl.BlockSpec((SEQ, HEADS * V_DIM), lambda b: (b, 0)),
        out_shape=jax.ShapeDtypeStruct((NP_TOK, HEADS * V_DIM), BF),
        compiler_params=_cp(1),
        name="attn_prompt",
    )(q, k, v)


def _attn_sample(q, k, v, kc, vc, i):
    s_blk0 = NP_TOK // DEC_SEQ
    q_blk0 = NP_TOK // TQ
    nq = DEC_SEQ // TQ
    hw = ATTN_PAIRS * 2 * HEAD_PAD
    vw = ATTN_PAIRS * 2 * V_DIM
    return pl.pallas_call(
        functools.partial(_attn_body, ATTN_PAIRS, True),
        grid=(DEC_BATCH, HEADS // (2 * ATTN_PAIRS), nq),
        in_specs=[
            pl.BlockSpec((TQ, hw), lambda b, hp, j: (q_blk0 + b * nq + j, hp)),
            pl.BlockSpec((DEC_SEQ, hw), lambda b, hp, j: (s_blk0 + b, hp)),
            pl.BlockSpec((DEC_SEQ, vw), lambda b, hp, j: (s_blk0 + b, hp)),
            pl.BlockSpec((1, PAST, hw), lambda b, hp, j: (i, b, hp)),
            pl.BlockSpec((1, PAST, vw), lambda b, hp, j: (i, b, hp)),
        ],
        out_specs=pl.BlockSpec((TQ, vw), lambda b, hp, j: (b * nq + j, hp)),
        out_shape=jax.ShapeDtypeStruct((NS_TOK, HEADS * V_DIM), BF),
        compiler_params=_cp(3),
        name="attn_sample",
    )(q, k, v, kc, vc)


def _even_out_kernel(n_x, *refs):
    x_refs, refs = refs[:n_x], refs[n_x:]
    (mod_ref, bg_ref, cv_ref, cvp_ref, cvn_ref, atp_ref, ats_ref, cw_ref, wo_ref, gf_ref, rwt_ref, rb_ref,
     xo_ref, h2c_ref, gidx_ref, grank_ref, wa_ref, cnt_ref, run_scr) = refs
    t = pl.program_id(0)
    npt = NP_TOK // TM
    per_seq = DEC_SEQ // TM
    cv = cv_ref[...].astype(F32)
    r = lax.broadcasted_iota(jnp.int32, (TM, 1), 0)
    is_prompt = t < npt
    tile_in_seq = (t - npt) % per_seq
    first_row = jnp.where(is_prompt, 0, jnp.where(tile_in_seq == 0, 0, -1))
    last_row = jnp.where(is_prompt, SEQ - 1, jnp.where(tile_in_seq == per_seq - 1, TM - 1, -1))
    period_mask = jnp.where(is_prompt, SEQ - 1, TM - 1)
    first = (r & period_mask) == first_row
    last = (r & period_mask) == last_row
    prev_row = cvp_ref[HALO - 1:HALO, :].astype(F32)
    next_row = cvn_ref[0:1, :].astype(F32)
    prev = jnp.where(r == 0, prev_row, pltpu.roll(cv, 1, 0))
    prev = jnp.where(first, 0.0, prev)
    nxt = jnp.where(r == TM - 1, next_row, pltpu.roll(cv, TM - 1, 0))
    nxt = jnp.where(last, 0.0, nxt)
    cw = cw_ref[...]
    conv = prev * cw[0:1, :] + cv * cw[1:2, :] + nxt * cw[2:3, :]
    yc = (bg_ref[...].astype(F32) * conv).astype(BF)
    attn = _tok_load((atp_ref, ats_ref))
    out = _dot(yc, wo_ref[0:D_CONV, :]) + _dot(attn, wo_ref[D_CONV:2 * D_CONV, :])
    gate = mod_ref[0, :, 2 * D:3 * D]
    x_new = _tok_load(x_refs) + gate * out
    xo_ref[...] = x_new
    _ffn_prep(x_new, mod_ref, gf_ref, rwt_ref, rb_ref, h2c_ref, gidx_ref, grank_ref, wa_ref, cnt_ref,
              run_scr)


def _even_out(x_parts, mod, l, bg, cv, attn_parts, conv_w, w_out, g_ffn, rwt, rb):
    nt = T // TM
    hb = TM // HALO
    nhb = T // HALO
    full = lambda shape: pl.BlockSpec(shape, lambda t: (0,) * len(shape))
    tok = lambda w: pl.BlockSpec((TM, w), lambda t: (t, 0))
    return pl.pallas_call(
        functools.partial(_even_out_kernel, len(x_parts)),
        grid=(nt,),
        in_specs=_tok_specs(x_parts, D) + [
            pl.BlockSpec((1, 1, 6 * D), lambda t: (l * MOD_ROWS + _mod_row(t, TM), 0, 0)),
            tok(D_CONV), tok(D_CONV),
            pl.BlockSpec((HALO, D_CONV), lambda t: (jnp.maximum(t * hb - 1, 0), 0)),
            pl.BlockSpec((HALO, D_CONV), lambda t: (jnp.minimum((t + 1) * hb, nhb - 1), 0)),
        ] + _tok_specs(attn_parts, HEADS * V_DIM) + [
            full((3, D_CONV)), full((2 * D_CONV, D)), full((1, D)),
            full((2 * N_EXPERTS, D)), full((N_EXPERTS, 1)),
        ],
        out_specs=_prep_out_specs(),
        out_shape=_prep_out_shapes(),
        scratch_shapes=[pltpu.VMEM((NB_PAD, 128), F32)],
        compiler_params=_cp(1),
        name="even_out",
    )(*x_parts, mod, bg, cv, cv, cv, *attn_parts, conv_w, w_out, g_ffn, rwt, rb)


def _odd_kernel(*refs):
    pend, refs, scratch = refs[:N_PENDING], refs[N_PENDING:-2], refs[-2:]
    (mod_ref, g_ref, win_ref, vg_ref, ws_ref, bst_ref, wo_ref, gf_ref, rwt_ref, rb_ref,
     xo_ref, h2c_ref, gidx_ref, grank_ref, wa_ref, cnt_ref, gated_scr, run_scr) = refs
    x = _apply_pending(*pend, *scratch)
    shift = mod_ref[0, :, 0:D]
    scale = mod_ref[0, :, D:2 * D]
    h = _norm_mod(x, g_ref[...], shift, scale).astype(BF)
    zl = _dot(h, win_ref[...])
    z = 0.5 * zl * (1.0 + lax.erf(zl * math.sqrt(0.5)))
    u = z[:, 0:D]
    v = _rms(z[:, D:2 * D], vg_ref[...]).astype(BF)
    n_chunks = TM // CHUNK
    gch = D // GM_GROUPS
    for g in range(GM_GROUPS):
        csl = slice(g * gch, (g + 1) * gch)
        vg = jnp.concatenate([v[n * CHUNK:(n + 1) * CHUNK, csl] for n in range(n_chunks)], axis=1)
        sg = _dot(ws_ref[g], vg) + bst_ref[:, g:g + 1]
        for n in range(n_chunks):
            rsl = slice(n * CHUNK, (n + 1) * CHUNK)
            gated_scr[rsl, csl] = (u[rsl, csl] * sg[:, n * gch:(n + 1) * gch]).astype(BF)
    out = _dot(gated_scr[...], wo_ref[...])
    gate = mod_ref[0, :, 2 * D:3 * D]
    x_new = x + gate * out
    xo_ref[...] = x_new
    _ffn_prep(x_new, mod_ref, gf_ref, rwt_ref, rb_ref, h2c_ref, gidx_ref, grank_ref, wa_ref, cnt_ref,
              run_scr)


def _odd(pending, mod, l, g_mix, win, v_g, w_s, b_st, w_out, g_ffn, rwt, rb):
    nt = T // TM
    full = lambda shape: pl.BlockSpec(shape, lambda t: (0,) * len(shape))
    return pl.pallas_call(
        _odd_kernel,
        grid=(nt,),
        in_specs=_pending_specs(pending["layer"]) + [
            pl.BlockSpec((1, 1, 6 * D), lambda t: (l * MOD_ROWS + _mod_row(t, TM), 0, 0)),
            full((1, D)), full((D, 2 * D)), full((1, D)),
            full((GM_GROUPS, CHUNK, CHUNK)), full((CHUNK, GM_GROUPS)), full((D, D)),
            full((1, D)), full((2 * N_EXPERTS, D)), full((N_EXPERTS, 1)),
        ],
        out_specs=_prep_out_specs(),
        out_shape=_prep_out_shapes(),
        scratch_shapes=[pltpu.VMEM((TM, D), BF), pltpu.VMEM((NB_PAD, 128), F32)]
        + PENDING_SCRATCH,
        compiler_params=_cp(1),
        name="odd",
    )(*_pending_args(pending), mod, g_mix, win, v_g, w_s, b_st, w_out, g_ffn, rwt, rb)


def _routing_plan(bucket, brank, cnt):
    counts = cnt[:NB, 0].astype(jnp.int32)
    padded = (counts + (TMM - 1)) // TMM * TMM
    ends = jnp.cumsum(padded)
    base = ends - padded
    b = bucket.reshape(T)
    hit = b[:, None] == jnp.arange(NB, dtype=jnp.int32)[None, :]
    slot = brank.reshape(T) + jnp.sum(jnp.where(hit, base[None, :], 0), axis=1)
    n_steps = ends[NB - 1] // TMM
    starts = jnp.minimum(jnp.arange(MAX_STEPS, dtype=jnp.int32) * TMM, ends[NB - 1] - TMM)
    bucket_of_step = jnp.sum((starts[:, None] >= ends[None, :NB - 1]).astype(jnp.int32), axis=1)
    pair_of_step = bucket_of_step % len(PAIRS)
    pairs = jnp.asarray(PAIRS, dtype=jnp.int32)
    sched = jnp.concatenate([bucket_of_step // len(PAIRS), pairs[pair_of_step, 0], pairs[pair_of_step, 1],
                             n_steps[None]]).astype(jnp.int32)
    fill_plan = jnp.concatenate([base + counts, ends, n_steps[None]]).astype(jnp.int32)
    return slot.astype(jnp.int32), sched, fill_plan


def _row_tile(ref, row, sub):
    return ref.at[pl.ds(pl.multiple_of(row * sub, sub), sub), :]


def _row_copies(n_rows, make_copy, per_row=None):
    def body(i, carry):
        for j in range(ISSUE_UNROLL):
            r = i * ISSUE_UNROLL + j
            make_copy(r).start(priority=j % 2)
            if per_row is not None:
                per_row(r)
        return carry
    lax.fori_loop(0, n_rows // ISSUE_UNROLL, body, 0)


def _dispatch_kernel(pad_ref, slot_ref, wa_ref, h_ref, o_hbm, ws_ref, zero_scr, zvec_scr, zsem, sem):
    t = pl.program_id(0)

    @pl.when(t == 0)
    def _():
        zvec_scr[...] = jnp.zeros_like(zvec_scr)
        clear = pltpu.make_async_copy(zvec_scr, ws_ref, zsem.at[0])
        clear.start()
        clear.wait()

    @pl.when(t == 0)
    def _():
        zero_scr[...] = jnp.zeros_like(zero_scr)

        def chunk_fill(row):
            return pltpu.make_async_copy(
                zero_scr.at[pl.ds(0, FILL * X_SUB), :],
                o_hbm.at[pl.ds(pl.multiple_of(row * X_SUB, X_SUB), FILL * X_SUB), :], zsem.at[0])

        def tile_fill(s):
            return pltpu.make_async_copy(
                zero_scr, o_hbm.at[pl.ds(s * TMM * X_SUB, TMM * X_SUB), :], zsem.at[1])

        for phase in ("start", "wait"):
            for k in range(NB):
                for c in range(TMM // FILL):
                    top = pad_ref[NB + k] - c * FILL

                    @pl.when(top > pad_ref[k])
                    def _():
                        getattr(chunk_fill(top - FILL), phase)()
        for phase in ("start", "wait"):
            for s in range(T // TMM, SORTED_TILES):
                @pl.when(s >= pad_ref[2 * NB])
                def _():
                    getattr(tile_fill(s), phase)()

    def put_weight(r):
        ws_ref[slot_ref[r]] = wa_ref[r]

    _row_copies(TM, lambda r: pltpu.make_async_copy(
        _row_tile(h_ref, r, X_SUB), _row_tile(o_hbm, slot_ref[r], X_SUB), sem), per_row=put_weight)
    pltpu.make_async_copy(h_ref, o_hbm.at[pl.ds(0, TM * X_SUB), :], sem).wait()


def _dispatch(pad_start, slot, wa, h2c):
    nt = T // TM
    return pl.pallas_call(
        _dispatch_kernel,
        grid_spec=pltpu.PrefetchScalarGridSpec(
            num_scalar_prefetch=1,
            grid=(nt,),
            in_specs=[
                pl.BlockSpec((TM,), lambda t, pad: (t,), memory_space=pltpu.SMEM),
                pl.BlockSpec((TM,), lambda t, pad: (t,), memory_space=pltpu.SMEM),
                pl.BlockSpec((TM * X_SUB, LANES), lambda t, pad: (t, 0)),
            ],
            out_specs=[pl.BlockSpec(memory_space=pl.ANY), pl.BlockSpec(memory_space=pltpu.SMEM)],
            scratch_shapes=[pltpu.VMEM((TMM * X_SUB, LANES), F32), pltpu.VMEM((SORTED_ROWS,), F32),
                            pltpu.SemaphoreType.DMA((2,)), pltpu.SemaphoreType.DMA(())],
        ),
        out_shape=[jax.ShapeDtypeStruct((SORTED_ROWS * X_SUB, LANES), F32),
                   jax.ShapeDtypeStruct((SORTED_ROWS,), F32)],
        compiler_params=_cp(1),
        name="dispatch",
    )(pad_start, slot, wa, h2c)


def _experts_kernel(sched_ref, xs_ref, ws_ref, w1_ref, w3_ref, w2_ref, o_ref, w1b, w3b, w2b, wt_scr):
    s = pl.program_id(0)

    @pl.when(jnp.logical_or(s == 0, sched_ref[s] != sched_ref[jnp.maximum(s - 1, 0)]))
    def _():
        w1b[...] = w1_ref[...].astype(BF)
        w3b[...] = w3_ref[...].astype(BF)
        w2b[...] = w2_ref[...].astype(BF)

    n_steps = sched_ref[3 * MAX_STEPS]

    @pl.when(s < n_steps)
    def _():
        h = jnp.concatenate([xs_ref[pl.ds(k, TMM, stride=X_SUB), :] for k in range(X_SUB)],
                            axis=1).astype(BF)
        wt_scr[...] = jnp.zeros_like(wt_scr)
        wt_scr[0:1, :] = ws_ref[...]
        w_first = wt_scr[...].T[:, 0:1]
        weights = (w_first, 1.0 - w_first)
        acc = None
        for j in range(2):
            e = sched_ref[(1 + j) * MAX_STEPS + s]
            a = _dot(h, w1b[e])
            b = _dot(h, w3b[e])
            hid = (a * (1.0 / (1.0 + jnp.exp(-a))) * b * weights[j]).astype(BF)
            y = _dot(hid, w2b[e])
            acc = y if acc is None else acc + y
        for k in range(D // LANES):
            o_ref[pl.ds(k, TMM, stride=Y_SUB), :] = acc[:, k * LANES:(k + 1) * LANES]

    @pl.when(s >= n_steps)
    def _():
        o_ref[...] = jnp.zeros_like(o_ref)


def _experts(sched, xs, ws, l, w1, w3, w2):
    row_blk = lambda s, sched: (jnp.maximum(jnp.minimum(s, sched[3 * MAX_STEPS] - 1), 0), 0)
    grp_blk = lambda s, sched: (l, sched[s], 0, 0)
    return pl.pallas_call(
        _experts_kernel,
        grid_spec=pltpu.PrefetchScalarGridSpec(
            num_scalar_prefetch=1,
            grid=(MAX_STEPS,),
            in_specs=[
                pl.BlockSpec((TMM * X_SUB, LANES), row_blk),
                pl.BlockSpec((None, 1, TMM), lambda s, sched: row_blk(s, sched) + (0,)),
                pl.BlockSpec((None, EPG, D, D_EXPERT), grp_blk),
                pl.BlockSpec((None, EPG, D, D_EXPERT), grp_blk),
                pl.BlockSpec((None, EPG, D_EXPERT, D), grp_blk),
            ],
            out_specs=pl.BlockSpec((TMM * Y_SUB, LANES), lambda s, sched: (s, 0)),
            scratch_shapes=[pltpu.VMEM((EPG, D, D_EXPERT), BF), pltpu.VMEM((EPG, D, D_EXPERT), BF),
                            pltpu.VMEM((EPG, D_EXPERT, D), BF), pltpu.VMEM((LANES, TMM), F32)],
        ),
        out_shape=jax.ShapeDtypeStruct((MAX_STEPS * TMM * Y_SUB, LANES), F32),
        compiler_params=_cp(1),
        name="experts",
    )(sched, xs, ws, w1, w3, w2)


def _pending_specs(l):
    nt = T // TM
    return [
        pl.BlockSpec((TM,), lambda t: (t,), memory_space=pltpu.SMEM),
        pl.BlockSpec((TM,), lambda t: (jnp.minimum(t + 1, nt - 1),), memory_space=pltpu.SMEM),
        pl.BlockSpec(memory_space=pl.ANY),
        pl.BlockSpec((TM, D), lambda t: (t, 0)),
        pl.BlockSpec((1, 1, 6 * D), lambda t: (l * MOD_ROWS + _mod_row(t, TM), 0, 0)),
    ]


def _pending_args(pending):
    return (pending["slot"], pending["slot"], pending["ys"], pending["x"], pending["mod"])


PENDING_SCRATCH = [pltpu.VMEM((2, TM * Y_SUB, LANES), F32), pltpu.SemaphoreType.DMA((2,))]
N_PENDING = 5


def _apply_pending(slot_ref, slot_next_ref, ys_hbm, x_ref, mod_ref, buf, sem):
    t = pl.program_id(0)
    nt = pl.num_programs(0)
    cur = t % 2

    def gather(idx_ref, b):
        _row_copies(TM, lambda r: pltpu.make_async_copy(
            _row_tile(ys_hbm, idx_ref[r], Y_SUB), _row_tile(buf.at[b], r, Y_SUB), sem.at[b]))

    @pl.when(t == 0)
    def _():
        gather(slot_ref, 0)

    @pl.when(t + 1 < nt)
    def _():
        gather(slot_next_ref, 1 - cur)

    pltpu.make_async_copy(ys_hbm.at[pl.ds(0, TM * Y_SUB), :], buf.at[cur], sem.at[cur]).wait()
    gate = mod_ref[0, :, 5 * D:6 * D]
    y = jnp.concatenate([buf[cur, pl.ds(k, TM, stride=Y_SUB), :] for k in range(Y_SUB)], axis=1)
    return x_ref[...] + gate * y


def _final_kernel(slot_ref, slot_next_ref, ys_hbm, x_ref, mod_ref, gfin_ref, op_ref, os_ref, buf, sem):
    t = pl.program_id(0)
    x_new = _apply_pending(slot_ref, slot_next_ref, ys_hbm, x_ref, mod_ref, buf, sem)
    y_out = _rms(x_new, gfin_ref[...])
    npt = NP_TOK // TM

    @pl.when(t < npt)
    def _():
        op_ref[...] = y_out

    @pl.when(t >= npt)
    def _():
        os_ref[...] = y_out


def _final(pending, g_final):
    nt = T // TM
    npt = NP_TOK // TM
    return pl.pallas_call(
        _final_kernel,
        grid=(nt,),
        in_specs=_pending_specs(pending["layer"]) + [pl.BlockSpec((1, D), lambda t: (0, 0))],
        out_specs=[pl.BlockSpec((TM, D), lambda t: (jnp.minimum(t, npt - 1), 0)),
                   pl.BlockSpec((TM, D), lambda t: (jnp.maximum(t - npt, 0), 0))],
        out_shape=[jax.ShapeDtypeStruct((NP_TOK, D), F32), jax.ShapeDtypeStruct((NS_TOK, D), F32)],
        scratch_shapes=PENDING_SCRATCH,
        compiler_params=_cp(1),
        name="final",
    )(*_pending_args(pending), g_final)


def _moe(x, mod, l, rows, bucket, brank, wa, cnt, w1, w3, w2):
    slot, sched, fill_plan = _routing_plan(bucket, brank, cnt)
    xs, ws = _dispatch(fill_plan, slot, wa.reshape(T), rows)
    ys = _experts(sched, xs, ws.reshape(SORTED_TILES, 1, TMM), l, w1, w3, w2)
    return {"slot": slot, "ys": ys, "x": x, "mod": mod, "layer": l}


def _rope_tables():
    pos = jnp.arange(DEC_SEQ)
    r = (pos // GRID_W).astype(F32)
    col = (pos % GRID_W).astype(F32)
    inv = ROPE_BASE ** (-jnp.arange(AX_FREQS, dtype=F32) / AX_FREQS)
    ang = jnp.stack([r[:, None] * inv, col[:, None] * inv], axis=1)
    cos = jnp.cos(ang)[:, :, None, :]
    sin = jnp.sin(ang)[:, :, None, :]
    c32 = jnp.broadcast_to(cos, (DEC_SEQ, 2, 2, AX_FREQS)).reshape(DEC_SEQ, ROPE)
    s32 = jnp.concatenate([-sin, sin], axis=2).reshape(DEC_SEQ, ROPE)
    pad = HEAD_PAD - NOPE - ROPE
    c = jnp.concatenate([jnp.ones((DEC_SEQ, NOPE), F32), c32, jnp.zeros((DEC_SEQ, pad), F32)], axis=1)
    s = jnp.concatenate([jnp.zeros((DEC_SEQ, NOPE), F32), s32, jnp.zeros((DEC_SEQ, pad), F32)], axis=1)
    c_id = jnp.concatenate([jnp.ones((TM, NOPE + ROPE), F32), jnp.zeros((TM, pad), F32)], axis=1)
    s_id = jnp.zeros((TM, HEAD_PAD), F32)
    return jnp.concatenate([c, c_id], axis=0), jnp.concatenate([s, s_id], axis=0)


def _swap_halves(w):
    lead = w.shape[:-1]
    return w.reshape(lead + (2, 2, AX_FREQS))[..., ::-1, :].reshape(lead + (ROPE,))


def _even_weights(w_in, w_q_up, w_kv_up):
    k_in = w_in.shape[0]
    base = 3 * D_CONV + Q_RANK + KV_RANK
    w_kr = w_in[:, base:base + ROPE]
    z = lambda n: jnp.zeros((k_in, n), F32)
    win_ext = jnp.concatenate([
        w_in[:, :base],
        w_kr, z(HEAD_PAD - ROPE),
        z(NOPE), w_kr, z(HEAD_PAD - NOPE - ROPE),
        z(NOPE), _swap_halves(w_kr), z(HEAD_PAD - NOPE - ROPE),
    ], axis=1).astype(BF)
    wq = w_q_up.reshape(Q_RANK, HEADS, QK_DIM)
    zq = lambda n: jnp.zeros((Q_RANK, HEADS, n), F32)
    wq1 = jnp.concatenate([wq, zq(HEAD_PAD - QK_DIM)], axis=2).reshape(Q_RANK, HEADS * HEAD_PAD).astype(BF)
    wq2 = jnp.concatenate([zq(NOPE), _swap_halves(wq[:, :, NOPE:]), zq(HEAD_PAD - QK_DIM)],
                          axis=2).reshape(Q_RANK, HEADS * HEAD_PAD).astype(BF)
    wkv = w_kv_up.reshape(KV_RANK, HEADS, NOPE + V_DIM)
    wk = jnp.concatenate([wkv[:, :, :NOPE], jnp.zeros((KV_RANK, HEADS, HEAD_PAD - NOPE), F32)],
                         axis=2).reshape(KV_RANK, HEADS * HEAD_PAD).astype(BF)
    wv = wkv[:, :, NOPE:].reshape(KV_RANK, HEADS * V_DIM).astype(BF)
    return win_ext, wq1, wq2, wk, wv


def kernel(x_prompt, x_sample, cache_ckv, cache_krope, c, c_ctx, w_ada, b_ada, g_mix, g_ffn, g_final,
           ev_w_in, conv_w, q_norm_g, w_q_up, kv_norm_g, w_kv_up, ev_w_out,
           gm_w_in, gm_v_g, gm_w_s, gm_b_s, gm_w_out, router_w, router_b, moe_w1, moe_w3, moe_w2):
    n_even = ev_w_in.shape[0]
    x_parts = (x_prompt.reshape(NP_TOK, D), x_sample.reshape(NS_TOK, D))
    cc =jnp.concatenate([c, c_ctx[None, :], jnp.zeros((MOD_ROWS - DEC_BATCH - 1, D), F32)], axis=0)
    mod = _ada(cc, w_ada, b_ada).reshape(DEPTH * MOD_ROWS, 1, 6 * D)

    rope_c, rope_s = _rope_tables()
    rw_hi = router_w.T.astype(BF)
    rw_lo = (router_w.T - rw_hi.astype(F32)).astype(BF)
    rwt = jnp.concatenate([rw_hi, rw_lo], axis=0)
    rb = router_b.reshape(N_EXPERTS, 1)
    ev = [_even_weights(ev_w_in[i], w_q_up[i], w_kv_up[i]) for i in range(n_even)]
    kr_ctx = jnp.pad(cache_krope, ((0, 0), (0, 0), (0, 0), (NOPE, HEAD_PAD - NOPE - ROPE)))
    kr_ctx = kr_ctx.transpose(1, 0, 2, 3).reshape(n_even, DEC_BATCH * PAST, HEAD_PAD)
    ckv_ctx = cache_ckv.transpose(1, 0, 2, 3).reshape(n_even, DEC_BATCH * PAST, KV_RANK)
    kc, vc = _ctx_kv(ckv_ctx, kr_ctx, jnp.stack([e[3] for e in ev]), jnp.stack([e[4] for e in ev]))

    ckv_states, kr_states = [], []
    pending = None
    for l in range(DEPTH):
        i = l // 2
        if l % 2 == 0:
            win_ext, wq1, wq2, wk, wv = ev[i]
            outs = _even_in(
                None if pending else x_parts, pending, mod, l, g_mix[l][None, :], win_ext,
                q_norm_g[i][None, :], wq1, wq2, kv_norm_g[i][None, :], wk, wv, rope_c, rope_s)
            bg, cv, q, k, v, ckv, kr = outs[:7]
            if pending:
                x_parts = (outs[7],)
            attn_parts = (_attn_prompt(q, k, v), _attn_sample(q, k, v, kc, vc, i))
            x, *routed = _even_out(x_parts, mod, l, bg, cv, attn_parts, conv_w[i],
                                                 ev_w_out[i].astype(BF), g_ffn[l][None, :], rwt, rb)
            ckv_states.append(ckv[:NP_TOK].reshape(BATCH, SEQ, KV_RANK))
            kr_states.append(kr[:NP_TOK, :ROPE].reshape(BATCH, SEQ, ROPE))
        else:
            x, *routed = _odd(pending, mod, l, g_mix[l][None, :], gm_w_in[i].astype(BF),
                                            gm_v_g[i][None, :], gm_w_s[i].astype(BF), gm_b_s[i].T,
                                            gm_w_out[i].astype(BF), g_ffn[l][None, :], rwt, rb)
        pending = _moe(x, mod, l, *routed, moe_w1, moe_w3, moe_w2)

    y_prompt, y_sample = _final(pending, g_final[None, :])
    y_prompt = y_prompt.reshape(BATCH, SEQ, D)
    y_sample = y_sample.reshape(DEC_BATCH, DEC_SEQ, D)
    return (y_prompt, y_sample, jnp.stack(ckv_states, axis=1), jnp.stack(kr_states, axis=1))
```

```python
import functools
import math

import jax
import jax.numpy as jnp
from jax import lax
from jax.experimental import pallas as pl
from jax.experimental.pallas import tpu as pltpu

D = 1024
BATCH, SEQ = 32, 256
DEC_BATCH, DEC_SEQ = 8, 2048
PAST = 256
DEPTH = 4
GRID_W = 64
D_CONV = 512
HEADS = 8
NOPE, ROPE, V_DIM = 64, 32, 64
QK_DIM = NOPE + ROPE
Q_RANK, KV_RANK = 384, 256
AX_FREQS = ROPE // 4
ROPE_BASE = 10000.0
CHUNK = 128
GM_GROUPS = 8
N_EXPERTS, N_GROUPS, EPG = 16, 4, 4
D_EXPERT = 256
EPS = 1e-6

NP_TOK = BATCH * SEQ
NS_TOK = DEC_BATCH * DEC_SEQ
T = NP_TOK + NS_TOK
MOD_ROWS = 16
HEAD_PAD = 128
HALO = 16

TM = 512
PAIRS = ((0, 1), (0, 2), (0, 3), (1, 2), (1, 3), (2, 3))
NB = N_GROUPS * len(PAIRS)
NB_PAD = 32
TMM = 512
FILL = 128
MAX_STEPS = T // TMM + NB
SORTED_TILES = MAX_STEPS + MAX_STEPS % 2
SORTED_ROWS = SORTED_TILES * TMM
LANES = 128
Y_SUB = D // LANES
X_SUB = Y_SUB
ISSUE_UNROLL = 8
TQ = 256
ATTN_PAIRS = 4
VMEM_LIMIT = 56 * 1024 * 1024

BF = jnp.bfloat16
F32 = jnp.float32


def _cp(n_axes):
    return pltpu.CompilerParams(dimension_semantics=("arbitrary",) * n_axes,
                                vmem_limit_bytes=VMEM_LIMIT)


def _mod_row(t, tm):
    n_prompt_tiles = NP_TOK // tm
    per_seq = DEC_SEQ // tm
    return jnp.where(t < n_prompt_tiles, DEC_BATCH, (t - n_prompt_tiles) // per_seq)


def _rms(x, g):
    return x * lax.rsqrt(jnp.mean(x * x, axis=-1, keepdims=True) + EPS) * g


def _norm_mod(x, g, shift, scale):
    return x * lax.rsqrt(jnp.mean(x * x, axis=-1, keepdims=True) + EPS) * (g * (1.0 + scale)) + shift


def _dot(a, b):
    return jnp.dot(a, b, preferred_element_type=F32)


def _dot_nt(a, b, precision=None):
    return lax.dot_general(a, b, (((1,), (1,)), ((), ())), precision=precision,
                           preferred_element_type=F32)


ADA_TN = 1536


def _ada_kernel(cc_ref, w_ref, b_ref, o_ref):
    cc = cc_ref[...]
    s = (cc / (1.0 + jnp.exp(-cc))).astype(BF)
    o_ref[0] = _dot(s, w_ref[0].astype(BF)) + b_ref[0]


def _ada(cc, w_ada, b_ada):
    n = 6 * D
    return pl.pallas_call(
        _ada_kernel,
        grid=(DEPTH, n // ADA_TN),
        in_specs=[
            pl.BlockSpec((MOD_ROWS, D), lambda l, j: (0, 0)),
            pl.BlockSpec((1, D, ADA_TN), lambda l, j: (l, 0, j)),
            pl.BlockSpec((1, 1, ADA_TN), lambda l, j: (l, 0, j)),
        ],
        out_specs=pl.BlockSpec((1, MOD_ROWS, ADA_TN), lambda l, j: (l, 0, j)),
        out_shape=jax.ShapeDtypeStruct((DEPTH, MOD_ROWS, n), F32),
        compiler_params=_cp(2),
        name="ada",
    )(cc, w_ada, b_ada.reshape(DEPTH, 1, n))


def _route(logits_t, rb):
    sc = 1.0 / (1.0 + jnp.exp(-logits_t))
    sel = sc + rb
    rows = [sel[e:e + 1, :] for e in range(N_EXPERTS)]
    srows = [sc[e:e + 1, :] for e in range(N_EXPERTS)]

    def top2sum(a, b, c, d):
        hi1, lo1 = jnp.maximum(a, b), jnp.minimum(a, b)
        hi2, lo2 = jnp.maximum(c, d), jnp.minimum(c, d)
        return jnp.maximum(hi1, hi2) + jnp.maximum(jnp.minimum(hi1, hi2), jnp.maximum(lo1, lo2))

    gs = [top2sum(*rows[EPG * g:EPG * (g + 1)]) for g in range(N_GROUPS)]
    best = gs[0]
    gidx = jnp.zeros_like(best, dtype=jnp.int32)
    for g in range(1, N_GROUPS):
        upd = gs[g] > best
        best = jnp.where(upd, gs[g], best)
        gidx = jnp.where(upd, g, gidx)

    picked = []
    for g in range(N_GROUPS):
        grp = rows[EPG * g:EPG * (g + 1)]
        in_g = gidx == g
        for j in range(EPG):
            rank = jnp.zeros_like(gidx)
            for k in range(EPG):
                if k == j:
                    continue
                ahead = grp[k] > grp[j]
                if k < j:
                    ahead = ahead | (grp[k] == grp[j])
                rank = rank + ahead.astype(jnp.int32)
            picked.append(in_g & (rank < 2))
    w = [jnp.where(picked[e], srows[e], 0.0) for e in range(N_EXPERTS)]
    wsum = w[0]
    for e in range(1, N_EXPERTS):
        wsum = wsum + w[e]
    inv = 1.0 / wsum
    pj, wj = [], []
    for j in range(EPG):
        hit, acc = picked[j], w[j]
        for g in range(1, N_GROUPS):
            hit = hit | picked[EPG * g + j]
            acc = acc + w[EPG * g + j]
        pj.append(hit)
        wj.append(acc * inv)
    pair = jnp.full_like(gidx, len(PAIRS) - 1)
    for idx in range(len(PAIRS) - 2, -1, -1):
        a, b = PAIRS[idx]
        pair = jnp.where(pj[a] & pj[b], idx, pair)
    w_a = jnp.where(pj[0], wj[0], jnp.where(pj[1], wj[1], wj[2]))
    w_b = jnp.where(pj[3], wj[3], jnp.where(pj[2], wj[2], wj[1]))
    return (w_a, w_b), gidx * len(PAIRS) + pair


def _ffn_prep(x_new, mod_ref, gf_ref, rwt_ref, rb_ref, h2c_ref, gidx_ref, grank_ref, wa_ref, cnt_ref,
              run_scr):
    t = pl.program_id(0)
    tm = x_new.shape[0]
    shift = mod_ref[0, :, 3 * D:4 * D]
    scale = mod_ref[0, :, 4 * D:5 * D]
    h2 = _norm_mod(x_new, gf_ref[...], shift, scale)
    h_hi = h2.astype(BF)
    h_lo = (h2 - h_hi.astype(F32)).astype(BF)
    by_hi = _dot_nt(rwt_ref[...], h_hi)
    by_lo = _dot_nt(rwt_ref[0:N_EXPERTS, :], h_lo)
    logits_t = by_hi[0:N_EXPERTS, :] + by_hi[N_EXPERTS:2 * N_EXPERTS, :] + by_lo
    wab, gidx = _route(logits_t, rb_ref[...])
    wa_ref[0] = wab[0]

    for k in range(X_SUB):
        h2c_ref[pl.ds(k, tm, stride=X_SUB), :] = h2[:, k * LANES:(k + 1) * LANES]

    @pl.when(t == 0)
    def _():
        run_scr[...] = jnp.zeros_like(run_scr)

    onehot = (lax.broadcasted_iota(jnp.int32, (NB_PAD, tm), 0) == gidx).astype(F32)
    earlier = (lax.broadcasted_iota(jnp.int32, (tm, tm), 0)
               < lax.broadcasted_iota(jnp.int32, (tm, tm), 1)).astype(BF)
    rank = _dot(onehot.astype(BF), earlier)
    run = run_scr[:, 0:1]
    grank = jnp.sum(onehot * (rank + run), axis=0, keepdims=True)
    gidx_ref[0] = gidx
    grank_ref[0] = grank.astype(jnp.int32)
    run_scr[...] = run_scr[...] + jnp.sum(onehot, axis=1, keepdims=True)
    cnt_ref[...] = run_scr[...]


def _prep_out_specs():
    tok = lambda w: pl.BlockSpec((TM, w), lambda t: (t, 0))
    lanes = pl.BlockSpec((1, 1, TM), lambda t: (t, 0, 0))
    rows = pl.BlockSpec((TM * X_SUB, LANES), lambda t: (t, 0))
    return [tok(D), rows, lanes, lanes, lanes, pl.BlockSpec((NB_PAD, 128), lambda t: (0, 0))]


def _prep_out_shapes():
    nt = T // TM
    return [jax.ShapeDtypeStruct((T, D), F32), jax.ShapeDtypeStruct((T * X_SUB, LANES), F32),
            jax.ShapeDtypeStruct((nt, 1, TM), jnp.int32), jax.ShapeDtypeStruct((nt, 1, TM), jnp.int32),
            jax.ShapeDtypeStruct((nt, 1, TM), F32), jax.ShapeDtypeStruct((NB_PAD, 128), F32)]


EV_EXT = 3 * D_CONV + Q_RANK + KV_RANK + 3 * HEAD_PAD


def _tok_specs(parts, width):
    npt = NP_TOK // TM
    if len(parts) == 1:
        return [pl.BlockSpec((TM, width), lambda t: (t, 0))]
    return [pl.BlockSpec((TM, width), lambda t: (jnp.minimum(t, npt - 1), 0)),
            pl.BlockSpec((TM, width), lambda t: (jnp.maximum(t - npt, 0), 0))]


def _tok_load(refs):
    if len(refs) == 1:
        return refs[0][...]
    return jnp.where(pl.program_id(0) < NP_TOK // TM, refs[0][...], refs[1][...])


def _even_in_kernel(n_x, *refs):
    if n_x:
        x_refs, refs = refs[:n_x], refs[n_x:]
        x = _tok_load(x_refs)
    else:
        pend, refs, scratch = refs[:N_PENDING], refs[N_PENDING:-2], refs[-2:]
        refs, xo_ref = refs[:-1], refs[-1]
        x = _apply_pending(*pend, *scratch)
        xo_ref[...] = x
    (mod_ref, g_ref, win_ref, qg_ref, wq1_ref, wq2_ref, kg_ref, wk_ref, wv_ref, rc_ref, rs_ref,
     bg_ref, cv_ref, q_ref, k_ref, v_ref, ckv_ref, kr_ref) = refs
    shift = mod_ref[0, :, 0:D]
    scale = mod_ref[0, :, D:2 * D]
    h = _norm_mod(x, g_ref[...], shift, scale).astype(BF)
    proj = _dot(h, win_ref[...])
    o = 0
    b_g = proj[:, o:o + D_CONV]; o += D_CONV
    c_g = proj[:, o:o + D_CONV]; o += D_CONV
    v_in = proj[:, o:o + D_CONV]; o += D_CONV
    q_a = proj[:, o:o + Q_RANK]; o += Q_RANK
    kv_a = proj[:, o:o + KV_RANK]; o += KV_RANK
    kr_raw = proj[:, o:o + HEAD_PAD]; o += HEAD_PAD
    kr_cat = proj[:, o:o + HEAD_PAD]; o += HEAD_PAD
    kr_sw = proj[:, o:o + HEAD_PAD]

    bg_ref[...] = b_g.astype(BF)
    cv_ref[...] = (c_g * v_in).astype(BF)
    kr_ref[...] = kr_raw

    rc = rc_ref[...]
    rs = rs_ref[...]
    qn = _rms(q_a, qg_ref[...]).astype(BF)
    q1 = _dot(qn, wq1_ref[...])
    q2 = _dot(qn, wq2_ref[...])
    qscale = QK_DIM ** -0.5
    for hd in range(HEADS):
        sl = slice(hd * HEAD_PAD, (hd + 1) * HEAD_PAD)
        q_ref[:, sl] = ((q1[:, sl] * rc + q2[:, sl] * rs) * qscale).astype(BF)

    ckv = _rms(kv_a, kg_ref[...])
    ckv_ref[...] = ckv
    ckv_b = ckv.astype(BF)
    kk = _dot(ckv_b, wk_ref[...])
    kr = kr_cat * rc + kr_sw * rs
    for hd in range(HEADS):
        sl = slice(hd * HEAD_PAD, (hd + 1) * HEAD_PAD)
        k_ref[:, sl] = (kk[:, sl] + kr).astype(BF)
    v_ref[...] = _dot(ckv_b, wv_ref[...]).astype(BF)


def _even_in(x_parts, pending, mod, l, g_mix, win, qg, wq1, wq2, kg, wk, wv, rope_c, rope_s):
    nt = T // TM
    npt = NP_TOK // TM
    per_seq = DEC_SEQ // TM
    ident_blk = DEC_SEQ // TM

    def rope_idx(t):
        return (jnp.where(t < npt, ident_blk, (t - npt) % per_seq), 0)

    full = lambda shape: pl.BlockSpec(shape, lambda t: (0,) * len(shape))
    tok = lambda w: pl.BlockSpec((TM, w), lambda t: (t, 0))
    out_specs = [tok(D_CONV), tok(D_CONV), tok(HEADS * HEAD_PAD), tok(HEADS * HEAD_PAD),
                 tok(HEADS * V_DIM), tok(KV_RANK), tok(HEAD_PAD)]
    out_shape = [
        jax.ShapeDtypeStruct((T, D_CONV), BF), jax.ShapeDtypeStruct((T, D_CONV), BF),
        jax.ShapeDtypeStruct((T, HEADS * HEAD_PAD), BF), jax.ShapeDtypeStruct((T, HEADS * HEAD_PAD), BF),
        jax.ShapeDtypeStruct((T, HEADS * V_DIM), BF),
        jax.ShapeDtypeStruct((T, KV_RANK), F32), jax.ShapeDtypeStruct((T, HEAD_PAD), F32),
    ]
    if pending is None:
        lead_specs, lead_args, scratch = _tok_specs(x_parts, D), tuple(x_parts), []
    else:
        lead_specs, lead_args, scratch = _pending_specs(pending["layer"]), _pending_args(pending), PENDING_SCRATCH
        out_specs.append(tok(D))
        out_shape.append(jax.ShapeDtypeStruct((T, D), F32))
    return pl.pallas_call(
        functools.partial(_even_in_kernel, 0 if pending is not None else len(x_parts)),
        grid=(nt,),
        in_specs=lead_specs + [
            pl.BlockSpec((1, 1, 6 * D), lambda t: (l * MOD_ROWS + _mod_row(t, TM), 0, 0)),
            full((1, D)), full((D, EV_EXT)), full((1, Q_RANK)),
            full((Q_RANK, HEADS * HEAD_PAD)), full((Q_RANK, HEADS * HEAD_PAD)),
            full((1, KV_RANK)), full((KV_RANK, HEADS * HEAD_PAD)), full((KV_RANK, HEADS * V_DIM)),
            pl.BlockSpec((TM, HEAD_PAD), rope_idx), pl.BlockSpec((TM, HEAD_PAD), rope_idx),
        ],
        out_specs=out_specs,
        out_shape=out_shape,
        scratch_shapes=scratch,
        compiler_params=_cp(1),
        name="even_in",
    )(*lead_args, mod, g_mix, win, qg, wq1, wq2, kg, wk, wv, rope_c, rope_s)


CTX_TM = 512


def _ctx_kv_kernel(ckv_ref, kr_ref, wk_ref, wv_ref, k_ref, v_ref):
    ckv_b = ckv_ref[0].astype(BF)
    kk = _dot(ckv_b, wk_ref[0])
    kr = kr_ref[0]
    for hd in range(HEADS):
        sl = slice(hd * HEAD_PAD, (hd + 1) * HEAD_PAD)
        k_ref[0, :, sl] = (kk[:, sl] + kr).astype(BF)
    v_ref[0] = _dot(ckv_b, wv_ref[0]).astype(BF)


def _ctx_kv(ckv_all, kr_all, wk_all, wv_all):
    n_even = ckv_all.shape[0]
    rows = DEC_BATCH * PAST
    return pl.pallas_call(
        _ctx_kv_kernel,
        grid=(n_even, rows // CTX_TM),
        in_specs=[
            pl.BlockSpec((1, CTX_TM, KV_RANK), lambda i, t: (i, t, 0)),
            pl.BlockSpec((1, CTX_TM, HEAD_PAD), lambda i, t: (i, t, 0)),
            pl.BlockSpec((1, KV_RANK, HEADS * HEAD_PAD), lambda i, t: (i, 0, 0)),
            pl.BlockSpec((1, KV_RANK, HEADS * V_DIM), lambda i, t: (i, 0, 0)),
        ],
        out_specs=[
            pl.BlockSpec((1, CTX_TM, HEADS * HEAD_PAD), lambda i, t: (i, t, 0)),
            pl.BlockSpec((1, CTX_TM, HEADS * V_DIM), lambda i, t: (i, t, 0)),
        ],
        out_shape=[jax.ShapeDtypeStruct((n_even, rows, HEADS * HEAD_PAD), BF),
                   jax.ShapeDtypeStruct((n_even, rows, HEADS * V_DIM), BF)],
        compiler_params=_cp(2),
        name="ctx_kv",
    )(ckv_all, kr_all, wk_all, wv_all)


def _attn_body(n_pairs, has_ctx, q_ref, k_ref, v_ref, *rest):
    if has_ctx:
        kc_ref, vc_ref, o_ref = rest
    else:
        (o_ref,) = rest
    lane = lax.broadcasted_iota(jnp.int32, (1, 2 * V_DIM), 1)
    w = 2 * V_DIM
    ones = jnp.ones((k_ref.shape[0], w), BF)
    ones_ctx = jnp.ones((PAST, w), BF)
    for pr in range(n_pairs):
        vsl = slice(pr * w, (pr + 1) * w)
        v = jnp.concatenate([v_ref[:, vsl], ones], axis=1)
        if has_ctx:
            vc = jnp.concatenate([vc_ref[0, :, vsl], ones_ctx], axis=1)
        outs = []
        for sub in range(2):
            hsl = slice((2 * pr + sub) * HEAD_PAD, (2 * pr + sub + 1) * HEAD_PAD)
            q = q_ref[:, hsl]
            s1 = _dot_nt(q, k_ref[:, hsl])
            m = jnp.max(s1, axis=-1, keepdims=True)
            if has_ctx:
                s2 = _dot_nt(q, kc_ref[0, :, hsl])
                m = jnp.maximum(m, jnp.max(s2, axis=-1, keepdims=True))
            acc = _dot(jnp.exp(s1 - m).astype(BF), v)
            if has_ctx:
                acc = acc + _dot(jnp.exp(s2 - m).astype(BF), vc)
            outs.append(acc[:, 0:w] * (1.0 / acc[:, w:2 * w]))
        o_ref[:, vsl] = jnp.where(lane < V_DIM, outs[0], outs[1]).astype(BF)


def _attn_prompt(q, k, v):
    return pl.pallas_call(
        functools.partial(_attn_body, HEADS // 2, False),
        grid=(BATCH,),
        in_specs=[
            pl.BlockSpec((SEQ, HEADS * HEAD_PAD), lambda b: (b, 0)),
            pl.BlockSpec((SEQ, HEADS * HEAD_PAD), lambda b: (b, 0)),
            pl.BlockSpec((SEQ, HEADS * V_DIM), lambda b: (b, 0)),
        ],
        out_specs=pl.BlockSpec((SEQ, HEADS * V_DIM), lambda b: (b, 0)),
        out_shape=jax.ShapeDtypeStruct((NP_TOK, HEADS * V_DIM), BF),
        compiler_params=_cp(1),
        name="attn_prompt",
    )(q, k, v)


def _attn_sample(q, k, v, kc, vc, i):
    s_blk0 = NP_TOK // DEC_SEQ
    q_blk0 = NP_TOK // TQ
    nq = DEC_SEQ // TQ
    hw = ATTN_PAIRS * 2 * HEAD_PAD
    vw = ATTN_PAIRS * 2 * V_DIM
    return pl.pallas_call(
        functools.partial(_attn_body, ATTN_PAIRS, True),
        grid=(DEC_BATCH, HEADS // (2 * ATTN_PAIRS), nq),
        in_specs=[
            pl.BlockSpec((TQ, hw), lambda b, hp, j: (q_blk0 + b * nq + j, hp)),
            pl.BlockSpec((DEC_SEQ, hw), lambda b, hp, j: (s_blk0 + b, hp)),
            pl.BlockSpec((DEC_SEQ, vw), lambda b, hp, j: (s_blk0 + b, hp)),
            pl.BlockSpec((1, PAST, hw), lambda b, hp, j: (i, b, hp)),
            pl.BlockSpec((1, PAST, vw), lambda b, hp, j: (i, b, hp)),
        ],
        out_specs=pl.BlockSpec((TQ, vw), lambda b, hp, j: (b * nq + j, hp)),
        out_shape=jax.ShapeDtypeStruct((NS_TOK, HEADS * V_DIM), BF),
        compiler_params=_cp(3),
        name="attn_sample",
    )(q, k, v, kc, vc)


def _even_out_kernel(n_x, *refs):
    x_refs, refs = refs[:n_x], refs[n_x:]
    (mod_ref, bg_ref, cv_ref, cvp_ref, cvn_ref, atp_ref, ats_ref, cw_ref, wo_ref, gf_ref, rwt_ref, rb_ref,
     xo_ref, h2c_ref, gidx_ref, grank_ref, wa_ref, cnt_ref, run_scr) = refs
    t = pl.program_id(0)
    npt = NP_TOK // TM
    per_seq = DEC_SEQ // TM
    cv = cv_ref[...].astype(F32)
    r = lax.broadcasted_iota(jnp.int32, (TM, 1), 0)
    is_prompt = t < npt
    tile_in_seq = (t - npt) % per_seq
    first_row = jnp.where(is_prompt, 0, jnp.where(tile_in_seq == 0, 0, -1))
    last_row = jnp.where(is_prompt, SEQ - 1, jnp.where(tile_in_seq == per_seq - 1, TM - 1, -1))
    period_mask = jnp.where(is_prompt, SEQ - 1, TM - 1)
    first = (r & period_mask) == first_row
    last = (r & period_mask) == last_row
    prev_row = cvp_ref[HALO - 1:HALO, :].astype(F32)
    next_row = cvn_ref[0:1, :].astype(F32)
    prev = jnp.where(r == 0, prev_row, pltpu.roll(cv, 1, 0))
    prev = jnp.where(first, 0.0, prev)
    nxt = jnp.where(r == TM - 1, next_row, pltpu.roll(cv, TM - 1, 0))
    nxt = jnp.where(last, 0.0, nxt)
    cw = cw_ref[...]
    conv = prev * cw[0:1, :] + cv * cw[1:2, :] + nxt * cw[2:3, :]
    yc = (bg_ref[...].astype(F32) * conv).astype(BF)
    attn = _tok_load((atp_ref, ats_ref))
    out = _dot(yc, wo_ref[0:D_CONV, :]) + _dot(attn, wo_ref[D_CONV:2 * D_CONV, :])
    gate = mod_ref[0, :, 2 * D:3 * D]
    x_new = _tok_load(x_refs) + gate * out
    xo_ref[...] = x_new
    _ffn_prep(x_new, mod_ref, gf_ref, rwt_ref, rb_ref, h2c_ref, gidx_ref, grank_ref, wa_ref, cnt_ref,
              run_scr)


def _even_out(x_parts, mod, l, bg, cv, attn_parts, conv_w, w_out, g_ffn, rwt, rb):
    nt = T // TM
    hb = TM // HALO
    nhb = T // HALO
    full = lambda shape: pl.BlockSpec(shape, lambda t: (0,) * len(shape))
    tok = lambda w: pl.BlockSpec((TM, w), lambda t: (t, 0))
    return pl.pallas_call(
        functools.partial(_even_out_kernel, len(x_parts)),
        grid=(nt,),
        in_specs=_tok_specs(x_parts, D) + [
            pl.BlockSpec((1, 1, 6 * D), lambda t: (l * MOD_ROWS + _mod_row(t, TM), 0, 0)),
            tok(D_CONV), tok(D_CONV),
            pl.BlockSpec((HALO, D_CONV), lambda t: (jnp.maximum(t * hb - 1, 0), 0)),
            pl.BlockSpec((HALO, D_CONV), lambda t: (jnp.minimum((t + 1) * hb, nhb - 1), 0)),
        ] + _tok_specs(attn_parts, HEADS * V_DIM) + [
            full((3, D_CONV)), full((2 * D_CONV, D)), full((1, D)),
            full((2 * N_EXPERTS, D)), full((N_EXPERTS, 1)),
        ],
        out_specs=_prep_out_specs(),
        out_shape=_prep_out_shapes(),
        scratch_shapes=[pltpu.VMEM((NB_PAD, 128), F32)],
        compiler_params=_cp(1),
        name="even_out",
    )(*x_parts, mod, bg, cv, cv, cv, *attn_parts, conv_w, w_out, g_ffn, rwt, rb)


def _odd_kernel(*refs):
    pend, refs, scratch = refs[:N_PENDING], refs[N_PENDING:-2], refs[-2:]
    (mod_ref, g_ref, win_ref, vg_ref, ws_ref, bst_ref, wo_ref, gf_ref, rwt_ref, rb_ref,
     xo_ref, h2c_ref, gidx_ref, grank_ref, wa_ref, cnt_ref, gated_scr, run_scr) = refs
    x = _apply_pending(*pend, *scratch)
    shift = mod_ref[0, :, 0:D]
    scale = mod_ref[0, :, D:2 * D]
    h = _norm_mod(x, g_ref[...], shift, scale).astype(BF)
    zl = _dot(h, win_ref[...])
    z = 0.5 * zl * (1.0 + lax.erf(zl * math.sqrt(0.5)))
    u = z[:, 0:D]
    v = _rms(z[:, D:2 * D], vg_ref[...]).astype(BF)
    n_chunks = TM // CHUNK
    gch = D // GM_GROUPS
    for g in range(GM_GROUPS):
        csl = slice(g * gch, (g + 1) * gch)
        vg = jnp.concatenate([v[n * CHUNK:(n + 1) * CHUNK, csl] for n in range(n_chunks)], axis=1)
        sg = _dot(ws_ref[g], vg) + bst_ref[:, g:g + 1]
        for n in range(n_chunks):
            rsl = slice(n * CHUNK, (n + 1) * CHUNK)
            gated_scr[rsl, csl] = (u[rsl, csl] * sg[:, n * gch:(n + 1) * gch]).astype(BF)
    out = _dot(gated_scr[...], wo_ref[...])
    gate = mod_ref[0, :, 2 * D:3 * D]
    x_new = x + gate * out
    xo_ref[...] = x_new
    _ffn_prep(x_new, mod_ref, gf_ref, rwt_ref, rb_ref, h2c_ref, gidx_ref, grank_ref, wa_ref, cnt_ref,
              run_scr)


def _odd(pending, mod, l, g_mix, win, v_g, w_s, b_st, w_out, g_ffn, rwt, rb):
    nt = T // TM
    full = lambda shape: pl.BlockSpec(shape, lambda t: (0,) * len(shape))
    return pl.pallas_call(
        _odd_kernel,
        grid=(nt,),
        in_specs=_pending_specs(pending["layer"]) + [
            pl.BlockSpec((1, 1, 6 * D), lambda t: (l * MOD_ROWS + _mod_row(t, TM), 0, 0)),
            full((1, D)), full((D, 2 * D)), full((1, D)),
            full((GM_GROUPS, CHUNK, CHUNK)), full((CHUNK, GM_GROUPS)), full((D, D)),
            full((1, D)), full((2 * N_EXPERTS, D)), full((N_EXPERTS, 1)),
        ],
        out_specs=_prep_out_specs(),
        out_shape=_prep_out_shapes(),
        scratch_shapes=[pltpu.VMEM((TM, D), BF), pltpu.VMEM((NB_PAD, 128), F32)]
        + PENDING_SCRATCH,
        compiler_params=_cp(1),
        name="odd",
    )(*_pending_args(pending), mod, g_mix, win, v_g, w_s, b_st, w_out, g_ffn, rwt, rb)


def _routing_plan(bucket, brank, cnt):
    counts = cnt[:NB, 0].astype(jnp.int32)
    padded = (counts + (TMM - 1)) // TMM * TMM
    ends = jnp.cumsum(padded)
    base = ends - padded
    b = bucket.reshape(T)
    hit = b[:, None] == jnp.arange(NB, dtype=jnp.int32)[None, :]
    slot = brank.reshape(T) + jnp.sum(jnp.where(hit, base[None, :], 0), axis=1)
    n_steps = ends[NB - 1] // TMM
    starts = jnp.minimum(jnp.arange(MAX_STEPS, dtype=jnp.int32) * TMM, ends[NB - 1] - TMM)
    bucket_of_step = jnp.sum((starts[:, None] >= ends[None, :NB - 1]).astype(jnp.int32), axis=1)
    pair_of_step = bucket_of_step % len(PAIRS)
    pairs = jnp.asarray(PAIRS, dtype=jnp.int32)
    sched = jnp.concatenate([bucket_of_step // len(PAIRS), pairs[pair_of_step, 0], pairs[pair_of_step, 1],
                             n_steps[None]]).astype(jnp.int32)
    fill_plan = jnp.concatenate([base + counts, ends, n_steps[None]]).astype(jnp.int32)
    return slot.astype(jnp.int32), sched, fill_plan


def _row_tile(ref, row, sub):
    return ref.at[pl.ds(pl.multiple_of(row * sub, sub), sub), :]


def _row_copies(n_rows, make_copy, per_row=None):
    def body(i, carry):
        for j in range(ISSUE_UNROLL):
            r = i * ISSUE_UNROLL + j
            make_copy(r).start(priority=j % 2)
            if per_row is not None:
                per_row(r)
        return carry
    lax.fori_loop(0, n_rows // ISSUE_UNROLL, body, 0)


def _dispatch_kernel(pad_ref, slot_ref, wa_ref, h_hbm, o_hbm, ws_ref, zero_scr, zvec_scr, stage,
                     zsem, in_sem, row_sem):
    t = pl.program_id(0)
    nt = T // TM
    cur = t % 2

    def load(tile, b):
        return pltpu.make_async_copy(
            h_hbm.at[pl.ds(pl.multiple_of(tile * (TM * X_SUB), TM * X_SUB), TM * X_SUB), :],
            stage.at[b], in_sem.at[b])

    def rows_done(b):
        pltpu.make_async_copy(stage.at[b], o_hbm.at[pl.ds(0, TM * X_SUB), :], row_sem.at[b]).wait()

    @pl.when(t == 0)
    def _():
        load(0, 0).start()

    @pl.when(t == 0)
    def _():
        zvec_scr[...] = jnp.zeros_like(zvec_scr)
        clear = pltpu.make_async_copy(zvec_scr, ws_ref, zsem.at[0])
        clear.start()
        clear.wait()

    @pl.when(t == 0)
    def _():
        zero_scr[...] = jnp.zeros_like(zero_scr)

        def chunk_fill(row):
            return pltpu.make_async_copy(
                zero_scr.at[pl.ds(0, FILL * X_SUB), :],
                o_hbm.at[pl.ds(pl.multiple_of(row * X_SUB, X_SUB), FILL * X_SUB), :], zsem.at[0])

        def tile_fill(s):
            return pltpu.make_async_copy(
                zero_scr, o_hbm.at[pl.ds(s * TMM * X_SUB, TMM * X_SUB), :], zsem.at[1])

        for phase in ("start", "wait"):
            for k in range(NB):
                for c in range(TMM // FILL):
                    top = pad_ref[NB + k] - c * FILL

                    @pl.when(top > pad_ref[k])
                    def _():
                        getattr(chunk_fill(top - FILL), phase)()
        for phase in ("start", "wait"):
            for s in range(T // TMM, SORTED_TILES):
                @pl.when(s >= pad_ref[2 * NB])
                def _():
                    getattr(tile_fill(s), phase)()

    def put_weight(r):
        ws_ref[slot_ref[r]] = wa_ref[r]

    @pl.when(t > 0)
    def _():
        rows_done(1 - cur)

    @pl.when(t + 1 < nt)
    def _():
        load(t + 1, 1 - cur).start()

    load(t, cur).wait()
    _row_copies(TM, lambda r: pltpu.make_async_copy(
        _row_tile(stage.at[cur], r, X_SUB), _row_tile(o_hbm, slot_ref[r], X_SUB), row_sem.at[cur]),
        per_row=put_weight)

    @pl.when(t == nt - 1)
    def _():
        rows_done(cur)


def _dispatch(pad_start, slot, wa, h2c):
    nt = T // TM
    return pl.pallas_call(
        _dispatch_kernel,
        grid_spec=pltpu.PrefetchScalarGridSpec(
            num_scalar_prefetch=1,
            grid=(nt,),
            in_specs=[
                pl.BlockSpec((TM,), lambda t, pad: (t,), memory_space=pltpu.SMEM),
                pl.BlockSpec((TM,), lambda t, pad: (t,), memory_space=pltpu.SMEM),
                pl.BlockSpec(memory_space=pl.ANY),
            ],
            out_specs=[pl.BlockSpec(memory_space=pl.ANY), pl.BlockSpec(memory_space=pltpu.SMEM)],
            scratch_shapes=[pltpu.VMEM((TMM * X_SUB, LANES), F32), pltpu.VMEM((SORTED_ROWS,), F32),
                            pltpu.VMEM((2, TM * X_SUB, LANES), F32),
                            pltpu.SemaphoreType.DMA((2,)), pltpu.SemaphoreType.DMA((2,)),
                            pltpu.SemaphoreType.DMA((2,))],
        ),
        out_shape=[jax.ShapeDtypeStruct((SORTED_ROWS * X_SUB, LANES), F32),
                   jax.ShapeDtypeStruct((SORTED_ROWS,), F32)],
        compiler_params=_cp(1),
        name="dispatch",
    )(pad_start, slot, wa, h2c)


def _experts_kernel(sched_ref, xs_ref, ws_ref, w1_ref, w3_ref, w2_ref, o_ref, w1b, w3b, w2b, wt_scr):
    s = pl.program_id(0)

    @pl.when(jnp.logical_or(s == 0, sched_ref[s] != sched_ref[jnp.maximum(s - 1, 0)]))
    def _():
        w1b[...] = w1_ref[...].astype(BF)
        w3b[...] = w3_ref[...].astype(BF)
        w2b[...] = w2_ref[...].astype(BF)

    n_steps = sched_ref[3 * MAX_STEPS]

    @pl.when(s < n_steps)
    def _():
        h = jnp.concatenate([xs_ref[pl.ds(k, TMM, stride=X_SUB), :] for k in range(X_SUB)],
                            axis=1).astype(BF)
        wt_scr[...] = jnp.zeros_like(wt_scr)
        wt_scr[0:1, :] = ws_ref[...]
        w_first = wt_scr[...].T[:, 0:1]
        weights = (w_first, 1.0 - w_first)
        acc = None
        for j in range(2):
            e = sched_ref[(1 + j) * MAX_STEPS + s]
            a = _dot(h, w1b[e])
            b = _dot(h, w3b[e])
            hid = (a * (1.0 / (1.0 + jnp.exp(-a))) * b * weights[j]).astype(BF)
            y = _dot(hid, w2b[e])
            acc = y if acc is None else acc + y
        for k in range(D // LANES):
            o_ref[pl.ds(k, TMM, stride=Y_SUB), :] = acc[:, k * LANES:(k + 1) * LANES]

    @pl.when(s >= n_steps)
    def _():
        o_ref[...] = jnp.zeros_like(o_ref)


def _experts(sched, xs, ws, l, w1, w3, w2):
    row_blk = lambda s, sched: (jnp.maximum(jnp.minimum(s, sched[3 * MAX_STEPS] - 1), 0), 0)
    grp_blk = lambda s, sched: (l, sched[s], 0, 0)
    return pl.pallas_call(
        _experts_kernel,
        grid_spec=pltpu.PrefetchScalarGridSpec(
            num_scalar_prefetch=1,
            grid=(MAX_STEPS,),
            in_specs=[
                pl.BlockSpec((TMM * X_SUB, LANES), row_blk),
                pl.BlockSpec((None, 1, TMM), lambda s, sched: row_blk(s, sched) + (0,)),
                pl.BlockSpec((None, EPG, D, D_EXPERT), grp_blk),
                pl.BlockSpec((None, EPG, D, D_EXPERT), grp_blk),
                pl.BlockSpec((None, EPG, D_EXPERT, D), grp_blk),
            ],
            out_specs=pl.BlockSpec((TMM * Y_SUB, LANES), lambda s, sched: (s, 0)),
            scratch_shapes=[pltpu.VMEM((EPG, D, D_EXPERT), BF), pltpu.VMEM((EPG, D, D_EXPERT), BF),
                            pltpu.VMEM((EPG, D_EXPERT, D), BF), pltpu.VMEM((LANES, TMM), F32)],
        ),
        out_shape=jax.ShapeDtypeStruct((MAX_STEPS * TMM * Y_SUB, LANES), F32),
        compiler_params=_cp(1),
        name="experts",
    )(sched, xs, ws, w1, w3, w2)


def _pending_specs(l):
    nt = T // TM
    return [
        pl.BlockSpec((TM,), lambda t: (t,), memory_space=pltpu.SMEM),
        pl.BlockSpec((TM,), lambda t: (jnp.minimum(t + 1, nt - 1),), memory_space=pltpu.SMEM),
        pl.BlockSpec((TM,), lambda t: (jnp.minimum(t + 2, nt - 1),), memory_space=pltpu.SMEM),
        pl.BlockSpec(memory_space=pl.ANY),
        pl.BlockSpec((TM, D), lambda t: (t, 0)),
        pl.BlockSpec((1, 1, 6 * D), lambda t: (l * MOD_ROWS + _mod_row(t, TM), 0, 0)),
    ]


def _pending_args(pending):
    return (pending["slot"],) * 3 + (pending["ys"], pending["x"], pending["mod"])


GATHER_BUFS = 3
PENDING_SCRATCH = [pltpu.VMEM((GATHER_BUFS, TM * Y_SUB, LANES), F32), pltpu.SemaphoreType.DMA((GATHER_BUFS,))]
N_PENDING = 6


def _apply_pending(slot_ref, slot1_ref, slot2_ref, ys_hbm, x_ref, mod_ref, buf, sem):
    t = pl.program_id(0)
    nt = T // TM
    cur = t % GATHER_BUFS

    def gather(idx_ref, b):
        _row_copies(TM, lambda r: pltpu.make_async_copy(
            _row_tile(ys_hbm, idx_ref[r], Y_SUB), _row_tile(buf.at[b], r, Y_SUB), sem.at[b]))

    @pl.when(t == 0)
    def _():
        gather(slot_ref, 0)
        if nt > 1:
            gather(slot1_ref, 1)

    @pl.when(t + 2 < nt)
    def _():
        gather(slot2_ref, (t + 2) % GATHER_BUFS)

    pltpu.make_async_copy(ys_hbm.at[pl.ds(0, TM * Y_SUB), :], buf.at[cur], sem.at[cur]).wait()
    gate = mod_ref[0, :, 5 * D:6 * D]
    y = jnp.concatenate([buf[cur, pl.ds(k, TM, stride=Y_SUB), :] for k in range(Y_SUB)], axis=1)
    return x_ref[...] + gate * y


def _final_kernel(slot_ref, slot1_ref, slot2_ref, ys_hbm, x_ref, mod_ref, gfin_ref, op_ref, os_ref,
                  buf, sem):
    t = pl.program_id(0)
    x_new = _apply_pending(slot_ref, slot1_ref, slot2_ref, ys_hbm, x_ref, mod_ref, buf, sem)
    y_out = _rms(x_new, gfin_ref[...])
    npt = NP_TOK // TM

    @pl.when(t < npt)
    def _():
        op_ref[...] = y_out

    @pl.when(t >= npt)
    def _():
        os_ref[...] = y_out


def _final(pending, g_final):
    nt = T // TM
    npt = NP_TOK // TM
    return pl.pallas_call(
        _final_kernel,
        grid=(nt,),
        in_specs=_pending_specs(pending["layer"]) + [pl.BlockSpec((1, D), lambda t: (0, 0))],
        out_specs=[pl.BlockSpec((TM, D), lambda t: (jnp.minimum(t, npt - 1), 0)),
                   pl.BlockSpec((TM, D), lambda t: (jnp.maximum(t - npt, 0), 0))],
        out_shape=[jax.ShapeDtypeStruct((NP_TOK, D), F32), jax.ShapeDtypeStruct((NS_TOK, D), F32)],
        scratch_shapes=PENDING_SCRATCH,
        compiler_params=_cp(1),
        name="final",
    )(*_pending_args(pending), g_final)


def _moe(x, mod, l, rows, bucket, brank, wa, cnt, w1, w3, w2):
    slot, sched, fill_plan = _routing_plan(bucket, brank, cnt)
    xs, ws = _dispatch(fill_plan, slot, wa.reshape(T), rows)
    ys = _experts(sched, xs, ws.reshape(SORTED_TILES, 1, TMM), l, w1, w3, w2)
    return {"slot": slot, "ys": ys, "x": x, "mod": mod, "layer": l}


def _rope_tables():
    pos = jnp.arange(DEC_SEQ)
    r = (pos // GRID_W).astype(F32)
    col = (pos % GRID_W).astype(F32)
    inv = ROPE_BASE ** (-jnp.arange(AX_FREQS, dtype=F32) / AX_FREQS)
    ang = jnp.stack([r[:, None] * inv, col[:, None] * inv], axis=1)
    cos = jnp.cos(ang)[:, :, None, :]
    sin = jnp.sin(ang)[:, :, None, :]
    c32 = jnp.broadcast_to(cos, (DEC_SEQ, 2, 2, AX_FREQS)).reshape(DEC_SEQ, ROPE)
    s32 = jnp.concatenate([-sin, sin], axis=2).reshape(DEC_SEQ, ROPE)
    pad = HEAD_PAD - NOPE - ROPE
    c = jnp.concatenate([jnp.ones((DEC_SEQ, NOPE), F32), c32, jnp.zeros((DEC_SEQ, pad), F32)], axis=1)
    s = jnp.concatenate([jnp.zeros((DEC_SEQ, NOPE), F32), s32, jnp.zeros((DEC_SEQ, pad), F32)], axis=1)
    c_id = jnp.concatenate([jnp.ones((TM, NOPE + ROPE), F32), jnp.zeros((TM, pad), F32)], axis=1)
    s_id = jnp.zeros((TM, HEAD_PAD), F32)
    return jnp.concatenate([c, c_id], axis=0), jnp.concatenate([s, s_id], axis=0)


def _swap_halves(w):
    lead = w.shape[:-1]
    return w.reshape(lead + (2, 2, AX_FREQS))[..., ::-1, :].reshape(lead + (ROPE,))


def _even_weights(w_in, w_q_up, w_kv_up):
    k_in = w_in.shape[0]
    base = 3 * D_CONV + Q_RANK + KV_RANK
    w_kr = w_in[:, base:base + ROPE]
    z = lambda n: jnp.zeros((k_in, n), F32)
    win_ext = jnp.concatenate([
        w_in[:, :base],
        w_kr, z(HEAD_PAD - ROPE),
        z(NOPE), w_kr, z(HEAD_PAD - NOPE - ROPE),
        z(NOPE), _swap_halves(w_kr), z(HEAD_PAD - NOPE - ROPE),
    ], axis=1).astype(BF)
    wq = w_q_up.reshape(Q_RANK, HEADS, QK_DIM)
    zq = lambda n: jnp.zeros((Q_RANK, HEADS, n), F32)
    wq1 = jnp.concatenate([wq, zq(HEAD_PAD - QK_DIM)], axis=2).reshape(Q_RANK, HEADS * HEAD_PAD).astype(BF)
    wq2 = jnp.concatenate([zq(NOPE), _swap_halves(wq[:, :, NOPE:]), zq(HEAD_PAD - QK_DIM)],
                          axis=2).reshape(Q_RANK, HEADS * HEAD_PAD).astype(BF)
    wkv = w_kv_up.reshape(KV_RANK, HEADS, NOPE + V_DIM)
    wk = jnp.concatenate([wkv[:, :, :NOPE], jnp.zeros((KV_RANK, HEADS, HEAD_PAD - NOPE), F32)],
                         axis=2).reshape(KV_RANK, HEADS * HEAD_PAD).astype(BF)
    wv = wkv[:, :, NOPE:].reshape(KV_RANK, HEADS * V_DIM).astype(BF)
    return win_ext, wq1, wq2, wk, wv


def kernel(x_prompt, x_sample, cache_ckv, cache_krope, c, c_ctx, w_ada, b_ada, g_mix, g_ffn, g_final,
           ev_w_in, conv_w, q_norm_g, w_q_up, kv_norm_g, w_kv_up, ev_w_out,
           gm_w_in, gm_v_g, gm_w_s, gm_b_s, gm_w_out, router_w, router_b, moe_w1, moe_w3, moe_w2):
    n_even = ev_w_in.shape[0]
    x_parts = (x_prompt.reshape(NP_TOK, D), x_sample.reshape(NS_TOK, D))
    cc =jnp.concatenate([c, c_ctx[None, :], jnp.zeros((MOD_ROWS - DEC_BATCH - 1, D), F32)], axis=0)
    mod = _ada(cc, w_ada, b_ada).reshape(DEPTH * MOD_ROWS, 1, 6 * D)

    rope_c, rope_s = _rope_tables()
    rw_hi = router_w.T.astype(BF)
    rw_lo = (router_w.T - rw_hi.astype(F32)).astype(BF)
    rwt = jnp.concatenate([rw_hi, rw_lo], axis=0)
    rb = router_b.reshape(N_EXPERTS, 1)
    ev = [_even_weights(ev_w_in[i], w_q_up[i], w_kv_up[i]) for i in range(n_even)]
    kr_ctx = jnp.pad(cache_krope, ((0, 0), (0, 0), (0, 0), (NOPE, HEAD_PAD - NOPE - ROPE)))
    kr_ctx = kr_ctx.transpose(1, 0, 2, 3).reshape(n_even, DEC_BATCH * PAST, HEAD_PAD)
    ckv_ctx = cache_ckv.transpose(1, 0, 2, 3).reshape(n_even, DEC_BATCH * PAST, KV_RANK)
    kc, vc = _ctx_kv(ckv_ctx, kr_ctx, jnp.stack([e[3] for e in ev]), jnp.stack([e[4] for e in ev]))

    ckv_states, kr_states = [], []
    pending = None
    for l in range(DEPTH):
        i = l // 2
        if l % 2 == 0:
            win_ext, wq1, wq2, wk, wv = ev[i]
            outs = _even_in(
                None if pending else x_parts, pending, mod, l, g_mix[l][None, :], win_ext,
                q_norm_g[i][None, :], wq1, wq2, kv_norm_g[i][None, :], wk, wv, rope_c, rope_s)
            bg, cv, q, k, v, ckv, kr = outs[:7]
            if pending:
                x_parts = (outs[7],)
            attn_parts = (_attn_prompt(q, k, v), _attn_sample(q, k, v, kc, vc, i))
            x, *routed = _even_out(x_parts, mod, l, bg, cv, attn_parts, conv_w[i],
                                                 ev_w_out[i].astype(BF), g_ffn[l][None, :], rwt, rb)
            ckv_states.append(ckv[:NP_TOK].reshape(BATCH, SEQ, KV_RANK))
            kr_states.append(kr[:NP_TOK, :ROPE].reshape(BATCH, SEQ, ROPE))
        else:
            x, *routed = _odd(pending, mod, l, g_mix[l][None, :], gm_w_in[i].astype(BF),
                                            gm_v_g[i][None, :], gm_w_s[i].astype(BF), gm_b_s[i].T,
                                            gm_w_out[i].astype(BF), g_ffn[l][None, :], rwt, rb)
        pending = _moe(x, mod, l, *routed, moe_w1, moe_w3, moe_w2)

    y_prompt, y_sample = _final(pending, g_final[None, :])
    y_prompt = y_prompt.reshape(BATCH, SEQ, D)
    y_sample = y_sample.reshape(DEC_BATCH, DEC_SEQ, D)
    return (y_prompt, y_sample, jnp.stack(ckv_states, axis=1), jnp.stack(kr_states, axis=1))
```

```python
import functools
import math

import jax
import jax.numpy as jnp
from jax import lax
from jax.experimental import pallas as pl
from jax.experimental.pallas import tpu as pltpu

D = 1024
BATCH, SEQ = 32, 256
DEC_BATCH, DEC_SEQ = 8, 2048
PAST = 256
DEPTH = 4
GRID_W = 64
D_CONV = 512
HEADS = 8
NOPE, ROPE, V_DIM = 64, 32, 64
QK_DIM = NOPE + ROPE
Q_RANK, KV_RANK = 384, 256
AX_FREQS = ROPE // 4
ROPE_BASE = 10000.0
CHUNK = 128
GM_GROUPS = 8
N_EXPERTS, N_GROUPS, EPG = 16, 4, 4
D_EXPERT = 256
EPS = 1e-6

NP_TOK = BATCH * SEQ
NS_TOK = DEC_BATCH * DEC_SEQ
T = NP_TOK + NS_TOK
MOD_ROWS = 16
HEAD_PAD = 128
HALO = 16

TM = 512
PAIRS = ((0, 1), (0, 2), (0, 3), (1, 2), (1, 3), (2, 3))
NB = N_GROUPS * len(PAIRS)
NB_PAD = 32
TMM = 512
FILL = 128
MAX_STEPS = T // TMM + NB
SORTED_TILES = MAX_STEPS + MAX_STEPS % 2
SORTED_ROWS = SORTED_TILES * TMM
LANES = 128
Y_SUB = D // LANES
X_SUB = Y_SUB
ISSUE_UNROLL = 32
TQ = 512
ATTN_PAIRS = 4
VMEM_LIMIT = 56 * 1024 * 1024

BF = jnp.bfloat16
F32 = jnp.float32


def _cp(n_axes):
    return pltpu.CompilerParams(dimension_semantics=("arbitrary",) * n_axes,
                                vmem_limit_bytes=VMEM_LIMIT)


def _mod_row(t, tm):
    n_prompt_tiles = NP_TOK // tm
    per_seq = DEC_SEQ // tm
    return jnp.where(t < n_prompt_tiles, DEC_BATCH, (t - n_prompt_tiles) // per_seq)


def _rms(x, g):
    return x * lax.rsqrt(jnp.mean(x * x, axis=-1, keepdims=True) + EPS) * g


def _norm_mod(x, g, shift, scale):
    return x * lax.rsqrt(jnp.mean(x * x, axis=-1, keepdims=True) + EPS) * (g * (1.0 + scale)) + shift


def _dot(a, b):
    return jnp.dot(a, b, preferred_element_type=F32)


def _dot_nt(a, b, precision=None):
    return lax.dot_general(a, b, (((1,), (1,)), ((), ())), precision=precision,
                           preferred_element_type=F32)


ADA_TN = 1536


def _ada_kernel(cc_ref, w_ref, b_ref, o_ref):
    cc = cc_ref[...]
    s = (cc / (1.0 + jnp.exp(-cc))).astype(BF)
    o_ref[0] = _dot(s, w_ref[0].astype(BF)) + b_ref[0]


def _ada(cc, w_ada, b_ada):
    n = 6 * D
    return pl.pallas_call(
        _ada_kernel,
        grid=(DEPTH, n // ADA_TN),
        in_specs=[
            pl.BlockSpec((MOD_ROWS, D), lambda l, j: (0, 0)),
            pl.BlockSpec((1, D, ADA_TN), lambda l, j: (l, 0, j)),
            pl.BlockSpec((1, 1, ADA_TN), lambda l, j: (l, 0, j)),
        ],
        out_specs=pl.BlockSpec((1, MOD_ROWS, ADA_TN), lambda l, j: (l, 0, j)),
        out_shape=jax.ShapeDtypeStruct((DEPTH, MOD_ROWS, n), F32),
        compiler_params=_cp(2),
        name="ada",
    )(cc, w_ada, b_ada.reshape(DEPTH, 1, n))


def _route(logits_t, rb):
    sc = 1.0 / (1.0 + jnp.exp(-logits_t))
    sel = sc + rb
    rows = [sel[e:e + 1, :] for e in range(N_EXPERTS)]
    srows = [sc[e:e + 1, :] for e in range(N_EXPERTS)]

    def top2sum(a, b, c, d):
        hi1, lo1 = jnp.maximum(a, b), jnp.minimum(a, b)
        hi2, lo2 = jnp.maximum(c, d), jnp.minimum(c, d)
        return jnp.maximum(hi1, hi2) + jnp.maximum(jnp.minimum(hi1, hi2), jnp.maximum(lo1, lo2))

    gs = [top2sum(*rows[EPG * g:EPG * (g + 1)]) for g in range(N_GROUPS)]
    best = gs[0]
    gidx = jnp.zeros_like(best, dtype=jnp.int32)
    for g in range(1, N_GROUPS):
        upd = gs[g] > best
        best = jnp.where(upd, gs[g], best)
        gidx = jnp.where(upd, g, gidx)

    picked = []
    for g in range(N_GROUPS):
        grp = rows[EPG * g:EPG * (g + 1)]
        in_g = gidx == g
        for j in range(EPG):
            rank = jnp.zeros_like(gidx)
            for k in range(EPG):
                if k == j:
                    continue
                ahead = grp[k] > grp[j]
                if k < j:
                    ahead = ahead | (grp[k] == grp[j])
                rank = rank + ahead.astype(jnp.int32)
            picked.append(in_g & (rank < 2))
    w = [jnp.where(picked[e], srows[e], 0.0) for e in range(N_EXPERTS)]
    wsum = w[0]
    for e in range(1, N_EXPERTS):
        wsum = wsum + w[e]
    inv = 1.0 / wsum
    pj, wj = [], []
    for j in range(EPG):
        hit, acc = picked[j], w[j]
        for g in range(1, N_GROUPS):
            hit = hit | picked[EPG * g + j]
            acc = acc + w[EPG * g + j]
        pj.append(hit)
        wj.append(acc * inv)
    pair = jnp.full_like(gidx, len(PAIRS) - 1)
    for idx in range(len(PAIRS) - 2, -1, -1):
        a, b = PAIRS[idx]
        pair = jnp.where(pj[a] & pj[b], idx, pair)
    w_a = jnp.where(pj[0], wj[0], jnp.where(pj[1], wj[1], wj[2]))
    w_b = jnp.where(pj[3], wj[3], jnp.where(pj[2], wj[2], wj[1]))
    return (w_a, w_b), gidx * len(PAIRS) + pair


def _ffn_prep(x_new, mod_ref, gf_ref, rwt_ref, rb_ref, h2c_ref, gidx_ref, grank_ref, wa_ref, cnt_ref,
              run_scr):
    t = pl.program_id(0)
    tm = x_new.shape[0]
    shift = mod_ref[0, :, 3 * D:4 * D]
    scale = mod_ref[0, :, 4 * D:5 * D]
    h2 = _norm_mod(x_new, gf_ref[...], shift, scale)
    h_hi = h2.astype(BF)
    h_lo = (h2 - h_hi.astype(F32)).astype(BF)
    by_hi = _dot_nt(rwt_ref[...], h_hi)
    by_lo = _dot_nt(rwt_ref[0:N_EXPERTS, :], h_lo)
    logits_t = by_hi[0:N_EXPERTS, :] + by_hi[N_EXPERTS:2 * N_EXPERTS, :] + by_lo
    wab, gidx = _route(logits_t, rb_ref[...])
    wa_ref[0] = wab[0]

    for k in range(X_SUB):
        h2c_ref[pl.ds(k, tm, stride=X_SUB), :] = h2[:, k * LANES:(k + 1) * LANES]

    @pl.when(t == 0)
    def _():
        run_scr[...] = jnp.zeros_like(run_scr)

    onehot = (lax.broadcasted_iota(jnp.int32, (NB_PAD, tm), 0) == gidx).astype(F32)
    earlier = (lax.broadcasted_iota(jnp.int32, (tm, tm), 0)
               < lax.broadcasted_iota(jnp.int32, (tm, tm), 1)).astype(BF)
    rank = _dot(onehot.astype(BF), earlier)
    run = run_scr[:, 0:1]
    grank = jnp.sum(onehot * (rank + run), axis=0, keepdims=True)
    gidx_ref[0] = gidx
    grank_ref[0] = grank.astype(jnp.int32)
    run_scr[...] = run_scr[...] + jnp.sum(onehot, axis=1, keepdims=True)
    cnt_ref[...] = run_scr[...]


def _prep_out_specs():
    tok = lambda w: pl.BlockSpec((TM, w), lambda t: (t, 0))
    lanes = pl.BlockSpec((1, 1, TM), lambda t: (t, 0, 0))
    rows = pl.BlockSpec((TM * X_SUB, LANES), lambda t: (t, 0))
    return [tok(D), rows, lanes, lanes, lanes, pl.BlockSpec((NB_PAD, 128), lambda t: (0, 0))]


def _prep_out_shapes():
    nt = T // TM
    return [jax.ShapeDtypeStruct((T, D), F32), jax.ShapeDtypeStruct((T * X_SUB, LANES), F32),
            jax.ShapeDtypeStruct((nt, 1, TM), jnp.int32), jax.ShapeDtypeStruct((nt, 1, TM), jnp.int32),
            jax.ShapeDtypeStruct((nt, 1, TM), F32), jax.ShapeDtypeStruct((NB_PAD, 128), F32)]


EV_EXT = 3 * D_CONV + Q_RANK + KV_RANK + 3 * HEAD_PAD


def _tok_specs(parts, width):
    npt = NP_TOK // TM
    if len(parts) == 1:
        return [pl.BlockSpec((TM, width), lambda t: (t, 0))]
    return [pl.BlockSpec((TM, width), lambda t: (jnp.minimum(t, npt - 1), 0)),
            pl.BlockSpec((TM, width), lambda t: (jnp.maximum(t - npt, 0), 0))]


def _tok_load(refs):
    if len(refs) == 1:
        return refs[0][...]
    return jnp.where(pl.program_id(0) < NP_TOK // TM, refs[0][...], refs[1][...])


def _even_in_kernel(n_x, *refs):
    if n_x:
        x_refs, refs = refs[:n_x], refs[n_x:]
        x = _tok_load(x_refs)
    else:
        pend, refs, scratch = refs[:N_PENDING], refs[N_PENDING:-2], refs[-2:]
        refs, xo_ref = refs[:-1], refs[-1]
        x = _apply_pending(*pend, *scratch)
        xo_ref[...] = x
    (mod_ref, g_ref, win_ref, qg_ref, wq1_ref, wq2_ref, kg_ref, wk_ref, wv_ref, rc_ref, rs_ref,
     bg_ref, cv_ref, q_ref, k_ref, v_ref, ckv_ref, kr_ref) = refs
    shift = mod_ref[0, :, 0:D]
    scale = mod_ref[0, :, D:2 * D]
    h = _norm_mod(x, g_ref[...], shift, scale).astype(BF)
    proj = _dot(h, win_ref[...])
    o = 0
    b_g = proj[:, o:o + D_CONV]; o += D_CONV
    c_g = proj[:, o:o + D_CONV]; o += D_CONV
    v_in = proj[:, o:o + D_CONV]; o += D_CONV
    q_a = proj[:, o:o + Q_RANK]; o += Q_RANK
    kv_a = proj[:, o:o + KV_RANK]; o += KV_RANK
    kr_raw = proj[:, o:o + HEAD_PAD]; o += HEAD_PAD
    kr_cat = proj[:, o:o + HEAD_PAD]; o += HEAD_PAD
    kr_sw = proj[:, o:o + HEAD_PAD]

    bg_ref[...] = b_g.astype(BF)
    cv_ref[...] = (c_g * v_in).astype(BF)
    kr_ref[...] = kr_raw

    rc = rc_ref[...]
    rs = rs_ref[...]
    qn = _rms(q_a, qg_ref[...]).astype(BF)
    q1 = _dot(qn, wq1_ref[...])
    q2 = _dot(qn, wq2_ref[...])
    qscale = QK_DIM ** -0.5
    for hd in range(HEADS):
        sl = slice(hd * HEAD_PAD, (hd + 1) * HEAD_PAD)
        q_ref[:, sl] = ((q1[:, sl] * rc + q2[:, sl] * rs) * qscale).astype(BF)

    ckv = _rms(kv_a, kg_ref[...])
    ckv_ref[...] = ckv
    ckv_b = ckv.astype(BF)
    kk = _dot(ckv_b, wk_ref[...])
    kr = kr_cat * rc + kr_sw * rs
    for hd in range(HEADS):
        sl = slice(hd * HEAD_PAD, (hd + 1) * HEAD_PAD)
        k_ref[:, sl] = (kk[:, sl] + kr).astype(BF)
    v_ref[...] = _dot(ckv_b, wv_ref[...]).astype(BF)


def _even_in(x_parts, pending, mod, l, g_mix, win, qg, wq1, wq2, kg, wk, wv, rope_c, rope_s):
    nt = T // TM
    npt = NP_TOK // TM
    per_seq = DEC_SEQ // TM
    ident_blk = DEC_SEQ // TM

    def rope_idx(t):
        return (jnp.where(t < npt, ident_blk, (t - npt) % per_seq), 0)

    full = lambda shape: pl.BlockSpec(shape, lambda t: (0,) * len(shape))
    tok = lambda w: pl.BlockSpec((TM, w), lambda t: (t, 0))
    out_specs = [tok(D_CONV), tok(D_CONV), tok(HEADS * HEAD_PAD), tok(HEADS * HEAD_PAD),
                 tok(HEADS * V_DIM), tok(KV_RANK), tok(HEAD_PAD)]
    out_shape = [
        jax.ShapeDtypeStruct((T, D_CONV), BF), jax.ShapeDtypeStruct((T, D_CONV), BF),
        jax.ShapeDtypeStruct((T, HEADS * HEAD_PAD), BF), jax.ShapeDtypeStruct((T, HEADS * HEAD_PAD), BF),
        jax.ShapeDtypeStruct((T, HEADS * V_DIM), BF),
        jax.ShapeDtypeStruct((T, KV_RANK), F32), jax.ShapeDtypeStruct((T, HEAD_PAD), F32),
    ]
    if pending is None:
        lead_specs, lead_args, scratch = _tok_specs(x_parts, D), tuple(x_parts), []
    else:
        lead_specs, lead_args, scratch = _pending_specs(pending["layer"]), _pending_args(pending), PENDING_SCRATCH
        out_specs.append(tok(D))
        out_shape.append(jax.ShapeDtypeStruct((T, D), F32))
    return pl.pallas_call(
        functools.partial(_even_in_kernel, 0 if pending is not None else len(x_parts)),
        grid=(nt,),
        in_specs=lead_specs + [
            pl.BlockSpec((1, 1, 6 * D), lambda t: (l * MOD_ROWS + _mod_row(t, TM), 0, 0)),
            full((1, D)), full((D, EV_EXT)), full((1, Q_RANK)),
            full((Q_RANK, HEADS * HEAD_PAD)), full((Q_RANK, HEADS * HEAD_PAD)),
            full((1, KV_RANK)), full((KV_RANK, HEADS * HEAD_PAD)), full((KV_RANK, HEADS * V_DIM)),
            pl.BlockSpec((TM, HEAD_PAD), rope_idx), pl.BlockSpec((TM, HEAD_PAD), rope_idx),
        ],
        out_specs=out_specs,
        out_shape=out_shape,
        scratch_shapes=scratch,
        compiler_params=_cp(1),
        name="even_in",
    )(*lead_args, mod, g_mix, win, qg, wq1, wq2, kg, wk, wv, rope_c, rope_s)


CTX_TM = 512


def _ctx_kv_kernel(ckv_ref, kr_ref, wk_ref, wv_ref, k_ref, v_ref):
    ckv_b = ckv_ref[0].astype(BF)
    kk = _dot(ckv_b, wk_ref[0])
    kr = kr_ref[0]
    for hd in range(HEADS):
        sl = slice(hd * HEAD_PAD, (hd + 1) * HEAD_PAD)
        k_ref[0, :, sl] = (kk[:, sl] + kr).astype(BF)
    v_ref[0] = _dot(ckv_b, wv_ref[0]).astype(BF)


def _ctx_kv(ckv_all, kr_all, wk_all, wv_all):
    n_even = ckv_all.shape[0]
    rows = DEC_BATCH * PAST
    return pl.pallas_call(
        _ctx_kv_kernel,
        grid=(n_even, rows // CTX_TM),
        in_specs=[
            pl.BlockSpec((1, CTX_TM, KV_RANK), lambda i, t: (i, t, 0)),
            pl.BlockSpec((1, CTX_TM, HEAD_PAD), lambda i, t: (i, t, 0)),
            pl.BlockSpec((1, KV_RANK, HEADS * HEAD_PAD), lambda i, t: (i, 0, 0)),
            pl.BlockSpec((1, KV_RANK, HEADS * V_DIM), lambda i, t: (i, 0, 0)),
        ],
        out_specs=[
            pl.BlockSpec((1, CTX_TM, HEADS * HEAD_PAD), lambda i, t: (i, t, 0)),
            pl.BlockSpec((1, CTX_TM, HEADS * V_DIM), lambda i, t: (i, t, 0)),
        ],
        out_shape=[jax.ShapeDtypeStruct((n_even, rows, HEADS * HEAD_PAD), BF),
                   jax.ShapeDtypeStruct((n_even, rows, HEADS * V_DIM), BF)],
        compiler_params=_cp(2),
        name="ctx_kv",
    )(ckv_all, kr_all, wk_all, wv_all)


def _attn_body(n_pairs, has_ctx, q_ref, k_ref, v_ref, *rest):
    if has_ctx:
        kc_ref, vc_ref, o_ref = rest
    else:
        (o_ref,) = rest
    lane = lax.broadcasted_iota(jnp.int32, (1, 2 * V_DIM), 1)
    w = 2 * V_DIM
    ones = jnp.ones((k_ref.shape[0], w), BF)
    ones_ctx = jnp.ones((PAST, w), BF)
    for pr in range(n_pairs):
        vsl = slice(pr * w, (pr + 1) * w)
        v = jnp.concatenate([v_ref[:, vsl], ones], axis=1)
        if has_ctx:
            vc = jnp.concatenate([vc_ref[0, :, vsl], ones_ctx], axis=1)
        outs = []
        for sub in range(2):
            hsl = slice((2 * pr + sub) * HEAD_PAD, (2 * pr + sub + 1) * HEAD_PAD)
            q = q_ref[:, hsl]
            s1 = _dot_nt(q, k_ref[:, hsl])
            m = jnp.max(s1, axis=-1, keepdims=True)
            if has_ctx:
                s2 = _dot_nt(q, kc_ref[0, :, hsl])
                m = jnp.maximum(m, jnp.max(s2, axis=-1, keepdims=True))
            acc = _dot(jnp.exp(s1 - m).astype(BF), v)
            if has_ctx:
                acc = acc + _dot(jnp.exp(s2 - m).astype(BF), vc)
            outs.append(acc[:, 0:w] * (1.0 / acc[:, w:2 * w]))
        o_ref[:, vsl] = jnp.where(lane < V_DIM, outs[0], outs[1]).astype(BF)


def _attn_prompt(q, k, v):
    return pl.pallas_call(
        functools.partial(_attn_body, HEADS // 2, False),
        grid=(BATCH,),
        in_specs=[
            pl.BlockSpec((SEQ, HEADS * HEAD_PAD), lambda b: (b, 0)),
            pl.BlockSpec((SEQ, HEADS * HEAD_PAD), lambda b: (b, 0)),
            pl.BlockSpec((SEQ, HEADS * V_DIM), lambda b: (b, 0)),
        ],
        out_specs=pl.BlockSpec((SEQ, HEADS * V_DIM), lambda b: (b, 0)),
        out_shape=jax.ShapeDtypeStruct((NP_TOK, HEADS * V_DIM), BF),
        compiler_params=_cp(1),
        name="attn_prompt",
    )(q, k, v)


def _attn_sample(q, k, v, kc, vc, i):
    s_blk0 = NP_TOK // DEC_SEQ
    q_blk0 = NP_TOK // TQ
    nq = DEC_SEQ // TQ
    hw = ATTN_PAIRS * 2 * HEAD_PAD
    vw = ATTN_PAIRS * 2 * V_DIM
    return pl.pallas_call(
        functools.partial(_attn_body, ATTN_PAIRS, True),
        grid=(DEC_BATCH, HEADS // (2 * ATTN_PAIRS), nq),
        in_specs=[
            pl.BlockSpec((TQ, hw), lambda b, hp, j: (q_blk0 + b * nq + j, hp)),
            pl.BlockSpec((DEC_SEQ, hw), lambda b, hp, j: (s_blk0 + b, hp)),
            pl.BlockSpec((DEC_SEQ, vw), lambda b, hp, j: (s_blk0 + b, hp)),
            pl.BlockSpec((1, PAST, hw), lambda b, hp, j: (i, b, hp)),
            pl.BlockSpec((1, PAST, vw), lambda b, hp, j: (i, b, hp)),
        ],
        out_specs=pl.BlockSpec((TQ, vw), lambda b, hp, j: (b * nq + j, hp)),
        out_shape=jax.ShapeDtypeStruct((NS_TOK, HEADS * V_DIM), BF),
        compiler_params=_cp(3),
        name="attn_sample",
    )(q, k, v, kc, vc)


def _even_out_kernel(n_x, *refs):
    x_refs, refs = refs[:n_x], refs[n_x:]
    (mod_ref, bg_ref, cv_ref, cvp_ref, cvn_ref, atp_ref, ats_ref, cw_ref, wo_ref, gf_ref, rwt_ref, rb_ref,
     xo_ref, h2c_ref, gidx_ref, grank_ref, wa_ref, cnt_ref, run_scr) = refs
    t = pl.program_id(0)
    npt = NP_TOK // TM
    per_seq = DEC_SEQ // TM
    cv = cv_ref[...].astype(F32)
    r = lax.broadcasted_iota(jnp.int32, (TM, 1), 0)
    is_prompt = t < npt
    tile_in_seq = (t - npt) % per_seq
    first_row = jnp.where(is_prompt, 0, jnp.where(tile_in_seq == 0, 0, -1))
    last_row = jnp.where(is_prompt, SEQ - 1, jnp.where(tile_in_seq == per_seq - 1, TM - 1, -1))
    period_mask = jnp.where(is_prompt, SEQ - 1, TM - 1)
    first = (r & period_mask) == first_row
    last = (r & period_mask) == last_row
    prev_row = cvp_ref[HALO - 1:HALO, :].astype(F32)
    next_row = cvn_ref[0:1, :].astype(F32)
    prev = jnp.where(r == 0, prev_row, pltpu.roll(cv, 1, 0))
    prev = jnp.where(first, 0.0, prev)
    nxt = jnp.where(r == TM - 1, next_row, pltpu.roll(cv, TM - 1, 0))
    nxt = jnp.where(last, 0.0, nxt)
    cw = cw_ref[...]
    conv = prev * cw[0:1, :] + cv * cw[1:2, :] + nxt * cw[2:3, :]
    yc = (bg_ref[...].astype(F32) * conv).astype(BF)
    attn = _tok_load((atp_ref, ats_ref))
    out = _dot(yc, wo_ref[0:D_CONV, :]) + _dot(attn, wo_ref[D_CONV:2 * D_CONV, :])
    gate = mod_ref[0, :, 2 * D:3 * D]
    x_new = _tok_load(x_refs) + gate * out
    xo_ref[...] = x_new
    _ffn_prep(x_new, mod_ref, gf_ref, rwt_ref, rb_ref, h2c_ref, gidx_ref, grank_ref, wa_ref, cnt_ref,
              run_scr)


def _even_out(x_parts, mod, l, bg, cv, attn_parts, conv_w, w_out, g_ffn, rwt, rb):
    nt = T // TM
    hb = TM // HALO
    nhb = T // HALO
    full = lambda shape: pl.BlockSpec(shape, lambda t: (0,) * len(shape))
    tok = lambda w: pl.BlockSpec((TM, w), lambda t: (t, 0))
    return pl.pallas_call(
        functools.partial(_even_out_kernel, len(x_parts)),
        grid=(nt,),
        in_specs=_tok_specs(x_parts, D) + [
            pl.BlockSpec((1, 1, 6 * D), lambda t: (l * MOD_ROWS + _mod_row(t, TM), 0, 0)),
            tok(D_CONV), tok(D_CONV),
            pl.BlockSpec((HALO, D_CONV), lambda t: (jnp.maximum(t * hb - 1, 0), 0)),
            pl.BlockSpec((HALO, D_CONV), lambda t: (jnp.minimum((t + 1) * hb, nhb - 1), 0)),
        ] + _tok_specs(attn_parts, HEADS * V_DIM) + [
            full((3, D_CONV)), full((2 * D_CONV, D)), full((1, D)),
            full((2 * N_EXPERTS, D)), full((N_EXPERTS, 1)),
        ],
        out_specs=_prep_out_specs(),
        out_shape=_prep_out_shapes(),
        scratch_shapes=[pltpu.VMEM((NB_PAD, 128), F32)],
        compiler_params=_cp(1),
        name="even_out",
    )(*x_parts, mod, bg, cv, cv, cv, *attn_parts, conv_w, w_out, g_ffn, rwt, rb)


def _odd_kernel(*refs):
    pend, refs, scratch = refs[:N_PENDING], refs[N_PENDING:-2], refs[-2:]
    (mod_ref, g_ref, win_ref, vg_ref, ws_ref, bst_ref, wo_ref, gf_ref, rwt_ref, rb_ref,
     xo_ref, h2c_ref, gidx_ref, grank_ref, wa_ref, cnt_ref, gated_scr, run_scr) = refs
    x = _apply_pending(*pend, *scratch)
    shift = mod_ref[0, :, 0:D]
    scale = mod_ref[0, :, D:2 * D]
    h = _norm_mod(x, g_ref[...], shift, scale).astype(BF)
    zl = _dot(h, win_ref[...])
    z = 0.5 * zl * (1.0 + lax.erf(zl * math.sqrt(0.5)))
    u = z[:, 0:D]
    v = _rms(z[:, D:2 * D], vg_ref[...]).astype(BF)
    n_chunks = TM // CHUNK
    gch = D // GM_GROUPS
    for g in range(GM_GROUPS):
        csl = slice(g * gch, (g + 1) * gch)
        vg = jnp.concatenate([v[n * CHUNK:(n + 1) * CHUNK, csl] for n in range(n_chunks)], axis=1)
        sg = _dot(ws_ref[g], vg) + bst_ref[:, g:g + 1]
        for n in range(n_chunks):
            rsl = slice(n * CHUNK, (n + 1) * CHUNK)
            gated_scr[rsl, csl] = (u[rsl, csl] * sg[:, n * gch:(n + 1) * gch]).astype(BF)
    out = _dot(gated_scr[...], wo_ref[...])
    gate = mod_ref[0, :, 2 * D:3 * D]
    x_new = x + gate * out
    xo_ref[...] = x_new
    _ffn_prep(x_new, mod_ref, gf_ref, rwt_ref, rb_ref, h2c_ref, gidx_ref, grank_ref, wa_ref, cnt_ref,
              run_scr)


def _odd(pending, mod, l, g_mix, win, v_g, w_s, b_st, w_out, g_ffn, rwt, rb):
    nt = T // TM
    full = lambda shape: pl.BlockSpec(shape, lambda t: (0,) * len(shape))
    return pl.pallas_call(
        _odd_kernel,
        grid=(nt,),
        in_specs=_pending_specs(pending["layer"]) + [
            pl.BlockSpec((1, 1, 6 * D), lambda t: (l * MOD_ROWS + _mod_row(t, TM), 0, 0)),
            full((1, D)), full((D, 2 * D)), full((1, D)),
            full((GM_GROUPS, CHUNK, CHUNK)), full((CHUNK, GM_GROUPS)), full((D, D)),
            full((1, D)), full((2 * N_EXPERTS, D)), full((N_EXPERTS, 1)),
        ],
        out_specs=_prep_out_specs(),
        out_shape=_prep_out_shapes(),
        scratch_shapes=[pltpu.VMEM((TM, D), BF), pltpu.VMEM((NB_PAD, 128), F32)]
        + PENDING_SCRATCH,
        compiler_params=_cp(1),
        name="odd",
    )(*_pending_args(pending), mod, g_mix, win, v_g, w_s, b_st, w_out, g_ffn, rwt, rb)


def _routing_plan(bucket, brank, cnt):
    counts = cnt[:NB, 0].astype(jnp.int32)
    padded = (counts + (TMM - 1)) // TMM * TMM
    ends = jnp.cumsum(padded)
    base = ends - padded
    b = bucket.reshape(T)
    hit = b[:, None] == jnp.arange(NB, dtype=jnp.int32)[None, :]
    slot = brank.reshape(T) + jnp.sum(jnp.where(hit, base[None, :], 0), axis=1)
    n_steps = ends[NB - 1] // TMM
    starts = jnp.minimum(jnp.arange(MAX_STEPS, dtype=jnp.int32) * TMM, ends[NB - 1] - TMM)
    bucket_of_step = jnp.sum((starts[:, None] >= ends[None, :NB - 1]).astype(jnp.int32), axis=1)
    pair_of_step = bucket_of_step % len(PAIRS)
    pairs = jnp.asarray(PAIRS, dtype=jnp.int32)
    sched = jnp.concatenate([bucket_of_step // len(PAIRS), pairs[pair_of_step, 0], pairs[pair_of_step, 1],
                             n_steps[None]]).astype(jnp.int32)
    fill_plan = jnp.concatenate([base + counts, ends, n_steps[None]]).astype(jnp.int32)
    return slot.astype(jnp.int32), sched, fill_plan


def _row_tile(ref, row, sub):
    return ref.at[pl.ds(pl.multiple_of(row * sub, sub), sub), :]


def _row_copies(n_rows, make_copy, per_row=None):
    def body(i, carry):
        for j in range(ISSUE_UNROLL):
            r = i * ISSUE_UNROLL + j
            make_copy(r).start(priority=j % 2)
            if per_row is not None:
                per_row(r)
        return carry
    lax.fori_loop(0, n_rows // ISSUE_UNROLL, body, 0)


def _dispatch_kernel(pad_ref, slot_ref, wa_ref, h_hbm, o_hbm, ws_ref, zero_scr, zvec_scr, stage,
                     zsem, in_sem, row_sem):
    t = pl.program_id(0)
    nt = T // TM
    cur = t % 2

    def load(tile, b):
        return pltpu.make_async_copy(
            h_hbm.at[pl.ds(pl.multiple_of(tile * (TM * X_SUB), TM * X_SUB), TM * X_SUB), :],
            stage.at[b], in_sem.at[b])

    def rows_done(b):
        pltpu.make_async_copy(stage.at[b], o_hbm.at[pl.ds(0, TM * X_SUB), :], row_sem.at[b]).wait()

    @pl.when(t == 0)
    def _():
        load(0, 0).start()

    @pl.when(t == 0)
    def _():
        zvec_scr[...] = jnp.zeros_like(zvec_scr)
        clear = pltpu.make_async_copy(zvec_scr, ws_ref, zsem.at[0])
        clear.start()
        clear.wait()

    @pl.when(t == 0)
    def _():
        zero_scr[...] = jnp.zeros_like(zero_scr)

        def chunk_fill(row):
            return pltpu.make_async_copy(
                zero_scr.at[pl.ds(0, FILL * X_SUB), :],
                o_hbm.at[pl.ds(pl.multiple_of(row * X_SUB, X_SUB), FILL * X_SUB), :], zsem.at[0])

        def tile_fill(s):
            return pltpu.make_async_copy(
                zero_scr, o_hbm.at[pl.ds(s * TMM * X_SUB, TMM * X_SUB), :], zsem.at[1])

        for phase in ("start", "wait"):
            for k in range(NB):
                for c in range(TMM // FILL):
                    top = pad_ref[NB + k] - c * FILL

                    @pl.when(top > pad_ref[k])
                    def _():
                        getattr(chunk_fill(top - FILL), phase)()
        for phase in ("start", "wait"):
            for s in range(T // TMM, SORTED_TILES):
                @pl.when(s >= pad_ref[2 * NB])
                def _():
                    getattr(tile_fill(s), phase)()

    def put_weight(r):
        ws_ref[slot_ref[r]] = wa_ref[r]

    @pl.when(t > 0)
    def _():
        rows_done(1 - cur)

    @pl.when(t + 1 < nt)
    def _():
        load(t + 1, 1 - cur).start()

    load(t, cur).wait()
    _row_copies(TM, lambda r: pltpu.make_async_copy(
        _row_tile(stage.at[cur], r, X_SUB), _row_tile(o_hbm, slot_ref[r], X_SUB), row_sem.at[cur]),
        per_row=put_weight)

    @pl.when(t == nt - 1)
    def _():
        rows_done(cur)


def _dispatch(pad_start, slot, wa, h2c):
    nt = T // TM
    return pl.pallas_call(
        _dispatch_kernel,
        grid_spec=pltpu.PrefetchScalarGridSpec(
            num_scalar_prefetch=1,
            grid=(nt,),
            in_specs=[
                pl.BlockSpec((TM,), lambda t, pad: (t,), memory_space=pltpu.SMEM),
                pl.BlockSpec((TM,), lambda t, pad: (t,), memory_space=pltpu.SMEM),
                pl.BlockSpec(memory_space=pl.ANY),
            ],
            out_specs=[pl.BlockSpec(memory_space=pl.ANY), pl.BlockSpec(memory_space=pltpu.SMEM)],
            scratch_shapes=[pltpu.VMEM((TMM * X_SUB, LANES), F32), pltpu.VMEM((SORTED_ROWS,), F32),
                            pltpu.VMEM((2, TM * X_SUB, LANES), F32),
                            pltpu.SemaphoreType.DMA((2,)), pltpu.SemaphoreType.DMA((2,)),
                            pltpu.SemaphoreType.DMA((2,))],
        ),
        out_shape=[jax.ShapeDtypeStruct((SORTED_ROWS * X_SUB, LANES), F32),
                   jax.ShapeDtypeStruct((SORTED_ROWS,), F32)],
        compiler_params=_cp(1),
        name="dispatch",
    )(pad_start, slot, wa, h2c)


def _experts_kernel(sched_ref, xs_ref, ws_ref, w1_ref, w3_ref, w2_ref, o_ref, w1b, w3b, w2b, wt_scr):
    s = pl.program_id(0)

    @pl.when(jnp.logical_or(s == 0, sched_ref[s] != sched_ref[jnp.maximum(s - 1, 0)]))
    def _():
        w1b[...] = w1_ref[...].astype(BF)
        w3b[...] = w3_ref[...].astype(BF)
        w2b[...] = w2_ref[...].astype(BF)

    n_steps = sched_ref[3 * MAX_STEPS]

    @pl.when(s < n_steps)
    def _():
        h = jnp.concatenate([xs_ref[pl.ds(k, TMM, stride=X_SUB), :] for k in range(X_SUB)],
                            axis=1).astype(BF)
        wt_scr[...] = jnp.zeros_like(wt_scr)
        wt_scr[0:1, :] = ws_ref[...]
        w_first = wt_scr[...].T[:, 0:1]
        weights = (w_first, 1.0 - w_first)
        acc = None
        for j in range(2):
            e = sched_ref[(1 + j) * MAX_STEPS + s]
            a = _dot(h, w1b[e])
            b = _dot(h, w3b[e])
            hid = (a * (1.0 / (1.0 + jnp.exp(-a))) * b * weights[j]).astype(BF)
            y = _dot(hid, w2b[e])
            acc = y if acc is None else acc + y
        for k in range(D // LANES):
            o_ref[pl.ds(k, TMM, stride=Y_SUB), :] = acc[:, k * LANES:(k + 1) * LANES]

    @pl.when(s >= n_steps)
    def _():
        o_ref[...] = jnp.zeros_like(o_ref)


def _experts(sched, xs, ws, l, w1, w3, w2):
    row_blk = lambda s, sched: (jnp.maximum(jnp.minimum(s, sched[3 * MAX_STEPS] - 1), 0), 0)
    grp_blk = lambda s, sched: (l, sched[s], 0, 0)
    return pl.pallas_call(
        _experts_kernel,
        grid_spec=pltpu.PrefetchScalarGridSpec(
            num_scalar_prefetch=1,
            grid=(MAX_STEPS,),
            in_specs=[
                pl.BlockSpec((TMM * X_SUB, LANES), row_blk),
                pl.BlockSpec((None, 1, TMM), lambda s, sched: row_blk(s, sched) + (0,)),
                pl.BlockSpec((None, EPG, D, D_EXPERT), grp_blk),
                pl.BlockSpec((None, EPG, D, D_EXPERT), grp_blk),
                pl.BlockSpec((None, EPG, D_EXPERT, D), grp_blk),
            ],
            out_specs=pl.BlockSpec((TMM * Y_SUB, LANES), lambda s, sched: (s, 0)),
            scratch_shapes=[pltpu.VMEM((EPG, D, D_EXPERT), BF), pltpu.VMEM((EPG, D, D_EXPERT), BF),
                            pltpu.VMEM((EPG, D_EXPERT, D), BF), pltpu.VMEM((LANES, TMM), F32)],
        ),
        out_shape=jax.ShapeDtypeStruct((MAX_STEPS * TMM * Y_SUB, LANES), F32),
        compiler_params=_cp(1),
        name="experts",
    )(sched, xs, ws, w1, w3, w2)


def _pending_specs(l):
    nt = T // TM
    return [
        pl.BlockSpec((TM,), lambda t: (t,), memory_space=pltpu.SMEM),
        pl.BlockSpec((TM,), lambda t: (jnp.minimum(t + 1, nt - 1),), memory_space=pltpu.SMEM),
        pl.BlockSpec((TM,), lambda t: (jnp.minimum(t + 2, nt - 1),), memory_space=pltpu.SMEM),
        pl.BlockSpec(memory_space=pl.ANY),
        pl.BlockSpec((TM, D), lambda t: (t, 0)),
        pl.BlockSpec((1, 1, 6 * D), lambda t: (l * MOD_ROWS + _mod_row(t, TM), 0, 0)),
    ]


def _pending_args(pending):
    return (pending["slot"],) * 3 + (pending["ys"], pending["x"], pending["mod"])


GATHER_BUFS = 3
PENDING_SCRATCH = [pltpu.VMEM((GATHER_BUFS, TM * Y_SUB, LANES), F32), pltpu.SemaphoreType.DMA((GATHER_BUFS,))]
N_PENDING = 6


def _apply_pending(slot_ref, slot1_ref, slot2_ref, ys_hbm, x_ref, mod_ref, buf, sem):
    t = pl.program_id(0)
    nt = T // TM
    cur = t % GATHER_BUFS

    def gather(idx_ref, b):
        _row_copies(TM, lambda r: pltpu.make_async_copy(
            _row_tile(ys_hbm, idx_ref[r], Y_SUB), _row_tile(buf.at[b], r, Y_SUB), sem.at[b]))

    @pl.when(t == 0)
    def _():
        gather(slot_ref, 0)
        if nt > 1:
            gather(slot1_ref, 1)

    @pl.when(t + 2 < nt)
    def _():
        gather(slot2_ref, (t + 2) % GATHER_BUFS)

    pltpu.make_async_copy(ys_hbm.at[pl.ds(0, TM * Y_SUB), :], buf.at[cur], sem.at[cur]).wait()
    gate = mod_ref[0, :, 5 * D:6 * D]
    y = jnp.concatenate([buf[cur, pl.ds(k, TM, stride=Y_SUB), :] for k in range(Y_SUB)], axis=1)
    return x_ref[...] + gate * y


def _final_kernel(slot_ref, slot1_ref, slot2_ref, ys_hbm, x_ref, mod_ref, gfin_ref, op_ref, os_ref,
                  buf, sem):
    t = pl.program_id(0)
    x_new = _apply_pending(slot_ref, slot1_ref, slot2_ref, ys_hbm, x_ref, mod_ref, buf, sem)
    y_out = _rms(x_new, gfin_ref[...])
    npt = NP_TOK // TM

    @pl.when(t < npt)
    def _():
        op_ref[...] = y_out

    @pl.when(t >= npt)
    def _():
        os_ref[...] = y_out


def _final(pending, g_final):
    nt = T // TM
    npt = NP_TOK // TM
    return pl.pallas_call(
        _final_kernel,
        grid=(nt,),
        in_specs=_pending_specs(pending["layer"]) + [pl.BlockSpec((1, D), lambda t: (0, 0))],
        out_specs=[pl.BlockSpec((TM, D), lambda t: (jnp.minimum(t, npt - 1), 0)),
                   pl.BlockSpec((TM, D), lambda t: (jnp.maximum(t - npt, 0), 0))],
        out_shape=[jax.ShapeDtypeStruct((NP_TOK, D), F32), jax.ShapeDtypeStruct((NS_TOK, D), F32)],
        scratch_shapes=PENDING_SCRATCH,
        compiler_params=_cp(1),
        name="final",
    )(*_pending_args(pending), g_final)


def _moe(x, mod, l, rows, bucket, brank, wa, cnt, w1, w3, w2):
    slot, sched, fill_plan = _routing_plan(bucket, brank, cnt)
    xs, ws = _dispatch(fill_plan, slot, wa.reshape(T), rows)
    ys = _experts(sched, xs, ws.reshape(SORTED_TILES, 1, TMM), l, w1, w3, w2)
    return {"slot": slot, "ys": ys, "x": x, "mod": mod, "layer": l}


def _rope_tables():
    pos = jnp.arange(DEC_SEQ)
    r = (pos // GRID_W).astype(F32)
    col = (pos % GRID_W).astype(F32)
    inv = ROPE_BASE ** (-jnp.arange(AX_FREQS, dtype=F32) / AX_FREQS)
    ang = jnp.stack([r[:, None] * inv, col[:, None] * inv], axis=1)
    cos = jnp.cos(ang)[:, :, None, :]
    sin = jnp.sin(ang)[:, :, None, :]
    c32 = jnp.broadcast_to(cos, (DEC_SEQ, 2, 2, AX_FREQS)).reshape(DEC_SEQ, ROPE)
    s32 = jnp.concatenate([-sin, sin], axis=2).reshape(DEC_SEQ, ROPE)
    pad = HEAD_PAD - NOPE - ROPE
    c = jnp.concatenate([jnp.ones((DEC_SEQ, NOPE), F32), c32, jnp.zeros((DEC_SEQ, pad), F32)], axis=1)
    s = jnp.concatenate([jnp.zeros((DEC_SEQ, NOPE), F32), s32, jnp.zeros((DEC_SEQ, pad), F32)], axis=1)
    c_id = jnp.concatenate([jnp.ones((TM, NOPE + ROPE), F32), jnp.zeros((TM, pad), F32)], axis=1)
    s_id = jnp.zeros((TM, HEAD_PAD), F32)
    return jnp.concatenate([c, c_id], axis=0), jnp.concatenate([s, s_id], axis=0)


def _swap_halves(w):
    lead = w.shape[:-1]
    return w.reshape(lead + (2, 2, AX_FREQS))[..., ::-1, :].reshape(lead + (ROPE,))


def _even_weights(w_in, w_q_up, w_kv_up):
    k_in = w_in.shape[0]
    base = 3 * D_CONV + Q_RANK + KV_RANK
    w_kr = w_in[:, base:base + ROPE]
    z = lambda n: jnp.zeros((k_in, n), F32)
    win_ext = jnp.concatenate([
        w_in[:, :base],
        w_kr, z(HEAD_PAD - ROPE),
        z(NOPE), w_kr, z(HEAD_PAD - NOPE - ROPE),
        z(NOPE), _swap_halves(w_kr), z(HEAD_PAD - NOPE - ROPE),
    ], axis=1).astype(BF)
    wq = w_q_up.reshape(Q_RANK, HEADS, QK_DIM)
    zq = lambda n: jnp.zeros((Q_RANK, HEADS, n), F32)
    wq1 = jnp.concatenate([wq, zq(HEAD_PAD - QK_DIM)], axis=2).reshape(Q_RANK, HEADS * HEAD_PAD).astype(BF)
    wq2 = jnp.concatenate([zq(NOPE), _swap_halves(wq[:, :, NOPE:]), zq(HEAD_PAD - QK_DIM)],
                          axis=2).reshape(Q_RANK, HEADS * HEAD_PAD).astype(BF)
    wkv = w_kv_up.reshape(KV_RANK, HEADS, NOPE + V_DIM)
    wk = jnp.concatenate([wkv[:, :, :NOPE], jnp.zeros((KV_RANK, HEADS, HEAD_PAD - NOPE), F32)],
                         axis=2).reshape(KV_RANK, HEADS * HEAD_PAD).astype(BF)
    wv = wkv[:, :, NOPE:].reshape(KV_RANK, HEADS * V_DIM).astype(BF)
    return win_ext, wq1, wq2, wk, wv


def kernel(x_prompt, x_sample, cache_ckv, cache_krope, c, c_ctx, w_ada, b_ada, g_mix, g_ffn, g_final,
           ev_w_in, conv_w, q_norm_g, w_q_up, kv_norm_g, w_kv_up, ev_w_out,
           gm_w_in, gm_v_g, gm_w_s, gm_b_s, gm_w_out, router_w, router_b, moe_w1, moe_w3, moe_w2):
    n_even = ev_w_in.shape[0]
    x_parts = (x_prompt.reshape(NP_TOK, D), x_sample.reshape(NS_TOK, D))
    cc =jnp.concatenate([c, c_ctx[None, :], jnp.zeros((MOD_ROWS - DEC_BATCH - 1, D), F32)], axis=0)
    mod = _ada(cc, w_ada, b_ada).reshape(DEPTH * MOD_ROWS, 1, 6 * D)

    rope_c, rope_s = _rope_tables()
    rw_hi = router_w.T.astype(BF)
    rw_lo = (router_w.T - rw_hi.astype(F32)).astype(BF)
    rwt = jnp.concatenate([rw_hi, rw_lo], axis=0)
    rb = router_b.reshape(N_EXPERTS, 1)
    ev = [_even_weights(ev_w_in[i], w_q_up[i], w_kv_up[i]) for i in range(n_even)]
    kr_ctx = jnp.pad(cache_krope, ((0, 0), (0, 0), (0, 0), (NOPE, HEAD_PAD - NOPE - ROPE)))
    kr_ctx = kr_ctx.transpose(1, 0, 2, 3).reshape(n_even, DEC_BATCH * PAST, HEAD_PAD)
    ckv_ctx = cache_ckv.transpose(1, 0, 2, 3).reshape(n_even, DEC_BATCH * PAST, KV_RANK)
    kc, vc = _ctx_kv(ckv_ctx, kr_ctx, jnp.stack([e[3] for e in ev]), jnp.stack([e[4] for e in ev]))

    ckv_states, kr_states = [], []
    pending = None
    for l in range(DEPTH):
        i = l // 2
        if l % 2 == 0:
            win_ext, wq1, wq2, wk, wv = ev[i]
            outs = _even_in(
                None if pending else x_parts, pending, mod, l, g_mix[l][None, :], win_ext,
                q_norm_g[i][None, :], wq1, wq2, kv_norm_g[i][None, :], wk, wv, rope_c, rope_s)
            bg, cv, q, k, v, ckv, kr = outs[:7]
            if pending:
                x_parts = (outs[7],)
            attn_parts = (_attn_prompt(q, k, v), _attn_sample(q, k, v, kc, vc, i))
            x, *routed = _even_out(x_parts, mod, l, bg, cv, attn_parts, conv_w[i],
                                                 ev_w_out[i].astype(BF), g_ffn[l][None, :], rwt, rb)
            ckv_states.append(ckv[:NP_TOK].reshape(BATCH, SEQ, KV_RANK))
            kr_states.append(kr[:NP_TOK, :ROPE].reshape(BATCH, SEQ, ROPE))
        else:
            x, *routed = _odd(pending, mod, l, g_mix[l][None, :], gm_w_in[i].astype(BF),
                                            gm_v_g[i][None, :], gm_w_s[i].astype(BF), gm_b_s[i].T,
                                            gm_w_out[i].astype(BF), g_ffn[l][None, :], rwt, rb)
        pending = _moe(x, mod, l, *routed, moe_w1, moe_w3, moe_w2)

    y_prompt, y_sample = _final(pending, g_final[None, :])
    y_prompt = y_prompt.reshape(BATCH, SEQ, D)
    y_sample = y_sample.reshape(DEC_BATCH, DEC_SEQ, D)
    return (y_prompt, y_sample, jnp.stack(ckv_states, axis=1), jnp.stack(kr_states, axis=1))
```

```python
import functools
import math

import jax
import jax.numpy as jnp
from jax import lax
from jax.experimental import pallas as pl
from jax.experimental.pallas import tpu as pltpu

D = 1024
BATCH, SEQ = 32, 256
DEC_BATCH, DEC_SEQ = 8, 2048
PAST = 256
DEPTH = 4
GRID_W = 64
D_CONV = 512
HEADS = 8
NOPE, ROPE, V_DIM = 64, 32, 64
QK_DIM = NOPE + ROPE
Q_RANK, KV_RANK = 384, 256
AX_FREQS = ROPE // 4
ROPE_BASE = 10000.0
CHUNK = 128
GM_GROUPS = 8
N_EXPERTS, N_GROUPS, EPG = 16, 4, 4
D_EXPERT = 256
EPS = 1e-6

NP_TOK = BATCH * SEQ
NS_TOK = DEC_BATCH * DEC_SEQ
T = NP_TOK + NS_TOK
MOD_ROWS = 16
HEAD_PAD = 128
HALO = 16

TM = 512
PAIRS = ((0, 1), (0, 2), (0, 3), (1, 2), (1, 3), (2, 3))
NB = N_GROUPS * len(PAIRS)
NB_PAD = 32
TMM = 512
FILL = 128
MAX_STEPS = T // TMM + NB
SORTED_TILES = MAX_STEPS + MAX_STEPS % 2
SORTED_ROWS = SORTED_TILES * TMM
LANES = 128
Y_SUB = D // LANES
X_SUB = Y_SUB
ISSUE_UNROLL = 32
TQ = 512
ATTN_PAIRS = 4
VMEM_LIMIT = 56 * 1024 * 1024

BF = jnp.bfloat16
F32 = jnp.float32


def _cp(n_axes):
    return pltpu.CompilerParams(dimension_semantics=("arbitrary",) * n_axes,
                                vmem_limit_bytes=VMEM_LIMIT)


def _mod_row(t, tm):
    n_prompt_tiles = NP_TOK // tm
    per_seq = DEC_SEQ // tm
    return jnp.where(t < n_prompt_tiles, DEC_BATCH, (t - n_prompt_tiles) // per_seq)


def _rms(x, g):
    return x * lax.rsqrt(jnp.mean(x * x, axis=-1, keepdims=True) + EPS) * g


def _norm_mod(x, g, shift, scale):
    return x * lax.rsqrt(jnp.mean(x * x, axis=-1, keepdims=True) + EPS) * (g * (1.0 + scale)) + shift


def _dot(a, b):
    return jnp.dot(a, b, preferred_element_type=F32)


def _dot_nt(a, b, precision=None):
    return lax.dot_general(a, b, (((1,), (1,)), ((), ())), precision=precision,
                           preferred_element_type=F32)


ADA_TN = 1536


def _ada_kernel(cc_ref, w_ref, b_ref, o_ref):
    cc = cc_ref[...]
    s = (cc / (1.0 + jnp.exp(-cc))).astype(BF)
    o_ref[0] = _dot(s, w_ref[0].astype(BF)) + b_ref[0]


def _ada(cc, w_ada, b_ada):
    n = 6 * D
    return pl.pallas_call(
        _ada_kernel,
        grid=(DEPTH, n // ADA_TN),
        in_specs=[
            pl.BlockSpec((MOD_ROWS, D), lambda l, j: (0, 0)),
            pl.BlockSpec((1, D, ADA_TN), lambda l, j: (l, 0, j)),
            pl.BlockSpec((1, 1, ADA_TN), lambda l, j: (l, 0, j)),
        ],
        out_specs=pl.BlockSpec((1, MOD_ROWS, ADA_TN), lambda l, j: (l, 0, j)),
        out_shape=jax.ShapeDtypeStruct((DEPTH, MOD_ROWS, n), F32),
        compiler_params=_cp(2),
        name="ada",
    )(cc, w_ada, b_ada.reshape(DEPTH, 1, n))


def _route(logits_t, rb):
    sc = 1.0 / (1.0 + jnp.exp(-logits_t))
    sel = sc + rb
    rows = [sel[e:e + 1, :] for e in range(N_EXPERTS)]
    srows = [sc[e:e + 1, :] for e in range(N_EXPERTS)]

    def top2sum(a, b, c, d):
        hi1, lo1 = jnp.maximum(a, b), jnp.minimum(a, b)
        hi2, lo2 = jnp.maximum(c, d), jnp.minimum(c, d)
        return jnp.maximum(hi1, hi2) + jnp.maximum(jnp.minimum(hi1, hi2), jnp.maximum(lo1, lo2))

    gs = [top2sum(*rows[EPG * g:EPG * (g + 1)]) for g in range(N_GROUPS)]
    best = gs[0]
    gidx = jnp.zeros_like(best, dtype=jnp.int32)
    for g in range(1, N_GROUPS):
        upd = gs[g] > best
        best = jnp.where(upd, gs[g], best)
        gidx = jnp.where(upd, g, gidx)

    picked = []
    for g in range(N_GROUPS):
        grp = rows[EPG * g:EPG * (g + 1)]
        in_g = gidx == g
        for j in range(EPG):
            rank = jnp.zeros_like(gidx)
            for k in range(EPG):
                if k == j:
                    continue
                ahead = grp[k] > grp[j]
                if k < j:
                    ahead = ahead | (grp[k] == grp[j])
                rank = rank + ahead.astype(jnp.int32)
            picked.append(in_g & (rank < 2))
    w = [jnp.where(picked[e], srows[e], 0.0) for e in range(N_EXPERTS)]
    wsum = w[0]
    for e in range(1, N_EXPERTS):
        wsum = wsum + w[e]
    inv = 1.0 / wsum
    pj, wj = [], []
    for j in range(EPG):
        hit, acc = picked[j], w[j]
        for g in range(1, N_GROUPS):
            hit = hit | picked[EPG * g + j]
            acc = acc + w[EPG * g + j]
        pj.append(hit)
        wj.append(acc * inv)
    pair = jnp.full_like(gidx, len(PAIRS) - 1)
    for idx in range(len(PAIRS) - 2, -1, -1):
        a, b = PAIRS[idx]
        pair = jnp.where(pj[a] & pj[b], idx, pair)
    w_a = jnp.where(pj[0], wj[0], jnp.where(pj[1], wj[1], wj[2]))
    w_b = jnp.where(pj[3], wj[3], jnp.where(pj[2], wj[2], wj[1]))
    return (w_a, w_b), gidx * len(PAIRS) + pair


def _ffn_prep(x_new, mod_ref, gf_ref, rwt_ref, rb_ref, h2c_ref, gidx_ref, grank_ref, wa_ref, cnt_ref,
              run_scr):
    t = pl.program_id(0)
    tm = x_new.shape[0]
    shift = mod_ref[0, :, 3 * D:4 * D]
    scale = mod_ref[0, :, 4 * D:5 * D]
    h2 = _norm_mod(x_new, gf_ref[...], shift, scale)
    h_hi = h2.astype(BF)
    h_lo = (h2 - h_hi.astype(F32)).astype(BF)
    by_hi = _dot_nt(rwt_ref[...], h_hi)
    by_lo = _dot_nt(rwt_ref[0:N_EXPERTS, :], h_lo)
    logits_t = by_hi[0:N_EXPERTS, :] + by_hi[N_EXPERTS:2 * N_EXPERTS, :] + by_lo
    wab, gidx = _route(logits_t, rb_ref[...])
    wa_ref[0] = wab[0]

    for k in range(X_SUB):
        h2c_ref[pl.ds(k, tm, stride=X_SUB), :] = h2[:, k * LANES:(k + 1) * LANES]

    @pl.when(t == 0)
    def _():
        run_scr[...] = jnp.zeros_like(run_scr)

    onehot = (lax.broadcasted_iota(jnp.int32, (NB_PAD, tm), 0) == gidx).astype(F32)
    earlier = (lax.broadcasted_iota(jnp.int32, (tm, tm), 0)
               < lax.broadcasted_iota(jnp.int32, (tm, tm), 1)).astype(BF)
    rank = _dot(onehot.astype(BF), earlier)
    run = run_scr[:, 0:1]
    grank = jnp.sum(onehot * (rank + run), axis=0, keepdims=True)
    gidx_ref[0] = gidx
    grank_ref[0] = grank.astype(jnp.int32)
    run_scr[...] = run_scr[...] + jnp.sum(onehot, axis=1, keepdims=True)
    cnt_ref[...] = run_scr[...]


def _prep_out_specs():
    tok = lambda w: pl.BlockSpec((TM, w), lambda t: (t, 0))
    lanes = pl.BlockSpec((1, 1, TM), lambda t: (t, 0, 0))
    rows = pl.BlockSpec((TM * X_SUB, LANES), lambda t: (t, 0))
    return [tok(D), rows, lanes, lanes, lanes, pl.BlockSpec((NB_PAD, 128), lambda t: (0, 0))]


def _prep_out_shapes():
    nt = T // TM
    return [jax.ShapeDtypeStruct((T, D), F32), jax.ShapeDtypeStruct((T * X_SUB, LANES), F32),
            jax.ShapeDtypeStruct((nt, 1, TM), jnp.int32), jax.ShapeDtypeStruct((nt, 1, TM), jnp.int32),
            jax.ShapeDtypeStruct((nt, 1, TM), F32), jax.ShapeDtypeStruct((NB_PAD, 128), F32)]


EV_EXT = 3 * D_CONV + Q_RANK + KV_RANK + HEAD_PAD


def _tok_specs(parts, width):
    npt = NP_TOK // TM
    if len(parts) == 1:
        return [pl.BlockSpec((TM, width), lambda t: (t, 0))]
    return [pl.BlockSpec((TM, width), lambda t: (jnp.minimum(t, npt - 1), 0)),
            pl.BlockSpec((TM, width), lambda t: (jnp.maximum(t - npt, 0), 0))]


def _tok_load(refs):
    if len(refs) == 1:
        return refs[0][...]
    return jnp.where(pl.program_id(0) < NP_TOK // TM, refs[0][...], refs[1][...])


def _even_in_kernel(n_x, *refs):
    if n_x:
        x_refs, refs = refs[:n_x], refs[n_x:]
        x = _tok_load(x_refs)
    else:
        pend, refs, scratch = refs[:N_PENDING], refs[N_PENDING:-2], refs[-2:]
        refs, xo_ref = refs[:-1], refs[-1]
        x = _apply_pending(*pend, *scratch)
        xo_ref[...] = x
    (mod_ref, g_ref, win_ref, qg_ref, wq_ref, kg_ref, wk_ref, wv_ref, rc_ref, rk_ref, rlo_ref, rhi_ref,
     bg_ref, cv_ref, q_ref, k_ref, v_ref, ckv_ref, kr_ref) = refs
    shift = mod_ref[0, :, 0:D]
    scale = mod_ref[0, :, D:2 * D]
    h = _norm_mod(x, g_ref[...], shift, scale).astype(BF)
    proj = _dot(h, win_ref[...])
    o = 0
    b_g = proj[:, o:o + D_CONV]; o += D_CONV
    c_g = proj[:, o:o + D_CONV]; o += D_CONV
    v_in = proj[:, o:o + D_CONV]; o += D_CONV
    q_a = proj[:, o:o + Q_RANK]; o += Q_RANK
    kv_a = proj[:, o:o + KV_RANK]; o += KV_RANK
    kr_blk = proj[:, o:o + HEAD_PAD]

    bg_ref[...] = b_g.astype(BF)
    cv_ref[...] = (c_g * v_in).astype(BF)
    kr_ref[...] = kr_blk

    rc = rc_ref[...]
    rlo = rlo_ref[...]
    rhi = rhi_ref[...]

    def rotate(blk, cos_tbl):
        up = pltpu.roll(blk, HEAD_PAD - AX_FREQS, 1)
        down = pltpu.roll(blk, AX_FREQS, 1)
        return blk * cos_tbl + up * rlo + down * rhi

    qn = _rms(q_a, qg_ref[...]).astype(BF)
    q1 = _dot(qn, wq_ref[...])
    qscale = QK_DIM ** -0.5
    for hd in range(HEADS):
        sl = slice(hd * HEAD_PAD, (hd + 1) * HEAD_PAD)
        q_ref[:, sl] = (rotate(q1[:, sl], rc) * qscale).astype(BF)

    ckv = _rms(kv_a, kg_ref[...])
    ckv_ref[...] = ckv
    ckv_b = ckv.astype(BF)
    kk = _dot(ckv_b, wk_ref[...])
    kr = rotate(kr_blk, rk_ref[...])
    for hd in range(HEADS):
        sl = slice(hd * HEAD_PAD, (hd + 1) * HEAD_PAD)
        k_ref[:, sl] = (kk[:, sl] + kr).astype(BF)
    v_ref[...] = _dot(ckv_b, wv_ref[...]).astype(BF)


def _even_in(x_parts, pending, mod, l, g_mix, win, qg, wq, kg, wk, wv, rope):
    nt = T // TM
    npt = NP_TOK // TM
    per_seq = DEC_SEQ // TM
    ident_blk = DEC_SEQ // TM

    def rope_idx(t):
        return (jnp.where(t < npt, ident_blk, (t - npt) % per_seq), 0)

    full = lambda shape: pl.BlockSpec(shape, lambda t: (0,) * len(shape))
    tok = lambda w: pl.BlockSpec((TM, w), lambda t: (t, 0))
    out_specs = [tok(D_CONV), tok(D_CONV), tok(HEADS * HEAD_PAD), tok(HEADS * HEAD_PAD),
                 tok(HEADS * V_DIM), tok(KV_RANK), tok(HEAD_PAD)]
    out_shape = [
        jax.ShapeDtypeStruct((T, D_CONV), BF), jax.ShapeDtypeStruct((T, D_CONV), BF),
        jax.ShapeDtypeStruct((T, HEADS * HEAD_PAD), BF), jax.ShapeDtypeStruct((T, HEADS * HEAD_PAD), BF),
        jax.ShapeDtypeStruct((T, HEADS * V_DIM), BF),
        jax.ShapeDtypeStruct((T, KV_RANK), F32), jax.ShapeDtypeStruct((T, HEAD_PAD), F32),
    ]
    if pending is None:
        lead_specs, lead_args, scratch = _tok_specs(x_parts, D), tuple(x_parts), []
    else:
        lead_specs, lead_args, scratch = _pending_specs(pending["layer"]), _pending_args(pending), PENDING_SCRATCH
        out_specs.append(tok(D))
        out_shape.append(jax.ShapeDtypeStruct((T, D), F32))
    return pl.pallas_call(
        functools.partial(_even_in_kernel, 0 if pending is not None else len(x_parts)),
        grid=(nt,),
        in_specs=lead_specs + [
            pl.BlockSpec((1, 1, 6 * D), lambda t: (l * MOD_ROWS + _mod_row(t, TM), 0, 0)),
            full((1, D)), full((D, EV_EXT)), full((1, Q_RANK)),
            full((Q_RANK, HEADS * HEAD_PAD)),
            full((1, KV_RANK)), full((KV_RANK, HEADS * HEAD_PAD)), full((KV_RANK, HEADS * V_DIM)),
        ] + [pl.BlockSpec((TM, HEAD_PAD), rope_idx)] * len(rope),
        out_specs=out_specs,
        out_shape=out_shape,
        scratch_shapes=scratch,
        compiler_params=_cp(1),
        name="even_in",
    )(*lead_args, mod, g_mix, win, qg, wq, kg, wk, wv, *rope)


CTX_TM = 512


def _ctx_kv_kernel(ckv_ref, kr_ref, wk_ref, wv_ref, k_ref, v_ref):
    ckv_b = ckv_ref[0].astype(BF)
    kk = _dot(ckv_b, wk_ref[0])
    kr = kr_ref[0]
    for hd in range(HEADS):
        sl = slice(hd * HEAD_PAD, (hd + 1) * HEAD_PAD)
        k_ref[0, :, sl] = (kk[:, sl] + kr).astype(BF)
    v_ref[0] = _dot(ckv_b, wv_ref[0]).astype(BF)


def _ctx_kv(ckv_all, kr_all, wk_all, wv_all):
    n_even = ckv_all.shape[0]
    rows = DEC_BATCH * PAST
    return pl.pallas_call(
        _ctx_kv_kernel,
        grid=(n_even, rows // CTX_TM),
        in_specs=[
            pl.BlockSpec((1, CTX_TM, KV_RANK), lambda i, t: (i, t, 0)),
            pl.BlockSpec((1, CTX_TM, HEAD_PAD), lambda i, t: (i, t, 0)),
            pl.BlockSpec((1, KV_RANK, HEADS * HEAD_PAD), lambda i, t: (i, 0, 0)),
            pl.BlockSpec((1, KV_RANK, HEADS * V_DIM), lambda i, t: (i, 0, 0)),
        ],
        out_specs=[
            pl.BlockSpec((1, CTX_TM, HEADS * HEAD_PAD), lambda i, t: (i, t, 0)),
            pl.BlockSpec((1, CTX_TM, HEADS * V_DIM), lambda i, t: (i, t, 0)),
        ],
        out_shape=[jax.ShapeDtypeStruct((n_even, rows, HEADS * HEAD_PAD), BF),
                   jax.ShapeDtypeStruct((n_even, rows, HEADS * V_DIM), BF)],
        compiler_params=_cp(2),
        name="ctx_kv",
    )(ckv_all, kr_all, wk_all, wv_all)


def _attn_body(n_pairs, has_ctx, q_ref, k_ref, v_ref, *rest):
    if has_ctx:
        kc_ref, vc_ref, o_ref = rest
    else:
        (o_ref,) = rest
    lane = lax.broadcasted_iota(jnp.int32, (1, 2 * V_DIM), 1)
    w = 2 * V_DIM
    ones = jnp.ones((k_ref.shape[0], w), BF)
    ones_ctx = jnp.ones((PAST, w), BF)
    for pr in range(n_pairs):
        vsl = slice(pr * w, (pr + 1) * w)
        v = jnp.concatenate([v_ref[:, vsl], ones], axis=1)
        if has_ctx:
            vc = jnp.concatenate([vc_ref[0, :, vsl], ones_ctx], axis=1)
        outs = []
        for sub in range(2):
            hsl = slice((2 * pr + sub) * HEAD_PAD, (2 * pr + sub + 1) * HEAD_PAD)
            q = q_ref[:, hsl]
            s1 = _dot_nt(q, k_ref[:, hsl])
            m = jnp.max(s1, axis=-1, keepdims=True)
            if has_ctx:
                s2 = _dot_nt(q, kc_ref[0, :, hsl])
                m = jnp.maximum(m, jnp.max(s2, axis=-1, keepdims=True))
            acc = _dot(jnp.exp(s1 - m).astype(BF), v)
            if has_ctx:
                acc = acc + _dot(jnp.exp(s2 - m).astype(BF), vc)
            outs.append(acc[:, 0:w] * (1.0 / acc[:, w:2 * w]))
        o_ref[:, vsl] = jnp.where(lane < V_DIM, outs[0], outs[1]).astype(BF)


def _attn_prompt(q, k, v):
    return pl.pallas_call(
        functools.partial(_attn_body, HEADS // 2, False),
        grid=(BATCH,),
        in_specs=[
            pl.BlockSpec((SEQ, HEADS * HEAD_PAD), lambda b: (b, 0)),
            pl.BlockSpec((SEQ, HEADS * HEAD_PAD), lambda b: (b, 0)),
            pl.BlockSpec((SEQ, HEADS * V_DIM), lambda b: (b, 0)),
        ],
        out_specs=pl.BlockSpec((SEQ, HEADS * V_DIM), lambda b: (b, 0)),
        out_shape=jax.ShapeDtypeStruct((NP_TOK, HEADS * V_DIM), BF),
        compiler_params=_cp(1),
        name="attn_prompt",
    )(q, k, v)


def _attn_sample(q, k, v, kc, vc, i):
    s_blk0 = NP_TOK // DEC_SEQ
    q_blk0 = NP_TOK // TQ
    nq = DEC_SEQ // TQ
    hw = ATTN_PAIRS * 2 * HEAD_PAD
    vw = ATTN_PAIRS * 2 * V_DIM
    return pl.pallas_call(
        functools.partial(_attn_body, ATTN_PAIRS, True),
        grid=(DEC_BATCH, HEADS // (2 * ATTN_PAIRS), nq),
        in_specs=[
            pl.BlockSpec((TQ, hw), lambda b, hp, j: (q_blk0 + b * nq + j, hp)),
            pl.BlockSpec((DEC_SEQ, hw), lambda b, hp, j: (s_blk0 + b, hp)),
            pl.BlockSpec((DEC_SEQ, vw), lambda b, hp, j: (s_blk0 + b, hp)),
            pl.BlockSpec((1, PAST, hw), lambda b, hp, j: (i, b, hp)),
            pl.BlockSpec((1, PAST, vw), lambda b, hp, j: (i, b, hp)),
        ],
        out_specs=pl.BlockSpec((TQ, vw), lambda b, hp, j: (b * nq + j, hp)),
        out_shape=jax.ShapeDtypeStruct((NS_TOK, HEADS * V_DIM), BF),
        compiler_params=_cp(3),
        name="attn_sample",
    )(q, k, v, kc, vc)


def _even_out_kernel(n_x, *refs):
    x_refs, refs = refs[:n_x], refs[n_x:]
    (mod_ref, bg_ref, cv_ref, cvp_ref, cvn_ref, atp_ref, ats_ref, cw_ref, wo_ref, gf_ref, rwt_ref, rb_ref,
     xo_ref, h2c_ref, gidx_ref, grank_ref, wa_ref, cnt_ref, run_scr) = refs
    t = pl.program_id(0)
    npt = NP_TOK // TM
    per_seq = DEC_SEQ // TM
    cv = cv_ref[...].astype(F32)
    r = lax.broadcasted_iota(jnp.int32, (TM, 1), 0)
    is_prompt = t < npt
    tile_in_seq = (t - npt) % per_seq
    first_row = jnp.where(is_prompt, 0, jnp.where(tile_in_seq == 0, 0, -1))
    last_row = jnp.where(is_prompt, SEQ - 1, jnp.where(tile_in_seq == per_seq - 1, TM - 1, -1))
    period_mask = jnp.where(is_prompt, SEQ - 1, TM - 1)
    first = (r & period_mask) == first_row
    last = (r & period_mask) == last_row
    prev_row = cvp_ref[HALO - 1:HALO, :].astype(F32)
    next_row = cvn_ref[0:1, :].astype(F32)
    prev = jnp.where(r == 0, prev_row, pltpu.roll(cv, 1, 0))
    prev = jnp.where(first, 0.0, prev)
    nxt = jnp.where(r == TM - 1, next_row, pltpu.roll(cv, TM - 1, 0))
    nxt = jnp.where(last, 0.0, nxt)
    cw = cw_ref[...]
    conv = prev * cw[0:1, :] + cv * cw[1:2, :] + nxt * cw[2:3, :]
    yc = (bg_ref[...].astype(F32) * conv).astype(BF)
    attn = _tok_load((atp_ref, ats_ref))
    out = _dot(yc, wo_ref[0:D_CONV, :]) + _dot(attn, wo_ref[D_CONV:2 * D_CONV, :])
    gate = mod_ref[0, :, 2 * D:3 * D]
    x_new = _tok_load(x_refs) + gate * out
    xo_ref[...] = x_new
    _ffn_prep(x_new, mod_ref, gf_ref, rwt_ref, rb_ref, h2c_ref, gidx_ref, grank_ref, wa_ref, cnt_ref,
              run_scr)


def _even_out(x_parts, mod, l, bg, cv, attn_parts, conv_w, w_out, g_ffn, rwt, rb):
    nt = T // TM
    hb = TM // HALO
    nhb = T // HALO
    full = lambda shape: pl.BlockSpec(shape, lambda t: (0,) * len(shape))
    tok = lambda w: pl.BlockSpec((TM, w), lambda t: (t, 0))
    return pl.pallas_call(
        functools.partial(_even_out_kernel, len(x_parts)),
        grid=(nt,),
        in_specs=_tok_specs(x_parts, D) + [
            pl.BlockSpec((1, 1, 6 * D), lambda t: (l * MOD_ROWS + _mod_row(t, TM), 0, 0)),
            tok(D_CONV), tok(D_CONV),
            pl.BlockSpec((HALO, D_CONV), lambda t: (jnp.maximum(t * hb - 1, 0), 0)),
            pl.BlockSpec((HALO, D_CONV), lambda t: (jnp.minimum((t + 1) * hb, nhb - 1), 0)),
        ] + _tok_specs(attn_parts, HEADS * V_DIM) + [
            full((3, D_CONV)), full((2 * D_CONV, D)), full((1, D)),
            full((2 * N_EXPERTS, D)), full((N_EXPERTS, 1)),
        ],
        out_specs=_prep_out_specs(),
        out_shape=_prep_out_shapes(),
        scratch_shapes=[pltpu.VMEM((NB_PAD, 128), F32)],
        compiler_params=_cp(1),
        name="even_out",
    )(*x_parts, mod, bg, cv, cv, cv, *attn_parts, conv_w, w_out, g_ffn, rwt, rb)


def _odd_kernel(*refs):
    pend, refs, scratch = refs[:N_PENDING], refs[N_PENDING:-2], refs[-2:]
    (mod_ref, g_ref, win_ref, vg_ref, ws_ref, bst_ref, wo_ref, gf_ref, rwt_ref, rb_ref,
     xo_ref, h2c_ref, gidx_ref, grank_ref, wa_ref, cnt_ref, gated_scr, run_scr) = refs
    x = _apply_pending(*pend, *scratch)
    shift = mod_ref[0, :, 0:D]
    scale = mod_ref[0, :, D:2 * D]
    h = _norm_mod(x, g_ref[...], shift, scale).astype(BF)
    zl = _dot(h, win_ref[...])
    z = 0.5 * zl * (1.0 + lax.erf(zl * math.sqrt(0.5)))
    u = z[:, 0:D]
    v = _rms(z[:, D:2 * D], vg_ref[...]).astype(BF)
    n_chunks = TM // CHUNK
    gch = D // GM_GROUPS
    for g in range(GM_GROUPS):
        csl = slice(g * gch, (g + 1) * gch)
        vg = jnp.concatenate([v[n * CHUNK:(n + 1) * CHUNK, csl] for n in range(n_chunks)], axis=1)
        sg = _dot(ws_ref[g], vg) + bst_ref[:, g:g + 1]
        for n in range(n_chunks):
            rsl = slice(n * CHUNK, (n + 1) * CHUNK)
            gated_scr[rsl, csl] = (u[rsl, csl] * sg[:, n * gch:(n + 1) * gch]).astype(BF)
    out = _dot(gated_scr[...], wo_ref[...])
    gate = mod_ref[0, :, 2 * D:3 * D]
    x_new = x + gate * out
    xo_ref[...] = x_new
    _ffn_prep(x_new, mod_ref, gf_ref, rwt_ref, rb_ref, h2c_ref, gidx_ref, grank_ref, wa_ref, cnt_ref,
              run_scr)


def _odd(pending, mod, l, g_mix, win, v_g, w_s, b_st, w_out, g_ffn, rwt, rb):
    nt = T // TM
    full = lambda shape: pl.BlockSpec(shape, lambda t: (0,) * len(shape))
    return pl.pallas_call(
        _odd_kernel,
        grid=(nt,),
        in_specs=_pending_specs(pending["layer"]) + [
            pl.BlockSpec((1, 1, 6 * D), lambda t: (l * MOD_ROWS + _mod_row(t, TM), 0, 0)),
            full((1, D)), full((D, 2 * D)), full((1, D)),
            full((GM_GROUPS, CHUNK, CHUNK)), full((CHUNK, GM_GROUPS)), full((D, D)),
            full((1, D)), full((2 * N_EXPERTS, D)), full((N_EXPERTS, 1)),
        ],
        out_specs=_prep_out_specs(),
        out_shape=_prep_out_shapes(),
        scratch_shapes=[pltpu.VMEM((TM, D), BF), pltpu.VMEM((NB_PAD, 128), F32)]
        + PENDING_SCRATCH,
        compiler_params=_cp(1),
        name="odd",
    )(*_pending_args(pending), mod, g_mix, win, v_g, w_s, b_st, w_out, g_ffn, rwt, rb)


def _routing_plan(bucket, brank, cnt):
    counts = cnt[:NB, 0].astype(jnp.int32)
    padded = (counts + (TMM - 1)) // TMM * TMM
    ends = jnp.cumsum(padded)
    base = ends - padded
    b = bucket.reshape(T)
    hit = b[:, None] == jnp.arange(NB, dtype=jnp.int32)[None, :]
    slot = brank.reshape(T) + jnp.sum(jnp.where(hit, base[None, :], 0), axis=1)
    n_steps = ends[NB - 1] // TMM
    starts = jnp.minimum(jnp.arange(MAX_STEPS, dtype=jnp.int32) * TMM, ends[NB - 1] - TMM)
    bucket_of_step = jnp.sum((starts[:, None] >= ends[None, :NB - 1]).astype(jnp.int32), axis=1)
    pair_of_step = bucket_of_step % len(PAIRS)
    pairs = jnp.asarray(PAIRS, dtype=jnp.int32)
    sched = jnp.concatenate([bucket_of_step // len(PAIRS), pairs[pair_of_step, 0], pairs[pair_of_step, 1],
                             n_steps[None]]).astype(jnp.int32)
    fill_plan = jnp.concatenate([base + counts, ends, n_steps[None]]).astype(jnp.int32)
    return slot.astype(jnp.int32), sched, fill_plan


def _row_tile(ref, row, sub):
    return ref.at[pl.ds(pl.multiple_of(row * sub, sub), sub), :]


def _row_copies(n_rows, make_copy, per_row=None):
    def body(i, carry):
        for j in range(ISSUE_UNROLL):
            r = i * ISSUE_UNROLL + j
            make_copy(r).start(priority=j % 2)
            if per_row is not None:
                per_row(r)
        return carry
    lax.fori_loop(0, n_rows // ISSUE_UNROLL, body, 0)


def _dispatch_kernel(pad_ref, slot_ref, wa_ref, h_hbm, o_hbm, ws_ref, zero_scr, zvec_scr, stage,
                     zsem, in_sem, row_sem):
    t = pl.program_id(0)
    nt = T // TM
    cur = t % 2

    def load(tile, b):
        return pltpu.make_async_copy(
            h_hbm.at[pl.ds(pl.multiple_of(tile * (TM * X_SUB), TM * X_SUB), TM * X_SUB), :],
            stage.at[b], in_sem.at[b])

    def rows_done(b):
        pltpu.make_async_copy(stage.at[b], o_hbm.at[pl.ds(0, TM * X_SUB), :], row_sem.at[b]).wait()

    @pl.when(t == 0)
    def _():
        load(0, 0).start()

    @pl.when(t == 0)
    def _():
        zvec_scr[...] = jnp.zeros_like(zvec_scr)
        clear = pltpu.make_async_copy(zvec_scr, ws_ref, zsem.at[0])
        clear.start()
        clear.wait()

    @pl.when(t == 0)
    def _():
        zero_scr[...] = jnp.zeros_like(zero_scr)

        def chunk_fill(row):
            return pltpu.make_async_copy(
                zero_scr.at[pl.ds(0, FILL * X_SUB), :],
                o_hbm.at[pl.ds(pl.multiple_of(row * X_SUB, X_SUB), FILL * X_SUB), :], zsem.at[0])

        def tile_fill(s):
            return pltpu.make_async_copy(
                zero_scr, o_hbm.at[pl.ds(s * TMM * X_SUB, TMM * X_SUB), :], zsem.at[1])

        for phase in ("start", "wait"):
            for k in range(NB):
                for c in range(TMM // FILL):
                    top = pad_ref[NB + k] - c * FILL

                    @pl.when(top > pad_ref[k])
                    def _():
                        getattr(chunk_fill(top - FILL), phase)()
        for phase in ("start", "wait"):
            for s in range(T // TMM, SORTED_TILES):
                @pl.when(s >= pad_ref[2 * NB])
                def _():
                    getattr(tile_fill(s), phase)()

    def put_weight(r):
        ws_ref[slot_ref[r]] = wa_ref[r]

    @pl.when(t > 0)
    def _():
        rows_done(1 - cur)

    @pl.when(t + 1 < nt)
    def _():
        load(t + 1, 1 - cur).start()

    load(t, cur).wait()
    _row_copies(TM, lambda r: pltpu.make_async_copy(
        _row_tile(stage.at[cur], r, X_SUB), _row_tile(o_hbm, slot_ref[r], X_SUB), row_sem.at[cur]),
        per_row=put_weight)

    @pl.when(t == nt - 1)
    def _():
        rows_done(cur)


def _dispatch(pad_start, slot, wa, h2c):
    nt = T // TM
    return pl.pallas_call(
        _dispatch_kernel,
        grid_spec=pltpu.PrefetchScalarGridSpec(
            num_scalar_prefetch=1,
            grid=(nt,),
            in_specs=[
                pl.BlockSpec((TM,), lambda t, pad: (t,), memory_space=pltpu.SMEM),
                pl.BlockSpec((TM,), lambda t, pad: (t,), memory_space=pltpu.SMEM),
                pl.BlockSpec(memory_space=pl.ANY),
            ],
            out_specs=[pl.BlockSpec(memory_space=pl.ANY), pl.BlockSpec(memory_space=pltpu.SMEM)],
            scratch_shapes=[pltpu.VMEM((TMM * X_SUB, LANES), F32), pltpu.VMEM((SORTED_ROWS,), F32),
                            pltpu.VMEM((2, TM * X_SUB, LANES), F32),
                            pltpu.SemaphoreType.DMA((2,)), pltpu.SemaphoreType.DMA((2,)),
                            pltpu.SemaphoreType.DMA((2,))],
        ),
        out_shape=[jax.ShapeDtypeStruct((SORTED_ROWS * X_SUB, LANES), F32),
                   jax.ShapeDtypeStruct((SORTED_ROWS,), F32)],
        compiler_params=_cp(1),
        name="dispatch",
    )(pad_start, slot, wa, h2c)


def _experts_kernel(sched_ref, xs_ref, ws_ref, w1_ref, w3_ref, w2_ref, o_ref, w1b, w3b, w2b, wt_scr):
    s = pl.program_id(0)

    @pl.when(jnp.logical_or(s == 0, sched_ref[s] != sched_ref[jnp.maximum(s - 1, 0)]))
    def _():
        w1b[...] = w1_ref[...].astype(BF)
        w3b[...] = w3_ref[...].astype(BF)
        w2b[...] = w2_ref[...].astype(BF)

    n_steps = sched_ref[3 * MAX_STEPS]

    @pl.when(s < n_steps)
    def _():
        h = jnp.concatenate([xs_ref[pl.ds(k, TMM, stride=X_SUB), :] for k in range(X_SUB)],
                            axis=1).astype(BF)
        wt_scr[...] = jnp.zeros_like(wt_scr)
        wt_scr[0:1, :] = ws_ref[...]
        w_first = wt_scr[...].T[:, 0:1]
        weights = (w_first, 1.0 - w_first)
        acc = None
        for j in range(2):
            e = sched_ref[(1 + j) * MAX_STEPS + s]
            a = _dot(h, w1b[e])
            b = _dot(h, w3b[e])
            hid = (a * (1.0 / (1.0 + jnp.exp(-a))) * b * weights[j]).astype(BF)
            y = _dot(hid, w2b[e])
            acc = y if acc is None else acc + y
        for k in range(D // LANES):
            o_ref[pl.ds(k, TMM, stride=Y_SUB), :] = acc[:, k * LANES:(k + 1) * LANES]

    @pl.when(s >= n_steps)
    def _():
        o_ref[...] = jnp.zeros_like(o_ref)


def _experts(sched, xs, ws, l, w1, w3, w2):
    row_blk = lambda s, sched: (jnp.maximum(jnp.minimum(s, sched[3 * MAX_STEPS] - 1), 0), 0)
    grp_blk = lambda s, sched: (l, sched[s], 0, 0)
    return pl.pallas_call(
        _experts_kernel,
        grid_spec=pltpu.PrefetchScalarGridSpec(
            num_scalar_prefetch=1,
            grid=(MAX_STEPS,),
            in_specs=[
                pl.BlockSpec((TMM * X_SUB, LANES), row_blk),
                pl.BlockSpec((None, 1, TMM), lambda s, sched: row_blk(s, sched) + (0,)),
                pl.BlockSpec((None, EPG, D, D_EXPERT), grp_blk),
                pl.BlockSpec((None, EPG, D, D_EXPERT), grp_blk),
                pl.BlockSpec((None, EPG, D_EXPERT, D), grp_blk),
            ],
            out_specs=pl.BlockSpec((TMM * Y_SUB, LANES), lambda s, sched: (s, 0)),
            scratch_shapes=[pltpu.VMEM((EPG, D, D_EXPERT), BF), pltpu.VMEM((EPG, D, D_EXPERT), BF),
                            pltpu.VMEM((EPG, D_EXPERT, D), BF), pltpu.VMEM((LANES, TMM), F32)],
        ),
        out_shape=jax.ShapeDtypeStruct((MAX_STEPS * TMM * Y_SUB, LANES), F32),
        compiler_params=_cp(1),
        name="experts",
    )(sched, xs, ws, w1, w3, w2)


def _pending_specs(l):
    nt = T // TM
    return [
        pl.BlockSpec((TM,), lambda t: (t,), memory_space=pltpu.SMEM),
        pl.BlockSpec((TM,), lambda t: (jnp.minimum(t + 1, nt - 1),), memory_space=pltpu.SMEM),
        pl.BlockSpec((TM,), lambda t: (jnp.minimum(t + 2, nt - 1),), memory_space=pltpu.SMEM),
        pl.BlockSpec(memory_space=pl.ANY),
        pl.BlockSpec((TM, D), lambda t: (t, 0)),
        pl.BlockSpec((1, 1, 6 * D), lambda t: (l * MOD_ROWS + _mod_row(t, TM), 0, 0)),
    ]


def _pending_args(pending):
    return (pending["slot"],) * 3 + (pending["ys"], pending["x"], pending["mod"])


GATHER_BUFS = 3
PENDING_SCRATCH = [pltpu.VMEM((GATHER_BUFS, TM * Y_SUB, LANES), F32), pltpu.SemaphoreType.DMA((GATHER_BUFS,))]
N_PENDING = 6


def _apply_pending(slot_ref, slot1_ref, slot2_ref, ys_hbm, x_ref, mod_ref, buf, sem):
    t = pl.program_id(0)
    nt = T // TM
    cur = t % GATHER_BUFS

    def gather(idx_ref, b):
        _row_copies(TM, lambda r: pltpu.make_async_copy(
            _row_tile(ys_hbm, idx_ref[r], Y_SUB), _row_tile(buf.at[b], r, Y_SUB), sem.at[b]))

    @pl.when(t == 0)
    def _():
        gather(slot_ref, 0)
        if nt > 1:
            gather(slot1_ref, 1)

    @pl.when(t + 2 < nt)
    def _():
        gather(slot2_ref, (t + 2) % GATHER_BUFS)

    pltpu.make_async_copy(ys_hbm.at[pl.ds(0, TM * Y_SUB), :], buf.at[cur], sem.at[cur]).wait()
    gate = mod_ref[0, :, 5 * D:6 * D]
    y = jnp.concatenate([buf[cur, pl.ds(k, TM, stride=Y_SUB), :] for k in range(Y_SUB)], axis=1)
    return x_ref[...] + gate * y


def _final_kernel(slot_ref, slot1_ref, slot2_ref, ys_hbm, x_ref, mod_ref, gfin_ref, op_ref, os_ref,
                  buf, sem):
    t = pl.program_id(0)
    x_new = _apply_pending(slot_ref, slot1_ref, slot2_ref, ys_hbm, x_ref, mod_ref, buf, sem)
    y_out = _rms(x_new, gfin_ref[...])
    npt = NP_TOK // TM

    @pl.when(t < npt)
    def _():
        op_ref[...] = y_out

    @pl.when(t >= npt)
    def _():
        os_ref[...] = y_out


def _final(pending, g_final):
    nt = T // TM
    npt = NP_TOK // TM
    return pl.pallas_call(
        _final_kernel,
        grid=(nt,),
        in_specs=_pending_specs(pending["layer"]) + [pl.BlockSpec((1, D), lambda t: (0, 0))],
        out_specs=[pl.BlockSpec((TM, D), lambda t: (jnp.minimum(t, npt - 1), 0)),
                   pl.BlockSpec((TM, D), lambda t: (jnp.maximum(t - npt, 0), 0))],
        out_shape=[jax.ShapeDtypeStruct((NP_TOK, D), F32), jax.ShapeDtypeStruct((NS_TOK, D), F32)],
        scratch_shapes=PENDING_SCRATCH,
        compiler_params=_cp(1),
        name="final",
    )(*_pending_args(pending), g_final)


def _moe(x, mod, l, rows, bucket, brank, wa, cnt, w1, w3, w2):
    slot, sched, fill_plan = _routing_plan(bucket, brank, cnt)
    xs, ws = _dispatch(fill_plan, slot, wa.reshape(T), rows)
    ys = _experts(sched, xs, ws.reshape(SORTED_TILES, 1, TMM), l, w1, w3, w2)
    return {"slot": slot, "ys": ys, "x": x, "mod": mod, "layer": l}


def _rope_tables():
    pos = jnp.arange(DEC_SEQ)
    r = (pos // GRID_W).astype(F32)
    col = (pos % GRID_W).astype(F32)
    inv = ROPE_BASE ** (-jnp.arange(AX_FREQS, dtype=F32) / AX_FREQS)
    ang = jnp.stack([r[:, None] * inv, col[:, None] * inv], axis=1)
    cos = jnp.cos(ang)[:, :, None, :]
    sin = jnp.sin(ang)[:, :, None, :]
    zero = jnp.zeros_like(sin)
    c32 = jnp.broadcast_to(cos, (DEC_SEQ, 2, 2, AX_FREQS)).reshape(DEC_SEQ, ROPE)
    lo32 = jnp.concatenate([-sin, zero], axis=2).reshape(DEC_SEQ, ROPE)
    hi32 = jnp.concatenate([zero, sin], axis=2).reshape(DEC_SEQ, ROPE)
    pad = HEAD_PAD - NOPE - ROPE

    def table(nope_value, rope_rows, rope_identity):
        body = jnp.concatenate([jnp.full((DEC_SEQ, NOPE), nope_value, F32), rope_rows,
                                jnp.zeros((DEC_SEQ, pad), F32)], axis=1)
        ident = jnp.concatenate([jnp.full((TM, NOPE), nope_value, F32),
                                 jnp.full((TM, ROPE), rope_identity, F32), jnp.zeros((TM, pad), F32)], axis=1)
        return jnp.concatenate([body, ident], axis=0)

    return (table(1.0, c32, 1.0), table(0.0, c32, 1.0), table(0.0, lo32, 0.0), table(0.0, hi32, 0.0))


def _even_weights(w_in, w_q_up, w_kv_up):
    k_in = w_in.shape[0]
    base = 3 * D_CONV + Q_RANK + KV_RANK
    w_kr = w_in[:, base:base + ROPE]
    z = lambda n: jnp.zeros((k_in, n), F32)
    win_ext = jnp.concatenate([
        w_in[:, :base],
        w_kr, z(NOPE - ROPE), w_kr, z(HEAD_PAD - NOPE - ROPE),
    ], axis=1).astype(BF)
    wq = w_q_up.reshape(Q_RANK, HEADS, QK_DIM)
    zq = lambda n: jnp.zeros((Q_RANK, HEADS, n), F32)
    wq1 = jnp.concatenate([wq, zq(HEAD_PAD - QK_DIM)], axis=2).reshape(Q_RANK, HEADS * HEAD_PAD).astype(BF)
    wkv = w_kv_up.reshape(KV_RANK, HEADS, NOPE + V_DIM)
    wk = jnp.concatenate([wkv[:, :, :NOPE], jnp.zeros((KV_RANK, HEADS, HEAD_PAD - NOPE), F32)],
                         axis=2).reshape(KV_RANK, HEADS * HEAD_PAD).astype(BF)
    wv = wkv[:, :, NOPE:].reshape(KV_RANK, HEADS * V_DIM).astype(BF)
    return win_ext, wq1, wk, wv


def kernel(x_prompt, x_sample, cache_ckv, cache_krope, c, c_ctx, w_ada, b_ada, g_mix, g_ffn, g_final,
           ev_w_in, conv_w, q_norm_g, w_q_up, kv_norm_g, w_kv_up, ev_w_out,
           gm_w_in, gm_v_g, gm_w_s, gm_b_s, gm_w_out, router_w, router_b, moe_w1, moe_w3, moe_w2):
    n_even = ev_w_in.shape[0]
    x_parts = (x_prompt.reshape(NP_TOK, D), x_sample.reshape(NS_TOK, D))
    cc =jnp.concatenate([c, c_ctx[None, :], jnp.zeros((MOD_ROWS - DEC_BATCH - 1, D), F32)], axis=0)
    mod = _ada(cc, w_ada, b_ada).reshape(DEPTH * MOD_ROWS, 1, 6 * D)

    rope = _rope_tables()
    rw_hi = router_w.T.astype(BF)
    rw_lo = (router_w.T - rw_hi.astype(F32)).astype(BF)
    rwt = jnp.concatenate([rw_hi, rw_lo], axis=0)
    rb = router_b.reshape(N_EXPERTS, 1)
    ev = [_even_weights(ev_w_in[i], w_q_up[i], w_kv_up[i]) for i in range(n_even)]
    kr_ctx = jnp.pad(cache_krope, ((0, 0), (0, 0), (0, 0), (NOPE, HEAD_PAD - NOPE - ROPE)))
    kr_ctx = kr_ctx.transpose(1, 0, 2, 3).reshape(n_even, DEC_BATCH * PAST, HEAD_PAD)
    ckv_ctx = cache_ckv.transpose(1, 0, 2, 3).reshape(n_even, DEC_BATCH * PAST, KV_RANK)
    kc, vc = _ctx_kv(ckv_ctx, kr_ctx, jnp.stack([e[2] for e in ev]), jnp.stack([e[3] for e in ev]))

    ckv_states, kr_states = [], []
    pending = None
    for l in range(DEPTH):
        i = l // 2
        if l % 2 == 0:
            win_ext, wq, wk, wv = ev[i]
            outs = _even_in(
                None if pending else x_parts, pending, mod, l, g_mix[l][None, :], win_ext,
                q_norm_g[i][None, :], wq, kv_norm_g[i][None, :], wk, wv, rope)
            bg, cv, q, k, v, ckv, kr = outs[:7]
            if pending:
                x_parts = (outs[7],)
            attn_parts = (_attn_prompt(q, k, v), _attn_sample(q, k, v, kc, vc, i))
            x, *routed = _even_out(x_parts, mod, l, bg, cv, attn_parts, conv_w[i],
                                                 ev_w_out[i].astype(BF), g_ffn[l][None, :], rwt, rb)
            ckv_states.append(ckv[:NP_TOK].reshape(BATCH, SEQ, KV_RANK))
            kr_states.append(kr[:NP_TOK, :ROPE].reshape(BATCH, SEQ, ROPE))
        else:
            x, *routed = _odd(pending, mod, l, g_mix[l][None, :], gm_w_in[i].astype(BF),
                                            gm_v_g[i][None, :], gm_w_s[i].astype(BF), gm_b_s[i].T,
                                            gm_w_out[i].astype(BF), g_ffn[l][None, :], rwt, rb)
        pending = _moe(x, mod, l, *routed, moe_w1, moe_w3, moe_w2)

    y_prompt, y_sample = _final(pending, g_final[None, :])
    y_prompt = y_prompt.reshape(BATCH, SEQ, D)
    y_sample = y_sample.reshape(DEC_BATCH, DEC_SEQ, D)
    return (y_prompt, y_sample, jnp.stack(ckv_states, axis=1), jnp.stack(kr_states, axis=1))
```

```python
import functools
import math

import jax
import jax.numpy as jnp
from jax import lax
from jax.experimental import pallas as pl
from jax.experimental.pallas import tpu as pltpu

D = 1024
BATCH, SEQ = 32, 256
DEC_BATCH, DEC_SEQ = 8, 2048
PAST = 256
DEPTH = 4
GRID_W = 64
D_CONV = 512
HEADS = 8
NOPE, ROPE, V_DIM = 64, 32, 64
QK_DIM = NOPE + ROPE
Q_RANK, KV_RANK = 384, 256
AX_FREQS = ROPE // 4
ROPE_BASE = 10000.0
CHUNK = 128
GM_GROUPS = 8
N_EXPERTS, N_GROUPS, EPG = 16, 4, 4
D_EXPERT = 256
EPS = 1e-6

NP_TOK = BATCH * SEQ
NS_TOK = DEC_BATCH * DEC_SEQ
T = NP_TOK + NS_TOK
MOD_ROWS = 16
HEAD_PAD = 128
HALO = 16

TM = 512
PAIRS = ((0, 1), (0, 2), (0, 3), (1, 2), (1, 3), (2, 3))
NB = N_GROUPS * len(PAIRS)
NB_PAD = 32
TMM = 512
FILL = 128
MAX_STEPS = T // TMM + NB
SORTED_TILES = MAX_STEPS + MAX_STEPS % 2
SORTED_ROWS = SORTED_TILES * TMM
LANES = 128
Y_SUB = D // LANES
X_SUB = Y_SUB
ISSUE_UNROLL = 32
TQ = 512
ATTN_PAIRS = 4
VMEM_LIMIT = 56 * 1024 * 1024

BF = jnp.bfloat16
F32 = jnp.float32


def _cp(n_axes):
    return pltpu.CompilerParams(dimension_semantics=("arbitrary",) * n_axes,
                                vmem_limit_bytes=VMEM_LIMIT)


def _mod_row(t, tm):
    n_prompt_tiles = NP_TOK // tm
    per_seq = DEC_SEQ // tm
    return jnp.where(t < n_prompt_tiles, DEC_BATCH, (t - n_prompt_tiles) // per_seq)


def _rms(x, g):
    return x * lax.rsqrt(jnp.mean(x * x, axis=-1, keepdims=True) + EPS) * g


def _norm_mod(x, g, shift, scale):
    return x * lax.rsqrt(jnp.mean(x * x, axis=-1, keepdims=True) + EPS) * (g * (1.0 + scale)) + shift


def _dot(a, b):
    return jnp.dot(a, b, preferred_element_type=F32)


def _dot_nt(a, b):
    return lax.dot_general(a, b, (((1,), (1,)), ((), ())), preferred_element_type=F32)


ADA_TN = 1536


def _ada_kernel(cc_ref, w_ref, b_ref, o_ref):
    cc = cc_ref[...]
    s = (cc / (1.0 + jnp.exp(-cc))).astype(BF)
    o_ref[0] = _dot(s, w_ref[0].astype(BF)) + b_ref[0]


def _ada(cc, w_ada, b_ada):
    n = 6 * D
    return pl.pallas_call(
        _ada_kernel,
        grid=(DEPTH, n // ADA_TN),
        in_specs=[
            pl.BlockSpec((MOD_ROWS, D), lambda l, j: (0, 0)),
            pl.BlockSpec((1, D, ADA_TN), lambda l, j: (l, 0, j)),
            pl.BlockSpec((1, 1, ADA_TN), lambda l, j: (l, 0, j)),
        ],
        out_specs=pl.BlockSpec((1, MOD_ROWS, ADA_TN), lambda l, j: (l, 0, j)),
        out_shape=jax.ShapeDtypeStruct((DEPTH, MOD_ROWS, n), F32),
        compiler_params=_cp(2),
        name="ada",
    )(cc, w_ada, b_ada.reshape(DEPTH, 1, n))


def _route(logits_t, rb):
    sc = 1.0 / (1.0 + jnp.exp(-logits_t))
    sel = sc + rb
    rows = [sel[e:e + 1, :] for e in range(N_EXPERTS)]
    srows = [sc[e:e + 1, :] for e in range(N_EXPERTS)]

    def top2sum(a, b, c, d):
        hi1, lo1 = jnp.maximum(a, b), jnp.minimum(a, b)
        hi2, lo2 = jnp.maximum(c, d), jnp.minimum(c, d)
        return jnp.maximum(hi1, hi2) + jnp.maximum(jnp.minimum(hi1, hi2), jnp.maximum(lo1, lo2))

    gs = [top2sum(*rows[EPG * g:EPG * (g + 1)]) for g in range(N_GROUPS)]
    best = gs[0]
    gidx = jnp.zeros_like(best, dtype=jnp.int32)
    for g in range(1, N_GROUPS):
        upd = gs[g] > best
        best = jnp.where(upd, gs[g], best)
        gidx = jnp.where(upd, g, gidx)

    picked = []
    for g in range(N_GROUPS):
        grp = rows[EPG * g:EPG * (g + 1)]
        in_g = gidx == g
        for j in range(EPG):
            rank = jnp.zeros_like(gidx)
            for k in range(EPG):
                if k == j:
                    continue
                ahead = grp[k] > grp[j]
                if k < j:
                    ahead = ahead | (grp[k] == grp[j])
                rank = rank + ahead.astype(jnp.int32)
            picked.append(in_g & (rank < 2))
    w = [jnp.where(picked[e], srows[e], 0.0) for e in range(N_EXPERTS)]
    wsum = w[0]
    for e in range(1, N_EXPERTS):
        wsum = wsum + w[e]
    inv = 1.0 / wsum
    pj, wj = [], []
    for j in range(EPG):
        hit, acc = picked[j], w[j]
        for g in range(1, N_GROUPS):
            hit = hit | picked[EPG * g + j]
            acc = acc + w[EPG * g + j]
        pj.append(hit)
        wj.append(acc * inv)
    pair = jnp.full_like(gidx, len(PAIRS) - 1)
    for idx in range(len(PAIRS) - 2, -1, -1):
        a, b = PAIRS[idx]
        pair = jnp.where(pj[a] & pj[b], idx, pair)
    w_a = jnp.where(pj[0], wj[0], jnp.where(pj[1], wj[1], wj[2]))
    w_b = jnp.where(pj[3], wj[3], jnp.where(pj[2], wj[2], wj[1]))
    return (w_a, w_b), gidx * len(PAIRS) + pair


def _ffn_prep(x_new, mod_ref, gf_ref, rwt_ref, rb_ref, h2c_ref, gidx_ref, grank_ref, wa_ref, cnt_ref,
              run_scr):
    t = pl.program_id(0)
    tm = x_new.shape[0]
    shift = mod_ref[0, :, 3 * D:4 * D]
    scale = mod_ref[0, :, 4 * D:5 * D]
    h2 = _norm_mod(x_new, gf_ref[...], shift, scale)
    h_hi = h2.astype(BF)
    h_lo = (h2 - h_hi.astype(F32)).astype(BF)
    by_hi = _dot_nt(rwt_ref[...], h_hi)
    by_lo = _dot_nt(rwt_ref[0:N_EXPERTS, :], h_lo)
    logits_t = by_hi[0:N_EXPERTS, :] + by_hi[N_EXPERTS:2 * N_EXPERTS, :] + by_lo
    wab, gidx = _route(logits_t, rb_ref[...])
    wa_ref[0] = wab[0]

    for k in range(X_SUB):
        h2c_ref[pl.ds(k, tm, stride=X_SUB), :] = h2[:, k * LANES:(k + 1) * LANES]

    @pl.when(t == 0)
    def _():
        run_scr[...] = jnp.zeros_like(run_scr)

    onehot = (lax.broadcasted_iota(jnp.int32, (NB_PAD, tm), 0) == gidx).astype(F32)
    earlier = (lax.broadcasted_iota(jnp.int32, (tm, tm), 0)
               < lax.broadcasted_iota(jnp.int32, (tm, tm), 1)).astype(BF)
    rank = _dot(onehot.astype(BF), earlier)
    run = run_scr[:, 0:1]
    grank = jnp.sum(onehot * (rank + run), axis=0, keepdims=True)
    gidx_ref[0] = gidx
    grank_ref[0] = grank.astype(jnp.int32)
    run_scr[...] = run_scr[...] + jnp.sum(onehot, axis=1, keepdims=True)
    cnt_ref[...] = run_scr[...]


def _prep_out_specs():
    tok = lambda w: pl.BlockSpec((TM, w), lambda t: (t, 0))
    lanes = pl.BlockSpec((1, 1, TM), lambda t: (t, 0, 0))
    rows = pl.BlockSpec((TM * X_SUB, LANES), lambda t: (t, 0))
    return [tok(D), rows, lanes, lanes, lanes, pl.BlockSpec((NB_PAD, 128), lambda t: (0, 0))]


def _prep_out_shapes():
    nt = T // TM
    return [jax.ShapeDtypeStruct((T, D), F32), jax.ShapeDtypeStruct((T * X_SUB, LANES), F32),
            jax.ShapeDtypeStruct((nt, 1, TM), jnp.int32), jax.ShapeDtypeStruct((nt, 1, TM), jnp.int32),
            jax.ShapeDtypeStruct((nt, 1, TM), F32), jax.ShapeDtypeStruct((NB_PAD, 128), F32)]


EV_EXT = 3 * D_CONV + Q_RANK + KV_RANK + 3 * HEAD_PAD


def _tok_specs(parts, width):
    npt = NP_TOK // TM
    if len(parts) == 1:
        return [pl.BlockSpec((TM, width), lambda t: (t, 0))]
    return [pl.BlockSpec((TM, width), lambda t: (jnp.minimum(t, npt - 1), 0)),
            pl.BlockSpec((TM, width), lambda t: (jnp.maximum(t - npt, 0), 0))]


def _tok_load(refs):
    if len(refs) == 1:
        return refs[0][...]
    return jnp.where(pl.program_id(0) < NP_TOK // TM, refs[0][...], refs[1][...])


def _even_in_kernel(n_x, *refs):
    if n_x:
        x_refs, refs = refs[:n_x], refs[n_x:]
        x = _tok_load(x_refs)
    else:
        pend, refs, scratch = refs[:N_PENDING], refs[N_PENDING:-2], refs[-2:]
        refs, xo_ref = refs[:-1], refs[-1]
        x = _apply_pending(*pend, *scratch)
        xo_ref[...] = x
    (mod_ref, g_ref, win_ref, qg_ref, wq1_ref, wq2_ref, kg_ref, wk_ref, wv_ref, rc_ref, rs_ref,
     bg_ref, cv_ref, q_ref, k_ref, v_ref, ckv_ref, kr_ref) = refs
    shift = mod_ref[0, :, 0:D]
    scale = mod_ref[0, :, D:2 * D]
    h = _norm_mod(x, g_ref[...], shift, scale).astype(BF)
    proj = _dot(h, win_ref[...])
    o = 0
    b_g = proj[:, o:o + D_CONV]; o += D_CONV
    c_g = proj[:, o:o + D_CONV]; o += D_CONV
    v_in = proj[:, o:o + D_CONV]; o += D_CONV
    q_a = proj[:, o:o + Q_RANK]; o += Q_RANK
    kv_a = proj[:, o:o + KV_RANK]; o += KV_RANK
    kr_raw = proj[:, o:o + HEAD_PAD]; o += HEAD_PAD
    kr_cat = proj[:, o:o + HEAD_PAD]; o += HEAD_PAD
    kr_sw = proj[:, o:o + HEAD_PAD]

    bg_ref[...] = b_g.astype(BF)
    cv_ref[...] = (c_g * v_in).astype(BF)
    kr_ref[...] = kr_raw

    rc = rc_ref[...]
    rs = rs_ref[...]
    qn = _rms(q_a, qg_ref[...]).astype(BF)
    q1 = _dot(qn, wq1_ref[...])
    q2 = _dot(qn, wq2_ref[...])
    qscale = QK_DIM ** -0.5
    for hd in range(HEADS):
        sl = slice(hd * HEAD_PAD, (hd + 1) * HEAD_PAD)
        q_ref[:, sl] = ((q1[:, sl] * rc + q2[:, sl] * rs) * qscale).astype(BF)

    ckv = _rms(kv_a, kg_ref[...])
    ckv_ref[...] = ckv
    ckv_b = ckv.astype(BF)
    kk = _dot(ckv_b, wk_ref[...])
    kr = kr_cat * rc + kr_sw * rs
    for hd in range(HEADS):
        sl = slice(hd * HEAD_PAD, (hd + 1) * HEAD_PAD)
        k_ref[:, sl] = (kk[:, sl] + kr).astype(BF)
    v_ref[...] = _dot(ckv_b, wv_ref[...]).astype(BF)


def _even_in(x_parts, pending, mod, l, g_mix, win, qg, wq1, wq2, kg, wk, wv, rope_c, rope_s):
    nt = T // TM
    npt = NP_TOK // TM
    per_seq = DEC_SEQ // TM
    ident_blk = DEC_SEQ // TM

    def rope_idx(t):
        return (jnp.where(t < npt, ident_blk, (t - npt) % per_seq), 0)

    full = lambda shape: pl.BlockSpec(shape, lambda t: (0,) * len(shape))
    tok = lambda w: pl.BlockSpec((TM, w), lambda t: (t, 0))
    out_specs = [tok(D_CONV), tok(D_CONV), tok(HEADS * HEAD_PAD), tok(HEADS * HEAD_PAD),
                 tok(HEADS * V_DIM), tok(KV_RANK), tok(HEAD_PAD)]
    out_shape = [
        jax.ShapeDtypeStruct((T, D_CONV), BF), jax.ShapeDtypeStruct((T, D_CONV), BF),
        jax.ShapeDtypeStruct((T, HEADS * HEAD_PAD), BF), jax.ShapeDtypeStruct((T, HEADS * HEAD_PAD), BF),
        jax.ShapeDtypeStruct((T, HEADS * V_DIM), BF),
        jax.ShapeDtypeStruct((T, KV_RANK), F32), jax.ShapeDtypeStruct((T, HEAD_PAD), F32),
    ]
    if pending is None:
        lead_specs, lead_args, scratch = _tok_specs(x_parts, D), tuple(x_parts), []
    else:
        lead_specs, lead_args, scratch = _pending_specs(pending["layer"]), _pending_args(pending), PENDING_SCRATCH
        out_specs.append(tok(D))
        out_shape.append(jax.ShapeDtypeStruct((T, D), F32))
    return pl.pallas_call(
        functools.partial(_even_in_kernel, 0 if pending is not None else len(x_parts)),
        grid=(nt,),
        in_specs=lead_specs + [
            pl.BlockSpec((1, 1, 6 * D), lambda t: (l * MOD_ROWS + _mod_row(t, TM), 0, 0)),
            full((1, D)), full((D, EV_EXT)), full((1, Q_RANK)),
            full((Q_RANK, HEADS * HEAD_PAD)), full((Q_RANK, HEADS * HEAD_PAD)),
            full((1, KV_RANK)), full((KV_RANK, HEADS * HEAD_PAD)), full((KV_RANK, HEADS * V_DIM)),
            pl.BlockSpec((TM, HEAD_PAD), rope_idx), pl.BlockSpec((TM, HEAD_PAD), rope_idx),
        ],
        out_specs=out_specs,
        out_shape=out_shape,
        scratch_shapes=scratch,
        compiler_params=_cp(1),
        name="even_in",
    )(*lead_args, mod, g_mix, win, qg, wq1, wq2, kg, wk, wv, rope_c, rope_s)


CTX_TM = 512


def _ctx_kv_kernel(ckv_ref, kr_ref, wk_ref, wv_ref, k_ref, v_ref):
    ckv_b = ckv_ref[0].astype(BF)
    kk = _dot(ckv_b, wk_ref[0])
    kr = kr_ref[0]
    for hd in range(HEADS):
        sl = slice(hd * HEAD_PAD, (hd + 1) * HEAD_PAD)
        k_ref[0, :, sl] = (kk[:, sl] + kr).astype(BF)
    v_ref[0] = _dot(ckv_b, wv_ref[0]).astype(BF)


def _ctx_kv(ckv_all, kr_all, wk_all, wv_all):
    n_even = ckv_all.shape[0]
    rows = DEC_BATCH * PAST
    return pl.pallas_call(
        _ctx_kv_kernel,
        grid=(n_even, rows // CTX_TM),
        in_specs=[
            pl.BlockSpec((1, CTX_TM, KV_RANK), lambda i, t: (i, t, 0)),
            pl.BlockSpec((1, CTX_TM, HEAD_PAD), lambda i, t: (i, t, 0)),
            pl.BlockSpec((1, KV_RANK, HEADS * HEAD_PAD), lambda i, t: (i, 0, 0)),
            pl.BlockSpec((1, KV_RANK, HEADS * V_DIM), lambda i, t: (i, 0, 0)),
        ],
        out_specs=[
            pl.BlockSpec((1, CTX_TM, HEADS * HEAD_PAD), lambda i, t: (i, t, 0)),
            pl.BlockSpec((1, CTX_TM, HEADS * V_DIM), lambda i, t: (i, t, 0)),
        ],
        out_shape=[jax.ShapeDtypeStruct((n_even, rows, HEADS * HEAD_PAD), BF),
                   jax.ShapeDtypeStruct((n_even, rows, HEADS * V_DIM), BF)],
        compiler_params=_cp(2),
        name="ctx_kv",
    )(ckv_all, kr_all, wk_all, wv_all)


def _attn_body(n_pairs, has_ctx, q_ref, k_ref, v_ref, *rest):
    if has_ctx:
        kc_ref, vc_ref, o_ref = rest
    else:
        (o_ref,) = rest
    lane = lax.broadcasted_iota(jnp.int32, (1, 2 * V_DIM), 1)
    w = 2 * V_DIM
    ones = jnp.ones((k_ref.shape[0], w), BF)
    ones_ctx = jnp.ones((PAST, w), BF)
    for pr in range(n_pairs):
        vsl = slice(pr * w, (pr + 1) * w)
        v = jnp.concatenate([v_ref[:, vsl], ones], axis=1)
        if has_ctx:
            vc = jnp.concatenate([vc_ref[0, :, vsl], ones_ctx], axis=1)
        outs = []
        for sub in range(2):
            hsl = slice((2 * pr + sub) * HEAD_PAD, (2 * pr + sub + 1) * HEAD_PAD)
            q = q_ref[:, hsl]
            s1 = _dot_nt(q, k_ref[:, hsl])
            m = jnp.max(s1, axis=-1, keepdims=True)
            if has_ctx:
                s2 = _dot_nt(q, kc_ref[0, :, hsl])
                m = jnp.maximum(m, jnp.max(s2, axis=-1, keepdims=True))
            acc = _dot(jnp.exp(s1 - m).astype(BF), v)
            if has_ctx:
                acc = acc + _dot(jnp.exp(s2 - m).astype(BF), vc)
            outs.append(acc[:, 0:w] * (1.0 / acc[:, w:2 * w]))
        o_ref[:, vsl] = jnp.where(lane < V_DIM, outs[0], outs[1]).astype(BF)


def _attn_prompt(q, k, v):
    return pl.pallas_call(
        functools.partial(_attn_body, HEADS // 2, False),
        grid=(BATCH,),
        in_specs=[
            pl.BlockSpec((SEQ, HEADS * HEAD_PAD), lambda b: (b, 0)),
            pl.BlockSpec((SEQ, HEADS * HEAD_PAD), lambda b: (b, 0)),
            pl.BlockSpec((SEQ, HEADS * V_DIM), lambda b: (b, 0)),
        ],
        out_specs=pl.BlockSpec((SEQ, HEADS * V_DIM), lambda b: (b, 0)),
        out_shape=jax.ShapeDtypeStruct((NP_TOK, HEADS * V_DIM), BF),
        compiler_params=_cp(1),
        name="attn_prompt",
    )(q, k, v)


def _attn_sample(q, k, v, kc, vc, i):
    s_blk0 = NP_TOK // DEC_SEQ
    q_blk0 = NP_TOK // TQ
    nq = DEC_SEQ // TQ
    hw = ATTN_PAIRS * 2 * HEAD_PAD
    vw = ATTN_PAIRS * 2 * V_DIM
    return pl.pallas_call(
        functools.partial(_attn_body, ATTN_PAIRS, True),
        grid=(DEC_BATCH, HEADS // (2 * ATTN_PAIRS), nq),
        in_specs=[
            pl.BlockSpec((TQ, hw), lambda b, hp, j: (q_blk0 + b * nq + j, hp)),
            pl.BlockSpec((DEC_SEQ, hw), lambda b, hp, j: (s_blk0 + b, hp)),
            pl.BlockSpec((DEC_SEQ, vw), lambda b, hp, j: (s_blk0 + b, hp)),
            pl.BlockSpec((1, PAST, hw), lambda b, hp, j: (i, b, hp)),
            pl.BlockSpec((1, PAST, vw), lambda b, hp, j: (i, b, hp)),
        ],
        out_specs=pl.BlockSpec((TQ, vw), lambda b, hp, j: (b * nq + j, hp)),
        out_shape=jax.ShapeDtypeStruct((NS_TOK, HEADS * V_DIM), BF),
        compiler_params=_cp(3),
        name="attn_sample",
    )(q, k, v, kc, vc)


def _even_out_kernel(n_x, *refs):
    x_refs, refs = refs[:n_x], refs[n_x:]
    (mod_ref, bg_ref, cv_ref, cvp_ref, cvn_ref, atp_ref, ats_ref, cw_ref, wo_ref, gf_ref, rwt_ref, rb_ref,
     xo_ref, h2c_ref, gidx_ref, grank_ref, wa_ref, cnt_ref, run_scr) = refs
    t = pl.program_id(0)
    npt = NP_TOK // TM
    per_seq = DEC_SEQ // TM
    cv = cv_ref[...].astype(F32)
    r = lax.broadcasted_iota(jnp.int32, (TM, 1), 0)
    is_prompt = t < npt
    tile_in_seq = (t - npt) % per_seq
    first_row = jnp.where(is_prompt, 0, jnp.where(tile_in_seq == 0, 0, -1))
    last_row = jnp.where(is_prompt, SEQ - 1, jnp.where(tile_in_seq == per_seq - 1, TM - 1, -1))
    period_mask = jnp.where(is_prompt, SEQ - 1, TM - 1)
    first = (r & period_mask) == first_row
    last = (r & period_mask) == last_row
    prev_row = cvp_ref[HALO - 1:HALO, :].astype(F32)
    next_row = cvn_ref[0:1, :].astype(F32)
    prev = jnp.where(r == 0, prev_row, pltpu.roll(cv, 1, 0))
    prev = jnp.where(first, 0.0, prev)
    nxt = jnp.where(r == TM - 1, next_row, pltpu.roll(cv, TM - 1, 0))
    nxt = jnp.where(last, 0.0, nxt)
    cw = cw_ref[...]
    conv = prev * cw[0:1, :] + cv * cw[1:2, :] + nxt * cw[2:3, :]
    yc = (bg_ref[...].astype(F32) * conv).astype(BF)
    attn = _tok_load((atp_ref, ats_ref))
    out = _dot(yc, wo_ref[0:D_CONV, :]) + _dot(attn, wo_ref[D_CONV:2 * D_CONV, :])
    gate = mod_ref[0, :, 2 * D:3 * D]
    x_new = _tok_load(x_refs) + gate * out
    xo_ref[...] = x_new
    _ffn_prep(x_new, mod_ref, gf_ref, rwt_ref, rb_ref, h2c_ref, gidx_ref, grank_ref, wa_ref, cnt_ref,
              run_scr)


def _even_out(x_parts, mod, l, bg, cv, attn_parts, conv_w, w_out, g_ffn, rwt, rb):
    nt = T // TM
    hb = TM // HALO
    nhb = T // HALO
    full = lambda shape: pl.BlockSpec(shape, lambda t: (0,) * len(shape))
    tok = lambda w: pl.BlockSpec((TM, w), lambda t: (t, 0))
    return pl.pallas_call(
        functools.partial(_even_out_kernel, len(x_parts)),
        grid=(nt,),
        in_specs=_tok_specs(x_parts, D) + [
            pl.BlockSpec((1, 1, 6 * D), lambda t: (l * MOD_ROWS + _mod_row(t, TM), 0, 0)),
            tok(D_CONV), tok(D_CONV),
            pl.BlockSpec((HALO, D_CONV), lambda t: (jnp.maximum(t * hb - 1, 0), 0)),
            pl.BlockSpec((HALO, D_CONV), lambda t: (jnp.minimum((t + 1) * hb, nhb - 1), 0)),
        ] + _tok_specs(attn_parts, HEADS * V_DIM) + [
            full((3, D_CONV)), full((2 * D_CONV, D)), full((1, D)),
            full((2 * N_EXPERTS, D)), full((N_EXPERTS, 1)),
        ],
        out_specs=_prep_out_specs(),
        out_shape=_prep_out_shapes(),
        scratch_shapes=[pltpu.VMEM((NB_PAD, 128), F32)],
        compiler_params=_cp(1),
        name="even_out",
    )(*x_parts, mod, bg, cv, cv, cv, *attn_parts, conv_w, w_out, g_ffn, rwt, rb)


def _odd_kernel(*refs):
    pend, refs, scratch = refs[:N_PENDING], refs[N_PENDING:-2], refs[-2:]
    (mod_ref, g_ref, win_ref, vg_ref, ws_ref, bst_ref, wo_ref, gf_ref, rwt_ref, rb_ref,
     xo_ref, h2c_ref, gidx_ref, grank_ref, wa_ref, cnt_ref, gated_scr, run_scr) = refs
    x = _apply_pending(*pend, *scratch)
    shift = mod_ref[0, :, 0:D]
    scale = mod_ref[0, :, D:2 * D]
    h = _norm_mod(x, g_ref[...], shift, scale).astype(BF)
    zl = _dot(h, win_ref[...])
    z = 0.5 * zl * (1.0 + lax.erf(zl * math.sqrt(0.5)))
    u = z[:, 0:D]
    v = _rms(z[:, D:2 * D], vg_ref[...]).astype(BF)
    n_chunks = TM // CHUNK
    gch = D // GM_GROUPS
    for g in range(GM_GROUPS):
        csl = slice(g * gch, (g + 1) * gch)
        vg = jnp.concatenate([v[n * CHUNK:(n + 1) * CHUNK, csl] for n in range(n_chunks)], axis=1)
        sg = _dot(ws_ref[g], vg) + bst_ref[:, g:g + 1]
        for n in range(n_chunks):
            rsl = slice(n * CHUNK, (n + 1) * CHUNK)
            gated_scr[rsl, csl] = (u[rsl, csl] * sg[:, n * gch:(n + 1) * gch]).astype(BF)
    out = _dot(gated_scr[...], wo_ref[...])
    gate = mod_ref[0, :, 2 * D:3 * D]
    x_new = x + gate * out
    xo_ref[...] = x_new
    _ffn_prep(x_new, mod_ref, gf_ref, rwt_ref, rb_ref, h2c_ref, gidx_ref, grank_ref, wa_ref, cnt_ref,
              run_scr)


def _odd(pending, mod, l, g_mix, win, v_g, w_s, b_st, w_out, g_ffn, rwt, rb):
    nt = T // TM
    full = lambda shape: pl.BlockSpec(shape, lambda t: (0,) * len(shape))
    return pl.pallas_call(
        _odd_kernel,
        grid=(nt,),
        in_specs=_pending_specs(pending["layer"]) + [
            pl.BlockSpec((1, 1, 6 * D), lambda t: (l * MOD_ROWS + _mod_row(t, TM), 0, 0)),
            full((1, D)), full((D, 2 * D)), full((1, D)),
            full((GM_GROUPS, CHUNK, CHUNK)), full((CHUNK, GM_GROUPS)), full((D, D)),
            full((1, D)), full((2 * N_EXPERTS, D)), full((N_EXPERTS, 1)),
        ],
        out_specs=_prep_out_specs(),
        out_shape=_prep_out_shapes(),
        scratch_shapes=[pltpu.VMEM((TM, D), BF), pltpu.VMEM((NB_PAD, 128), F32)]
        + PENDING_SCRATCH,
        compiler_params=_cp(1),
        name="odd",
    )(*_pending_args(pending), mod, g_mix, win, v_g, w_s, b_st, w_out, g_ffn, rwt, rb)


def _routing_plan(bucket, brank, cnt):
    counts = cnt[:NB, 0].astype(jnp.int32)
    padded = (counts + (TMM - 1)) // TMM * TMM
    ends = jnp.cumsum(padded)
    base = ends - padded
    b = bucket.reshape(T)
    hit = b[:, None] == jnp.arange(NB, dtype=jnp.int32)[None, :]
    slot = brank.reshape(T) + jnp.sum(jnp.where(hit, base[None, :], 0), axis=1)
    n_steps = ends[NB - 1] // TMM
    starts = jnp.minimum(jnp.arange(MAX_STEPS, dtype=jnp.int32) * TMM, ends[NB - 1] - TMM)
    bucket_of_step = jnp.sum((starts[:, None] >= ends[None, :NB - 1]).astype(jnp.int32), axis=1)
    pair_of_step = bucket_of_step % len(PAIRS)
    pairs = jnp.asarray(PAIRS, dtype=jnp.int32)
    sched = jnp.concatenate([bucket_of_step // len(PAIRS), pairs[pair_of_step, 0], pairs[pair_of_step, 1],
                             n_steps[None]]).astype(jnp.int32)
    fill_plan = jnp.concatenate([base + counts, ends, n_steps[None]]).astype(jnp.int32)
    return slot.astype(jnp.int32), sched, fill_plan


def _row_tile(ref, row, sub):
    return ref.at[pl.ds(pl.multiple_of(row * sub, sub), sub), :]


def _row_copies(n_rows, make_copy, per_row=None):
    def body(i, carry):
        for j in range(ISSUE_UNROLL):
            r = i * ISSUE_UNROLL + j
            make_copy(r).start(priority=j % 2)
            if per_row is not None:
                per_row(r)
        return carry
    lax.fori_loop(0, n_rows // ISSUE_UNROLL, body, 0)


def _dispatch_kernel(pad_ref, slot_ref, wa_ref, h_hbm, o_hbm, ws_ref, zero_scr, zvec_scr, stage,
                     zsem, in_sem, row_sem):
    t = pl.program_id(0)
    nt = T // TM
    cur = t % 2

    def load(tile, b):
        return pltpu.make_async_copy(
            h_hbm.at[pl.ds(pl.multiple_of(tile * (TM * X_SUB), TM * X_SUB), TM * X_SUB), :],
            stage.at[b], in_sem.at[b])

    def rows_done(b):
        pltpu.make_async_copy(stage.at[b], o_hbm.at[pl.ds(0, TM * X_SUB), :], row_sem.at[b]).wait()

    @pl.when(t == 0)
    def _():
        load(0, 0).start()

    @pl.when(t == 0)
    def _():
        zvec_scr[...] = jnp.zeros_like(zvec_scr)
        clear = pltpu.make_async_copy(zvec_scr, ws_ref, zsem.at[0])
        clear.start()
        clear.wait()

    @pl.when(t == 0)
    def _():
        zero_scr[...] = jnp.zeros_like(zero_scr)

        def chunk_fill(row):
            return pltpu.make_async_copy(
                zero_scr.at[pl.ds(0, FILL * X_SUB), :],
                o_hbm.at[pl.ds(pl.multiple_of(row * X_SUB, X_SUB), FILL * X_SUB), :], zsem.at[0])

        def tile_fill(s):
            return pltpu.make_async_copy(
                zero_scr, o_hbm.at[pl.ds(s * TMM * X_SUB, TMM * X_SUB), :], zsem.at[1])

        for phase in ("start", "wait"):
            for k in range(NB):
                for c in range(TMM // FILL):
                    top = pad_ref[NB + k] - c * FILL

                    @pl.when(top > pad_ref[k])
                    def _():
                        getattr(chunk_fill(top - FILL), phase)()
        for phase in ("start", "wait"):
            for s in range(T // TMM, SORTED_TILES):
                @pl.when(s >= pad_ref[2 * NB])
                def _():
                    getattr(tile_fill(s), phase)()

    def put_weight(r):
        ws_ref[slot_ref[r]] = wa_ref[r]

    @pl.when(t > 0)
    def _():
        rows_done(1 - cur)

    @pl.when(t + 1 < nt)
    def _():
        load(t + 1, 1 - cur).start()

    load(t, cur).wait()
    _row_copies(TM, lambda r: pltpu.make_async_copy(
        _row_tile(stage.at[cur], r, X_SUB), _row_tile(o_hbm, slot_ref[r], X_SUB), row_sem.at[cur]),
        per_row=put_weight)

    @pl.when(t == nt - 1)
    def _():
        rows_done(cur)


def _dispatch(pad_start, slot, wa, h2c):
    nt = T // TM
    return pl.pallas_call(
        _dispatch_kernel,
        grid_spec=pltpu.PrefetchScalarGridSpec(
            num_scalar_prefetch=1,
            grid=(nt,),
            in_specs=[
                pl.BlockSpec((TM,), lambda t, pad: (t,), memory_space=pltpu.SMEM),
                pl.BlockSpec((TM,), lambda t, pad: (t,), memory_space=pltpu.SMEM),
                pl.BlockSpec(memory_space=pl.ANY),
            ],
            out_specs=[pl.BlockSpec(memory_space=pl.ANY), pl.BlockSpec(memory_space=pltpu.SMEM)],
            scratch_shapes=[pltpu.VMEM((TMM * X_SUB, LANES), F32), pltpu.VMEM((SORTED_ROWS,), F32),
                            pltpu.VMEM((2, TM * X_SUB, LANES), F32),
                            pltpu.SemaphoreType.DMA((2,)), pltpu.SemaphoreType.DMA((2,)),
                            pltpu.SemaphoreType.DMA((2,))],
        ),
        out_shape=[jax.ShapeDtypeStruct((SORTED_ROWS * X_SUB, LANES), F32),
                   jax.ShapeDtypeStruct((SORTED_ROWS,), F32)],
        compiler_params=_cp(1),
        name="dispatch",
    )(pad_start, slot, wa, h2c)


def _experts_kernel(sched_ref, xs_ref, ws_ref, w1_ref, w3_ref, w2_ref, o_ref, w1b, w3b, w2b, wt_scr):
    s = pl.program_id(0)

    @pl.when(jnp.logical_or(s == 0, sched_ref[s] != sched_ref[jnp.maximum(s - 1, 0)]))
    def _():
        w1b[...] = w1_ref[...].astype(BF)
        w3b[...] = w3_ref[...].astype(BF)
        w2b[...] = w2_ref[...].astype(BF)

    n_steps = sched_ref[3 * MAX_STEPS]

    @pl.when(s < n_steps)
    def _():
        h = jnp.concatenate([xs_ref[pl.ds(k, TMM, stride=X_SUB), :] for k in range(X_SUB)],
                            axis=1).astype(BF)
        wt_scr[...] = jnp.zeros_like(wt_scr)
        wt_scr[0:1, :] = ws_ref[...]
        w_first = wt_scr[...].T[:, 0:1]
        weights = (w_first, 1.0 - w_first)
        acc = None
        for j in range(2):
            e = sched_ref[(1 + j) * MAX_STEPS + s]
            a = _dot(h, w1b[e])
            b = _dot(h, w3b[e])
            hid = (a * (1.0 / (1.0 + jnp.exp(-a))) * b * weights[j]).astype(BF)
            y = _dot(hid, w2b[e])
            acc = y if acc is None else acc + y
        for k in range(D // LANES):
            o_ref[pl.ds(k, TMM, stride=Y_SUB), :] = acc[:, k * LANES:(k + 1) * LANES]

    @pl.when(s >= n_steps)
    def _():
        o_ref[...] = jnp.zeros_like(o_ref)


def _experts(sched, xs, ws, l, w1, w3, w2):
    row_blk = lambda s, sched: (jnp.maximum(jnp.minimum(s, sched[3 * MAX_STEPS] - 1), 0), 0)
    grp_blk = lambda s, sched: (l, sched[s], 0, 0)
    return pl.pallas_call(
        _experts_kernel,
        grid_spec=pltpu.PrefetchScalarGridSpec(
            num_scalar_prefetch=1,
            grid=(MAX_STEPS,),
            in_specs=[
                pl.BlockSpec((TMM * X_SUB, LANES), row_blk),
                pl.BlockSpec((None, 1, TMM), lambda s, sched: row_blk(s, sched) + (0,)),
                pl.BlockSpec((None, EPG, D, D_EXPERT), grp_blk),
                pl.BlockSpec((None, EPG, D, D_EXPERT), grp_blk),
                pl.BlockSpec((None, EPG, D_EXPERT, D), grp_blk),
            ],
            out_specs=pl.BlockSpec((TMM * Y_SUB, LANES), lambda s, sched: (s, 0)),
            scratch_shapes=[pltpu.VMEM((EPG, D, D_EXPERT), BF), pltpu.VMEM((EPG, D, D_EXPERT), BF),
                            pltpu.VMEM((EPG, D_EXPERT, D), BF), pltpu.VMEM((LANES, TMM), F32)],
        ),
        out_shape=jax.ShapeDtypeStruct((MAX_STEPS * TMM * Y_SUB, LANES), F32),
        compiler_params=_cp(1),
        name="experts",
    )(sched, xs, ws, w1, w3, w2)


def _pending_specs(l):
    nt = T // TM
    return [
        pl.BlockSpec((TM,), lambda t: (t,), memory_space=pltpu.SMEM),
        pl.BlockSpec((TM,), lambda t: (jnp.minimum(t + 1, nt - 1),), memory_space=pltpu.SMEM),
        pl.BlockSpec((TM,), lambda t: (jnp.minimum(t + 2, nt - 1),), memory_space=pltpu.SMEM),
        pl.BlockSpec(memory_space=pl.ANY),
        pl.BlockSpec((TM, D), lambda t: (t, 0)),
        pl.BlockSpec((1, 1, 6 * D), lambda t: (l * MOD_ROWS + _mod_row(t, TM), 0, 0)),
    ]


def _pending_args(pending):
    return (pending["slot"],) * 3 + (pending["ys"], pending["x"], pending["mod"])


GATHER_BUFS = 3
PENDING_SCRATCH = [pltpu.VMEM((GATHER_BUFS, TM * Y_SUB, LANES), F32), pltpu.SemaphoreType.DMA((GATHER_BUFS,))]
N_PENDING = 6


def _apply_pending(slot_ref, slot1_ref, slot2_ref, ys_hbm, x_ref, mod_ref, buf, sem):
    t = pl.program_id(0)
    nt = T // TM
    cur = t % GATHER_BUFS

    def gather(idx_ref, b):
        _row_copies(TM, lambda r: pltpu.make_async_copy(
            _row_tile(ys_hbm, idx_ref[r], Y_SUB), _row_tile(buf.at[b], r, Y_SUB), sem.at[b]))

    @pl.when(t == 0)
    def _():
        gather(slot_ref, 0)
        if nt > 1:
            gather(slot1_ref, 1)

    @pl.when(t + 2 < nt)
    def _():
        gather(slot2_ref, (t + 2) % GATHER_BUFS)

    pltpu.make_async_copy(ys_hbm.at[pl.ds(0, TM * Y_SUB), :], buf.at[cur], sem.at[cur]).wait()
    gate = mod_ref[0, :, 5 * D:6 * D]
    y = jnp.concatenate([buf[cur, pl.ds(k, TM, stride=Y_SUB), :] for k in range(Y_SUB)], axis=1)
    return x_ref[...] + gate * y


def _final_kernel(slot_ref, slot1_ref, slot2_ref, ys_hbm, x_ref, mod_ref, gfin_ref, op_ref, os_ref,
                  buf, sem):
    t = pl.program_id(0)
    x_new = _apply_pending(slot_ref, slot1_ref, slot2_ref, ys_hbm, x_ref, mod_ref, buf, sem)
    y_out = _rms(x_new, gfin_ref[...])
    npt = NP_TOK // TM

    @pl.when(t < npt)
    def _():
        op_ref[...] = y_out

    @pl.when(t >= npt)
    def _():
        os_ref[...] = y_out


def _final(pending, g_final):
    nt = T // TM
    npt = NP_TOK // TM
    return pl.pallas_call(
        _final_kernel,
        grid=(nt,),
        in_specs=_pending_specs(pending["layer"]) + [pl.BlockSpec((1, D), lambda t: (0, 0))],
        out_specs=[pl.BlockSpec((TM, D), lambda t: (jnp.minimum(t, npt - 1), 0)),
                   pl.BlockSpec((TM, D), lambda t: (jnp.maximum(t - npt, 0), 0))],
        out_shape=[jax.ShapeDtypeStruct((NP_TOK, D), F32), jax.ShapeDtypeStruct((NS_TOK, D), F32)],
        scratch_shapes=PENDING_SCRATCH,
        compiler_params=_cp(1),
        name="final",
    )(*_pending_args(pending), g_final)


def _moe(x, mod, l, rows, bucket, brank, wa, cnt, w1, w3, w2):
    slot, sched, fill_plan = _routing_plan(bucket, brank, cnt)
    xs, ws = _dispatch(fill_plan, slot, wa.reshape(T), rows)
    ys = _experts(sched, xs, ws.reshape(SORTED_TILES, 1, TMM), l, w1, w3, w2)
    return {"slot": slot, "ys": ys, "x": x, "mod": mod, "layer": l}


def _rope_tables():
    pos = jnp.arange(DEC_SEQ)
    r = (pos // GRID_W).astype(F32)
    col = (pos % GRID_W).astype(F32)
    inv = ROPE_BASE ** (-jnp.arange(AX_FREQS, dtype=F32) / AX_FREQS)
    ang = jnp.stack([r[:, None] * inv, col[:, None] * inv], axis=1)
    cos = jnp.cos(ang)[:, :, None, :]
    sin = jnp.sin(ang)[:, :, None, :]
    c32 = jnp.broadcast_to(cos, (DEC_SEQ, 2, 2, AX_FREQS)).reshape(DEC_SEQ, ROPE)
    s32 = jnp.concatenate([-sin, sin], axis=2).reshape(DEC_SEQ, ROPE)
    pad = HEAD_PAD - NOPE - ROPE
    c = jnp.concatenate([jnp.ones((DEC_SEQ, NOPE), F32), c32, jnp.zeros((DEC_SEQ, pad), F32)], axis=1)
    s = jnp.concatenate([jnp.zeros((DEC_SEQ, NOPE), F32), s32, jnp.zeros((DEC_SEQ, pad), F32)], axis=1)
    c_id = jnp.concatenate([jnp.ones((TM, NOPE + ROPE), F32), jnp.zeros((TM, pad), F32)], axis=1)
    s_id = jnp.zeros((TM, HEAD_PAD), F32)
    return jnp.concatenate([c, c_id], axis=0), jnp.concatenate([s, s_id], axis=0)


def _swap_halves(w):
    lead = w.shape[:-1]
    return w.reshape(lead + (2, 2, AX_FREQS))[..., ::-1, :].reshape(lead + (ROPE,))


def _even_weights(w_in, w_q_up, w_kv_up):
    k_in = w_in.shape[0]
    base = 3 * D_CONV + Q_RANK + KV_RANK
    w_kr = w_in[:, base:base + ROPE]
    z = lambda n: jnp.zeros((k_in, n), F32)
    win_ext = jnp.concatenate([
        w_in[:, :base],
        w_kr, z(HEAD_PAD - ROPE),
        z(NOPE), w_kr, z(HEAD_PAD - NOPE - ROPE),
        z(NOPE), _swap_halves(w_kr), z(HEAD_PAD - NOPE - ROPE),
    ], axis=1).astype(BF)
    wq = w_q_up.reshape(Q_RANK, HEADS, QK_DIM)
    zq = lambda n: jnp.zeros((Q_RANK, HEADS, n), F32)
    wq1 = jnp.concatenate([wq, zq(HEAD_PAD - QK_DIM)], axis=2).reshape(Q_RANK, HEADS * HEAD_PAD).astype(BF)
    wq2 = jnp.concatenate([zq(NOPE), _swap_halves(wq[:, :, NOPE:]), zq(HEAD_PAD - QK_DIM)],
                          axis=2).reshape(Q_RANK, HEADS * HEAD_PAD).astype(BF)
    wkv = w_kv_up.reshape(KV_RANK, HEADS, NOPE + V_DIM)
    wk = jnp.concatenate([wkv[:, :, :NOPE], jnp.zeros((KV_RANK, HEADS, HEAD_PAD - NOPE), F32)],
                         axis=2).reshape(KV_RANK, HEADS * HEAD_PAD).astype(BF)
    wv = wkv[:, :, NOPE:].reshape(KV_RANK, HEADS * V_DIM).astype(BF)
    return win_ext, wq1, wq2, wk, wv


def kernel(x_prompt, x_sample, cache_ckv, cache_krope, c, c_ctx, w_ada, b_ada, g_mix, g_ffn, g_final,
           ev_w_in, conv_w, q_norm_g, w_q_up, kv_norm_g, w_kv_up, ev_w_out,
           gm_w_in, gm_v_g, gm_w_s, gm_b_s, gm_w_out, router_w, router_b, moe_w1, moe_w3, moe_w2):
    n_even = ev_w_in.shape[0]
    x_parts = (x_prompt.reshape(NP_TOK, D), x_sample.reshape(NS_TOK, D))
    cc = jnp.concatenate([c, c_ctx[None, :], jnp.zeros((MOD_ROWS - DEC_BATCH - 1, D), F32)], axis=0)
    mod = _ada(cc, w_ada, b_ada).reshape(DEPTH * MOD_ROWS, 1, 6 * D)

    rope_c, rope_s = _rope_tables()
    rw_hi = router_w.T.astype(BF)
    rw_lo = (router_w.T - rw_hi.astype(F32)).astype(BF)
    rwt = jnp.concatenate([rw_hi, rw_lo], axis=0)
    rb = router_b.reshape(N_EXPERTS, 1)
    ev = [_even_weights(ev_w_in[i], w_q_up[i], w_kv_up[i]) for i in range(n_even)]
    kr_ctx = jnp.pad(cache_krope, ((0, 0), (0, 0), (0, 0), (NOPE, HEAD_PAD - NOPE - ROPE)))
    kr_ctx = kr_ctx.transpose(1, 0, 2, 3).reshape(n_even, DEC_BATCH * PAST, HEAD_PAD)
    ckv_ctx = cache_ckv.transpose(1, 0, 2, 3).reshape(n_even, DEC_BATCH * PAST, KV_RANK)
    kc, vc = _ctx_kv(ckv_ctx, kr_ctx, jnp.stack([e[3] for e in ev]), jnp.stack([e[4] for e in ev]))

    ckv_states, kr_states = [], []
    pending = None
    for l in range(DEPTH):
        i = l // 2
        if l % 2 == 0:
            win_ext, wq1, wq2, wk, wv = ev[i]
            outs = _even_in(
                None if pending else x_parts, pending, mod, l, g_mix[l][None, :], win_ext,
                q_norm_g[i][None, :], wq1, wq2, kv_norm_g[i][None, :], wk, wv, rope_c, rope_s)
            bg, cv, q, k, v, ckv, kr = outs[:7]
            if pending:
                x_parts = (outs[7],)
            attn_parts = (_attn_prompt(q, k, v), _attn_sample(q, k, v, kc, vc, i))
            x, *routed = _even_out(x_parts, mod, l, bg, cv, attn_parts, conv_w[i],
                                                 ev_w_out[i].astype(BF), g_ffn[l][None, :], rwt, rb)
            ckv_states.append(ckv[:NP_TOK].reshape(BATCH, SEQ, KV_RANK))
            kr_states.append(kr[:NP_TOK, :ROPE].reshape(BATCH, SEQ, ROPE))
        else:
            x, *routed = _odd(pending, mod, l, g_mix[l][None, :], gm_w_in[i].astype(BF),
                                            gm_v_g[i][None, :], gm_w_s[i].astype(BF), gm_b_s[i].T,
                                            gm_w_out[i].astype(BF), g_ffn[l][None, :], rwt, rb)
        pending = _moe(x, mod, l, *routed, moe_w1, moe_w3, moe_w2)

    y_prompt, y_sample = _final(pending, g_final[None, :])
    y_prompt = y_prompt.reshape(BATCH, SEQ, D)
    y_sample = y_sample.reshape(DEC_BATCH, DEC_SEQ, D)
    return (y_prompt, y_sample, jnp.stack(ckv_states, axis=1), jnp.stack(kr_states, axis=1))
```

```python
import functools
import math

import jax
import jax.numpy as jnp
from jax import lax
from jax.experimental import pallas as pl
from jax.experimental.pallas import tpu as pltpu

D = 1024
BATCH, SEQ = 32, 256
DEC_BATCH, DEC_SEQ = 8, 2048
PAST = 256
DEPTH = 4
GRID_W = 64
D_CONV = 512
HEADS = 8
NOPE, ROPE, V_DIM = 64, 32, 64
QK_DIM = NOPE + ROPE
Q_RANK, KV_RANK = 384, 256
AX_FREQS = ROPE // 4
ROPE_BASE = 10000.0
CHUNK = 128
GM_GROUPS = 8
N_EXPERTS, N_GROUPS, EPG = 16, 4, 4
D_EXPERT = 256
EPS = 1e-6

NP_TOK = BATCH * SEQ
NS_TOK = DEC_BATCH * DEC_SEQ
T = NP_TOK + NS_TOK
MOD_ROWS = 16
HEAD_PAD = 128
HALO = 16

TM = 512
PAIRS = ((0, 1), (0, 2), (0, 3), (1, 2), (1, 3), (2, 3))
NB = N_GROUPS * len(PAIRS)
NB_PAD = 32
TMM = 512
FILL = 128
MAX_STEPS = T // TMM + NB
SORTED_TILES = MAX_STEPS + MAX_STEPS % 2
SORTED_ROWS = SORTED_TILES * TMM
LANES = 128
Y_SUB = D // LANES
X_SUB = Y_SUB
ISSUE_UNROLL = 32
TQ = 512
ATTN_PAIRS = 4
VMEM_LIMIT = 56 * 1024 * 1024

BF = jnp.bfloat16
F32 = jnp.float32


def _cp(n_axes):
    return pltpu.CompilerParams(dimension_semantics=("arbitrary",) * n_axes,
                                vmem_limit_bytes=VMEM_LIMIT)


def _mod_row(t, tm):
    n_prompt_tiles = NP_TOK // tm
    per_seq = DEC_SEQ // tm
    return jnp.where(t < n_prompt_tiles, DEC_BATCH, (t - n_prompt_tiles) // per_seq)


def _rms(x, g):
    return x * lax.rsqrt(jnp.mean(x * x, axis=-1, keepdims=True) + EPS) * g


def _norm_mod(x, g, shift, scale):
    return x * lax.rsqrt(jnp.mean(x * x, axis=-1, keepdims=True) + EPS) * (g * (1.0 + scale)) + shift


def _dot(a, b):
    return jnp.dot(a, b, preferred_element_type=F32)


def _dot_nt(a, b):
    return lax.dot_general(a, b, (((1,), (1,)), ((), ())), preferred_element_type=F32)


ADA_TN = 1536


def _ada_kernel(cc_ref, w_ref, b_ref, o_ref):
    cc = cc_ref[...]
    s = (cc / (1.0 + jnp.exp(-cc))).astype(BF)
    o_ref[0] = _dot(s, w_ref[0].astype(BF)) + b_ref[0]


def _ada(cc, w_ada, b_ada):
    n = 6 * D
    return pl.pallas_call(
        _ada_kernel,
        grid=(DEPTH, n // ADA_TN),
        in_specs=[
            pl.BlockSpec((MOD_ROWS, D), lambda l, j: (0, 0)),
            pl.BlockSpec((1, D, ADA_TN), lambda l, j: (l, 0, j)),
            pl.BlockSpec((1, 1, ADA_TN), lambda l, j: (l, 0, j)),
        ],
        out_specs=pl.BlockSpec((1, MOD_ROWS, ADA_TN), lambda l, j: (l, 0, j)),
        out_shape=jax.ShapeDtypeStruct((DEPTH, MOD_ROWS, n), F32),
        compiler_params=_cp(2),
        name="ada",
    )(cc, w_ada, b_ada.reshape(DEPTH, 1, n))


def _route(logits_t, rb):
    sc = 1.0 / (1.0 + jnp.exp(-logits_t))
    sel = sc + rb
    rows = [sel[e:e + 1, :] for e in range(N_EXPERTS)]
    srows = [sc[e:e + 1, :] for e in range(N_EXPERTS)]

    def top2sum(a, b, c, d):
        hi1, lo1 = jnp.maximum(a, b), jnp.minimum(a, b)
        hi2, lo2 = jnp.maximum(c, d), jnp.minimum(c, d)
        return jnp.maximum(hi1, hi2) + jnp.maximum(jnp.minimum(hi1, hi2), jnp.maximum(lo1, lo2))

    gs = [top2sum(*rows[EPG * g:EPG * (g + 1)]) for g in range(N_GROUPS)]
    best = gs[0]
    gidx = jnp.zeros_like(best, dtype=jnp.int32)
    for g in range(1, N_GROUPS):
        upd = gs[g] > best
        best = jnp.where(upd, gs[g], best)
        gidx = jnp.where(upd, g, gidx)

    picked = []
    for g in range(N_GROUPS):
        grp = rows[EPG * g:EPG * (g + 1)]
        in_g = gidx == g
        for j in range(EPG):
            rank = jnp.zeros_like(gidx)
            for k in range(EPG):
                if k == j:
                    continue
                ahead = grp[k] > grp[j]
                if k < j:
                    ahead = ahead | (grp[k] == grp[j])
                rank = rank + ahead.astype(jnp.int32)
            picked.append(in_g & (rank < 2))
    w = [jnp.where(picked[e], srows[e], 0.0) for e in range(N_EXPERTS)]
    wsum = w[0]
    for e in range(1, N_EXPERTS):
        wsum = wsum + w[e]
    inv = 1.0 / wsum
    pj, wj = [], []
    for j in range(EPG):
        hit, acc = picked[j], w[j]
        for g in range(1, N_GROUPS):
            hit = hit | picked[EPG * g + j]
            acc = acc + w[EPG * g + j]
        pj.append(hit)
        wj.append(acc * inv)
    pair = jnp.full_like(gidx, len(PAIRS) - 1)
    for idx in range(len(PAIRS) - 2, -1, -1):
        a, b = PAIRS[idx]
        pair = jnp.where(pj[a] & pj[b], idx, pair)
    w_a = jnp.where(pj[0], wj[0], jnp.where(pj[1], wj[1], wj[2]))
    w_b = jnp.where(pj[3], wj[3], jnp.where(pj[2], wj[2], wj[1]))
    return (w_a, w_b), gidx * len(PAIRS) + pair


def _ffn_prep(x_new, mod_ref, gf_ref, rwt_ref, rb_ref, h2c_ref, gidx_ref, grank_ref, wa_ref, cnt_ref,
              run_scr):
    t = pl.program_id(0)
    tm = x_new.shape[0]
    shift = mod_ref[0, :, 3 * D:4 * D]
    scale = mod_ref[0, :, 4 * D:5 * D]
    h2 = _norm_mod(x_new, gf_ref[...], shift, scale)
    h_hi = h2.astype(BF)
    h_lo = (h2 - h_hi.astype(F32)).astype(BF)
    by_hi = _dot_nt(rwt_ref[...], h_hi)
    by_lo = _dot_nt(rwt_ref[0:N_EXPERTS, :], h_lo)
    logits_t = by_hi[0:N_EXPERTS, :] + by_hi[N_EXPERTS:2 * N_EXPERTS, :] + by_lo
    wab, gidx = _route(logits_t, rb_ref[...])
    wa_ref[0] = wab[0]

    for k in range(X_SUB):
        h2c_ref[pl.ds(k, tm, stride=X_SUB), :] = h2[:, k * LANES:(k + 1) * LANES]

    @pl.when(t == 0)
    def _():
        run_scr[...] = jnp.zeros_like(run_scr)

    onehot = (lax.broadcasted_iota(jnp.int32, (NB_PAD, tm), 0) == gidx).astype(F32)
    earlier = (lax.broadcasted_iota(jnp.int32, (tm, tm), 0)
               < lax.broadcasted_iota(jnp.int32, (tm, tm), 1)).astype(BF)
    rank = _dot(onehot.astype(BF), earlier)
    run = run_scr[:, 0:1]
    grank = jnp.sum(onehot * (rank + run), axis=0, keepdims=True)
    gidx_ref[0] = gidx
    grank_ref[0] = grank.astype(jnp.int32)
    run_scr[...] = run_scr[...] + jnp.sum(onehot, axis=1, keepdims=True)
    cnt_ref[...] = run_scr[...]


def _prep_out_specs():
    tok = lambda w: pl.BlockSpec((TM, w), lambda t: (t, 0))
    lanes = pl.BlockSpec((1, 1, TM), lambda t: (t, 0, 0))
    rows = pl.BlockSpec((TM * X_SUB, LANES), lambda t: (t, 0))
    return [tok(D), rows, lanes, lanes, lanes, pl.BlockSpec((NB_PAD, 128), lambda t: (0, 0))]


def _prep_out_shapes():
    nt = T // TM
    return [jax.ShapeDtypeStruct((T, D), F32), jax.ShapeDtypeStruct((T * X_SUB, LANES), F32),
            jax.ShapeDtypeStruct((nt, 1, TM), jnp.int32), jax.ShapeDtypeStruct((nt, 1, TM), jnp.int32),
            jax.ShapeDtypeStruct((nt, 1, TM), F32), jax.ShapeDtypeStruct((NB_PAD, 128), F32)]


EV_EXT = 3 * D_CONV + Q_RANK + KV_RANK + 3 * HEAD_PAD


def _tok_specs(parts, width):
    npt = NP_TOK // TM
    if len(parts) == 1:
        return [pl.BlockSpec((TM, width), lambda t: (t, 0))]
    return [pl.BlockSpec((TM, width), lambda t: (jnp.minimum(t, npt - 1), 0)),
            pl.BlockSpec((TM, width), lambda t: (jnp.maximum(t - npt, 0), 0))]


def _tok_load(refs):
    if len(refs) == 1:
        return refs[0][...]
    return jnp.where(pl.program_id(0) < NP_TOK // TM, refs[0][...], refs[1][...])


def _even_in_kernel(n_x, *refs):
    if n_x:
        x_refs, refs = refs[:n_x], refs[n_x:]
        x = _tok_load(x_refs)
    else:
        pend, refs, scratch = refs[:N_PENDING], refs[N_PENDING:-2], refs[-2:]
        refs, xo_ref = refs[:-1], refs[-1]
        x = _apply_pending(*pend, *scratch)
        xo_ref[...] = x
    (mod_ref, g_ref, win_ref, qg_ref, wq1_ref, wq2_ref, kg_ref, wk_ref, wv_ref, rc_ref, rs_ref,
     bg_ref, cv_ref, q_ref, k_ref, v_ref, ckv_ref, kr_ref) = refs
    shift = mod_ref[0, :, 0:D]
    scale = mod_ref[0, :, D:2 * D]
    h = _norm_mod(x, g_ref[...], shift, scale).astype(BF)
    proj = _dot(h, win_ref[...])
    o = 0
    b_g = proj[:, o:o + D_CONV]; o += D_CONV
    c_g = proj[:, o:o + D_CONV]; o += D_CONV
    v_in = proj[:, o:o + D_CONV]; o += D_CONV
    q_a = proj[:, o:o + Q_RANK]; o += Q_RANK
    kv_a = proj[:, o:o + KV_RANK]; o += KV_RANK
    kr_raw = proj[:, o:o + HEAD_PAD]; o += HEAD_PAD
    kr_cat = proj[:, o:o + HEAD_PAD]; o += HEAD_PAD
    kr_sw = proj[:, o:o + HEAD_PAD]

    bg_ref[...] = b_g.astype(BF)
    cv_ref[...] = (c_g * v_in).astype(BF)
    kr_ref[...] = kr_raw

    rc = rc_ref[...]
    rs = rs_ref[...]
    qn = _rms(q_a, qg_ref[...]).astype(BF)
    q1 = _dot(qn, wq1_ref[...])
    q2 = _dot(qn, wq2_ref[...])
    qscale = QK_DIM ** -0.5
    for hd in range(HEADS):
        sl = slice(hd * HEAD_PAD, (hd + 1) * HEAD_PAD)
        q_ref[:, sl] = ((q1[:, sl] * rc + q2[:, sl] * rs) * qscale).astype(BF)

    ckv = _rms(kv_a, kg_ref[...])
    ckv_ref[...] = ckv
    ckv_b = ckv.astype(BF)
    kk = _dot(ckv_b, wk_ref[...])
    kr = kr_cat * rc + kr_sw * rs
    for hd in range(HEADS):
        sl = slice(hd * HEAD_PAD, (hd + 1) * HEAD_PAD)
        k_ref[:, sl] = (kk[:, sl] + kr).astype(BF)
    v_ref[...] = _dot(ckv_b, wv_ref[...]).astype(BF)


def _even_in(x_parts, pending, mod, l, g_mix, win, qg, wq1, wq2, kg, wk, wv, rope_c, rope_s):
    nt = T // TM
    npt = NP_TOK // TM
    per_seq = DEC_SEQ // TM
    ident_blk = DEC_SEQ // TM

    def rope_idx(t):
        return (jnp.where(t < npt, ident_blk, (t - npt) % per_seq), 0)

    full = lambda shape: pl.BlockSpec(shape, lambda t: (0,) * len(shape))
    tok = lambda w: pl.BlockSpec((TM, w), lambda t: (t, 0))
    out_specs = [tok(D_CONV), tok(D_CONV), tok(HEADS * HEAD_PAD), tok(HEADS * HEAD_PAD),
                 tok(HEADS * V_DIM), tok(KV_RANK), tok(HEAD_PAD)]
    out_shape = [
        jax.ShapeDtypeStruct((T, D_CONV), BF), jax.ShapeDtypeStruct((T, D_CONV), BF),
        jax.ShapeDtypeStruct((T, HEADS * HEAD_PAD), BF), jax.ShapeDtypeStruct((T, HEADS * HEAD_PAD), BF),
        jax.ShapeDtypeStruct((T, HEADS * V_DIM), BF),
        jax.ShapeDtypeStruct((T, KV_RANK), F32), jax.ShapeDtypeStruct((T, HEAD_PAD), F32),
    ]
    if pending is None:
        lead_specs, lead_args, scratch = _tok_specs(x_parts, D), tuple(x_parts), []
    else:
        lead_specs, lead_args, scratch = _pending_specs(pending["layer"]), _pending_args(pending), PENDING_SCRATCH
        out_specs.append(tok(D))
        out_shape.append(jax.ShapeDtypeStruct((T, D), F32))
    return pl.pallas_call(
        functools.partial(_even_in_kernel, 0 if pending is not None else len(x_parts)),
        grid=(nt,),
        in_specs=lead_specs + [
            pl.BlockSpec((1, 1, 6 * D), lambda t: (l * MOD_ROWS + _mod_row(t, TM), 0, 0)),
            full((1, D)), full((D, EV_EXT)), full((1, Q_RANK)),
            full((Q_RANK, HEADS * HEAD_PAD)), full((Q_RANK, HEADS * HEAD_PAD)),
            full((1, KV_RANK)), full((KV_RANK, HEADS * HEAD_PAD)), full((KV_RANK, HEADS * V_DIM)),
            pl.BlockSpec((TM, HEAD_PAD), rope_idx), pl.BlockSpec((TM, HEAD_PAD), rope_idx),
        ],
        out_specs=out_specs,
        out_shape=out_shape,
        scratch_shapes=scratch,
        compiler_params=_cp(1),
        name="even_in",
    )(*lead_args, mod, g_mix, win, qg, wq1, wq2, kg, wk, wv, rope_c, rope_s)


CTX_TM = 512


def _ctx_kv_kernel(ckv_ref, kr_ref, wk_ref, wv_ref, k_ref, v_ref):
    ckv_b = ckv_ref[0].astype(BF)
    kk = _dot(ckv_b, wk_ref[0])
    kr = kr_ref[0]
    for hd in range(HEADS):
        sl = slice(hd * HEAD_PAD, (hd + 1) * HEAD_PAD)
        k_ref[0, :, sl] = (kk[:, sl] + kr).astype(BF)
    v_ref[0] = _dot(ckv_b, wv_ref[0]).astype(BF)


def _ctx_kv(ckv_all, kr_all, wk_all, wv_all):
    n_even = ckv_all.shape[0]
    rows = DEC_BATCH * PAST
    return pl.pallas_call(
        _ctx_kv_kernel,
        grid=(n_even, rows // CTX_TM),
        in_specs=[
            pl.BlockSpec((1, CTX_TM, KV_RANK), lambda i, t: (i, t, 0)),
            pl.BlockSpec((1, CTX_TM, HEAD_PAD), lambda i, t: (i, t, 0)),
            pl.BlockSpec((1, KV_RANK, HEADS * HEAD_PAD), lambda i, t: (i, 0, 0)),
            pl.BlockSpec((1, KV_RANK, HEADS * V_DIM), lambda i, t: (i, 0, 0)),
        ],
        out_specs=[
            pl.BlockSpec((1, CTX_TM, HEADS * HEAD_PAD), lambda i, t: (i, t, 0)),
            pl.BlockSpec((1, CTX_TM, HEADS * V_DIM), lambda i, t: (i, t, 0)),
        ],
        out_shape=[jax.ShapeDtypeStruct((n_even, rows, HEADS * HEAD_PAD), BF),
                   jax.ShapeDtypeStruct((n_even, rows, HEADS * V_DIM), BF)],
        compiler_params=_cp(2),
        name="ctx_kv",
    )(ckv_all, kr_all, wk_all, wv_all)


def _attn_body(n_pairs, has_ctx, q_ref, k_ref, v_ref, *rest):
    if has_ctx:
        kc_ref, vc_ref, o_ref = rest
    else:
        (o_ref,) = rest
    lane = lax.broadcasted_iota(jnp.int32, (1, 2 * V_DIM), 1)
    w = 2 * V_DIM
    ones = jnp.ones((k_ref.shape[0], w), BF)
    ones_ctx = jnp.ones((PAST, w), BF)
    for pr in range(n_pairs):
        vsl = slice(pr * w, (pr + 1) * w)
        v = jnp.concatenate([v_ref[:, vsl], ones], axis=1)
        if has_ctx:
            vc = jnp.concatenate([vc_ref[0, :, vsl], ones_ctx], axis=1)
        outs = []
        for sub in range(2):
            hsl = slice((2 * pr + sub) * HEAD_PAD, (2 * pr + sub + 1) * HEAD_PAD)
            q = q_ref[:, hsl]
            s1 = _dot_nt(q, k_ref[:, hsl])
            m = jnp.max(s1, axis=-1, keepdims=True)
            if has_ctx:
                s2 = _dot_nt(q, kc_ref[0, :, hsl])
                m = jnp.maximum(m, jnp.max(s2, axis=-1, keepdims=True))
            acc = _dot(jnp.exp(s1 - m).astype(BF), v)
            if has_ctx:
                acc = acc + _dot(jnp.exp(s2 - m).astype(BF), vc)
            outs.append(acc[:, 0:w] * (1.0 / acc[:, w:2 * w]))
        o_ref[:, vsl] = jnp.where(lane < V_DIM, outs[0], outs[1]).astype(BF)


def _attn_prompt(q, k, v):
    return pl.pallas_call(
        functools.partial(_attn_body, HEADS // 2, False),
        grid=(BATCH,),
        in_specs=[
            pl.BlockSpec((SEQ, HEADS * HEAD_PAD), lambda b: (b, 0)),
            pl.BlockSpec((SEQ, HEADS * HEAD_PAD), lambda b: (b, 0)),
            pl.BlockSpec((SEQ, HEADS * V_DIM), lambda b: (b, 0)),
        ],
        out_specs=pl.BlockSpec((SEQ, HEADS * V_DIM), lambda b: (b, 0)),
        out_shape=jax.ShapeDtypeStruct((NP_TOK, HEADS * V_DIM), BF),
        compiler_params=_cp(1),
        name="attn_prompt",
    )(q, k, v)


def _attn_sample(q, k, v, kc, vc, i):
    s_blk0 = NP_TOK // DEC_SEQ
    q_blk0 = NP_TOK // TQ
    nq = DEC_SEQ // TQ
    hw = ATTN_PAIRS * 2 * HEAD_PAD
    vw = ATTN_PAIRS * 2 * V_DIM
    return pl.pallas_call(
        functools.partial(_attn_body, ATTN_PAIRS, True),
        grid=(DEC_BATCH, HEADS // (2 * ATTN_PAIRS), nq),
        in_specs=[
            pl.BlockSpec((TQ, hw), lambda b, hp, j: (q_blk0 + b * nq + j, hp)),
            pl.BlockSpec((DEC_SEQ, hw), lambda b, hp, j: (s_blk0 + b, hp)),
            pl.BlockSpec((DEC_SEQ, vw), lambda b, hp, j: (s_blk0 + b, hp)),
            pl.BlockSpec((1, PAST, hw), lambda b, hp, j: (i, b, hp)),
            pl.BlockSpec((1, PAST, vw), lambda b, hp, j: (i, b, hp)),
        ],
        out_specs=pl.BlockSpec((TQ, vw), lambda b, hp, j: (b * nq + j, hp)),
        out_shape=jax.ShapeDtypeStruct((NS_TOK, HEADS * V_DIM), BF),
        compiler_params=_cp(3),
        name="attn_sample",
    )(q, k, v, kc, vc)


def _even_out_kernel(n_x, *refs):
    x_refs, refs = refs[:n_x], refs[n_x:]
    (mod_ref, bg_ref, cv_ref, cvp_ref, cvn_ref, atp_ref, ats_ref, cw_ref, wo_ref, gf_ref, rwt_ref, rb_ref,
     xo_ref, h2c_ref, gidx_ref, grank_ref, wa_ref, cnt_ref, run_scr) = refs
    t = pl.program_id(0)
    npt = NP_TOK // TM
    per_seq = DEC_SEQ // TM
    cv = cv_ref[...].astype(F32)
    r = lax.broadcasted_iota(jnp.int32, (TM, 1), 0)
    is_prompt = t < npt
    tile_in_seq = (t - npt) % per_seq
    first_row = jnp.where(is_prompt, 0, jnp.where(tile_in_seq == 0, 0, -1))
    last_row = jnp.where(is_prompt, SEQ - 1, jnp.where(tile_in_seq == per_seq - 1, TM - 1, -1))
    period_mask = jnp.where(is_prompt, SEQ - 1, TM - 1)
    first = (r & period_mask) == first_row
    last = (r & period_mask) == last_row
    prev_row = cvp_ref[HALO - 1:HALO, :].astype(F32)
    next_row = cvn_ref[0:1, :].astype(F32)
    prev = jnp.where(r == 0, prev_row, pltpu.roll(cv, 1, 0))
    prev = jnp.where(first, 0.0, prev)
    nxt = jnp.where(r == TM - 1, next_row, pltpu.roll(cv, TM - 1, 0))
    nxt = jnp.where(last, 0.0, nxt)
    cw = cw_ref[...]
    conv = prev * cw[0:1, :] + cv * cw[1:2, :] + nxt * cw[2:3, :]
    yc = (bg_ref[...].astype(F32) * conv).astype(BF)
    attn = _tok_load((atp_ref, ats_ref))
    out = _dot(yc, wo_ref[0:D_CONV, :]) + _dot(attn, wo_ref[D_CONV:2 * D_CONV, :])
    gate = mod_ref[0, :, 2 * D:3 * D]
    x_new = _tok_load(x_refs) + gate * out
    xo_ref[...] = x_new
    _ffn_prep(x_new, mod_ref, gf_ref, rwt_ref, rb_ref, h2c_ref, gidx_ref, grank_ref, wa_ref, cnt_ref,
              run_scr)


def _even_out(x_parts, mod, l, bg, cv, attn_parts, conv_w, w_out, g_ffn, rwt, rb):
    nt = T // TM
    hb = TM // HALO
    nhb = T // HALO
    full = lambda shape: pl.BlockSpec(shape, lambda t: (0,) * len(shape))
    tok = lambda w: pl.BlockSpec((TM, w), lambda t: (t, 0))
    return pl.pallas_call(
        functools.partial(_even_out_kernel, len(x_parts)),
        grid=(nt,),
        in_specs=_tok_specs(x_parts, D) + [
            pl.BlockSpec((1, 1, 6 * D), lambda t: (l * MOD_ROWS + _mod_row(t, TM), 0, 0)),
            tok(D_CONV), tok(D_CONV),
            pl.BlockSpec((HALO, D_CONV), lambda t: (jnp.maximum(t * hb - 1, 0), 0)),
            pl.BlockSpec((HALO, D_CONV), lambda t: (jnp.minimum((t + 1) * hb, nhb - 1), 0)),
        ] + _tok_specs(attn_parts, HEADS * V_DIM) + [
            full((3, D_CONV)), full((2 * D_CONV, D)), full((1, D)),
            full((2 * N_EXPERTS, D)), full((N_EXPERTS, 1)),
        ],
        out_specs=_prep_out_specs(),
        out_shape=_prep_out_shapes(),
        scratch_shapes=[pltpu.VMEM((NB_PAD, 128), F32)],
        compiler_params=_cp(1),
        name="even_out",
    )(*x_parts, mod, bg, cv, cv, cv, *attn_parts, conv_w, w_out, g_ffn, rwt, rb)


def _odd_kernel(*refs):
    pend, refs, scratch = refs[:N_PENDING], refs[N_PENDING:-2], refs[-2:]
    (mod_ref, g_ref, win_ref, vg_ref, ws_ref, bst_ref, wo_ref, gf_ref, rwt_ref, rb_ref,
     xo_ref, h2c_ref, gidx_ref, grank_ref, wa_ref, cnt_ref, gated_scr, run_scr) = refs
    x = _apply_pending(*pend, *scratch)
    shift = mod_ref[0, :, 0:D]
    scale = mod_ref[0, :, D:2 * D]
    h = _norm_mod(x, g_ref[...], shift, scale).astype(BF)
    zl = _dot(h, win_ref[...])
    z = 0.5 * zl * (1.0 + lax.erf(zl * math.sqrt(0.5)))
    u = z[:, 0:D]
    v = _rms(z[:, D:2 * D], vg_ref[...]).astype(BF)
    n_chunks = TM // CHUNK
    gch = D // GM_GROUPS
    for g in range(GM_GROUPS):
        csl = slice(g * gch, (g + 1) * gch)
        vg = jnp.concatenate([v[n * CHUNK:(n + 1) * CHUNK, csl] for n in range(n_chunks)], axis=1)
        sg = _dot(ws_ref[g], vg) + bst_ref[:, g:g + 1]
        for n in range(n_chunks):
            rsl = slice(n * CHUNK, (n + 1) * CHUNK)
            gated_scr[rsl, csl] = (u[rsl, csl] * sg[:, n * gch:(n + 1) * gch]).astype(BF)
    out = _dot(gated_scr[...], wo_ref[...])
    gate = mod_ref[0, :, 2 * D:3 * D]
    x_new = x + gate * out
    xo_ref[...] = x_new
    _ffn_prep(x_new, mod_ref, gf_ref, rwt_ref, rb_ref, h2c_ref, gidx_ref, grank_ref, wa_ref, cnt_ref,
              run_scr)


def _odd(pending, mod, l, g_mix, win, v_g, w_s, b_st, w_out, g_ffn, rwt, rb):
    nt = T // TM
    full = lambda shape: pl.BlockSpec(shape, lambda t: (0,) * len(shape))
    return pl.pallas_call(
        _odd_kernel,
        grid=(nt,),
        in_specs=_pending_specs(pending["layer"]) + [
            pl.BlockSpec((1, 1, 6 * D), lambda t: (l * MOD_ROWS + _mod_row(t, TM), 0, 0)),
            full((1, D)), full((D, 2 * D)), full((1, D)),
            full((GM_GROUPS, CHUNK, CHUNK)), full((CHUNK, GM_GROUPS)), full((D, D)),
            full((1, D)), full((2 * N_EXPERTS, D)), full((N_EXPERTS, 1)),
        ],
        out_specs=_prep_out_specs(),
        out_shape=_prep_out_shapes(),
        scratch_shapes=[pltpu.VMEM((TM, D), BF), pltpu.VMEM((NB_PAD, 128), F32)]
        + PENDING_SCRATCH,
        compiler_params=_cp(1),
        name="odd",
    )(*_pending_args(pending), mod, g_mix, win, v_g, w_s, b_st, w_out, g_ffn, rwt, rb)


def _routing_plan(bucket, brank, cnt):
    counts = cnt[:NB, 0].astype(jnp.int32)
    padded = (counts + (TMM - 1)) // TMM * TMM
    ends = jnp.cumsum(padded)
    base = ends - padded
    b = bucket.reshape(T)
    hit = b[:, None] == jnp.arange(NB, dtype=jnp.int32)[None, :]
    slot = brank.reshape(T) + jnp.sum(jnp.where(hit, base[None, :], 0), axis=1)
    n_steps = ends[NB - 1] // TMM
    starts = jnp.minimum(jnp.arange(MAX_STEPS, dtype=jnp.int32) * TMM, ends[NB - 1] - TMM)
    bucket_of_step = jnp.sum((starts[:, None] >= ends[None, :NB - 1]).astype(jnp.int32), axis=1)
    pair_of_step = bucket_of_step % len(PAIRS)
    pairs = jnp.asarray(PAIRS, dtype=jnp.int32)
    sched = jnp.concatenate([bucket_of_step // len(PAIRS), pairs[pair_of_step, 0], pairs[pair_of_step, 1],
                             n_steps[None]]).astype(jnp.int32)
    fill_plan = jnp.concatenate([base + counts, ends, n_steps[None]]).astype(jnp.int32)
    return slot.astype(jnp.int32), sched, fill_plan


def _row_tile(ref, row, sub):
    return ref.at[pl.ds(pl.multiple_of(row * sub, sub), sub), :]


def _row_copies(n_rows, make_copy, per_row=None):
    def body(i, carry):
        for j in range(ISSUE_UNROLL):
            r = i * ISSUE_UNROLL + j
            make_copy(r).start(priority=j % 2)
            if per_row is not None:
                per_row(r)
        return carry
    lax.fori_loop(0, n_rows // ISSUE_UNROLL, body, 0)


def _dispatch_kernel(pad_ref, slot_ref, wa_ref, h_ref, o_hbm, ws_ref, zero_scr, zvec_scr, zsem, sem):
    t = pl.program_id(0)

    @pl.when(t == 0)
    def _():
        zvec_scr[...] = jnp.zeros_like(zvec_scr)
        clear = pltpu.make_async_copy(zvec_scr, ws_ref, zsem.at[0])
        clear.start()
        clear.wait()

    @pl.when(t == 0)
    def _():
        zero_scr[...] = jnp.zeros_like(zero_scr)

        def chunk_fill(row):
            return pltpu.make_async_copy(
                zero_scr.at[pl.ds(0, FILL * X_SUB), :],
                o_hbm.at[pl.ds(pl.multiple_of(row * X_SUB, X_SUB), FILL * X_SUB), :], zsem.at[0])

        def tile_fill(s):
            return pltpu.make_async_copy(
                zero_scr, o_hbm.at[pl.ds(s * TMM * X_SUB, TMM * X_SUB), :], zsem.at[1])

        for phase in ("start", "wait"):
            for k in range(NB):
                for c in range(TMM // FILL):
                    top = pad_ref[NB + k] - c * FILL

                    @pl.when(top > pad_ref[k])
                    def _():
                        getattr(chunk_fill(top - FILL), phase)()
        for phase in ("start", "wait"):
            for s in range(T // TMM, SORTED_TILES):
                @pl.when(s >= pad_ref[2 * NB])
                def _():
                    getattr(tile_fill(s), phase)()

    def put_weight(r):
        ws_ref[slot_ref[r]] = wa_ref[r]

    _row_copies(TM, lambda r: pltpu.make_async_copy(
        _row_tile(h_ref, r, X_SUB), _row_tile(o_hbm, slot_ref[r], X_SUB), sem), per_row=put_weight)
    pltpu.make_async_copy(h_ref, o_hbm.at[pl.ds(0, TM * X_SUB), :], sem).wait()


def _dispatch(pad_start, slot, wa, h2c):
    nt = T // TM
    return pl.pallas_call(
        _dispatch_kernel,
        grid_spec=pltpu.PrefetchScalarGridSpec(
            num_scalar_prefetch=1,
            grid=(nt,),
            in_specs=[
                pl.BlockSpec((TM,), lambda t, pad: (t,), memory_space=pltpu.SMEM),
                pl.BlockSpec((TM,), lambda t, pad: (t,), memory_space=pltpu.SMEM),
                pl.BlockSpec((TM * X_SUB, LANES), lambda t, pad: (t, 0)),
            ],
            out_specs=[pl.BlockSpec(memory_space=pl.ANY), pl.BlockSpec(memory_space=pltpu.SMEM)],
            scratch_shapes=[pltpu.VMEM((TMM * X_SUB, LANES), F32), pltpu.VMEM((SORTED_ROWS,), F32),
                            pltpu.SemaphoreType.DMA((2,)), pltpu.SemaphoreType.DMA(())],
        ),
        out_shape=[jax.ShapeDtypeStruct((SORTED_ROWS * X_SUB, LANES), F32),
                   jax.ShapeDtypeStruct((SORTED_ROWS,), F32)],
        compiler_params=_cp(1),
        name="dispatch",
    )(pad_start, slot, wa, h2c)


def _experts_kernel(sched_ref, xs_ref, ws_ref, w1_ref, w3_ref, w2_ref, o_ref, w1b, w3b, w2b, wt_scr):
    s = pl.program_id(0)

    @pl.when(jnp.logical_or(s == 0, sched_ref[s] != sched_ref[jnp.maximum(s - 1, 0)]))
    def _():
        w1b[...] = w1_ref[...].astype(BF)
        w3b[...] = w3_ref[...].astype(BF)
        w2b[...] = w2_ref[...].astype(BF)

    n_steps = sched_ref[3 * MAX_STEPS]

    @pl.when(s < n_steps)
    def _():
        h = jnp.concatenate([xs_ref[pl.ds(k, TMM, stride=X_SUB), :] for k in range(X_SUB)],
                            axis=1).astype(BF)
        wt_scr[...] = jnp.zeros_like(wt_scr)
        wt_scr[0:1, :] = ws_ref[...]
        w_first = wt_scr[...].T[:, 0:1]
        weights = (w_first, 1.0 - w_first)
        acc = None
        for j in range(2):
            e = sched_ref[(1 + j) * MAX_STEPS + s]
            a = _dot(h, w1b[e])
            b = _dot(h, w3b[e])
            hid = (a * (1.0 / (1.0 + jnp.exp(-a))) * b * weights[j]).astype(BF)
            y = _dot(hid, w2b[e])
            acc = y if acc is None else acc + y
        for k in range(D // LANES):
            o_ref[pl.ds(k, TMM, stride=Y_SUB), :] = acc[:, k * LANES:(k + 1) * LANES]

    @pl.when(s >= n_steps)
    def _():
        o_ref[...] = jnp.zeros_like(o_ref)


def _experts(sched, xs, ws, l, w1, w3, w2):
    row_blk = lambda s, sched: (jnp.maximum(jnp.minimum(s, sched[3 * MAX_STEPS] - 1), 0), 0)
    grp_blk = lambda s, sched: (l, sched[s], 0, 0)
    return pl.pallas_call(
        _experts_kernel,
        grid_spec=pltpu.PrefetchScalarGridSpec(
            num_scalar_prefetch=1,
            grid=(MAX_STEPS,),
            in_specs=[
                pl.BlockSpec((TMM * X_SUB, LANES), row_blk),
                pl.BlockSpec((None, 1, TMM), lambda s, sched: row_blk(s, sched) + (0,)),
                pl.BlockSpec((None, EPG, D, D_EXPERT), grp_blk),
                pl.BlockSpec((None, EPG, D, D_EXPERT), grp_blk),
                pl.BlockSpec((None, EPG, D_EXPERT, D), grp_blk),
            ],
            out_specs=pl.BlockSpec((TMM * Y_SUB, LANES), lambda s, sched: (s, 0)),
            scratch_shapes=[pltpu.VMEM((EPG, D, D_EXPERT), BF), pltpu.VMEM((EPG, D, D_EXPERT), BF),
                            pltpu.VMEM((EPG, D_EXPERT, D), BF), pltpu.VMEM((LANES, TMM), F32)],
        ),
        out_shape=jax.ShapeDtypeStruct((MAX_STEPS * TMM * Y_SUB, LANES), F32),
        compiler_params=_cp(1),
        name="experts",
    )(sched, xs, ws, w1, w3, w2)


def _pending_specs(l):
    nt = T // TM
    return [
        pl.BlockSpec((TM,), lambda t: (t,), memory_space=pltpu.SMEM),
        pl.BlockSpec((TM,), lambda t: (jnp.minimum(t + 1, nt - 1),), memory_space=pltpu.SMEM),
        pl.BlockSpec(memory_space=pl.ANY),
        pl.BlockSpec((TM, D), lambda t: (t, 0)),
        pl.BlockSpec((1, 1, 6 * D), lambda t: (l * MOD_ROWS + _mod_row(t, TM), 0, 0)),
    ]


def _pending_args(pending):
    return (pending["slot"],) * 2 + (pending["ys"], pending["x"], pending["mod"])


PENDING_SCRATCH = [pltpu.VMEM((2, TM * Y_SUB, LANES), F32), pltpu.SemaphoreType.DMA((2,))]
N_PENDING = 5


def _apply_pending(slot_ref, slot_next_ref, ys_hbm, x_ref, mod_ref, buf, sem):
    t = pl.program_id(0)
    nt = T // TM
    cur = t % 2

    def gather(idx_ref, b):
        _row_copies(TM, lambda r: pltpu.make_async_copy(
            _row_tile(ys_hbm, idx_ref[r], Y_SUB), _row_tile(buf.at[b], r, Y_SUB), sem.at[b]))

    @pl.when(t == 0)
    def _():
        gather(slot_ref, 0)

    @pl.when(t + 1 < nt)
    def _():
        gather(slot_next_ref, 1 - cur)

    pltpu.make_async_copy(ys_hbm.at[pl.ds(0, TM * Y_SUB), :], buf.at[cur], sem.at[cur]).wait()
    gate = mod_ref[0, :, 5 * D:6 * D]
    y = jnp.concatenate([buf[cur, pl.ds(k, TM, stride=Y_SUB), :] for k in range(Y_SUB)], axis=1)
    return x_ref[...] + gate * y


def _final_kernel(slot_ref, slot_next_ref, ys_hbm, x_ref, mod_ref, gfin_ref, op_ref, os_ref, buf, sem):
    t = pl.program_id(0)
    x_new = _apply_pending(slot_ref, slot_next_ref, ys_hbm, x_ref, mod_ref, buf, sem)
    y_out = _rms(x_new, gfin_ref[...])
    npt = NP_TOK // TM

    @pl.when(t < npt)
    def _():
        op_ref[...] = y_out

    @pl.when(t >= npt)
    def _():
        os_ref[...] = y_out


def _final(pending, g_final):
    nt = T // TM
    npt = NP_TOK // TM
    return pl.pallas_call(
        _final_kernel,
        grid=(nt,),
        in_specs=_pending_specs(pending["layer"]) + [pl.BlockSpec((1, D), lambda t: (0, 0))],
        out_specs=[pl.BlockSpec((TM, D), lambda t: (jnp.minimum(t, npt - 1), 0)),
                   pl.BlockSpec((TM, D), lambda t: (jnp.maximum(t - npt, 0), 0))],
        out_shape=[jax.ShapeDtypeStruct((NP_TOK, D), F32), jax.ShapeDtypeStruct((NS_TOK, D), F32)],
        scratch_shapes=PENDING_SCRATCH,
        compiler_params=_cp(1),
        name="final",
    )(*_pending_args(pending), g_final)


def _moe(x, mod, l, rows, bucket, brank, wa, cnt, w1, w3, w2):
    slot, sched, fill_plan = _routing_plan(bucket, brank, cnt)
    xs, ws = _dispatch(fill_plan, slot, wa.reshape(T), rows)
    ys = _experts(sched, xs, ws.reshape(SORTED_TILES, 1, TMM), l, w1, w3, w2)
    return {"slot": slot, "ys": ys, "x": x, "mod": mod, "layer": l}


def _rope_tables():
    pos = jnp.arange(DEC_SEQ)
    r = (pos // GRID_W).astype(F32)
    col = (pos % GRID_W).astype(F32)
    inv = ROPE_BASE ** (-jnp.arange(AX_FREQS, dtype=F32) / AX_FREQS)
    ang = jnp.stack([r[:, None] * inv, col[:, None] * inv], axis=1)
    cos = jnp.cos(ang)[:, :, None, :]
    sin = jnp.sin(ang)[:, :, None, :]
    c32 = jnp.broadcast_to(cos, (DEC_SEQ, 2, 2, AX_FREQS)).reshape(DEC_SEQ, ROPE)
    s32 = jnp.concatenate([-sin, sin], axis=2).reshape(DEC_SEQ, ROPE)
    pad = HEAD_PAD - NOPE - ROPE
    c = jnp.concatenate([jnp.ones((DEC_SEQ, NOPE), F32), c32, jnp.zeros((DEC_SEQ, pad), F32)], axis=1)
    s = jnp.concatenate([jnp.zeros((DEC_SEQ, NOPE), F32), s32, jnp.zeros((DEC_SEQ, pad), F32)], axis=1)
    c_id = jnp.concatenate([jnp.ones((TM, NOPE + ROPE), F32), jnp.zeros((TM, pad), F32)], axis=1)
    s_id = jnp.zeros((TM, HEAD_PAD), F32)
    return jnp.concatenate([c, c_id], axis=0), jnp.concatenate([s, s_id], axis=0)


def _swap_halves(w):
    lead = w.shape[:-1]
    return w.reshape(lead + (2, 2, AX_FREQS))[..., ::-1, :].reshape(lead + (ROPE,))


def _even_weights(w_in, w_q_up, w_kv_up):
    k_in = w_in.shape[0]
    base = 3 * D_CONV + Q_RANK + KV_RANK
    w_kr = w_in[:, base:base + ROPE]
    z = lambda n: jnp.zeros((k_in, n), F32)
    win_ext = jnp.concatenate([
        w_in[:, :base],
        w_kr, z(HEAD_PAD - ROPE),
        z(NOPE), w_kr, z(HEAD_PAD - NOPE - ROPE),
        z(NOPE), _swap_halves(w_kr), z(HEAD_PAD - NOPE - ROPE),
    ], axis=1).astype(BF)
    wq = w_q_up.reshape(Q_RANK, HEADS, QK_DIM)
    zq = lambda n: jnp.zeros((Q_RANK, HEADS, n), F32)
    wq1 = jnp.concatenate([wq, zq(HEAD_PAD - QK_DIM)], axis=2).reshape(Q_RANK, HEADS * HEAD_PAD).astype(BF)
    wq2 = jnp.concatenate([zq(NOPE), _swap_halves(wq[:, :, NOPE:]), zq(HEAD_PAD - QK_DIM)],
                          axis=2).reshape(Q_RANK, HEADS * HEAD_PAD).astype(BF)
    wkv = w_kv_up.reshape(KV_RANK, HEADS, NOPE + V_DIM)
    wk = jnp.concatenate([wkv[:, :, :NOPE], jnp.zeros((KV_RANK, HEADS, HEAD_PAD - NOPE), F32)],
                         axis=2).reshape(KV_RANK, HEADS * HEAD_PAD).astype(BF)
    wv = wkv[:, :, NOPE:].reshape(KV_RANK, HEADS * V_DIM).astype(BF)
    return win_ext, wq1, wq2, wk, wv


def kernel(x_prompt, x_sample, cache_ckv, cache_krope, c, c_ctx, w_ada, b_ada, g_mix, g_ffn, g_final,
           ev_w_in, conv_w, q_norm_g, w_q_up, kv_norm_g, w_kv_up, ev_w_out,
           gm_w_in, gm_v_g, gm_w_s, gm_b_s, gm_w_out, router_w, router_b, moe_w1, moe_w3, moe_w2):
    n_even = ev_w_in.shape[0]
    x_parts = (x_prompt.reshape(NP_TOK, D), x_sample.reshape(NS_TOK, D))
    cc = jnp.concatenate([c, c_ctx[None, :], jnp.zeros((MOD_ROWS - DEC_BATCH - 1, D), F32)], axis=0)
    mod = _ada(cc, w_ada, b_ada).reshape(DEPTH * MOD_ROWS, 1, 6 * D)

    rope_c, rope_s = _rope_tables()
    rw_hi = router_w.T.astype(BF)
    rw_lo = (router_w.T - rw_hi.astype(F32)).astype(BF)
    rwt = jnp.concatenate([rw_hi, rw_lo], axis=0)
    rb = router_b.reshape(N_EXPERTS, 1)
    ev = [_even_weights(ev_w_in[i], w_q_up[i], w_kv_up[i]) for i in range(n_even)]
    kr_ctx = jnp.pad(cache_krope, ((0, 0), (0, 0), (0, 0), (NOPE, HEAD_PAD - NOPE - ROPE)))
    kr_ctx = kr_ctx.transpose(1, 0, 2, 3).reshape(n_even, DEC_BATCH * PAST, HEAD_PAD)
    ckv_ctx = cache_ckv.transpose(1, 0, 2, 3).reshape(n_even, DEC_BATCH * PAST, KV_RANK)
    kc, vc = _ctx_kv(ckv_ctx, kr_ctx, jnp.stack([e[3] for e in ev]), jnp.stack([e[4] for e in ev]))

    ckv_states, kr_states = [], []
    pending = None
    for l in range(DEPTH):
        i = l // 2
        if l % 2 == 0:
            win_ext, wq1, wq2, wk, wv = ev[i]
            outs = _even_in(
                None if pending else x_parts, pending, mod, l, g_mix[l][None, :], win_ext,
                q_norm_g[i][None, :], wq1, wq2, kv_norm_g[i][None, :], wk, wv, rope_c, rope_s)
            bg, cv, q, k, v, ckv, kr = outs[:7]
            if pending:
                x_parts = (outs[7],)
            attn_parts = (_attn_prompt(q, k, v), _attn_sample(q, k, v, kc, vc, i))
            x, *routed = _even_out(x_parts, mod, l, bg, cv, attn_parts, conv_w[i],
                                                 ev_w_out[i].astype(BF), g_ffn[l][None, :], rwt, rb)
            ckv_states.append(ckv[:NP_TOK].reshape(BATCH, SEQ, KV_RANK))
            kr_states.append(kr[:NP_TOK, :ROPE].reshape(BATCH, SEQ, ROPE))
        else:
            x, *routed = _odd(pending, mod, l, g_mix[l][None, :], gm_w_in[i].astype(BF),
                                            gm_v_g[i][None, :], gm_w_s[i].astype(BF), gm_b_s[i].T,
                                            gm_w_out[i].astype(BF), g_ffn[l][None, :], rwt, rb)
        pending = _moe(x, mod, l, *routed, moe_w1, moe_w3, moe_w2)

    y_prompt, y_sample = _final(pending, g_final[None, :])
    y_prompt = y_prompt.reshape(BATCH, SEQ, D)
    y_sample = y_sample.reshape(DEC_BATCH, DEC_SEQ, D)
    return (y_prompt, y_sample, jnp.stack(ckv_states, axis=1), jnp.stack(kr_states, axis=1))
```

```python
import functools
import math

import jax
import jax.numpy as jnp
from jax import lax
from jax.experimental import pallas as pl
from jax.experimental.pallas import tpu as pltpu

D = 1024
BATCH, SEQ = 32, 256
DEC_BATCH, DEC_SEQ = 8, 2048
PAST = 256
DEPTH = 4
GRID_W = 64
D_CONV = 512
HEADS = 8
NOPE, ROPE, V_DIM = 64, 32, 64
QK_DIM = NOPE + ROPE
Q_RANK, KV_RANK = 384, 256
AX_FREQS = ROPE // 4
ROPE_BASE = 10000.0
CHUNK = 128
GM_GROUPS = 8
N_EXPERTS, N_GROUPS, EPG = 16, 4, 4
D_EXPERT = 256
EPS = 1e-6

NP_TOK = BATCH * SEQ
NS_TOK = DEC_BATCH * DEC_SEQ
T = NP_TOK + NS_TOK
MOD_ROWS = 16
HEAD_PAD = 128
HALO = 16

TM = 512
PAIRS = ((0, 1), (0, 2), (0, 3), (1, 2), (1, 3), (2, 3))
NB = N_GROUPS * len(PAIRS)
NB_PAD = 32
TMM = 512
FILL = 128
MAX_STEPS = T // TMM + NB
SORTED_TILES = MAX_STEPS + MAX_STEPS % 2
SORTED_ROWS = SORTED_TILES * TMM
LANES = 128
Y_SUB = D // LANES
X_SUB = Y_SUB
ISSUE_UNROLL = 32
TQ = 512
ATTN_PAIRS = 4
VMEM_LIMIT = 56 * 1024 * 1024

BF = jnp.bfloat16
F32 = jnp.float32


def _cp(n_axes):
    return pltpu.CompilerParams(dimension_semantics=("arbitrary",) * n_axes,
                                vmem_limit_bytes=VMEM_LIMIT)


def _mod_row(t, tm):
    n_prompt_tiles = NP_TOK // tm
    per_seq = DEC_SEQ // tm
    return jnp.where(t < n_prompt_tiles, DEC_BATCH, (t - n_prompt_tiles) // per_seq)


def _rms(x, g):
    return x * lax.rsqrt(jnp.mean(x * x, axis=-1, keepdims=True) + EPS) * g


def _norm_mod(x, g, shift, scale):
    return x * lax.rsqrt(jnp.mean(x * x, axis=-1, keepdims=True) + EPS) * (g * (1.0 + scale)) + shift


def _dot(a, b):
    return jnp.dot(a, b, preferred_element_type=F32)


def _dot_nt(a, b):
    return lax.dot_general(a, b, (((1,), (1,)), ((), ())), preferred_element_type=F32)


ADA_TN = 1536


def _ada_kernel(cc_ref, w_ref, b_ref, o_ref):
    cc = cc_ref[...]
    s = (cc / (1.0 + jnp.exp(-cc))).astype(BF)
    o_ref[0] = _dot(s, w_ref[0].astype(BF)) + b_ref[0]


def _ada(cc, w_ada, b_ada):
    n = 6 * D
    return pl.pallas_call(
        _ada_kernel,
        grid=(DEPTH, n // ADA_TN),
        in_specs=[
            pl.BlockSpec((MOD_ROWS, D), lambda l, j: (0, 0)),
            pl.BlockSpec((1, D, ADA_TN), lambda l, j: (l, 0, j)),
            pl.BlockSpec((1, 1, ADA_TN), lambda l, j: (l, 0, j)),
        ],
        out_specs=pl.BlockSpec((1, MOD_ROWS, ADA_TN), lambda l, j: (l, 0, j)),
        out_shape=jax.ShapeDtypeStruct((DEPTH, MOD_ROWS, n), F32),
        compiler_params=_cp(2),
        name="ada",
    )(cc, w_ada, b_ada.reshape(DEPTH, 1, n))


def _route(logits_t, rb):
    sc = 1.0 / (1.0 + jnp.exp(-logits_t))
    sel = sc + rb
    rows = [sel[e:e + 1, :] for e in range(N_EXPERTS)]
    srows = [sc[e:e + 1, :] for e in range(N_EXPERTS)]

    def top2sum(a, b, c, d):
        hi1, lo1 = jnp.maximum(a, b), jnp.minimum(a, b)
        hi2, lo2 = jnp.maximum(c, d), jnp.minimum(c, d)
        return jnp.maximum(hi1, hi2) + jnp.maximum(jnp.minimum(hi1, hi2), jnp.maximum(lo1, lo2))

    gs = [top2sum(*rows[EPG * g:EPG * (g + 1)]) for g in range(N_GROUPS)]
    best = gs[0]
    gidx = jnp.zeros_like(best, dtype=jnp.int32)
    for g in range(1, N_GROUPS):
        upd = gs[g] > best
        best = jnp.where(upd, gs[g], best)
        gidx = jnp.where(upd, g, gidx)

    picked = []
    for g in range(N_GROUPS):
        grp = rows[EPG * g:EPG * (g + 1)]
        in_g = gidx == g
        for j in range(EPG):
            rank = jnp.zeros_like(gidx)
            for k in range(EPG):
                if k == j:
                    continue
                ahead = grp[k] > grp[j]
                if k < j:
                    ahead = ahead | (grp[k] == grp[j])
                rank = rank + ahead.astype(jnp.int32)
            picked.append(in_g & (rank < 2))
    w = [jnp.where(picked[e], srows[e], 0.0) for e in range(N_EXPERTS)]
    wsum = w[0]
    for e in range(1, N_EXPERTS):
        wsum = wsum + w[e]
    inv = 1.0 / wsum
    pj, wj = [], []
    for j in range(EPG):
        hit, acc = picked[j], w[j]
        for g in range(1, N_GROUPS):
            hit = hit | picked[EPG * g + j]
            acc = acc + w[EPG * g + j]
        pj.append(hit)
        wj.append(acc * inv)
    pair = jnp.full_like(gidx, len(PAIRS) - 1)
    for idx in range(len(PAIRS) - 2, -1, -1):
        a, b = PAIRS[idx]
        pair = jnp.where(pj[a] & pj[b], idx, pair)
    w_a = jnp.where(pj[0], wj[0], jnp.where(pj[1], wj[1], wj[2]))
    w_b = jnp.where(pj[3], wj[3], jnp.where(pj[2], wj[2], wj[1]))
    return (w_a, w_b), gidx * len(PAIRS) + pair


def _ffn_prep(x_new, mod_ref, gf_ref, rwt_ref, rb_ref, h2c_ref, gidx_ref, grank_ref, wa_ref, cnt_ref,
              run_scr):
    t = pl.program_id(0)
    tm = x_new.shape[0]
    shift = mod_ref[0, :, 3 * D:4 * D]
    scale = mod_ref[0, :, 4 * D:5 * D]
    h2 = _norm_mod(x_new, gf_ref[...], shift, scale)
    h_hi = h2.astype(BF)
    h_lo = (h2 - h_hi.astype(F32)).astype(BF)
    by_hi = _dot_nt(rwt_ref[...], h_hi)
    by_lo = _dot_nt(rwt_ref[0:N_EXPERTS, :], h_lo)
    logits_t = by_hi[0:N_EXPERTS, :] + by_hi[N_EXPERTS:2 * N_EXPERTS, :] + by_lo
    wab, gidx = _route(logits_t, rb_ref[...])
    wa_ref[0] = wab[0]

    for k in range(X_SUB):
        h2c_ref[pl.ds(k, tm, stride=X_SUB), :] = h2[:, k * LANES:(k + 1) * LANES]

    @pl.when(t == 0)
    def _():
        run_scr[...] = jnp.zeros_like(run_scr)

    onehot = (lax.broadcasted_iota(jnp.int32, (NB_PAD, tm), 0) == gidx).astype(F32)
    earlier = (lax.broadcasted_iota(jnp.int32, (tm, tm), 0)
               < lax.broadcasted_iota(jnp.int32, (tm, tm), 1)).astype(BF)
    rank = _dot(onehot.astype(BF), earlier)
    run = run_scr[:, 0:1]
    grank = jnp.sum(onehot * (rank + run), axis=0, keepdims=True)
    gidx_ref[0] = gidx
    grank_ref[0] = grank.astype(jnp.int32)
    run_scr[...] = run_scr[...] + jnp.sum(onehot, axis=1, keepdims=True)
    cnt_ref[...] = run_scr[...]


def _prep_out_specs():
    tok = lambda w: pl.BlockSpec((TM, w), lambda t: (t, 0))
    lanes = pl.BlockSpec((1, 1, TM), lambda t: (t, 0, 0))
    rows = pl.BlockSpec((TM * X_SUB, LANES), lambda t: (t, 0))
    return [tok(D), rows, lanes, lanes, lanes, pl.BlockSpec((NB_PAD, 128), lambda t: (0, 0))]


def _prep_out_shapes():
    nt = T // TM
    return [jax.ShapeDtypeStruct((T, D), F32), jax.ShapeDtypeStruct((T * X_SUB, LANES), F32),
            jax.ShapeDtypeStruct((nt, 1, TM), jnp.int32), jax.ShapeDtypeStruct((nt, 1, TM), jnp.int32),
            jax.ShapeDtypeStruct((nt, 1, TM), F32), jax.ShapeDtypeStruct((NB_PAD, 128), F32)]


EV_EXT = 3 * D_CONV + Q_RANK + KV_RANK + 3 * HEAD_PAD


def _tok_specs(parts, width):
    npt = NP_TOK // TM
    if len(parts) == 1:
        return [pl.BlockSpec((TM, width), lambda t: (t, 0))]
    return [pl.BlockSpec((TM, width), lambda t: (jnp.minimum(t, npt - 1), 0)),
            pl.BlockSpec((TM, width), lambda t: (jnp.maximum(t - npt, 0), 0))]


def _tok_load(refs):
    if len(refs) == 1:
        return refs[0][...]
    return jnp.where(pl.program_id(0) < NP_TOK // TM, refs[0][...], refs[1][...])


def _even_in_kernel(n_x, *refs):
    if n_x:
        x_refs, refs = refs[:n_x], refs[n_x:]
        x = _tok_load(x_refs)
    else:
        pend, refs, scratch = refs[:N_PENDING], refs[N_PENDING:-2], refs[-2:]
        refs, xo_ref = refs[:-1], refs[-1]
        x = _apply_pending(*pend, *scratch)
        xo_ref[...] = x
    (mod_ref, g_ref, win_ref, qg_ref, wq1_ref, wq2_ref, kg_ref, wk_ref, wv_ref, rc_ref, rs_ref,
     bg_ref, cv_ref, q_ref, k_ref, v_ref, ckv_ref, kr_ref) = refs
    shift = mod_ref[0, :, 0:D]
    scale = mod_ref[0, :, D:2 * D]
    h = _norm_mod(x, g_ref[...], shift, scale).astype(BF)
    proj = _dot(h, win_ref[...])
    o = 0
    b_g = proj[:, o:o + D_CONV]; o += D_CONV
    c_g = proj[:, o:o + D_CONV]; o += D_CONV
    v_in = proj[:, o:o + D_CONV]; o += D_CONV
    q_a = proj[:, o:o + Q_RANK]; o += Q_RANK
    kv_a = proj[:, o:o + KV_RANK]; o += KV_RANK
    kr_raw = proj[:, o:o + HEAD_PAD]; o += HEAD_PAD
    kr_cat = proj[:, o:o + HEAD_PAD]; o += HEAD_PAD
    kr_sw = proj[:, o:o + HEAD_PAD]

    bg_ref[...] = b_g.astype(BF)
    cv_ref[...] = (c_g * v_in).astype(BF)
    kr_ref[...] = kr_raw

    rc = rc_ref[...]
    rs = rs_ref[...]
    qn = _rms(q_a, qg_ref[...]).astype(BF)
    q1 = _dot(qn, wq1_ref[...])
    q2 = _dot(qn, wq2_ref[...])
    qscale = QK_DIM ** -0.5
    for hd in range(HEADS):
        sl = slice(hd * HEAD_PAD, (hd + 1) * HEAD_PAD)
        q_ref[:, sl] = ((q1[:, sl] * rc + q2[:, sl] * rs) * qscale).astype(BF)

    ckv = _rms(kv_a, kg_ref[...])
    ckv_ref[...] = ckv
    ckv_b = ckv.astype(BF)
    kk = _dot(ckv_b, wk_ref[...])
    kr = kr_cat * rc + kr_sw * rs
    for hd in range(HEADS):
        sl = slice(hd * HEAD_PAD, (hd + 1) * HEAD_PAD)
        k_ref[:, sl] = (kk[:, sl] + kr).astype(BF)
    v_ref[...] = _dot(ckv_b, wv_ref[...]).astype(BF)


def _even_in(x_parts, pending, mod, l, g_mix, win, qg, wq1, wq2, kg, wk, wv, rope_c, rope_s):
    nt = T // TM
    npt = NP_TOK // TM
    per_seq = DEC_SEQ // TM
    ident_blk = DEC_SEQ // TM

    def rope_idx(t):
        return (jnp.where(t < npt, ident_blk, (t - npt) % per_seq), 0)

    full = lambda shape: pl.BlockSpec(shape, lambda t: (0,) * len(shape))
    tok = lambda w: pl.BlockSpec((TM, w), lambda t: (t, 0))
    out_specs = [tok(D_CONV), tok(D_CONV), tok(HEADS * HEAD_PAD), tok(HEADS * HEAD_PAD),
                 tok(HEADS * V_DIM), tok(KV_RANK), tok(HEAD_PAD)]
    out_shape = [
        jax.ShapeDtypeStruct((T, D_CONV), BF), jax.ShapeDtypeStruct((T, D_CONV), BF),
        jax.ShapeDtypeStruct((T, HEADS * HEAD_PAD), BF), jax.ShapeDtypeStruct((T, HEADS * HEAD_PAD), BF),
        jax.ShapeDtypeStruct((T, HEADS * V_DIM), BF),
        jax.ShapeDtypeStruct((T, KV_RANK), F32), jax.ShapeDtypeStruct((T, HEAD_PAD), F32),
    ]
    if pending is None:
        lead_specs, lead_args, scratch = _tok_specs(x_parts, D), tuple(x_parts), []
    else:
        lead_specs, lead_args, scratch = _pending_specs(pending["layer"]), _pending_args(pending), PENDING_SCRATCH
        out_specs.append(tok(D))
        out_shape.append(jax.ShapeDtypeStruct((T, D), F32))
    return pl.pallas_call(
        functools.partial(_even_in_kernel, 0 if pending is not None else len(x_parts)),
        grid=(nt,),
        in_specs=lead_specs + [
            pl.BlockSpec((1, 1, 6 * D), lambda t: (l * MOD_ROWS + _mod_row(t, TM), 0, 0)),
            full((1, D)), full((D, EV_EXT)), full((1, Q_RANK)),
            full((Q_RANK, HEADS * HEAD_PAD)), full((Q_RANK, HEADS * HEAD_PAD)),
            full((1, KV_RANK)), full((KV_RANK, HEADS * HEAD_PAD)), full((KV_RANK, HEADS * V_DIM)),
            pl.BlockSpec((TM, HEAD_PAD), rope_idx), pl.BlockSpec((TM, HEAD_PAD), rope_idx),
        ],
        out_specs=out_specs,
        out_shape=out_shape,
        scratch_shapes=scratch,
        compiler_params=_cp(1),
        name="even_in",
    )(*lead_args, mod, g_mix, win, qg, wq1, wq2, kg, wk, wv, rope_c, rope_s)


CTX_TM = 512


def _ctx_kv_kernel(ckv_ref, kr_ref, wk_ref, wv_ref, k_ref, v_ref):
    ckv_b = ckv_ref[0].astype(BF)
    kk = _dot(ckv_b, wk_ref[0])
    kr = kr_ref[0]
    for hd in range(HEADS):
        sl = slice(hd * HEAD_PAD, (hd + 1) * HEAD_PAD)
        k_ref[0, :, sl] = (kk[:, sl] + kr).astype(BF)
    v_ref[0] = _dot(ckv_b, wv_ref[0]).astype(BF)


def _ctx_kv(ckv_all, kr_all, wk_all, wv_all):
    n_even = ckv_all.shape[0]
    rows = DEC_BATCH * PAST
    return pl.pallas_call(
        _ctx_kv_kernel,
        grid=(n_even, rows // CTX_TM),
        in_specs=[
            pl.BlockSpec((1, CTX_TM, KV_RANK), lambda i, t: (i, t, 0)),
            pl.BlockSpec((1, CTX_TM, HEAD_PAD), lambda i, t: (i, t, 0)),
            pl.BlockSpec((1, KV_RANK, HEADS * HEAD_PAD), lambda i, t: (i, 0, 0)),
            pl.BlockSpec((1, KV_RANK, HEADS * V_DIM), lambda i, t: (i, 0, 0)),
        ],
        out_specs=[
            pl.BlockSpec((1, CTX_TM, HEADS * HEAD_PAD), lambda i, t: (i, t, 0)),
            pl.BlockSpec((1, CTX_TM, HEADS * V_DIM), lambda i, t: (i, t, 0)),
        ],
        out_shape=[jax.ShapeDtypeStruct((n_even, rows, HEADS * HEAD_PAD), BF),
                   jax.ShapeDtypeStruct((n_even, rows, HEADS * V_DIM), BF)],
        compiler_params=_cp(2),
        name="ctx_kv",
    )(ckv_all, kr_all, wk_all, wv_all)


def _attn_body(n_pairs, has_ctx, q_ref, k_ref, v_ref, *rest):
    if has_ctx:
        kc_ref, vc_ref, o_ref = rest
    else:
        (o_ref,) = rest
    lane = lax.broadcasted_iota(jnp.int32, (1, 2 * V_DIM), 1)
    w = 2 * V_DIM
    ones = jnp.ones((k_ref.shape[0], w), BF)
    ones_ctx = jnp.ones((PAST, w), BF)
    for pr in range(n_pairs):
        vsl = slice(pr * w, (pr + 1) * w)
        v = jnp.concatenate([v_ref[:, vsl], ones], axis=1)
        if has_ctx:
            vc = jnp.concatenate([vc_ref[0, :, vsl], ones_ctx], axis=1)
        outs = []
        for sub in range(2):
            hsl = slice((2 * pr + sub) * HEAD_PAD, (2 * pr + sub + 1) * HEAD_PAD)
            q = q_ref[:, hsl]
            s1 = _dot_nt(q, k_ref[:, hsl])
            m = jnp.max(s1, axis=-1, keepdims=True)
            if has_ctx:
                s2 = _dot_nt(q, kc_ref[0, :, hsl])
                m = jnp.maximum(m, jnp.max(s2, axis=-1, keepdims=True))
            acc = _dot(jnp.exp(s1 - m).astype(BF), v)
            if has_ctx:
                acc = acc + _dot(jnp.exp(s2 - m).astype(BF), vc)
            outs.append(acc[:, 0:w] * (1.0 / acc[:, w:2 * w]))
        o_ref[:, vsl] = jnp.where(lane < V_DIM, outs[0], outs[1]).astype(BF)


def _attn_prompt(q, k, v):
    return pl.pallas_call(
        functools.partial(_attn_body, HEADS // 2, False),
        grid=(BATCH,),
        in_specs=[
            pl.BlockSpec((SEQ, HEADS * HEAD_PAD), lambda b: (b, 0)),
            pl.BlockSpec((SEQ, HEADS * HEAD_PAD), lambda b: (b, 0)),
            pl.BlockSpec((SEQ, HEADS * V_DIM), lambda b: (b, 0)),
        ],
        out_specs=pl.BlockSpec((SEQ, HEADS * V_DIM), lambda b: (b, 0)),
        out_shape=jax.ShapeDtypeStruct((NP_TOK, HEADS * V_DIM), BF),
        compiler_params=_cp(1),
        name="attn_prompt",
    )(q, k, v)


def _attn_sample(q, k, v, kc, vc, i):
    s_blk0 = NP_TOK // DEC_SEQ
    q_blk0 = NP_TOK // TQ
    nq = DEC_SEQ // TQ
    hw = ATTN_PAIRS * 2 * HEAD_PAD
    vw = ATTN_PAIRS * 2 * V_DIM
    return pl.pallas_call(
        functools.partial(_attn_body, ATTN_PAIRS, True),
        grid=(DEC_BATCH, HEADS // (2 * ATTN_PAIRS), nq),
        in_specs=[
            pl.BlockSpec((TQ, hw), lambda b, hp, j: (q_blk0 + b * nq + j, hp)),
            pl.BlockSpec((DEC_SEQ, hw), lambda b, hp, j: (s_blk0 + b, hp)),
            pl.BlockSpec((DEC_SEQ, vw), lambda b, hp, j: (s_blk0 + b, hp)),
            pl.BlockSpec((1, PAST, hw), lambda b, hp, j: (i, b, hp)),
            pl.BlockSpec((1, PAST, vw), lambda b, hp, j: (i, b, hp)),
        ],
        out_specs=pl.BlockSpec((TQ, vw), lambda b, hp, j: (b * nq + j, hp)),
        out_shape=jax.ShapeDtypeStruct((NS_TOK, HEADS * V_DIM), BF),
        compiler_params=_cp(3),
        name="attn_sample",
    )(q, k, v, kc, vc)


def _even_out_kernel(n_x, *refs):
    x_refs, refs = refs[:n_x], refs[n_x:]
    (mod_ref, bg_ref, cv_ref, cvp_ref, cvn_ref, atp_ref, ats_ref, cw_ref, wo_ref, gf_ref, rwt_ref, rb_ref,
     xo_ref, h2c_ref, gidx_ref, grank_ref, wa_ref, cnt_ref, run_scr) = refs
    t = pl.program_id(0)
    npt = NP_TOK // TM
    per_seq = DEC_SEQ // TM
    cv = cv_ref[...].astype(F32)
    r = lax.broadcasted_iota(jnp.int32, (TM, 1), 0)
    is_prompt = t < npt
    tile_in_seq = (t - npt) % per_seq
    first_row = jnp.where(is_prompt, 0, jnp.where(tile_in_seq == 0, 0, -1))
    last_row = jnp.where(is_prompt, SEQ - 1, jnp.where(tile_in_seq == per_seq - 1, TM - 1, -1))
    period_mask = jnp.where(is_prompt, SEQ - 1, TM - 1)
    first = (r & period_mask) == first_row
    last = (r & period_mask) == last_row
    prev_row = cvp_ref[HALO - 1:HALO, :].astype(F32)
    next_row = cvn_ref[0:1, :].astype(F32)
    prev = jnp.where(r == 0, prev_row, pltpu.roll(cv, 1, 0))
    prev = jnp.where(first, 0.0, prev)
    nxt = jnp.where(r == TM - 1, next_row, pltpu.roll(cv, TM - 1, 0))
    nxt = jnp.where(last, 0.0, nxt)
    cw = cw_ref[...]
    conv = prev * cw[0:1, :] + cv * cw[1:2, :] + nxt * cw[2:3, :]
    yc = (bg_ref[...].astype(F32) * conv).astype(BF)
    attn = _tok_load((atp_ref, ats_ref))
    out = _dot(yc, wo_ref[0:D_CONV, :]) + _dot(attn, wo_ref[D_CONV:2 * D_CONV, :])
    gate = mod_ref[0, :, 2 * D:3 * D]
    x_new = _tok_load(x_refs) + gate * out
    xo_ref[...] = x_new
    _ffn_prep(x_new, mod_ref, gf_ref, rwt_ref, rb_ref, h2c_ref, gidx_ref, grank_ref, wa_ref, cnt_ref,
              run_scr)


def _even_out(x_parts, mod, l, bg, cv, attn_parts, conv_w, w_out, g_ffn, rwt, rb):
    nt = T // TM
    hb = TM // HALO
    nhb = T // HALO
    full = lambda shape: pl.BlockSpec(shape, lambda t: (0,) * len(shape))
    tok = lambda w: pl.BlockSpec((TM, w), lambda t: (t, 0))
    return pl.pallas_call(
        functools.partial(_even_out_kernel, len(x_parts)),
        grid=(nt,),
        in_specs=_tok_specs(x_parts, D) + [
            pl.BlockSpec((1, 1, 6 * D), lambda t: (l * MOD_ROWS + _mod_row(t, TM), 0, 0)),
            tok(D_CONV), tok(D_CONV),
            pl.BlockSpec((HALO, D_CONV), lambda t: (jnp.maximum(t * hb - 1, 0), 0)),
            pl.BlockSpec((HALO, D_CONV), lambda t: (jnp.minimum((t + 1) * hb, nhb - 1), 0)),
        ] + _tok_specs(attn_parts, HEADS * V_DIM) + [
            full((3, D_CONV)), full((2 * D_CONV, D)), full((1, D)),
            full((2 * N_EXPERTS, D)), full((N_EXPERTS, 1)),
        ],
        out_specs=_prep_out_specs(),
        out_shape=_prep_out_shapes(),
        scratch_shapes=[pltpu.VMEM((NB_PAD, 128), F32)],
        compiler_params=_cp(1),
        name="even_out",
    )(*x_parts, mod, bg, cv, cv, cv, *attn_parts, conv_w, w_out, g_ffn, rwt, rb)


def _odd_kernel(*refs):
    pend, refs, scratch = refs[:N_PENDING], refs[N_PENDING:-2], refs[-2:]
    (mod_ref, g_ref, win_ref, vg_ref, ws_ref, bst_ref, wo_ref, gf_ref, rwt_ref, rb_ref,
     xo_ref, h2c_ref, gidx_ref, grank_ref, wa_ref, cnt_ref, gated_scr, run_scr) = refs
    x = _apply_pending(*pend, *scratch)
    shift = mod_ref[0, :, 0:D]
    scale = mod_ref[0, :, D:2 * D]
    h = _norm_mod(x, g_ref[...], shift, scale).astype(BF)
    def gelu(zl):
        return 0.5 * zl * (1.0 + lax.erf(zl * math.sqrt(0.5)))

    v = _rms(gelu(_dot(h, win_ref[:, D:2 * D])), vg_ref[...]).astype(BF)
    u = gelu(_dot(h, win_ref[:, 0:D]))
    n_chunks = TM // CHUNK
    gch = D // GM_GROUPS
    for g in range(GM_GROUPS):
        csl = slice(g * gch, (g + 1) * gch)
        vg = jnp.concatenate([v[n * CHUNK:(n + 1) * CHUNK, csl] for n in range(n_chunks)], axis=1)
        sg = _dot(ws_ref[g], vg) + bst_ref[:, g:g + 1]
        for n in range(n_chunks):
            rsl = slice(n * CHUNK, (n + 1) * CHUNK)
            gated_scr[rsl, csl] = (u[rsl, csl] * sg[:, n * gch:(n + 1) * gch]).astype(BF)
    out = _dot(gated_scr[...], wo_ref[...])
    gate = mod_ref[0, :, 2 * D:3 * D]
    x_new = x + gate * out
    xo_ref[...] = x_new
    _ffn_prep(x_new, mod_ref, gf_ref, rwt_ref, rb_ref, h2c_ref, gidx_ref, grank_ref, wa_ref, cnt_ref,
              run_scr)


def _odd(pending, mod, l, g_mix, win, v_g, w_s, b_st, w_out, g_ffn, rwt, rb):
    nt = T // TM
    full = lambda shape: pl.BlockSpec(shape, lambda t: (0,) * len(shape))
    return pl.pallas_call(
        _odd_kernel,
        grid=(nt,),
        in_specs=_pending_specs(pending["layer"]) + [
            pl.BlockSpec((1, 1, 6 * D), lambda t: (l * MOD_ROWS + _mod_row(t, TM), 0, 0)),
            full((1, D)), full((D, 2 * D)), full((1, D)),
            full((GM_GROUPS, CHUNK, CHUNK)), full((CHUNK, GM_GROUPS)), full((D, D)),
            full((1, D)), full((2 * N_EXPERTS, D)), full((N_EXPERTS, 1)),
        ],
        out_specs=_prep_out_specs(),
        out_shape=_prep_out_shapes(),
        scratch_shapes=[pltpu.VMEM((TM, D), BF), pltpu.VMEM((NB_PAD, 128), F32)]
        + PENDING_SCRATCH,
        compiler_params=_cp(1),
        name="odd",
    )(*_pending_args(pending), mod, g_mix, win, v_g, w_s, b_st, w_out, g_ffn, rwt, rb)


def _routing_plan(bucket, brank, cnt):
    counts = cnt[:NB, 0].astype(jnp.int32)
    padded = (counts + (TMM - 1)) // TMM * TMM
    ends = jnp.cumsum(padded)
    base = ends - padded
    b = bucket.reshape(T)
    hit = b[:, None] == jnp.arange(NB, dtype=jnp.int32)[None, :]
    slot = brank.reshape(T) + jnp.sum(jnp.where(hit, base[None, :], 0), axis=1)
    n_steps = ends[NB - 1] // TMM
    starts = jnp.minimum(jnp.arange(MAX_STEPS, dtype=jnp.int32) * TMM, ends[NB - 1] - TMM)
    bucket_of_step = jnp.sum((starts[:, None] >= ends[None, :NB - 1]).astype(jnp.int32), axis=1)
    pair_of_step = bucket_of_step % len(PAIRS)
    pairs = jnp.asarray(PAIRS, dtype=jnp.int32)
    sched = jnp.concatenate([bucket_of_step // len(PAIRS), pairs[pair_of_step, 0], pairs[pair_of_step, 1],
                             n_steps[None]]).astype(jnp.int32)
    fill_plan = jnp.concatenate([base + counts, ends, n_steps[None]]).astype(jnp.int32)
    return slot.astype(jnp.int32), sched, fill_plan


def _row_tile(ref, row, sub):
    return ref.at[pl.ds(pl.multiple_of(row * sub, sub), sub), :]


def _row_copies(n_rows, make_copy, per_row=None):
    def body(i, carry):
        for j in range(ISSUE_UNROLL):
            r = i * ISSUE_UNROLL + j
            make_copy(r).start(priority=j % 2)
            if per_row is not None:
                per_row(r)
        return carry
    lax.fori_loop(0, n_rows // ISSUE_UNROLL, body, 0)


def _dispatch_kernel(pad_ref, slot_ref, wa_ref, h_hbm, o_hbm, ws_ref, zero_scr, zvec_scr, stage,
                     zsem, in_sem, row_sem):
    t = pl.program_id(0)
    nt = T // TM
    cur = t % 2

    def load(tile, b):
        return pltpu.make_async_copy(
            h_hbm.at[pl.ds(pl.multiple_of(tile * (TM * X_SUB), TM * X_SUB), TM * X_SUB), :],
            stage.at[b], in_sem.at[b])

    def rows_done(b):
        pltpu.make_async_copy(stage.at[b], o_hbm.at[pl.ds(0, TM * X_SUB), :], row_sem.at[b]).wait()

    @pl.when(t == 0)
    def _():
        load(0, 0).start()

    @pl.when(t == 0)
    def _():
        zvec_scr[...] = jnp.zeros_like(zvec_scr)
        clear = pltpu.make_async_copy(zvec_scr, ws_ref, zsem.at[0])
        clear.start()
        clear.wait()

    @pl.when(t == 0)
    def _():
        zero_scr[...] = jnp.zeros_like(zero_scr)

        def chunk_fill(row):
            return pltpu.make_async_copy(
                zero_scr.at[pl.ds(0, FILL * X_SUB), :],
                o_hbm.at[pl.ds(pl.multiple_of(row * X_SUB, X_SUB), FILL * X_SUB), :], zsem.at[0])

        def tile_fill(s):
            return pltpu.make_async_copy(
                zero_scr, o_hbm.at[pl.ds(s * TMM * X_SUB, TMM * X_SUB), :], zsem.at[1])

        for phase in ("start", "wait"):
            for k in range(NB):
                for c in range(TMM // FILL):
                    top = pad_ref[NB + k] - c * FILL

                    @pl.when(top > pad_ref[k])
                    def _():
                        getattr(chunk_fill(top - FILL), phase)()
        for phase in ("start", "wait"):
            for s in range(T // TMM, SORTED_TILES):
                @pl.when(s >= pad_ref[2 * NB])
                def _():
                    getattr(tile_fill(s), phase)()

    def put_weight(r):
        ws_ref[slot_ref[r]] = wa_ref[r]

    @pl.when(t > 0)
    def _():
        rows_done(1 - cur)

    @pl.when(t + 1 < nt)
    def _():
        load(t + 1, 1 - cur).start()

    load(t, cur).wait()
    _row_copies(TM, lambda r: pltpu.make_async_copy(
        _row_tile(stage.at[cur], r, X_SUB), _row_tile(o_hbm, slot_ref[r], X_SUB), row_sem.at[cur]),
        per_row=put_weight)

    @pl.when(t == nt - 1)
    def _():
        rows_done(cur)


def _dispatch(pad_start, slot, wa, h2c):
    nt = T // TM
    return pl.pallas_call(
        _dispatch_kernel,
        grid_spec=pltpu.PrefetchScalarGridSpec(
            num_scalar_prefetch=1,
            grid=(nt,),
            in_specs=[
                pl.BlockSpec((TM,), lambda t, pad: (t,), memory_space=pltpu.SMEM),
                pl.BlockSpec((TM,), lambda t, pad: (t,), memory_space=pltpu.SMEM),
                pl.BlockSpec(memory_space=pl.ANY),
            ],
            out_specs=[pl.BlockSpec(memory_space=pl.ANY), pl.BlockSpec(memory_space=pltpu.SMEM)],
            scratch_shapes=[pltpu.VMEM((TMM * X_SUB, LANES), F32), pltpu.VMEM((SORTED_ROWS,), F32),
                            pltpu.VMEM((2, TM * X_SUB, LANES), F32),
                            pltpu.SemaphoreType.DMA((2,)), pltpu.SemaphoreType.DMA((2,)),
                            pltpu.SemaphoreType.DMA((2,))],
        ),
        out_shape=[jax.ShapeDtypeStruct((SORTED_ROWS * X_SUB, LANES), F32),
                   jax.ShapeDtypeStruct((SORTED_ROWS,), F32)],
        compiler_params=_cp(1),
        name="dispatch",
    )(pad_start, slot, wa, h2c)


def _experts_kernel(sched_ref, xs_ref, ws_ref, w1_ref, w3_ref, w2_ref, o_ref, w1b, w3b, w2b, wt_scr):
    s = pl.program_id(0)

    @pl.when(jnp.logical_or(s == 0, sched_ref[s] != sched_ref[jnp.maximum(s - 1, 0)]))
    def _():
        w1b[...] = w1_ref[...].astype(BF)
        w3b[...] = w3_ref[...].astype(BF)
        w2b[...] = w2_ref[...].astype(BF)

    n_steps = sched_ref[3 * MAX_STEPS]

    @pl.when(s < n_steps)
    def _():
        h = jnp.concatenate([xs_ref[pl.ds(k, TMM, stride=X_SUB), :] for k in range(X_SUB)],
                            axis=1).astype(BF)
        wt_scr[...] = jnp.zeros_like(wt_scr)
        wt_scr[0:1, :] = ws_ref[...]
        w_first = wt_scr[...].T[:, 0:1]
        weights = (w_first, 1.0 - w_first)
        acc = None
        for j in range(2):
            e = sched_ref[(1 + j) * MAX_STEPS + s]
            a = _dot(h, w1b[e])
            b = _dot(h, w3b[e])
            hid = (a * (1.0 / (1.0 + jnp.exp(-a))) * b * weights[j]).astype(BF)
            y = _dot(hid, w2b[e])
            acc = y if acc is None else acc + y
        for k in range(D // LANES):
            o_ref[pl.ds(k, TMM, stride=Y_SUB), :] = acc[:, k * LANES:(k + 1) * LANES]

    @pl.when(s >= n_steps)
    def _():
        o_ref[...] = jnp.zeros_like(o_ref)


def _experts(sched, xs, ws, l, w1, w3, w2):
    row_blk = lambda s, sched: (jnp.maximum(jnp.minimum(s, sched[3 * MAX_STEPS] - 1), 0), 0)
    grp_blk = lambda s, sched: (l, sched[s], 0, 0)
    return pl.pallas_call(
        _experts_kernel,
        grid_spec=pltpu.PrefetchScalarGridSpec(
            num_scalar_prefetch=1,
            grid=(MAX_STEPS,),
            in_specs=[
                pl.BlockSpec((TMM * X_SUB, LANES), row_blk),
                pl.BlockSpec((None, 1, TMM), lambda s, sched: row_blk(s, sched) + (0,)),
                pl.BlockSpec((None, EPG, D, D_EXPERT), grp_blk),
                pl.BlockSpec((None, EPG, D, D_EXPERT), grp_blk),
                pl.BlockSpec((None, EPG, D_EXPERT, D), grp_blk),
            ],
            out_specs=pl.BlockSpec((TMM * Y_SUB, LANES), lambda s, sched: (s, 0)),
            scratch_shapes=[pltpu.VMEM((EPG, D, D_EXPERT), BF), pltpu.VMEM((EPG, D, D_EXPERT), BF),
                            pltpu.VMEM((EPG, D_EXPERT, D), BF), pltpu.VMEM((LANES, TMM), F32)],
        ),
        out_shape=jax.ShapeDtypeStruct((MAX_STEPS * TMM * Y_SUB, LANES), F32),
        compiler_params=_cp(1),
        name="experts",
    )(sched, xs, ws, w1, w3, w2)


def _pending_specs(l):
    nt = T // TM
    return [
        pl.BlockSpec((TM,), lambda t: (t,), memory_space=pltpu.SMEM),
        pl.BlockSpec((TM,), lambda t: (jnp.minimum(t + 1, nt - 1),), memory_space=pltpu.SMEM),
        pl.BlockSpec((TM,), lambda t: (jnp.minimum(t + 2, nt - 1),), memory_space=pltpu.SMEM),
        pl.BlockSpec(memory_space=pl.ANY),
        pl.BlockSpec((TM, D), lambda t: (t, 0)),
        pl.BlockSpec((1, 1, 6 * D), lambda t: (l * MOD_ROWS + _mod_row(t, TM), 0, 0)),
    ]


def _pending_args(pending):
    return (pending["slot"],) * 3 + (pending["ys"], pending["x"], pending["mod"])


GATHER_BUFS = 3
PENDING_SCRATCH = [pltpu.VMEM((GATHER_BUFS, TM * Y_SUB, LANES), F32), pltpu.SemaphoreType.DMA((GATHER_BUFS,))]
N_PENDING = 6


def _apply_pending(slot_ref, slot1_ref, slot2_ref, ys_hbm, x_ref, mod_ref, buf, sem):
    t = pl.program_id(0)
    nt = T // TM
    cur = t % GATHER_BUFS

    def gather(idx_ref, b):
        _row_copies(TM, lambda r: pltpu.make_async_copy(
            _row_tile(ys_hbm, idx_ref[r], Y_SUB), _row_tile(buf.at[b], r, Y_SUB), sem.at[b]))

    @pl.when(t == 0)
    def _():
        gather(slot_ref, 0)
        if nt > 1:
            gather(slot1_ref, 1)

    @pl.when(t + 2 < nt)
    def _():
        gather(slot2_ref, (t + 2) % GATHER_BUFS)

    pltpu.make_async_copy(ys_hbm.at[pl.ds(0, TM * Y_SUB), :], buf.at[cur], sem.at[cur]).wait()
    gate = mod_ref[0, :, 5 * D:6 * D]
    y = jnp.concatenate([buf[cur, pl.ds(k, TM, stride=Y_SUB), :] for k in range(Y_SUB)], axis=1)
    return x_ref[...] + gate * y


def _final_kernel(slot_ref, slot1_ref, slot2_ref, ys_hbm, x_ref, mod_ref, gfin_ref, op_ref, os_ref,
                  buf, sem):
    t = pl.program_id(0)
    x_new = _apply_pending(slot_ref, slot1_ref, slot2_ref, ys_hbm, x_ref, mod_ref, buf, sem)
    y_out = _rms(x_new, gfin_ref[...])
    npt = NP_TOK // TM

    @pl.when(t < npt)
    def _():
        op_ref[...] = y_out

    @pl.when(t >= npt)
    def _():
        os_ref[...] = y_out


def _final(pending, g_final):
    nt = T // TM
    npt = NP_TOK // TM
    return pl.pallas_call(
        _final_kernel,
        grid=(nt,),
        in_specs=_pending_specs(pending["layer"]) + [pl.BlockSpec((1, D), lambda t: (0, 0))],
        out_specs=[pl.BlockSpec((TM, D), lambda t: (jnp.minimum(t, npt - 1), 0)),
                   pl.BlockSpec((TM, D), lambda t: (jnp.maximum(t - npt, 0), 0))],
        out_shape=[jax.ShapeDtypeStruct((NP_TOK, D), F32), jax.ShapeDtypeStruct((NS_TOK, D), F32)],
        scratch_shapes=PENDING_SCRATCH,
        compiler_params=_cp(1),
        name="final",
    )(*_pending_args(pending), g_final)


def _moe(x, mod, l, rows, bucket, brank, wa, cnt, w1, w3, w2):
    slot, sched, fill_plan = _routing_plan(bucket, brank, cnt)
    xs, ws = _dispatch(fill_plan, slot, wa.reshape(T), rows)
    ys = _experts(sched, xs, ws.reshape(SORTED_TILES, 1, TMM), l, w1, w3, w2)
    return {"slot": slot, "ys": ys, "x": x, "mod": mod, "layer": l}


def _rope_tables():
    pos = jnp.arange(DEC_SEQ)
    r = (pos // GRID_W).astype(F32)
    col = (pos % GRID_W).astype(F32)
    inv = ROPE_BASE ** (-jnp.arange(AX_FREQS, dtype=F32) / AX_FREQS)
    ang = jnp.stack([r[:, None] * inv, col[:, None] * inv], axis=1)
    cos = jnp.cos(ang)[:, :, None, :]
    sin = jnp.sin(ang)[:, :, None, :]
    c32 = jnp.broadcast_to(cos, (DEC_SEQ, 2, 2, AX_FREQS)).reshape(DEC_SEQ, ROPE)
    s32 = jnp.concatenate([-sin, sin], axis=2).reshape(DEC_SEQ, ROPE)
    pad = HEAD_PAD - NOPE - ROPE
    c = jnp.concatenate([jnp.ones((DEC_SEQ, NOPE), F32), c32, jnp.zeros((DEC_SEQ, pad), F32)], axis=1)
    s = jnp.concatenate([jnp.zeros((DEC_SEQ, NOPE), F32), s32, jnp.zeros((DEC_SEQ, pad), F32)], axis=1)
    c_id = jnp.concatenate([jnp.ones((TM, NOPE + ROPE), F32), jnp.zeros((TM, pad), F32)], axis=1)
    s_id = jnp.zeros((TM, HEAD_PAD), F32)
    return jnp.concatenate([c, c_id], axis=0), jnp.concatenate([s, s_id], axis=0)


def _swap_halves(w):
    lead = w.shape[:-1]
    return w.reshape(lead + (2, 2, AX_FREQS))[..., ::-1, :].reshape(lead + (ROPE,))


def _even_weights(w_in, w_q_up, w_kv_up):
    k_in = w_in.shape[0]
    base = 3 * D_CONV + Q_RANK + KV_RANK
    w_kr = w_in[:, base:base + ROPE]
    z = lambda n: jnp.zeros((k_in, n), F32)
    win_ext = jnp.concatenate([
        w_in[:, :base],
        w_kr, z(HEAD_PAD - ROPE),
        z(NOPE), w_kr, z(HEAD_PAD - NOPE - ROPE),
        z(NOPE), _swap_halves(w_kr), z(HEAD_PAD - NOPE - ROPE),
    ], axis=1).astype(BF)
    wq = w_q_up.reshape(Q_RANK, HEADS, QK_DIM)
    zq = lambda n: jnp.zeros((Q_RANK, HEADS, n), F32)
    wq1 = jnp.concatenate([wq, zq(HEAD_PAD - QK_DIM)], axis=2).reshape(Q_RANK, HEADS * HEAD_PAD).astype(BF)
    wq2 = jnp.concatenate([zq(NOPE), _swap_halves(wq[:, :, NOPE:]), zq(HEAD_PAD - QK_DIM)],
                          axis=2).reshape(Q_RANK, HEADS * HEAD_PAD).astype(BF)
    wkv = w_kv_up.reshape(KV_RANK, HEADS, NOPE + V_DIM)
    wk = jnp.concatenate([wkv[:, :, :NOPE], jnp.zeros((KV_RANK, HEADS, HEAD_PAD - NOPE), F32)],
                         axis=2).reshape(KV_RANK, HEADS * HEAD_PAD).astype(BF)
    wv = wkv[:, :, NOPE:].reshape(KV_RANK, HEADS * V_DIM).astype(BF)
    return win_ext, wq1, wq2, wk, wv


def kernel(x_prompt, x_sample, cache_ckv, cache_krope, c, c_ctx, w_ada, b_ada, g_mix, g_ffn, g_final,
           ev_w_in, conv_w, q_norm_g, w_q_up, kv_norm_g, w_kv_up, ev_w_out,
           gm_w_in, gm_v_g, gm_w_s, gm_b_s, gm_w_out, router_w, router_b, moe_w1, moe_w3, moe_w2):
    n_even = ev_w_in.shape[0]
    x_parts = (x_prompt.reshape(NP_TOK, D), x_sample.reshape(NS_TOK, D))
    cc = jnp.concatenate([c, c_ctx[None, :], jnp.zeros((MOD_ROWS - DEC_BATCH - 1, D), F32)], axis=0)
    mod = _ada(cc, w_ada, b_ada).reshape(DEPTH * MOD_ROWS, 1, 6 * D)

    rope_c, rope_s = _rope_tables()
    rw_hi = router_w.T.astype(BF)
    rw_lo = (router_w.T - rw_hi.astype(F32)).astype(BF)
    rwt = jnp.concatenate([rw_hi, rw_lo], axis=0)
    rb = router_b.reshape(N_EXPERTS, 1)
    ev = [_even_weights(ev_w_in[i], w_q_up[i], w_kv_up[i]) for i in range(n_even)]
    kr_ctx = jnp.pad(cache_krope, ((0, 0), (0, 0), (0, 0), (NOPE, HEAD_PAD - NOPE - ROPE)))
    kr_ctx = kr_ctx.transpose(1, 0, 2, 3).reshape(n_even, DEC_BATCH * PAST, HEAD_PAD)
    ckv_ctx = cache_ckv.transpose(1, 0, 2, 3).reshape(n_even, DEC_BATCH * PAST, KV_RANK)
    kc, vc = _ctx_kv(ckv_ctx, kr_ctx, jnp.stack([e[3] for e in ev]), jnp.stack([e[4] for e in ev]))

    ckv_states, kr_states = [], []
    pending = None
    for l in range(DEPTH):
        i = l // 2
        if l % 2 == 0:
            win_ext, wq1, wq2, wk, wv = ev[i]
            outs = _even_in(
                None if pending else x_parts, pending, mod, l, g_mix[l][None, :], win_ext,
                q_norm_g[i][None, :], wq1, wq2, kv_norm_g[i][None, :], wk, wv, rope_c, rope_s)
            bg, cv, q, k, v, ckv, kr = outs[:7]
            if pending:
                x_parts = (outs[7],)
            attn_parts = (_attn_prompt(q, k, v), _attn_sample(q, k, v, kc, vc, i))
            x, *routed = _even_out(x_parts, mod, l, bg, cv, attn_parts, conv_w[i],
                                                 ev_w_out[i].astype(BF), g_ffn[l][None, :], rwt, rb)
            ckv_states.append(ckv[:NP_TOK].reshape(BATCH, SEQ, KV_RANK))
            kr_states.append(kr[:NP_TOK, :ROPE].reshape(BATCH, SEQ, ROPE))
        else:
            x, *routed = _odd(pending, mod, l, g_mix[l][None, :], gm_w_in[i].astype(BF),
                                            gm_v_g[i][None, :], gm_w_s[i].astype(BF), gm_b_s[i].T,
                                            gm_w_out[i].astype(BF), g_ffn[l][None, :], rwt, rb)
        pending = _moe(x, mod, l, *routed, moe_w1, moe_w3, moe_w2)

    y_prompt, y_sample = _final(pending, g_final[None, :])
    y_prompt = y_prompt.reshape(BATCH, SEQ, D)
    y_sample = y_sample.reshape(DEC_BATCH, DEC_SEQ, D)
    return (y_prompt, y_sample, jnp.stack(ckv_states, axis=1), jnp.stack(kr_states, axis=1))
```
